```python
import jax, jax.numpy as jnp
from jax import lax
import numpy as np

D_MODEL = 1024
BATCH = 32
SEQ = 2048
DEPTH = 2
DEC_BATCH = 16
DEC_SEQ = 32
PAST_LEN = 4096

CHUNK = 64
PLE_DIM = 256
MLA_HEADS = 8
NOPE_DIM = 64
ROPE_DIM = 32
V_DIM = 64
Q_LORA = 384
KV_LORA = 256
ROPE_THETA = 10000.0
Q_BLOCK = 128
ATTN_SCALE = (NOPE_DIM + ROPE_DIM) ** -0.5
HG_HEADS = 4
HG_DK = 128
HG_DV = 128
HG_FDIM = HG_HEADS * HG_DK
HG_IDIM = HG_HEADS * HG_DV
HG_BLOCK = 16
N_GROUPS = 4
EXPERTS_PER_GROUP = 8
N_EXPERTS = N_GROUPS * EXPERTS_PER_GROUP
TOP_K = 2
D_EXPERT = 512
MOE_BLOCK = 128
DN_ALPHA = (2 * DEPTH) ** 0.25
DN_BETA = (8 * DEPTH) ** -0.25
EPS = 1e-6
SPLITS = (Q_LORA, KV_LORA, ROPE_DIM, HG_FDIM, HG_FDIM, HG_IDIM, HG_IDIM, D_MODEL, D_MODEL)
D_IN = Q_LORA + KV_LORA + ROPE_DIM + 2 * HG_FDIM + 2 * HG_IDIM + 2 * D_MODEL

kernel_name = 'hybrid_mla_hgrn2_hmoe_stream_step'


def layernorm(x, g, b):
    xf = x.astype(jnp.float32)
    mu = xf.mean(-1, keepdims=True)
    var = jnp.square(xf - mu).mean(-1, keepdims=True)
    return ((xf - mu) * lax.rsqrt(var + EPS)).astype(x.dtype) * g + b


def rmsnorm(x, g):
    xf = x.astype(jnp.float32)
    return (xf * lax.rsqrt(jnp.square(xf).mean(-1, keepdims=True) + EPS)).astype(x.dtype) * g


def rope(x, pos):
    half = ROPE_DIM // 2
    inv = ROPE_THETA ** (-jnp.arange(half, dtype=jnp.float32) / half)
    ang = pos.astype(jnp.float32)[:, None] * inv[None, :]
    shape = (1, x.shape[1]) + (1,) * (x.ndim - 3) + (half,)
    c, s = jnp.cos(ang).reshape(shape), jnp.sin(ang).reshape(shape)
    xf = x.astype(jnp.float32)
    x1, x2 = xf[..., :half], xf[..., half:]
    return jnp.concatenate([x1 * c - x2 * s, x2 * c + x1 * s], axis=-1).astype(x.dtype)


def mla_prompt_attention(q_nope, q_rope, ckv, k_rope, w_uk, w_uv):
    B, S = ckv.shape[:2]
    k_nope = jnp.einsum('bsc,chn->bshn', ckv, w_uk)
    v = jnp.einsum('bsc,chv->bshv', ckv, w_uv)
    nqb = S // Q_BLOCK
    key_chunk = jnp.arange(S) // CHUNK

    def block(args):
        qn, qr, j = args
        s = jnp.einsum('bqhn,bkhn->bhqk', qn, k_nope) + jnp.einsum('bqhr,bkr->bhqk', qr, k_rope)
        s = s.astype(jnp.float32) * ATTN_SCALE
        q_chunk = (j * Q_BLOCK + jnp.arange(Q_BLOCK)) // CHUNK
        mask = key_chunk[None, :] <= q_chunk[:, None]
        p = jax.nn.softmax(jnp.where(mask, s, -jnp.inf), axis=-1).astype(v.dtype)
        return jnp.einsum('bhqk,bkhv->bqhv', p, v)

    qn_b = q_nope.reshape(B, nqb, Q_BLOCK, MLA_HEADS, NOPE_DIM).swapaxes(0, 1)
    qr_b = q_rope.reshape(B, nqb, Q_BLOCK, MLA_HEADS, ROPE_DIM).swapaxes(0, 1)
    out = lax.map(block, (qn_b, qr_b, jnp.arange(nqb)))
    return out.swapaxes(0, 1).reshape(B, S, MLA_HEADS * V_DIM)


def mla_sample_attention(q_nope, q_rope, ckv_all, krope_all, w_uk, w_uv):
    B, L = q_nope.shape[:2]
    q_lat = jnp.einsum('bqhn,chn->bqhc', q_nope, w_uk)
    s = jnp.einsum('bqhc,bkc->bhqk', q_lat, ckv_all) + jnp.einsum('bqhr,bkr->bhqk', q_rope, krope_all)
    p = jax.nn.softmax(s.astype(jnp.float32) * ATTN_SCALE, axis=-1).astype(ckv_all.dtype)
    lat = jnp.einsum('bhqk,bkc->bqhc', p, ckv_all)
    out = jnp.einsum('bqhc,chv->bqhv', lat, w_uv)
    return out.reshape(B, L, MLA_HEADS * V_DIM)


def hgrn2_lower_bounds(hg_lb):
    c = jnp.cumsum(jax.nn.softmax(hg_lb.astype(jnp.float32), axis=0), axis=0)
    return c - c[0:1]


def hgrn2_features(f_pre, q_pre, i_pre, lb):
    B, L, _ = f_pre.shape
    z = f_pre.astype(jnp.float32).reshape(B, L, HG_HEADS, HG_DK)
    lb = lb.reshape(HG_HEADS, HG_DK)
    log_f = jnp.logaddexp(jnp.log(lb), jnp.log1p(-lb) + jax.nn.log_sigmoid(z))
    k = (1.0 - lb) * jax.nn.sigmoid(-z)
    q = jax.nn.silu(q_pre.astype(jnp.float32)).reshape(B, L, HG_HEADS, HG_DK)
    v = i_pre.astype(jnp.float32).reshape(B, L, HG_HEADS, HG_DV)
    return q, k, v, log_f


def hgrn2_block(q, k, v, log_f, S0):
    L = q.shape[1]
    b = jnp.cumsum(log_f, axis=1)
    inter = jnp.einsum('blhk,bhkv->blhv', q * jnp.exp(b), S0)
    diff = b[:, :, None] - b[:, None, :]
    causal = (jnp.arange(L)[:, None] >= jnp.arange(L)[None, :])[None, :, :, None, None]
    decay = jnp.exp(jnp.where(causal, diff, -jnp.inf))
    A = jnp.einsum('bthk,btshk,bshk->bhts', q, decay, k)
    o = inter + jnp.einsum('bhts,bshv->bthv', A, v)
    b_end = b[:, -1]
    S = jnp.exp(b_end)[..., None] * S0 + jnp.einsum('bshk,bshv->bhkv', k * jnp.exp(b_end[:, None] - b), v)
    return o, S


def hgrn2_prompt(q, k, v, log_f):
    B, L = q.shape[:2]
    nb = L // HG_BLOCK

    def to_blocks(t):
        return t.reshape((B, nb, HG_BLOCK) + t.shape[2:]).swapaxes(0, 1)

    def step(S, blk):
        qb, kb, vb, fb = blk
        o, S = hgrn2_block(qb, kb, vb, fb, S)
        return S, o

    S0 = jnp.zeros((B, HG_HEADS, HG_DK, HG_DV), jnp.float32)
    S, o = lax.scan(step, S0, (to_blocks(q), to_blocks(k), to_blocks(v), to_blocks(log_f)))
    return o.swapaxes(0, 1).reshape(B, L, HG_HEADS, HG_DV), S


def moe_dispatch(xt, eidx, wts, w1, w3, w2):
    T, D = xt.shape
    A = T * TOP_K
    flat_e = eidx.reshape(A).astype(jnp.int32)
    flat_tok = jnp.repeat(jnp.arange(T, dtype=jnp.int32), TOP_K)
    flat_w = wts.reshape(A)
    order = jnp.argsort(flat_e)
    se, stok, sw = flat_e[order], flat_tok[order], flat_w[order]
    counts = jnp.zeros((N_EXPERTS,), jnp.int32).at[flat_e].add(1)
    start = jnp.cumsum(counts) - counts
    padded = (counts + MOE_BLOCK - 1) // MOE_BLOCK * MOE_BLOCK
    pend = jnp.cumsum(padded)
    pstart = pend - padded
    pos = pstart[se] + jnp.arange(A, dtype=jnp.int32) - start[se]
    n_blk = (A + N_EXPERTS * (MOE_BLOCK - 1) + MOE_BLOCK - 1) // MOE_BLOCK
    buf = jnp.zeros((n_blk * MOE_BLOCK, D), xt.dtype).at[pos].set(xt[stok])
    blk_e = jnp.searchsorted(pend, jnp.arange(n_blk, dtype=jnp.int32) * MOE_BLOCK, side='right')
    blk_e = jnp.minimum(blk_e, N_EXPERTS - 1)

    def expert_block(args):
        xb, e = args
        h = jax.nn.silu(xb @ w1[e]) * (xb @ w3[e])
        return h @ w2[e]

    out = lax.map(expert_block, (buf.reshape(n_blk, MOE_BLOCK, D), blk_e)).reshape(n_blk * MOE_BLOCK, D)
    return jnp.zeros((T, D), xt.dtype).at[stok].add(out[pos] * sw[:, None].astype(xt.dtype))


def hier_moe(x, w_rg, w_re, w1, w3, w2):
    B, L, D = x.shape
    xt = x.reshape(B * L, D)
    T = B * L
    g_prob = jax.nn.softmax((xt @ w_rg).astype(jnp.float32), axis=-1)
    g_top, g_idx = lax.top_k(g_prob, 1)
    e_logits = (xt @ w_re).astype(jnp.float32).reshape(T, N_GROUPS, EXPERTS_PER_GROUP)
    e_logits = jnp.take_along_axis(e_logits, g_idx[:, :, None], axis=1)[:, 0]
    e_top, e_idx = lax.top_k(jax.nn.softmax(e_logits, axis=-1), TOP_K)
    e_top = e_top / e_top.sum(-1, keepdims=True)
    wts = g_top * e_top
    eidx = g_idx * EXPERTS_PER_GROUP + e_idx
    return moe_dispatch(xt, eidx, wts, w1, w3, w2).reshape(B, L, D)


def trunk_layer(x, ple, pos, lp, lb, cache):
    B, L, _ = x.shape
    proj = x @ lp['w_in']
    cq, ckv, kr, f_pre, q_pre, i_pre, g_pre, ga, gb = jnp.split(proj, np.cumsum(SPLITS)[:-1].tolist(), axis=-1)
    q = (rmsnorm(cq, lp['q_norm']) @ lp['w_uq']).reshape(B, L, MLA_HEADS, NOPE_DIM + ROPE_DIM)
    q_nope, q_rope = q[..., :NOPE_DIM], rope(q[..., NOPE_DIM:], pos)
    ckv = rmsnorm(ckv, lp['kv_norm'])
    kr = rope(kr, pos)
    hq, hk, hv, hlogf = hgrn2_features(f_pre, q_pre, i_pre, lb)
    if cache is None:
        a = mla_prompt_attention(q_nope, q_rope, ckv, kr, lp['w_uk'], lp['w_uv'])
        o, S = hgrn2_prompt(hq, hk, hv, hlogf)
    else:
        ckv_all = jnp.concatenate([cache[0], ckv], axis=1)
        kr_all = jnp.concatenate([cache[1], kr], axis=1)
        a = mla_sample_attention(q_nope, q_rope, ckv_all, kr_all, lp['w_uk'], lp['w_uv'])
        o, S = hgrn2_block(hq, hk, hv, hlogf, cache[2].astype(jnp.float32))
    o = o * lax.rsqrt(jnp.square(o).mean(-1, keepdims=True) + EPS) * lp['hg_norm'].reshape(HG_HEADS, HG_DV).astype(jnp.float32)
    o = o * jax.nn.silu(g_pre.astype(jnp.float32)).reshape(B, L, HG_HEADS, HG_DV)
    o = o.reshape(B, L, HG_IDIM).astype(x.dtype)
    merged = jax.nn.sigmoid(ga) * (a @ lp['w_br_a']) + jax.nn.sigmoid(gb) * (o @ lp['w_br_b'])
    x = layernorm(DN_ALPHA * x + merged @ lp['w_out'], lp['ln1_g'], lp['ln1_b'])
    moe_out = hier_moe(x, lp['w_rg'], lp['w_re'], lp['w1'], lp['w3'], lp['w2'])
    ple_term = jax.nn.sigmoid(x @ lp['w_ple_gate']) * (ple @ lp['w_ple'])
    x = layernorm(DN_ALPHA * x + moe_out + ple_term, lp['ln2_g'], lp['ln2_b'])
    return x, ckv, kr, S.astype(x.dtype)


def setup_inputs(seed: int = 0) -> dict:
    key = jax.random.key(seed)
    ks = jax.random.split(key, 40)

    def nrm(k, shape, scale):
        return jax.random.normal(k, shape, jnp.float32) * scale

    col_scale = jnp.concatenate([
        jnp.ones((Q_LORA + KV_LORA + ROPE_DIM + 2 * HG_FDIM,), jnp.float32),
        jnp.full((HG_IDIM,), DN_BETA, jnp.float32),
        jnp.ones((HG_IDIM + 2 * D_MODEL,), jnp.float32)])
    return {
        'x_prompt': nrm(ks[0], (BATCH, SEQ, D_MODEL), 1.0),
        'x_sample': nrm(ks[1], (DEC_BATCH, DEC_SEQ, D_MODEL), 1.0),
        'p_prompt': nrm(ks[2], (DEPTH, BATCH, SEQ, PLE_DIM), 1.0),
        'p_sample': nrm(ks[3], (DEPTH, DEC_BATCH, DEC_SEQ, PLE_DIM), 1.0),
        'cache_ckv': nrm(ks[4], (DEPTH, DEC_BATCH, PAST_LEN, KV_LORA), 1.0),
        'cache_krope': nrm(ks[5], (DEPTH, DEC_BATCH, PAST_LEN, ROPE_DIM), 1.0),
        'state_hgrn': nrm(ks[6], (DEPTH, DEC_BATCH, HG_HEADS, HG_DK, HG_DV), 0.3),
        'ln0_g': 1.0 + nrm(ks[7], (D_MODEL,), 0.05),
        'ln0_b': nrm(ks[8], (D_MODEL,), 0.02),
        'w_in': nrm(ks[9], (DEPTH, D_MODEL, D_IN), D_MODEL ** -0.5) * col_scale,
        'q_norm': 1.0 + nrm(ks[10], (DEPTH, Q_LORA), 0.05),
        'w_uq': nrm(ks[11], (DEPTH, Q_LORA, MLA_HEADS * (NOPE_DIM + ROPE_DIM)), Q_LORA ** -0.5),
        'kv_norm': 1.0 + nrm(ks[12], (DEPTH, KV_LORA), 0.05),
        'w_uk': nrm(ks[13], (DEPTH, KV_LORA, MLA_HEADS, NOPE_DIM), KV_LORA ** -0.5),
        'w_uv': nrm(ks[14], (DEPTH, KV_LORA, MLA_HEADS, V_DIM), KV_LORA ** -0.5 * DN_BETA),
        'hg_lb': nrm(ks[15], (DEPTH, HG_FDIM), 0.5),
        'hg_norm': 1.0 + nrm(ks[16], (DEPTH, HG_IDIM), 0.05),
        'w_br_a': nrm(ks[17], (DEPTH, MLA_HEADS * V_DIM, D_MODEL), (MLA_HEADS * V_DIM) ** -0.5),
        'w_br_b': nrm(ks[18], (DEPTH, HG_IDIM, D_MODEL), HG_IDIM ** -0.5),
        'w_out': nrm(ks[19], (DEPTH, D_MODEL, D_MODEL), D_MODEL ** -0.5 * DN_BETA),
        'ln1_g': 1.0 + nrm(ks[20], (DEPTH, D_MODEL), 0.05),
        'ln1_b': nrm(ks[21], (DEPTH, D_MODEL), 0.02),
        'w_router_group': nrm(ks[22], (DEPTH, D_MODEL, N_GROUPS), D_MODEL ** -0.5),
        'w_router_expert': nrm(ks[23], (DEPTH, D_MODEL, N_EXPERTS), D_MODEL ** -0.5),
        'w_e_gate': nrm(ks[24], (DEPTH, N_EXPERTS, D_MODEL, D_EXPERT), D_MODEL ** -0.5 * DN_BETA),
        'w_e_up': nrm(ks[25], (DEPTH, N_EXPERTS, D_MODEL, D_EXPERT), D_MODEL ** -0.5 * DN_BETA),
        'w_e_down': nrm(ks[26], (DEPTH, N_EXPERTS, D_EXPERT, D_MODEL), D_EXPERT ** -0.5 * DN_BETA),
        'w_ple': nrm(ks[27], (DEPTH, PLE_DIM, D_MODEL), PLE_DIM ** -0.5),
        'w_ple_gate': nrm(ks[28], (DEPTH, D_MODEL, D_MODEL), D_MODEL ** -0.5),
        'ln2_g': 1.0 + nrm(ks[29], (DEPTH, D_MODEL), 0.05),
        'ln2_b': nrm(ks[30], (DEPTH, D_MODEL), 0.02),
    }


def reference(x_prompt, x_sample, p_prompt, p_sample, cache_ckv, cache_krope, state_hgrn,
              ln0_g, ln0_b, w_in, q_norm, w_uq, kv_norm, w_uk, w_uv, hg_lb, hg_norm,
              w_br_a, w_br_b, w_out, ln1_g, ln1_b, w_router_group, w_router_expert,
              w_e_gate, w_e_up, w_e_down, w_ple, w_ple_gate, ln2_g, ln2_b):
    lbs = hgrn2_lower_bounds(hg_lb)
    xp = layernorm(x_prompt, ln0_g, ln0_b)
    xs = layernorm(x_sample, ln0_g, ln0_b)
    pos_p = jnp.arange(x_prompt.shape[1], dtype=jnp.int32)
    pos_s = PAST_LEN + jnp.arange(x_sample.shape[1], dtype=jnp.int32)
    ckv_p, kr_p, st_p, ckv_s, kr_s, st_s = [], [], [], [], [], []
    for i in range(DEPTH):
        lp = {'w_in': w_in[i], 'q_norm': q_norm[i], 'w_uq': w_uq[i], 'kv_norm': kv_norm[i],
              'w_uk': w_uk[i], 'w_uv': w_uv[i], 'hg_norm': hg_norm[i], 'w_br_a': w_br_a[i],
              'w_br_b': w_br_b[i], 'w_out': w_out[i], 'ln1_g': ln1_g[i], 'ln1_b': ln1_b[i],
              'w_rg': w_router_group[i], 'w_re': w_router_expert[i], 'w1': w_e_gate[i],
              'w3': w_e_up[i], 'w2': w_e_down[i], 'w_ple': w_ple[i], 'w_ple_gate': w_ple_gate[i],
              'ln2_g': ln2_g[i], 'ln2_b': ln2_b[i]}
        xp, c, r, s = trunk_layer(xp, p_prompt[i], pos_p, lp, lbs[i], None)
        ckv_p.append(c)
        kr_p.append(r)
        st_p.append(s)
        xs, c, r, s = trunk_layer(xs, p_sample[i], pos_s, lp, lbs[i], (cache_ckv[i], cache_krope[i], state_hgrn[i]))
        ckv_s.append(c)
        kr_s.append(r)
        st_s.append(s)
    return (xp, xs, jnp.stack(ckv_p), jnp.stack(kr_p), jnp.stack(st_p), jnp.stack(ckv_s), jnp.stack(kr_s), jnp.stack(st_s))
```

```python
import functools

import numpy as np
import jax
import jax.numpy as jnp
from jax import lax
from jax.experimental import pallas as pl
from jax.experimental.pallas import tpu as pltpu

F32 = jnp.float32
BF16 = jnp.bfloat16

D_MODEL = 1024
CHUNK = 64
PLE_DIM = 256
MLA_HEADS = 8
NOPE_DIM = 64
ROPE_DIM = 32
V_DIM = 64
Q_LORA = 384
KV_LORA = 256
ROPE_THETA = 10000.0
ATTN_SCALE = (NOPE_DIM + ROPE_DIM) ** -0.5
HG_HEADS = 4
HG_DK = 128
HG_DV = 128
HG_FDIM = HG_HEADS * HG_DK
HG_IDIM = HG_HEADS * HG_DV
N_GROUPS = 4
EXPERTS_PER_GROUP = 8
N_EXPERTS = N_GROUPS * EXPERTS_PER_GROUP
TOP_K = 2
D_EXPERT = 512
EPS = 1e-6

LANES = 128
HEAD_PAD = LANES
QK_PAD = MLA_HEADS * HEAD_PAD
VMEM_LIMIT = 56 * 1024 * 1024

TOKEN_TILE = 512
ATTN_TILE = 256
MOE_TILE = 256
GATHER_ROWS = 512
HG_SUB = 16


def _cparams(sem):
    return pltpu.CompilerParams(dimension_semantics=sem, vmem_limit_bytes=VMEM_LIMIT)


def _const_spec(shape):
    nd = len(shape)
    return pl.BlockSpec(shape, lambda *_: (0,) * nd, pipeline_mode=pl.Buffered(1))


def _dot(a, b):
    return jnp.dot(a, b, preferred_element_type=F32)


def _dot_nt(a, b):
    return lax.dot_general(a, b, (((1,), (1,)), ((), ())), preferred_element_type=F32)


def _dot_tn(a, b):
    return lax.dot_general(a, b, (((0,), (0,)), ((), ())), preferred_element_type=F32)


def _div_pow2(x, d):
    return jnp.right_shift(x, int(d).bit_length() - 1)


def _sigmoid(x):
    return 1.0 / (1.0 + jnp.exp(-x))


def _layernorm(x, g, b):
    mu = jnp.mean(x, axis=-1, keepdims=True)
    xc = x - mu
    var = jnp.mean(xc * xc, axis=-1, keepdims=True)
    return xc * lax.rsqrt(var + EPS) * g + b


def _rmsnorm(x, g):
    return x * lax.rsqrt(jnp.mean(x * x, axis=-1, keepdims=True) + EPS) * g


def _ln_kernel(x_ref, g_ref, b_ref, o_ref):
    o_ref[...] = _layernorm(x_ref[...], g_ref[...], b_ref[...])


def _ln_rows(x, g, b):
    t, d = x.shape
    tm = TOKEN_TILE
    return pl.pallas_call(
        _ln_kernel,
        grid=(t // tm,),
        in_specs=[pl.BlockSpec((tm, d), lambda i: (i, 0)), _const_spec((1, d)), _const_spec((1, d))],
        out_specs=pl.BlockSpec((tm, d), lambda i: (i, 0)),
        out_shape=jax.ShapeDtypeStruct((t, d), F32),
        compiler_params=_cparams(("parallel",)),
        name="ln0",
    )(x, g.reshape(1, d), b.reshape(1, d))


N_LAT = Q_LORA + KV_LORA
N_HG = 2 * HG_FDIM + 2 * HG_IDIM
COL_HG = N_LAT
COL_KR = N_LAT + N_HG
N_PROJ = COL_KR + 2 * LANES


def _proj_kernel(x_ref, tab_ref, wa_ref, qn_ref, kvn_ref, wq_ref, wqr_ref, wkv_ref,
                 q_ref, k_ref, va_ref, vb_ref, ckv_ref, kr_ref, f_ref, hq_ref, hv_ref, hg_ref):
    xb = x_ref[...].astype(BF16)
    cq_tab = tab_ref[:, 0:LANES]
    sq_tab = tab_ref[:, LANES:2 * LANES]
    ck_tab = tab_ref[:, 2 * LANES:3 * LANES]
    sk_tab = tab_ref[:, 3 * LANES:4 * LANES]

    lat = _dot(xb, wa_ref[:, 0:N_LAT])
    cqn = _rmsnorm(lat[:, 0:Q_LORA], qn_ref[...]).astype(BF16)
    ckvn = _rmsnorm(lat[:, Q_LORA:N_LAT], kvn_ref[...])
    ckv_ref[...] = ckvn

    qa = _dot(cqn, wq_ref[...])
    qb = _dot(cqn, wqr_ref[...])
    for h in range(MLA_HEADS):
        sl = slice(h * HEAD_PAD, (h + 1) * HEAD_PAD)
        q_ref[:, sl] = (qa[:, sl] * cq_tab + qb[:, sl] * sq_tab).astype(BF16)

    kr2 = _dot(xb, wa_ref[:, COL_KR:N_PROJ])
    kr = kr2[:, 0:LANES] * ck_tab + kr2[:, LANES:2 * LANES] * sk_tab
    kr_ref[...] = kr[:, 0:ROPE_DIM]

    kvin = jnp.concatenate([ckvn.astype(BF16), kr.astype(BF16)], axis=1)
    kv = _dot(kvin, wkv_ref[...])
    k_ref[...] = kv[:, 0:QK_PAD].astype(BF16)
    va_ref[...] = kv[:, QK_PAD:QK_PAD + 512].astype(BF16)
    vb_ref[...] = kv[:, QK_PAD + 512:QK_PAD + 1024].astype(BF16)

    f_ref[...] = _dot(xb, wa_ref[:, COL_HG:COL_HG + 512])
    hq_ref[...] = _dot(xb, wa_ref[:, COL_HG + 512:COL_HG + 1024])
    hv_ref[...] = _dot(xb, wa_ref[:, COL_HG + 1024:COL_HG + 1536]).astype(BF16)
    hg_ref[...] = _dot(xb, wa_ref[:, COL_HG + 1536:COL_HG + 2048])


def _proj(xn, tab, lw):
    t = xn.shape[0]
    tm = TOKEN_TILE
    ntab = tab.shape[0] // tm
    row = lambda i: (i, 0)
    outs = [
        (QK_PAD, BF16), (QK_PAD, BF16), (512, BF16), (512, BF16), (KV_LORA, F32), (ROPE_DIM, F32),
        (HG_FDIM, F32), (HG_FDIM, F32), (HG_IDIM, BF16), (HG_IDIM, F32),
    ]
    return pl.pallas_call(
        _proj_kernel,
        grid=(t // tm,),
        in_specs=[
            pl.BlockSpec((tm, D_MODEL), row),
            pl.BlockSpec((tm, 4 * LANES), lambda i: (i % ntab, 0)),
            _const_spec((D_MODEL, N_PROJ)),
            _const_spec((1, Q_LORA)), _const_spec((1, KV_LORA)),
            _const_spec((Q_LORA, QK_PAD)), _const_spec((Q_LORA, QK_PAD)),
            _const_spec((KV_LORA + LANES, 2 * QK_PAD)),
        ],
        out_specs=[pl.BlockSpec((tm, w), row) for w, _ in outs],
        out_shape=[jax.ShapeDtypeStruct((t, w), dt) for w, dt in outs],
        compiler_params=_cparams(("parallel",)),
        name="proj",
    )(xn, tab, lw["wa"], lw["q_norm"], lw["kv_norm"], lw["wq"], lw["wq_rot"], lw["wkv"])


def _attn_prompt_kernel(q_ref, k_ref, va_ref, vb_ref, o_ref):
    tq = ATTN_TILE
    qi = pl.program_id(1)
    rq = _div_pow2(lax.broadcasted_iota(jnp.int32, (tq, tq), 0), CHUNK)
    ck = _div_pow2(lax.broadcasted_iota(jnp.int32, (tq, tq), 1), CHUNK)
    diag_mask = ck <= rq

    def tile(q, h, v_ref, vcol, ks, carry, mask):
        m, l, acc = carry
        k = k_ref[pl.ds(ks, tq), h * HEAD_PAD:(h + 1) * HEAD_PAD]
        s = _dot_nt(q, k) * ATTN_SCALE
        if mask is not None:
            s = jnp.where(mask, s, -jnp.inf)
        m_new = jnp.maximum(m, jnp.max(s, axis=-1, keepdims=True))
        p = jnp.exp(s - m_new)
        alpha = jnp.exp(m - m_new)
        l = alpha * l + jnp.sum(p, axis=-1, keepdims=True)
        v = v_ref[pl.ds(ks, tq), vcol * LANES:(vcol + 1) * LANES]
        acc = alpha * acc + _dot(p.astype(BF16), v)
        return m_new, l, acc

    for pair in range(MLA_HEADS // 2):
        out_pair = None
        for hh in range(2):
            h = 2 * pair + hh
            v_ref = va_ref if hh == 0 else vb_ref
            q = q_ref[:, h * HEAD_PAD:(h + 1) * HEAD_PAD]
            init = (jnp.full((tq, 1), -jnp.inf, F32), jnp.zeros((tq, 1), F32), jnp.zeros((tq, LANES), F32))

            def body(kt, carry, q=q, h=h, v_ref=v_ref, pair=pair):
                return tile(q, h, v_ref, pair, pl.multiple_of(kt * tq, tq), carry, None)

            carry = lax.fori_loop(0, qi, body, init)
            m, l, acc = tile(q, h, v_ref, pair, pl.multiple_of(qi * tq, tq), carry, diag_mask)
            o = acc / l
            out_pair = o if out_pair is None else out_pair + o
        o_ref[:, pair * LANES:(pair + 1) * LANES] = out_pair.astype(BF16)


def _attn_prompt(q, k, va, vb, batch, seq):
    tq = ATTN_TILE
    nq = seq // tq
    return pl.pallas_call(
        _attn_prompt_kernel,
        grid=(batch, nq),
        in_specs=[
            pl.BlockSpec((tq, QK_PAD), lambda b, i: (b * nq + i, 0)),
            pl.BlockSpec((seq, QK_PAD), lambda b, i: (b, 0)),
            pl.BlockSpec((seq, 512), lambda b, i: (b, 0)),
            pl.BlockSpec((seq, 512), lambda b, i: (b, 0)),
        ],
        out_specs=pl.BlockSpec((tq, 512), lambda b, i: (b * nq + i, 0)),
        out_shape=jax.ShapeDtypeStruct((batch * seq, 512), BF16),
        compiler_params=_cparams(("parallel", "arbitrary")),
        name="attn_prompt",
    )(q, k, va, vb)


SAMPLE_KEY_TILE = 1024


def _attn_sample_kernel(q_ref, ckvn_ref, krn_ref, cc_ref, ck_ref, wukt_ref, psel_ref, wva_ref, wvb_ref,
                        o_ref, s_ref):
    ln = q_ref.shape[0]
    past = cc_ref.shape[0]
    nh = MLA_HEADS
    qlat, qrope = [], []
    for h in range(nh):
        qh = q_ref[:, h * HEAD_PAD:(h + 1) * HEAD_PAD]
        qlat.append(_dot(qh, wukt_ref[h]).astype(BF16))
        qrope.append(_dot(qh, psel_ref[...]).astype(BF16))
    qlat = jnp.concatenate(qlat, axis=0)
    qrope = jnp.concatenate(qrope, axis=0)

    ckv_new = ckvn_ref[...].astype(BF16)
    kr_new = krn_ref[...].astype(BF16)
    s_new = (_dot_nt(qlat, ckv_new) + _dot_nt(qrope, kr_new)) * ATTN_SCALE
    m = jnp.max(s_new, axis=-1, keepdims=True)
    tk = SAMPLE_KEY_TILE
    for kt in range(past // tk):
        rows = slice(kt * tk, (kt + 1) * tk)
        s = (_dot_nt(qlat, cc_ref[rows, :].astype(BF16)) + _dot_nt(qrope, ck_ref[rows, :].astype(BF16))) * ATTN_SCALE
        s_ref[:, rows] = s
        m = jnp.maximum(m, jnp.max(s, axis=-1, keepdims=True))
    p_new = jnp.exp(s_new - m)
    l = jnp.sum(p_new, axis=-1, keepdims=True)
    lat = _dot(p_new.astype(BF16), ckv_new)
    for kt in range(past // tk):
        rows = slice(kt * tk, (kt + 1) * tk)
        p = jnp.exp(s_ref[:, rows] - m)
        l = l + jnp.sum(p, axis=-1, keepdims=True)
        lat = lat + _dot(p.astype(BF16), cc_ref[rows, :].astype(BF16))
    lat = (lat / l).astype(BF16)
    for pair in range(nh // 2):
        he, ho = 2 * pair, 2 * pair + 1
        o = _dot(lat[he * ln:(he + 1) * ln], wva_ref[pair]) + _dot(lat[ho * ln:(ho + 1) * ln], wvb_ref[pair])
        o_ref[:, pair * LANES:(pair + 1) * LANES] = o.astype(BF16)


def _attn_sample(q, ckv_new, kr_new, cache_ckv, cache_kr, lw, batch, ln):
    past = cache_ckv.shape[1]
    nh = MLA_HEADS
    return pl.pallas_call(
        _attn_sample_kernel,
        grid=(batch,),
        in_specs=[
            pl.BlockSpec((ln, QK_PAD), lambda b: (b, 0)),
            pl.BlockSpec((ln, KV_LORA), lambda b: (b, 0)),
            pl.BlockSpec((ln, ROPE_DIM), lambda b: (b, 0)),
            pl.BlockSpec((None, past, KV_LORA), lambda b: (b, 0, 0)),
            pl.BlockSpec((None, past, ROPE_DIM), lambda b: (b, 0, 0)),
            _const_spec((nh, HEAD_PAD, KV_LORA)),
            _const_spec((HEAD_PAD, ROPE_DIM)),
            _const_spec((nh // 2, KV_LORA, LANES)),
            _const_spec((nh // 2, KV_LORA, LANES)),
        ],
        out_specs=pl.BlockSpec((ln, 512), lambda b: (b, 0)),
        out_shape=jax.ShapeDtypeStruct((batch * ln, 512), BF16),
        scratch_shapes=[pltpu.VMEM((nh * ln, past), F32)],
        compiler_params=_cparams(("parallel",)),
        name="attn_sample",
    )(q, ckv_new, kr_new, cache_ckv, cache_kr, lw["wuk_t"], lw["p_sel"], lw["wuv_a3"], lw["wuv_b3"])


def _hgrn_kernel(*refs, layer, chunk, has_init):
    if has_init:
        f_ref, q_ref, v_ref, g_ref, lb_ref, nrm_ref, s0_ref, o_ref, sout_ref, st_ref = refs
    else:
        f_ref, q_ref, v_ref, g_ref, lb_ref, nrm_ref, o_ref, sout_ref, st_ref = refs
        s0_ref = None
    tt = f_ref.shape[0]
    c = chunk
    ti = pl.program_id(1)

    @pl.when(ti == 0)
    def _():
        for h in range(HG_HEADS):
            if has_init:
                st_ref[h] = s0_ref[h].T
            else:
                st_ref[h] = jnp.zeros((HG_DV, HG_DK), F32)

    lbp = lb_ref[...]
    e = jnp.exp(lbp - jnp.max(lbp, axis=0, keepdims=True))
    tot = jnp.sum(e, axis=0, keepdims=True)
    part = jnp.zeros_like(tot)
    for j in range(1, layer + 1):
        part = part + e[j:j + 1]
    lb = part / tot
    log_lb = jnp.log(lb)
    log_1m = jnp.log1p(-lb)
    one_m = 1.0 - lb

    rid = lax.broadcasted_iota(jnp.int32, (c, 1), 0)
    rr = lax.broadcasted_iota(jnp.int32, (c, c), 0)
    cc = lax.broadcasted_iota(jnp.int32, (c, c), 1)
    tri = (cc <= rr).astype(BF16)
    base_mask = jnp.logical_and(_div_pow2(rr, HG_SUB) == _div_pow2(cc, HG_SUB), cc <= rr)

    for ci in range(tt // c):
        rows = slice(ci * c, (ci + 1) * c)
        z = f_ref[rows, :]
        ez = jnp.exp(-jnp.abs(z))
        rz = 1.0 / (1.0 + ez)
        log_sig = jnp.minimum(z, 0.0) - jnp.log1p(ez)
        kk = one_m * jnp.where(z >= 0, ez * rz, rz)
        y = log_1m + log_sig
        logf = jnp.maximum(log_lb, y) + jnp.log1p(jnp.exp(-jnp.abs(log_lb - y)))
        qp = q_ref[rows, :]
        qq = qp * _sigmoid(qp)

        hi = logf.astype(BF16)
        lo = (logf - hi.astype(F32)).astype(BF16)
        b = _dot(tri, hi) + _dot(tri, lo)
        bend = b[c - 1:c, :]

        qs, ks, masks = [], [], []
        g = c
        while g > HG_SUB:
            half = g // 2
            bref = b[half - 1:half, :]
            for gi in range(1, c // g):
                bref = jnp.where(rid >= gi * g, b[gi * g + half - 1:gi * g + half, :], bref)
            x = jnp.exp(-jnp.abs(b - bref))
            right = jnp.bitwise_and(rid, g - 1) >= half
            qs.append(jnp.where(right, qq * x, 0.0).astype(BF16))
            ks.append(jnp.where(right, 0.0, kk * x).astype(BF16))
            masks.append(None if g == c else (_div_pow2(rr, g) == _div_pow2(cc, g)))
            g = half
        bst = jnp.zeros_like(bend)
        for gi in range(1, c // HG_SUB):
            bst = jnp.where(rid >= gi * HG_SUB, b[gi * HG_SUB - 1:gi * HG_SUB, :], bst)
        e0 = b - bst
        qs.append((qq * jnp.exp(e0)).astype(BF16))
        ks.append((kk * jnp.exp(-e0)).astype(BF16))
        masks.append(base_mask)

        q_in = (qq * jnp.exp(b)).astype(BF16)
        k_out = (kk * jnp.exp(bend - b)).astype(BF16)
        dec = jnp.exp(bend)
        gp = g_ref[rows, :]
        gate = gp * _sigmoid(gp) * nrm_ref[...]

        for h in range(HG_HEADS):
            hs = slice(h * HG_DK, (h + 1) * HG_DK)
            a = None
            for ql, kl, mk in zip(qs, ks, masks):
                al = _dot_nt(ql[:, hs], kl[:, hs])
                if mk is not None:
                    al = jnp.where(mk, al, 0.0)
                a = al if a is None else a + al
            vh = v_ref[rows, hs]
            st = st_ref[h]
            o = _dot_nt(q_in[:, hs], st.astype(BF16)) + _dot(a.astype(BF16), vh)
            o = o * lax.rsqrt(jnp.mean(o * o, axis=-1, keepdims=True) + EPS) * gate[:, hs]
            o_ref[rows, hs] = o.astype(BF16)
            st_ref[h] = st * dec[:, hs] + _dot_tn(vh, k_out[:, hs])

    @pl.when(ti == pl.num_programs(1) - 1)
    def _():
        for h in range(HG_HEADS):
            sout_ref[h] = st_ref[h].T


def _hgrn(f_pre, q_pre, v, g_pre, hg_lb, hg_norm, s0, layer, batch, ln):
    tt = min(ln, 256)
    chunk = min(tt, CHUNK)
    nt = ln // tt
    has_init = s0 is not None
    depth = hg_lb.shape[0]
    row = lambda b, t: (b * nt + t, 0)
    st_spec = pl.BlockSpec((None, HG_HEADS, HG_DK, HG_DV), lambda b, t: (b, 0, 0, 0))
    in_specs = [
        pl.BlockSpec((tt, HG_FDIM), row), pl.BlockSpec((tt, HG_FDIM), row),
        pl.BlockSpec((tt, HG_IDIM), row), pl.BlockSpec((tt, HG_IDIM), row),
        _const_spec((depth, HG_FDIM)), _const_spec((1, HG_IDIM)),
    ]
    args = [f_pre, q_pre, v, g_pre, hg_lb, hg_norm]
    if has_init:
        in_specs.append(st_spec)
        args.append(s0)
    return pl.pallas_call(
        functools.partial(_hgrn_kernel, layer=layer, chunk=chunk, has_init=has_init),
        grid=(batch, nt),
        in_specs=in_specs,
        out_specs=[pl.BlockSpec((tt, HG_IDIM), row), st_spec],
        out_shape=[jax.ShapeDtypeStruct((batch * ln, HG_IDIM), BF16),
                   jax.ShapeDtypeStruct((batch, HG_HEADS, HG_DK, HG_DV), F32)],
        scratch_shapes=[pltpu.VMEM((HG_HEADS, HG_DV, HG_DK), F32)],
        compiler_params=_cparams(("parallel", "arbitrary")),
        name="hgrn",
    )(*args)


ROUTE_GROUP_LANE = N_EXPERTS


def _route(logits):
    lane_i = lax.broadcasted_iota(jnp.int32, logits.shape, 1)
    lane = lane_i.astype(F32)
    big = float(LANES)
    neg = -jnp.inf
    is_g = jnp.logical_and(lane_i >= ROUTE_GROUP_LANE, lane_i < ROUTE_GROUP_LANE + N_GROUPS)
    gl = jnp.where(is_g, logits, neg)
    gmax = jnp.max(gl, axis=-1, keepdims=True)
    gidx = jnp.min(jnp.where(gl == gmax, lane - ROUTE_GROUP_LANE, big), axis=-1, keepdims=True)
    g_top = 1.0 / jnp.sum(jnp.exp(gl - gmax), axis=-1, keepdims=True)
    lane_group = _div_pow2(lane_i, EXPERTS_PER_GROUP).astype(F32)
    in_group = jnp.logical_and(lane_i < N_EXPERTS, lane_group == gidx)
    el = jnp.where(in_group, logits, neg)
    m1 = jnp.max(el, axis=-1, keepdims=True)
    i1 = jnp.min(jnp.where(el == m1, lane, big), axis=-1, keepdims=True)
    el2 = jnp.where(lane == i1, neg, el)
    m2 = jnp.max(el2, axis=-1, keepdims=True)
    i2 = jnp.min(jnp.where(el2 == m2, lane, big), axis=-1, keepdims=True)
    r = jnp.exp(m2 - m1)
    w1 = g_top / (1.0 + r)
    w2 = g_top * r / (1.0 + r)
    out = jnp.where(lane_i == 0, i1, 0.0)
    out = jnp.where(lane_i == 1, i2, out)
    out = jnp.where(lane_i == 2, w1, out)
    out = jnp.where(lane_i == 3, w2, out)
    return out


def _post_kernel(x_ref, a_ref, o_ref, ple_ref, wg_ref, wa_ref, wb_ref, wo_ref, g1_ref, b1_ref,
                 wr_ref, wpg_ref, wp_ref, x1_ref, base_ref, route_ref, *, alpha):
    x = x_ref[...]
    xb = x.astype(BF16)
    ya = _dot(a_ref[...], wa_ref[...])
    merged = _sigmoid(_dot(xb, wg_ref[:, 0:D_MODEL])) * ya
    yb = _dot(o_ref[...], wb_ref[...])
    merged = merged + _sigmoid(_dot(xb, wg_ref[:, D_MODEL:2 * D_MODEL])) * yb
    mix = _dot(merged.astype(BF16), wo_ref[...])
    x1 = _layernorm(alpha * x + mix, g1_ref[...], b1_ref[...])
    x1_ref[...] = x1
    x1b = x1.astype(BF16)
    route_ref[...] = _route(_dot(x1b, wr_ref[...]))
    ple = _dot(ple_ref[...].astype(BF16), wp_ref[...])
    base_ref[...] = alpha * x1 + _sigmoid(_dot(x1b, wpg_ref[...])) * ple


def _post(xn, a, o, ple, lw, alpha):
    t = xn.shape[0]
    tm = TOKEN_TILE
    row = lambda i: (i, 0)
    d = D_MODEL
    return pl.pallas_call(
        functools.partial(_post_kernel, alpha=alpha),
        grid=(t // tm,),
        in_specs=[
            pl.BlockSpec((tm, d), row), pl.BlockSpec((tm, 512), row), pl.BlockSpec((tm, HG_IDIM), row),
            pl.BlockSpec((tm, PLE_DIM), row),
            _const_spec((d, 2 * d)), _const_spec((512, d)), _const_spec((HG_IDIM, d)), _const_spec((d, d)),
            _const_spec((1, d)), _const_spec((1, d)),
            _const_spec((d, LANES)), _const_spec((d, d)), _const_spec((PLE_DIM, d)),
        ],
        out_specs=[pl.BlockSpec((tm, d), row), pl.BlockSpec((tm, d), row), pl.BlockSpec((tm, LANES), row)],
        out_shape=[jax.ShapeDtypeStruct((t, d), F32), jax.ShapeDtypeStruct((t, d), F32),
                   jax.ShapeDtypeStruct((t, LANES), F32)],
        compiler_params=_cparams(("parallel",)),
        name="post",
    )(xn, a, o, ple, lw["w_gate"], lw["w_br_a"], lw["w_br_b"], lw["w_out"], lw["ln1_g"], lw["ln1_b"],
      lw["w_route"], lw["w_ple_gate"], lw["w_ple"])


def _gather_kernel(idx_ref, src_ref, out_ref, sem):
    i = pl.program_id(0)
    n = pl.num_programs(0)
    r = GATHER_ROWS
    slot = i % 2

    def issue(j, carry):
        pltpu.make_async_copy(src_ref.at[pl.ds(idx_ref[0, j], 1)], out_ref.at[pl.ds(i * r + j, 1)],
                              sem.at[slot]).start()
        return carry

    lax.fori_loop(0, r, issue, 0, unroll=8)

    def wait_step(s):
        pltpu.make_async_copy(src_ref.at[pl.ds(0, r)], out_ref.at[pl.ds(0, r)], sem.at[s]).wait()

    @pl.when(i > 0)
    def _():
        wait_step(1 - slot)

    @pl.when(i == n - 1)
    def _():
        wait_step(slot)


def _gather_rows(src, idx):
    m = idx.shape[0]
    r = GATHER_ROWS
    return pl.pallas_call(
        _gather_kernel,
        grid=(m // r,),
        in_specs=[pl.BlockSpec((None, 1, r), lambda i: (i, 0, 0), memory_space=pltpu.SMEM),
                  pl.BlockSpec(memory_space=pl.ANY)],
        out_specs=pl.BlockSpec(memory_space=pl.ANY),
        out_shape=jax.ShapeDtypeStruct((m, src.shape[1]), src.dtype),
        scratch_shapes=[pltpu.SemaphoreType.DMA((2,))],
        compiler_params=pltpu.CompilerParams(dimension_semantics=("arbitrary",)),
        name="gather_rows",
    )(idx.reshape(m // r, 1, r), src)


def _moe_kernel(te_ref, nu_ref, x_ref, w1_ref, w3_ref, w2_ref, y_ref):
    i = pl.program_id(0)

    @pl.when(i < nu_ref[0])
    def _():
        xb = x_ref[...].astype(BF16)
        h1 = _dot(xb, w1_ref[...])
        h = h1 * _sigmoid(h1) * _dot(xb, w3_ref[...])
        y_ref[...] = _dot(h.astype(BF16), w2_ref[...])

    @pl.when(i >= nu_ref[0])
    def _():
        y_ref[...] = jnp.zeros_like(y_ref)


def _moe(xs, tile_e, n_used, w1, w3, w2):
    tm = MOE_TILE
    n_tiles = xs.shape[0] // tm
    d = D_MODEL
    grid_spec = pltpu.PrefetchScalarGridSpec(
        num_scalar_prefetch=2,
        grid=(n_tiles,),
        in_specs=[
            pl.BlockSpec((tm, d), lambda i, te, nu: (i, 0)),
            pl.BlockSpec((None, d, D_EXPERT), lambda i, te, nu: (te[i], 0, 0)),
            pl.BlockSpec((None, d, D_EXPERT), lambda i, te, nu: (te[i], 0, 0)),
            pl.BlockSpec((None, D_EXPERT, d), lambda i, te, nu: (te[i], 0, 0)),
        ],
        out_specs=pl.BlockSpec((tm, d), lambda i, te, nu: (i, 0)),
    )
    return pl.pallas_call(
        _moe_kernel,
        grid_spec=grid_spec,
        out_shape=jax.ShapeDtypeStruct(xs.shape, F32),
        compiler_params=_cparams(("arbitrary",)),
        name="moe_experts",
    )(tile_e, n_used, xs, w1, w3, w2)


def _dispatch_plan(route, n_tiles):
    t = route.shape[0]
    a = t * TOP_K
    tm = MOE_TILE
    flat_e = route[:, 0:TOP_K].astype(jnp.int32).reshape(a)
    onehot = (flat_e[:, None] == jnp.arange(N_EXPERTS, dtype=jnp.int32)[None, :]).astype(jnp.int32)
    csum = jnp.cumsum(onehot, axis=0)
    counts = csum[-1]
    rank = jnp.take_along_axis(csum, flat_e[:, None], axis=1)[:, 0] - 1
    tiles_e = (counts + tm - 1) // tm
    tile_end = jnp.cumsum(tiles_e)
    pstart = (tile_end - tiles_e) * tm
    start = jnp.cumsum(counts) - counts
    pos = pstart[flat_e] + rank
    order = jnp.argsort(flat_e, stable=True)
    tile_e = jnp.minimum(jnp.searchsorted(tile_end, jnp.arange(n_tiles, dtype=jnp.int32), side="right"),
                         N_EXPERTS - 1).astype(jnp.int32)
    p = jnp.arange(n_tiles * tm, dtype=jnp.int32)
    pe = tile_e[p // tm]
    off = p - pstart[pe]
    valid = off < counts[pe]
    src = jnp.where(valid, order[jnp.minimum(start[pe] + off, a - 1)] // TOP_K, 0).astype(jnp.int32)
    n_used = tile_end[-1:].astype(jnp.int32)
    return src, pos.astype(jnp.int32), tile_e, n_used


def _final_kernel(base_ref, y_ref, route_ref, g_ref, b_ref, o_ref):
    w = route_ref[...]
    h = base_ref[...] + w[:, 2:3] * y_ref[:, 0:D_MODEL] + w[:, 3:4] * y_ref[:, D_MODEL:2 * D_MODEL]
    o_ref[...] = _layernorm(h, g_ref[...], b_ref[...])


def _final(base, y2, route, g, b):
    t, d = base.shape
    tm = TOKEN_TILE
    row = lambda i: (i, 0)
    return pl.pallas_call(
        _final_kernel,
        grid=(t // tm,),
        in_specs=[pl.BlockSpec((tm, d), row), pl.BlockSpec((tm, 2 * d), row), pl.BlockSpec((tm, LANES), row),
                  _const_spec((1, d)), _const_spec((1, d))],
        out_specs=pl.BlockSpec((tm, d), row),
        out_shape=jax.ShapeDtypeStruct((t, d), F32),
        compiler_params=_cparams(("parallel",)),
        name="final_ln",
    )(base, y2, route, g, b)


def _rot_cols(w):
    half = ROPE_DIM // 2
    return jnp.concatenate([-w[..., half:], w[..., :half]], axis=-1)


def _layer_weights(i, w_in, q_norm, w_uq, kv_norm, w_uk, w_uv, hg_norm, w_br_a, w_br_b, w_out, ln1_g, ln1_b,
                   w_rg, w_re, w_e_gate, w_e_up, w_e_down, w_ple, w_ple_gate, ln2_g, ln2_b):
    d = D_MODEL
    nh = MLA_HEADS
    win = w_in[i]
    c_kr = Q_LORA + KV_LORA
    c_f = c_kr + ROPE_DIM
    c_ga = c_f + N_HG
    w_kr = win[:, c_kr:c_f]
    zpad = jnp.zeros((d, LANES - ROPE_DIM), F32)
    wa = jnp.concatenate([win[:, 0:c_kr], win[:, c_f:c_ga], w_kr, zpad, _rot_cols(w_kr), zpad], axis=1)

    wq3 = w_uq[i].reshape(Q_LORA, nh, NOPE_DIM + ROPE_DIM)
    zq = jnp.zeros((Q_LORA, nh, HEAD_PAD - NOPE_DIM - ROPE_DIM), F32)
    wq = jnp.concatenate([wq3, zq], axis=-1).reshape(Q_LORA, QK_PAD)
    wq_rot = jnp.concatenate([jnp.zeros((Q_LORA, nh, NOPE_DIM), F32), _rot_cols(wq3[..., NOPE_DIM:]), zq],
                             axis=-1).reshape(Q_LORA, QK_PAD)

    wuk = w_uk[i]
    wuv = w_uv[i]
    k_nope = jnp.concatenate([wuk, jnp.zeros((KV_LORA, nh, HEAD_PAD - NOPE_DIM), F32)], axis=-1)
    eye = jnp.eye(ROPE_DIM, dtype=F32)
    k_rope = jnp.concatenate([jnp.zeros((ROPE_DIM, NOPE_DIM), F32), eye,
                              jnp.zeros((ROPE_DIM, HEAD_PAD - NOPE_DIM - ROPE_DIM), F32)], axis=-1)
    k_rope = jnp.concatenate([jnp.tile(k_rope[:, None, :], (1, nh, 1)).reshape(ROPE_DIM, QK_PAD),
                              jnp.zeros((LANES - ROPE_DIM, QK_PAD), F32)], axis=0)
    zv = jnp.zeros((KV_LORA, nh // 2, V_DIM), F32)
    wuv_a3 = jnp.concatenate([wuv[:, 0::2, :], zv], axis=-1)
    wuv_b3 = jnp.concatenate([zv, wuv[:, 1::2, :]], axis=-1)
    wkv_top = jnp.concatenate([k_nope.reshape(KV_LORA, QK_PAD), wuv_a3.reshape(KV_LORA, 512),
                               wuv_b3.reshape(KV_LORA, 512)], axis=1)
    wkv_bot = jnp.concatenate([k_rope, jnp.zeros((LANES, QK_PAD), F32)], axis=1)
    wkv = jnp.concatenate([wkv_top, wkv_bot], axis=0)

    wuk_t = jnp.concatenate([jnp.transpose(wuk, (1, 2, 0)),
                             jnp.zeros((nh, HEAD_PAD - NOPE_DIM, KV_LORA), F32)], axis=1)
    p_sel = jnp.concatenate([jnp.zeros((NOPE_DIM, ROPE_DIM), F32), eye,
                             jnp.zeros((HEAD_PAD - NOPE_DIM - ROPE_DIM, ROPE_DIM), F32)], axis=0)

    w_route = jnp.concatenate([w_re[i], w_rg[i], jnp.zeros((d, LANES - N_EXPERTS - N_GROUPS), F32)], axis=1)
    bf = lambda w: w.astype(BF16)
    return {
        "wa": bf(wa), "q_norm": q_norm[i].reshape(1, -1), "kv_norm": kv_norm[i].reshape(1, -1),
        "wq": bf(wq), "wq_rot": bf(wq_rot), "wkv": bf(wkv),
        "wuk_t": bf(wuk_t), "p_sel": bf(p_sel),
        "wuv_a3": bf(jnp.transpose(wuv_a3, (1, 0, 2))), "wuv_b3": bf(jnp.transpose(wuv_b3, (1, 0, 2))),
        "hg_norm": hg_norm[i].reshape(1, -1),
        "w_gate": bf(win[:, c_ga:c_ga + 2 * d]), "w_br_a": bf(w_br_a[i]), "w_br_b": bf(w_br_b[i]),
        "w_out": bf(w_out[i]), "ln1_g": ln1_g[i].reshape(1, -1), "ln1_b": ln1_b[i].reshape(1, -1),
        "w_route": bf(w_route), "w_ple_gate": bf(w_ple_gate[i]), "w_ple": bf(w_ple[i]),
        "w1": bf(w_e_gate[i]), "w3": bf(w_e_up[i]), "w2": bf(w_e_down[i]),
        "ln2_g": ln2_g[i].reshape(1, -1), "ln2_b": ln2_b[i].reshape(1, -1),
    }


def _rope_table(pos, rows):
    half = ROPE_DIM // 2
    inv = ROPE_THETA ** (-jnp.arange(half, dtype=F32) / half)
    ang = pos.astype(F32)[:, None] * inv[None, :]
    c2 = jnp.concatenate([jnp.cos(ang), jnp.cos(ang)], axis=1)
    s2 = jnp.concatenate([jnp.sin(ang), jnp.sin(ang)], axis=1)
    n = pos.shape[0]
    ones = jnp.ones((n, NOPE_DIM), F32)
    z = lambda w: jnp.zeros((n, w), F32)
    tab = jnp.concatenate([
        ones, c2, z(HEAD_PAD - NOPE_DIM - ROPE_DIM),
        z(NOPE_DIM), s2, z(HEAD_PAD - NOPE_DIM - ROPE_DIM),
        c2, z(LANES - ROPE_DIM),
        s2, z(LANES - ROPE_DIM)], axis=1)
    if rows > n:
        tab = jnp.tile(tab, (rows // n, 1))
    return tab


def _trunk_layer(i, xn, ple, tab, lw, hg_lb, batch, ln, cache, alpha):
    q, k, va, vb, ckv, kr, f_pre, q_pre, hv, g_pre = _proj(xn, tab, lw)
    if cache is None:
        a = _attn_prompt(q, k, va, vb, batch, ln)
        o, st = _hgrn(f_pre, q_pre, hv, g_pre, hg_lb, lw["hg_norm"], None, i, batch, ln)
    else:
        a = _attn_sample(q, ckv, kr, cache[0], cache[1], lw, batch, ln)
        o, st = _hgrn(f_pre, q_pre, hv, g_pre, hg_lb, lw["hg_norm"], cache[2], i, batch, ln)
    x1, base, route = _post(xn, a, o, ple, lw, alpha)

    t = xn.shape[0]
    n_tiles = (t * TOP_K) // MOE_TILE + N_EXPERTS
    n_tiles = -(-n_tiles * MOE_TILE // GATHER_ROWS) * GATHER_ROWS // MOE_TILE
    src, pos, tile_e, n_used = _dispatch_plan(route, n_tiles)
    xs = _gather_rows(x1, src)
    y = _moe(xs, tile_e, n_used, lw["w1"], lw["w3"], lw["w2"])
    y2 = _gather_rows(y, pos).reshape(t, TOP_K * D_MODEL)
    x2 = _final(base, y2, route, lw["ln2_g"], lw["ln2_b"])
    return x2, ckv, kr, st


def kernel(x_prompt, x_sample, p_prompt, p_sample, cache_ckv, cache_krope, state_hgrn, ln0_g, ln0_b, w_in, q_norm, w_uq, kv_norm, w_uk, w_uv, hg_lb, hg_norm, w_br_a, w_br_b, w_out, ln1_g, ln1_b, w_router_group, w_router_expert, w_e_gate, w_e_up, w_e_down, w_ple, w_ple_gate, ln2_g, ln2_b):
    depth = w_in.shape[0]
    bp, sp, d = x_prompt.shape
    bs, ss, _ = x_sample.shape
    past = cache_ckv.shape[2]
    alpha = (2 * depth) ** 0.25

    xp = _ln_rows(x_prompt.reshape(bp * sp, d), ln0_g, ln0_b)
    xs = _ln_rows(x_sample.reshape(bs * ss, d), ln0_g, ln0_b)
    tab_p = _rope_table(jnp.arange(sp, dtype=jnp.int32), max(sp, TOKEN_TILE))
    tab_s = _rope_table(past + jnp.arange(ss, dtype=jnp.int32), max(ss, TOKEN_TILE))

    outs = [[] for _ in range(6)]
    for i in range(depth):
        lw = _layer_weights(i, w_in, q_norm, w_uq, kv_norm, w_uk, w_uv, hg_norm, w_br_a, w_br_b, w_out,
                            ln1_g, ln1_b, w_router_group, w_router_expert, w_e_gate, w_e_up, w_e_down,
                            w_ple, w_ple_gate, ln2_g, ln2_b)
        xp, c, r, s = _trunk_layer(i, xp, p_prompt[i].reshape(bp * sp, -1), tab_p, lw, hg_lb, bp, sp, None, alpha)
        outs[0].append(c.reshape(bp, sp, -1))
        outs[1].append(r.reshape(bp, sp, -1))
        outs[2].append(s)
        cache = (cache_ckv[i], cache_krope[i], state_hgrn[i])
        xs, c, r, s = _trunk_layer(i, xs, p_sample[i].reshape(bs * ss, -1), tab_s, lw, hg_lb, bs, ss, cache, alpha)
        outs[3].append(c.reshape(bs, ss, -1))
        outs[4].append(r.reshape(bs, ss, -1))
        outs[5].append(s)
    return (xp.reshape(bp, sp, d), xs.reshape(bs, ss, d)) + tuple(jnp.stack(o) for o in outs)
```

```python
import functools

import numpy as np
import jax
import jax.numpy as jnp
from jax import lax
from jax.experimental import pallas as pl
from jax.experimental.pallas import tpu as pltpu

F32 = jnp.float32
BF16 = jnp.bfloat16

D_MODEL = 1024
CHUNK = 64
PLE_DIM = 256
MLA_HEADS = 8
NOPE_DIM = 64
ROPE_DIM = 32
V_DIM = 64
Q_LORA = 384
KV_LORA = 256
ROPE_THETA = 10000.0
ATTN_SCALE = (NOPE_DIM + ROPE_DIM) ** -0.5
HG_HEADS = 4
HG_DK = 128
HG_DV = 128
HG_FDIM = HG_HEADS * HG_DK
HG_IDIM = HG_HEADS * HG_DV
N_GROUPS = 4
EXPERTS_PER_GROUP = 8
N_EXPERTS = N_GROUPS * EXPERTS_PER_GROUP
TOP_K = 2
D_EXPERT = 512
EPS = 1e-6

LANES = 128
HEAD_PAD = LANES
QK_PAD = MLA_HEADS * HEAD_PAD
VMEM_LIMIT = 56 * 1024 * 1024

TOKEN_TILE = 512
ATTN_TILE = 256
MOE_TILE = 256
FINAL_TILE = 256
HG_SUB = 16


def _cparams(sem):
    return pltpu.CompilerParams(dimension_semantics=sem, vmem_limit_bytes=VMEM_LIMIT)


def _const_spec(shape):
    nd = len(shape)
    return pl.BlockSpec(shape, lambda *_: (0,) * nd, pipeline_mode=pl.Buffered(1))


def _dot(a, b):
    return jnp.dot(a, b, preferred_element_type=F32)


def _dot_nt(a, b):
    return lax.dot_general(a, b, (((1,), (1,)), ((), ())), preferred_element_type=F32)


def _dot_tn(a, b):
    return lax.dot_general(a, b, (((0,), (0,)), ((), ())), preferred_element_type=F32)


def _div_pow2(x, d):
    return jnp.right_shift(x, int(d).bit_length() - 1)


def _sigmoid(x):
    return 1.0 / (1.0 + jnp.exp(-x))


def _layernorm(x, g, b):
    mu = jnp.mean(x, axis=-1, keepdims=True)
    xc = x - mu
    var = jnp.mean(xc * xc, axis=-1, keepdims=True)
    return xc * lax.rsqrt(var + EPS) * g + b


def _rmsnorm(x, g):
    return x * lax.rsqrt(jnp.mean(x * x, axis=-1, keepdims=True) + EPS) * g


def _ln_kernel(x_ref, g_ref, b_ref, o_ref):
    o_ref[...] = _layernorm(x_ref[...], g_ref[...], b_ref[...])


def _ln_rows(x, g, b):
    t, d = x.shape
    tm = TOKEN_TILE
    return pl.pallas_call(
        _ln_kernel,
        grid=(t // tm,),
        in_specs=[pl.BlockSpec((tm, d), lambda i: (i, 0)), _const_spec((1, d)), _const_spec((1, d))],
        out_specs=pl.BlockSpec((tm, d), lambda i: (i, 0)),
        out_shape=jax.ShapeDtypeStruct((t, d), F32),
        compiler_params=_cparams(("parallel",)),
        name="ln0",
    )(x, g.reshape(1, d), b.reshape(1, d))


N_LAT = Q_LORA + KV_LORA
N_HG = 2 * HG_FDIM + 2 * HG_IDIM
COL_HG = N_LAT
COL_KR = N_LAT + N_HG
N_PROJ = COL_KR + 2 * LANES


def _proj_kernel(x_ref, tab_ref, wa_ref, qn_ref, kvn_ref, wq_ref, wqr_ref, wkv_ref,
                 q_ref, k_ref, va_ref, vb_ref, ckv_ref, kr_ref, f_ref, hq_ref, hv_ref, hg_ref):
    xb = x_ref[...].astype(BF16)
    cq_tab = tab_ref[:, 0:LANES]
    sq_tab = tab_ref[:, LANES:2 * LANES]
    ck_tab = tab_ref[:, 2 * LANES:3 * LANES]
    sk_tab = tab_ref[:, 3 * LANES:4 * LANES]

    lat = _dot(xb, wa_ref[:, 0:N_LAT])
    cqn = _rmsnorm(lat[:, 0:Q_LORA], qn_ref[...]).astype(BF16)
    ckvn = _rmsnorm(lat[:, Q_LORA:N_LAT], kvn_ref[...])
    ckv_ref[...] = ckvn

    qa = _dot(cqn, wq_ref[...])
    qb = _dot(cqn, wqr_ref[...])
    for h in range(MLA_HEADS):
        sl = slice(h * HEAD_PAD, (h + 1) * HEAD_PAD)
        q_ref[:, sl] = (qa[:, sl] * cq_tab + qb[:, sl] * sq_tab).astype(BF16)

    kr2 = _dot(xb, wa_ref[:, COL_KR:N_PROJ])
    kr = kr2[:, 0:LANES] * ck_tab + kr2[:, LANES:2 * LANES] * sk_tab
    kr_ref[...] = kr[:, 0:ROPE_DIM]

    kvin = jnp.concatenate([ckvn.astype(BF16), kr.astype(BF16)], axis=1)
    kv = _dot(kvin, wkv_ref[...])
    k_ref[...] = kv[:, 0:QK_PAD].astype(BF16)
    va_ref[...] = kv[:, QK_PAD:QK_PAD + 512].astype(BF16)
    vb_ref[...] = kv[:, QK_PAD + 512:QK_PAD + 1024].astype(BF16)

    f_ref[...] = _dot(xb, wa_ref[:, COL_HG:COL_HG + 512])
    hq_ref[...] = _dot(xb, wa_ref[:, COL_HG + 512:COL_HG + 1024])
    hv_ref[...] = _dot(xb, wa_ref[:, COL_HG + 1024:COL_HG + 1536]).astype(BF16)
    hg_ref[...] = _dot(xb, wa_ref[:, COL_HG + 1536:COL_HG + 2048])


def _proj(xn, tab, lw):
    t = xn.shape[0]
    tm = TOKEN_TILE
    ntab = tab.shape[0] // tm
    row = lambda i: (i, 0)
    outs = [
        (QK_PAD, BF16), (QK_PAD, BF16), (512, BF16), (512, BF16), (KV_LORA, F32), (ROPE_DIM, F32),
        (HG_FDIM, F32), (HG_FDIM, F32), (HG_IDIM, BF16), (HG_IDIM, F32),
    ]
    return pl.pallas_call(
        _proj_kernel,
        grid=(t // tm,),
        in_specs=[
            pl.BlockSpec((tm, D_MODEL), row),
            pl.BlockSpec((tm, 4 * LANES), lambda i: (i % ntab, 0)),
            _const_spec((D_MODEL, N_PROJ)),
            _const_spec((1, Q_LORA)), _const_spec((1, KV_LORA)),
            _const_spec((Q_LORA, QK_PAD)), _const_spec((Q_LORA, QK_PAD)),
            _const_spec((KV_LORA + LANES, 2 * QK_PAD)),
        ],
        out_specs=[pl.BlockSpec((tm, w), row) for w, _ in outs],
        out_shape=[jax.ShapeDtypeStruct((t, w), dt) for w, dt in outs],
        compiler_params=_cparams(("parallel",)),
        name="proj",
    )(xn, tab, lw["wa"], lw["q_norm"], lw["kv_norm"], lw["wq"], lw["wq_rot"], lw["wkv"])


def _attn_prompt_kernel(q_ref, k_ref, va_ref, vb_ref, o_ref):
    tq = ATTN_TILE
    qi = pl.program_id(1)
    rq = _div_pow2(lax.broadcasted_iota(jnp.int32, (tq, tq), 0), CHUNK)
    ck = _div_pow2(lax.broadcasted_iota(jnp.int32, (tq, tq), 1), CHUNK)
    diag_mask = ck <= rq

    def tile(q, h, v_ref, vcol, ks, carry, mask):
        m, l, acc = carry
        k = k_ref[pl.ds(ks, tq), h * HEAD_PAD:(h + 1) * HEAD_PAD]
        s = _dot_nt(q, k) * ATTN_SCALE
        if mask is not None:
            s = jnp.where(mask, s, -jnp.inf)
        m_new = jnp.maximum(m, jnp.max(s, axis=-1, keepdims=True))
        p = jnp.exp(s - m_new)
        alpha = jnp.exp(m - m_new)
        l = alpha * l + jnp.sum(p, axis=-1, keepdims=True)
        v = v_ref[pl.ds(ks, tq), vcol * LANES:(vcol + 1) * LANES]
        acc = alpha * acc + _dot(p.astype(BF16), v)
        return m_new, l, acc

    for pair in range(MLA_HEADS // 2):
        out_pair = None
        for hh in range(2):
            h = 2 * pair + hh
            v_ref = va_ref if hh == 0 else vb_ref
            q = q_ref[:, h * HEAD_PAD:(h + 1) * HEAD_PAD]
            init = (jnp.full((tq, 1), -jnp.inf, F32), jnp.zeros((tq, 1), F32), jnp.zeros((tq, LANES), F32))

            def body(kt, carry, q=q, h=h, v_ref=v_ref, pair=pair):
                return tile(q, h, v_ref, pair, pl.multiple_of(kt * tq, tq), carry, None)

            carry = lax.fori_loop(0, qi, body, init)
            m, l, acc = tile(q, h, v_ref, pair, pl.multiple_of(qi * tq, tq), carry, diag_mask)
            o = acc / l
            out_pair = o if out_pair is None else out_pair + o
        o_ref[:, pair * LANES:(pair + 1) * LANES] = out_pair.astype(BF16)


def _attn_prompt(q, k, va, vb, batch, seq):
    tq = ATTN_TILE
    nq = seq // tq
    return pl.pallas_call(
        _attn_prompt_kernel,
        grid=(batch, nq),
        in_specs=[
            pl.BlockSpec((tq, QK_PAD), lambda b, i: (b * nq + i, 0)),
            pl.BlockSpec((seq, QK_PAD), lambda b, i: (b, 0)),
            pl.BlockSpec((seq, 512), lambda b, i: (b, 0)),
            pl.BlockSpec((seq, 512), lambda b, i: (b, 0)),
        ],
        out_specs=pl.BlockSpec((tq, 512), lambda b, i: (b * nq + i, 0)),
        out_shape=jax.ShapeDtypeStruct((batch * seq, 512), BF16),
        compiler_params=_cparams(("parallel", "arbitrary")),
        name="attn_prompt",
    )(q, k, va, vb)


SAMPLE_KEY_TILE = 1024


def _attn_sample_kernel(q_ref, ckvn_ref, krn_ref, cc_ref, ck_ref, wukt_ref, psel_ref, wva_ref, wvb_ref,
                        o_ref, s_ref):
    ln = q_ref.shape[0]
    past = cc_ref.shape[0]
    nh = MLA_HEADS
    qlat, qrope = [], []
    for h in range(nh):
        qh = q_ref[:, h * HEAD_PAD:(h + 1) * HEAD_PAD]
        qlat.append(_dot(qh, wukt_ref[h]).astype(BF16))
        qrope.append(_dot(qh, psel_ref[...]).astype(BF16))
    qlat = jnp.concatenate(qlat, axis=0)
    qrope = jnp.concatenate(qrope, axis=0)

    ckv_new = ckvn_ref[...].astype(BF16)
    kr_new = krn_ref[...].astype(BF16)
    s_new = (_dot_nt(qlat, ckv_new) + _dot_nt(qrope, kr_new)) * ATTN_SCALE
    m = jnp.max(s_new, axis=-1, keepdims=True)
    tk = SAMPLE_KEY_TILE
    for kt in range(past // tk):
        rows = slice(kt * tk, (kt + 1) * tk)
        s = (_dot_nt(qlat, cc_ref[rows, :].astype(BF16)) + _dot_nt(qrope, ck_ref[rows, :].astype(BF16))) * ATTN_SCALE
        s_ref[:, rows] = s
        m = jnp.maximum(m, jnp.max(s, axis=-1, keepdims=True))
    p_new = jnp.exp(s_new - m)
    l = jnp.sum(p_new, axis=-1, keepdims=True)
    lat = _dot(p_new.astype(BF16), ckv_new)
    for kt in range(past // tk):
        rows = slice(kt * tk, (kt + 1) * tk)
        p = jnp.exp(s_ref[:, rows] - m)
        l = l + jnp.sum(p, axis=-1, keepdims=True)
        lat = lat + _dot(p.astype(BF16), cc_ref[rows, :].astype(BF16))
    lat = (lat / l).astype(BF16)
    for pair in range(nh // 2):
        he, ho = 2 * pair, 2 * pair + 1
        o = _dot(lat[he * ln:(he + 1) * ln], wva_ref[pair]) + _dot(lat[ho * ln:(ho + 1) * ln], wvb_ref[pair])
        o_ref[:, pair * LANES:(pair + 1) * LANES] = o.astype(BF16)


def _attn_sample(q, ckv_new, kr_new, cache_ckv, cache_kr, lw, batch, ln):
    past = cache_ckv.shape[1]
    nh = MLA_HEADS
    return pl.pallas_call(
        _attn_sample_kernel,
        grid=(batch,),
        in_specs=[
            pl.BlockSpec((ln, QK_PAD), lambda b: (b, 0)),
            pl.BlockSpec((ln, KV_LORA), lambda b: (b, 0)),
            pl.BlockSpec((ln, ROPE_DIM), lambda b: (b, 0)),
            pl.BlockSpec((None, past, KV_LORA), lambda b: (b, 0, 0)),
            pl.BlockSpec((None, past, ROPE_DIM), lambda b: (b, 0, 0)),
            _const_spec((nh, HEAD_PAD, KV_LORA)),
            _const_spec((HEAD_PAD, ROPE_DIM)),
            _const_spec((nh // 2, KV_LORA, LANES)),
            _const_spec((nh // 2, KV_LORA, LANES)),
        ],
        out_specs=pl.BlockSpec((ln, 512), lambda b: (b, 0)),
        out_shape=jax.ShapeDtypeStruct((batch * ln, 512), BF16),
        scratch_shapes=[pltpu.VMEM((nh * ln, past), F32)],
        compiler_params=_cparams(("parallel",)),
        name="attn_sample",
    )(q, ckv_new, kr_new, cache_ckv, cache_kr, lw["wuk_t"], lw["p_sel"], lw["wuv_a3"], lw["wuv_b3"])


def _hgrn_kernel(*refs, layer, chunk, has_init):
    if has_init:
        f_ref, q_ref, v_ref, g_ref, lb_ref, nrm_ref, s0_ref, o_ref, sout_ref, st_ref = refs
    else:
        f_ref, q_ref, v_ref, g_ref, lb_ref, nrm_ref, o_ref, sout_ref, st_ref = refs
        s0_ref = None
    tt = f_ref.shape[0]
    c = chunk
    ti = pl.program_id(1)

    @pl.when(ti == 0)
    def _():
        for h in range(HG_HEADS):
            if has_init:
                st_ref[h] = s0_ref[h].T
            else:
                st_ref[h] = jnp.zeros((HG_DV, HG_DK), F32)

    lbp = lb_ref[...]
    e = jnp.exp(lbp - jnp.max(lbp, axis=0, keepdims=True))
    tot = jnp.sum(e, axis=0, keepdims=True)
    part = jnp.zeros_like(tot)
    for j in range(1, layer + 1):
        part = part + e[j:j + 1]
    lb = part / tot
    log_lb = jnp.log(lb)
    log_1m = jnp.log1p(-lb)
    one_m = 1.0 - lb

    rid = lax.broadcasted_iota(jnp.int32, (c, 1), 0)
    rr = lax.broadcasted_iota(jnp.int32, (c, c), 0)
    cc = lax.broadcasted_iota(jnp.int32, (c, c), 1)
    tri = (cc <= rr).astype(BF16)
    base_mask = jnp.logical_and(_div_pow2(rr, HG_SUB) == _div_pow2(cc, HG_SUB), cc <= rr)

    for ci in range(tt // c):
        rows = slice(ci * c, (ci + 1) * c)
        z = f_ref[rows, :]
        ez = jnp.exp(-jnp.abs(z))
        rz = 1.0 / (1.0 + ez)
        log_sig = jnp.minimum(z, 0.0) - jnp.log1p(ez)
        kk = one_m * jnp.where(z >= 0, ez * rz, rz)
        y = log_1m + log_sig
        logf = jnp.maximum(log_lb, y) + jnp.log1p(jnp.exp(-jnp.abs(log_lb - y)))
        qp = q_ref[rows, :]
        qq = qp * _sigmoid(qp)

        hi = logf.astype(BF16)
        lo = (logf - hi.astype(F32)).astype(BF16)
        b = _dot(tri, hi) + _dot(tri, lo)
        bend = b[c - 1:c, :]

        qs, ks, masks = [], [], []
        g = c
        while g > HG_SUB:
            half = g // 2
            bref = b[half - 1:half, :]
            for gi in range(1, c // g):
                bref = jnp.where(rid >= gi * g, b[gi * g + half - 1:gi * g + half, :], bref)
            x = jnp.exp(-jnp.abs(b - bref))
            right = jnp.bitwise_and(rid, g - 1) >= half
            qs.append(jnp.where(right, qq * x, 0.0).astype(BF16))
            ks.append(jnp.where(right, 0.0, kk * x).astype(BF16))
            masks.append(None if g == c else (_div_pow2(rr, g) == _div_pow2(cc, g)))
            g = half
        bst = jnp.zeros_like(bend)
        for gi in range(1, c // HG_SUB):
            bst = jnp.where(rid >= gi * HG_SUB, b[gi * HG_SUB - 1:gi * HG_SUB, :], bst)
        e0 = b - bst
        qs.append((qq * jnp.exp(e0)).astype(BF16))
        ks.append((kk * jnp.exp(-e0)).astype(BF16))
        masks.append(base_mask)

        q_in = (qq * jnp.exp(b)).astype(BF16)
        k_out = (kk * jnp.exp(bend - b)).astype(BF16)
        dec = jnp.exp(bend)
        gp = g_ref[rows, :]
        gate = gp * _sigmoid(gp) * nrm_ref[...]

        for h in range(HG_HEADS):
            hs = slice(h * HG_DK, (h + 1) * HG_DK)
            a = None
            for ql, kl, mk in zip(qs, ks, masks):
                al = _dot_nt(ql[:, hs], kl[:, hs])
                if mk is not None:
                    al = jnp.where(mk, al, 0.0)
                a = al if a is None else a + al
            vh = v_ref[rows, hs]
            st = st_ref[h]
            o = _dot_nt(q_in[:, hs], st.astype(BF16)) + _dot(a.astype(BF16), vh)
            o = o * lax.rsqrt(jnp.mean(o * o, axis=-1, keepdims=True) + EPS) * gate[:, hs]
            o_ref[rows, hs] = o.astype(BF16)
            st_ref[h] = st * dec[:, hs] + _dot_tn(vh, k_out[:, hs])

    @pl.when(ti == pl.num_programs(1) - 1)
    def _():
        for h in range(HG_HEADS):
            sout_ref[h] = st_ref[h].T


def _hgrn(f_pre, q_pre, v, g_pre, hg_lb, hg_norm, s0, layer, batch, ln):
    tt = min(ln, 256)
    chunk = min(tt, CHUNK)
    nt = ln // tt
    has_init = s0 is not None
    depth = hg_lb.shape[0]
    row = lambda b, t: (b * nt + t, 0)
    st_spec = pl.BlockSpec((None, HG_HEADS, HG_DK, HG_DV), lambda b, t: (b, 0, 0, 0))
    in_specs = [
        pl.BlockSpec((tt, HG_FDIM), row), pl.BlockSpec((tt, HG_FDIM), row),
        pl.BlockSpec((tt, HG_IDIM), row), pl.BlockSpec((tt, HG_IDIM), row),
        _const_spec((depth, HG_FDIM)), _const_spec((1, HG_IDIM)),
    ]
    args = [f_pre, q_pre, v, g_pre, hg_lb, hg_norm]
    if has_init:
        in_specs.append(st_spec)
        args.append(s0)
    return pl.pallas_call(
        functools.partial(_hgrn_kernel, layer=layer, chunk=chunk, has_init=has_init),
        grid=(batch, nt),
        in_specs=in_specs,
        out_specs=[pl.BlockSpec((tt, HG_IDIM), row), st_spec],
        out_shape=[jax.ShapeDtypeStruct((batch * ln, HG_IDIM), BF16),
                   jax.ShapeDtypeStruct((batch, HG_HEADS, HG_DK, HG_DV), F32)],
        scratch_shapes=[pltpu.VMEM((HG_HEADS, HG_DV, HG_DK), F32)],
        compiler_params=_cparams(("parallel", "arbitrary")),
        name="hgrn",
    )(*args)


ROUTE_GROUP_LANE = N_EXPERTS


def _route(logits):
    lane_i = lax.broadcasted_iota(jnp.int32, logits.shape, 1)
    lane = lane_i.astype(F32)
    big = float(LANES)
    neg = -jnp.inf
    is_g = jnp.logical_and(lane_i >= ROUTE_GROUP_LANE, lane_i < ROUTE_GROUP_LANE + N_GROUPS)
    gl = jnp.where(is_g, logits, neg)
    gmax = jnp.max(gl, axis=-1, keepdims=True)
    gidx = jnp.min(jnp.where(gl == gmax, lane - ROUTE_GROUP_LANE, big), axis=-1, keepdims=True)
    g_top = 1.0 / jnp.sum(jnp.exp(gl - gmax), axis=-1, keepdims=True)
    lane_group = _div_pow2(lane_i, EXPERTS_PER_GROUP).astype(F32)
    in_group = jnp.logical_and(lane_i < N_EXPERTS, lane_group == gidx)
    el = jnp.where(in_group, logits, neg)
    m1 = jnp.max(el, axis=-1, keepdims=True)
    i1 = jnp.min(jnp.where(el == m1, lane, big), axis=-1, keepdims=True)
    el2 = jnp.where(lane == i1, neg, el)
    m2 = jnp.max(el2, axis=-1, keepdims=True)
    i2 = jnp.min(jnp.where(el2 == m2, lane, big), axis=-1, keepdims=True)
    r = jnp.exp(m2 - m1)
    w1 = g_top / (1.0 + r)
    w2 = g_top * r / (1.0 + r)
    out = jnp.where(lane_i == 0, i1, 0.0)
    out = jnp.where(lane_i == 1, i2, out)
    out = jnp.where(lane_i == 2, w1, out)
    out = jnp.where(lane_i == 3, w2, out)
    return out


def _post_kernel(x_ref, a_ref, o_ref, ple_ref, wg_ref, wa_ref, wb_ref, wo_ref, g1_ref, b1_ref,
                 wr_ref, wpg_ref, wp_ref, x1_ref, base_ref, route_ref, *, alpha):
    x = x_ref[...]
    xb = x.astype(BF16)
    ya = _dot(a_ref[...], wa_ref[...])
    merged = _sigmoid(_dot(xb, wg_ref[:, 0:D_MODEL])) * ya
    yb = _dot(o_ref[...], wb_ref[...])
    merged = merged + _sigmoid(_dot(xb, wg_ref[:, D_MODEL:2 * D_MODEL])) * yb
    mix = _dot(merged.astype(BF16), wo_ref[...])
    x1 = _layernorm(alpha * x + mix, g1_ref[...], b1_ref[...])
    x1_ref[...] = x1
    x1b = x1.astype(BF16)
    route_ref[...] = _route(_dot(x1b, wr_ref[...]))
    ple = _dot(ple_ref[...].astype(BF16), wp_ref[...])
    base_ref[...] = alpha * x1 + _sigmoid(_dot(x1b, wpg_ref[...])) * ple


def _post(xn, a, o, ple, lw, alpha):
    t = xn.shape[0]
    tm = TOKEN_TILE
    row = lambda i: (i, 0)
    d = D_MODEL
    return pl.pallas_call(
        functools.partial(_post_kernel, alpha=alpha),
        grid=(t // tm,),
        in_specs=[
            pl.BlockSpec((tm, d), row), pl.BlockSpec((tm, 512), row), pl.BlockSpec((tm, HG_IDIM), row),
            pl.BlockSpec((tm, PLE_DIM), row),
            _const_spec((d, 2 * d)), _const_spec((512, d)), _const_spec((HG_IDIM, d)), _const_spec((d, d)),
            _const_spec((1, d)), _const_spec((1, d)),
            _const_spec((d, LANES)), _const_spec((d, d)), _const_spec((PLE_DIM, d)),
        ],
        out_specs=[pl.BlockSpec((tm, d), row), pl.BlockSpec((tm, d), row), pl.BlockSpec((tm, LANES), row)],
        out_shape=[jax.ShapeDtypeStruct((t, d), F32), jax.ShapeDtypeStruct((t, d), F32),
                   jax.ShapeDtypeStruct((t, LANES), F32)],
        compiler_params=_cparams(("parallel",)),
        name="post",
    )(xn, a, o, ple, lw["w_gate"], lw["w_br_a"], lw["w_br_b"], lw["w_out"], lw["ln1_g"], lw["ln1_b"],
      lw["w_route"], lw["w_ple_gate"], lw["w_ple"])


def _issue_row_gather(idx_ref, src_hbm, dst, sem, n_rows, dst_off=0):
    def body(j, carry):
        pltpu.make_async_copy(src_hbm.at[pl.ds(idx_ref[0, j], 1)], dst.at[pl.ds(dst_off + j, 1)], sem).start()
        return carry

    lax.fori_loop(0, n_rows, body, 0, unroll=8)


def _wait_row_gather(src_hbm, dst, sem):
    pltpu.make_async_copy(src_hbm.at[pl.ds(0, dst.shape[0])], dst, sem).wait()


def _idx_specs(tile, n_blocks):
    first = pl.BlockSpec((None, 1, tile), lambda i, *_: (0, 0, 0), memory_space=pltpu.SMEM)
    nxt = pl.BlockSpec((None, 1, tile), lambda i, *_: (jnp.minimum(i + 1, n_blocks - 1), 0, 0),
                       memory_space=pltpu.SMEM)
    return first, nxt


def _moe_kernel(te_ref, nu_ref, idx0_ref, idxn_ref, x_hbm, w1_ref, w3_ref, w2_ref, y_ref, xbuf, sem):
    i = pl.program_id(0)
    n_used = nu_ref[0]
    slot = i % 2

    @pl.when(i == 0)
    def _():
        _issue_row_gather(idx0_ref, x_hbm, xbuf.at[0], sem.at[0], MOE_TILE)

    @pl.when(i + 1 < n_used)
    def _():
        _issue_row_gather(idxn_ref, x_hbm, xbuf.at[1 - slot], sem.at[1 - slot], MOE_TILE)

    @pl.when(i < n_used)
    def _():
        _wait_row_gather(x_hbm, xbuf.at[slot], sem.at[slot])
        xb = xbuf[slot].astype(BF16)
        h1 = _dot(xb, w1_ref[...])
        h = h1 * _sigmoid(h1) * _dot(xb, w3_ref[...])
        y_ref[...] = _dot(h.astype(BF16), w2_ref[...])

    @pl.when(i >= n_used)
    def _():
        y_ref[...] = jnp.zeros_like(y_ref)


def _moe(x, src, tile_e, n_used, w1, w3, w2):
    tm = MOE_TILE
    n_tiles = src.shape[0] // tm
    d = D_MODEL
    first, nxt = _idx_specs(tm, n_tiles)
    grid_spec = pltpu.PrefetchScalarGridSpec(
        num_scalar_prefetch=2,
        grid=(n_tiles,),
        in_specs=[
            first, nxt,
            pl.BlockSpec(memory_space=pl.ANY),
            pl.BlockSpec((None, d, D_EXPERT), lambda i, te, nu: (te[i], 0, 0)),
            pl.BlockSpec((None, d, D_EXPERT), lambda i, te, nu: (te[i], 0, 0)),
            pl.BlockSpec((None, D_EXPERT, d), lambda i, te, nu: (te[i], 0, 0)),
        ],
        out_specs=pl.BlockSpec((tm, d), lambda i, te, nu: (i, 0)),
        scratch_shapes=[pltpu.VMEM((2, tm, d), F32), pltpu.SemaphoreType.DMA((2,))],
    )
    src3 = src.reshape(n_tiles, 1, tm)
    return pl.pallas_call(
        _moe_kernel,
        grid_spec=grid_spec,
        out_shape=jax.ShapeDtypeStruct((n_tiles * tm, d), F32),
        compiler_params=_cparams(("arbitrary",)),
        name="moe_experts",
    )(tile_e, n_used, src3, src3, x, w1, w3, w2)


def _dispatch_plan(route, n_tiles):
    t = route.shape[0]
    a = t * TOP_K
    tm = MOE_TILE
    flat_e = route[:, 0:TOP_K].astype(jnp.int32).reshape(a)
    onehot = (flat_e[:, None] == jnp.arange(N_EXPERTS, dtype=jnp.int32)[None, :]).astype(jnp.int32)
    csum = jnp.cumsum(onehot, axis=0)
    counts = csum[-1]
    rank = jnp.take_along_axis(csum, flat_e[:, None], axis=1)[:, 0] - 1
    tiles_e = (counts + tm - 1) // tm
    tile_end = jnp.cumsum(tiles_e)
    pstart = (tile_end - tiles_e) * tm
    start = jnp.cumsum(counts) - counts
    pos = pstart[flat_e] + rank
    order = jnp.argsort(flat_e, stable=True)
    tile_e = jnp.minimum(jnp.searchsorted(tile_end, jnp.arange(n_tiles, dtype=jnp.int32), side="right"),
                         N_EXPERTS - 1).astype(jnp.int32)
    p = jnp.arange(n_tiles * tm, dtype=jnp.int32)
    pe = tile_e[p // tm]
    off = p - pstart[pe]
    valid = off < counts[pe]
    src = jnp.where(valid, order[jnp.minimum(start[pe] + off, a - 1)] // TOP_K, 0).astype(jnp.int32)
    n_used = tile_end[-1:].astype(jnp.int32)
    return src, pos.astype(jnp.int32), tile_e, n_used


def _final_kernel(p0c_ref, p1c_ref, p0n_ref, p1n_ref, base_ref, route_ref, g_ref, b_ref, y_hbm, o_ref, ybuf, sem):
    i = pl.program_id(0)
    n = pl.num_programs(0)
    tm = FINAL_TILE
    slot = i % 2

    def issue(pa_ref, pb_ref, s):
        _issue_row_gather(pa_ref, y_hbm, ybuf.at[s], sem.at[s], tm, 0)
        _issue_row_gather(pb_ref, y_hbm, ybuf.at[s], sem.at[s], tm, tm)

    @pl.when(i == 0)
    def _():
        issue(p0c_ref, p1c_ref, 0)

    @pl.when(i + 1 < n)
    def _():
        issue(p0n_ref, p1n_ref, 1 - slot)

    _wait_row_gather(y_hbm, ybuf.at[slot], sem.at[slot])
    w = route_ref[...]
    h = base_ref[...] + w[:, 2:3] * ybuf[slot, 0:tm, :] + w[:, 3:4] * ybuf[slot, tm:2 * tm, :]
    o_ref[...] = _layernorm(h, g_ref[...], b_ref[...])


def _final(base, y, pos, route, g, b):
    t, d = base.shape
    tm = FINAL_TILE
    nb = t // tm
    row = lambda i: (i, 0)
    first, nxt = _idx_specs(tm, nb)
    pos2 = pos.reshape(t, TOP_K)
    p0 = pos2[:, 0].reshape(nb, 1, tm)
    p1 = pos2[:, 1].reshape(nb, 1, tm)
    return pl.pallas_call(
        _final_kernel,
        grid=(nb,),
        in_specs=[first, first, nxt, nxt,
                  pl.BlockSpec((tm, d), row), pl.BlockSpec((tm, LANES), row),
                  _const_spec((1, d)), _const_spec((1, d)),
                  pl.BlockSpec(memory_space=pl.ANY)],
        out_specs=pl.BlockSpec((tm, d), row),
        out_shape=jax.ShapeDtypeStruct((t, d), F32),
        scratch_shapes=[pltpu.VMEM((2, 2 * tm, d), F32), pltpu.SemaphoreType.DMA((2,))],
        compiler_params=_cparams(("arbitrary",)),
        name="final_ln",
    )(p0, p1, p0, p1, base, route, g, b, y)


def _rot_cols(w):
    half = ROPE_DIM // 2
    return jnp.concatenate([-w[..., half:], w[..., :half]], axis=-1)


def _layer_weights(i, w_in, q_norm, w_uq, kv_norm, w_uk, w_uv, hg_norm, w_br_a, w_br_b, w_out, ln1_g, ln1_b,
                   w_rg, w_re, w_e_gate, w_e_up, w_e_down, w_ple, w_ple_gate, ln2_g, ln2_b):
    d = D_MODEL
    nh = MLA_HEADS
    win = w_in[i]
    c_kr = Q_LORA + KV_LORA
    c_f = c_kr + ROPE_DIM
    c_ga = c_f + N_HG
    w_kr = win[:, c_kr:c_f]
    zpad = jnp.zeros((d, LANES - ROPE_DIM), F32)
    wa = jnp.concatenate([win[:, 0:c_kr], win[:, c_f:c_ga], w_kr, zpad, _rot_cols(w_kr), zpad], axis=1)

    wq3 = w_uq[i].reshape(Q_LORA, nh, NOPE_DIM + ROPE_DIM)
    zq = jnp.zeros((Q_LORA, nh, HEAD_PAD - NOPE_DIM - ROPE_DIM), F32)
    wq = jnp.concatenate([wq3, zq], axis=-1).reshape(Q_LORA, QK_PAD)
    wq_rot = jnp.concatenate([jnp.zeros((Q_LORA, nh, NOPE_DIM), F32), _rot_cols(wq3[..., NOPE_DIM:]), zq],
                             axis=-1).reshape(Q_LORA, QK_PAD)

    wuk = w_uk[i]
    wuv = w_uv[i]
    k_nope = jnp.concatenate([wuk, jnp.zeros((KV_LORA, nh, HEAD_PAD - NOPE_DIM), F32)], axis=-1)
    eye = jnp.eye(ROPE_DIM, dtype=F32)
    k_rope = jnp.concatenate([jnp.zeros((ROPE_DIM, NOPE_DIM), F32), eye,
                              jnp.zeros((ROPE_DIM, HEAD_PAD - NOPE_DIM - ROPE_DIM), F32)], axis=-1)
    k_rope = jnp.concatenate([jnp.tile(k_rope[:, None, :], (1, nh, 1)).reshape(ROPE_DIM, QK_PAD),
                              jnp.zeros((LANES - ROPE_DIM, QK_PAD), F32)], axis=0)
    zv = jnp.zeros((KV_LORA, nh // 2, V_DIM), F32)
    wuv_a3 = jnp.concatenate([wuv[:, 0::2, :], zv], axis=-1)
    wuv_b3 = jnp.concatenate([zv, wuv[:, 1::2, :]], axis=-1)
    wkv_top = jnp.concatenate([k_nope.reshape(KV_LORA, QK_PAD), wuv_a3.reshape(KV_LORA, 512),
                               wuv_b3.reshape(KV_LORA, 512)], axis=1)
    wkv_bot = jnp.concatenate([k_rope, jnp.zeros((LANES, QK_PAD), F32)], axis=1)
    wkv = jnp.concatenate([wkv_top, wkv_bot], axis=0)

    wuk_t = jnp.concatenate([jnp.transpose(wuk, (1, 2, 0)),
                             jnp.zeros((nh, HEAD_PAD - NOPE_DIM, KV_LORA), F32)], axis=1)
    p_sel = jnp.concatenate([jnp.zeros((NOPE_DIM, ROPE_DIM), F32), eye,
                             jnp.zeros((HEAD_PAD - NOPE_DIM - ROPE_DIM, ROPE_DIM), F32)], axis=0)

    w_route = jnp.concatenate([w_re[i], w_rg[i], jnp.zeros((d, LANES - N_EXPERTS - N_GROUPS), F32)], axis=1)
    bf = lambda w: w.astype(BF16)
    return {
        "wa": bf(wa), "q_norm": q_norm[i].reshape(1, -1), "kv_norm": kv_norm[i].reshape(1, -1),
        "wq": bf(wq), "wq_rot": bf(wq_rot), "wkv": bf(wkv),
        "wuk_t": bf(wuk_t), "p_sel": bf(p_sel),
        "wuv_a3": bf(jnp.transpose(wuv_a3, (1, 0, 2))), "wuv_b3": bf(jnp.transpose(wuv_b3, (1, 0, 2))),
        "hg_norm": hg_norm[i].reshape(1, -1),
        "w_gate": bf(win[:, c_ga:c_ga + 2 * d]), "w_br_a": bf(w_br_a[i]), "w_br_b": bf(w_br_b[i]),
        "w_out": bf(w_out[i]), "ln1_g": ln1_g[i].reshape(1, -1), "ln1_b": ln1_b[i].reshape(1, -1),
        "w_route": bf(w_route), "w_ple_gate": bf(w_ple_gate[i]), "w_ple": bf(w_ple[i]),
        "w1": bf(w_e_gate[i]), "w3": bf(w_e_up[i]), "w2": bf(w_e_down[i]),
        "ln2_g": ln2_g[i].reshape(1, -1), "ln2_b": ln2_b[i].reshape(1, -1),
    }


def _rope_table(pos, rows):
    half = ROPE_DIM // 2
    inv = ROPE_THETA ** (-jnp.arange(half, dtype=F32) / half)
    ang = pos.astype(F32)[:, None] * inv[None, :]
    c2 = jnp.concatenate([jnp.cos(ang), jnp.cos(ang)], axis=1)
    s2 = jnp.concatenate([jnp.sin(ang), jnp.sin(ang)], axis=1)
    n = pos.shape[0]
    ones = jnp.ones((n, NOPE_DIM), F32)
    z = lambda w: jnp.zeros((n, w), F32)
    tab = jnp.concatenate([
        ones, c2, z(HEAD_PAD - NOPE_DIM - ROPE_DIM),
        z(NOPE_DIM), s2, z(HEAD_PAD - NOPE_DIM - ROPE_DIM),
        c2, z(LANES - ROPE_DIM),
        s2, z(LANES - ROPE_DIM)], axis=1)
    if rows > n:
        tab = jnp.tile(tab, (rows // n, 1))
    return tab


def _trunk_layer(i, xn, ple, tab, lw, hg_lb, batch, ln, cache, alpha):
    q, k, va, vb, ckv, kr, f_pre, q_pre, hv, g_pre = _proj(xn, tab, lw)
    if cache is None:
        a = _attn_prompt(q, k, va, vb, batch, ln)
        o, st = _hgrn(f_pre, q_pre, hv, g_pre, hg_lb, lw["hg_norm"], None, i, batch, ln)
    else:
        a = _attn_sample(q, ckv, kr, cache[0], cache[1], lw, batch, ln)
        o, st = _hgrn(f_pre, q_pre, hv, g_pre, hg_lb, lw["hg_norm"], cache[2], i, batch, ln)
    x1, base, route = _post(xn, a, o, ple, lw, alpha)

    t = xn.shape[0]
    n_tiles = (t * TOP_K) // MOE_TILE + N_EXPERTS
    src, pos, tile_e, n_used = _dispatch_plan(route, n_tiles)
    y = _moe(x1, src, tile_e, n_used, lw["w1"], lw["w3"], lw["w2"])
    x2 = _final(base, y, pos, route, lw["ln2_g"], lw["ln2_b"])
    return x2, ckv, kr, st


def kernel(x_prompt, x_sample, p_prompt, p_sample, cache_ckv, cache_krope, state_hgrn, ln0_g, ln0_b, w_in, q_norm, w_uq, kv_norm, w_uk, w_uv, hg_lb, hg_norm, w_br_a, w_br_b, w_out, ln1_g, ln1_b, w_router_group, w_router_expert, w_e_gate, w_e_up, w_e_down, w_ple, w_ple_gate, ln2_g, ln2_b):
    depth = w_in.shape[0]
    bp, sp, d = x_prompt.shape
    bs, ss, _ = x_sample.shape
    past = cache_ckv.shape[2]
    alpha = (2 * depth) ** 0.25

    xp = _ln_rows(x_prompt.reshape(bp * sp, d), ln0_g, ln0_b)
    xs = _ln_rows(x_sample.reshape(bs * ss, d), ln0_g, ln0_b)
    tab_p = _rope_table(jnp.arange(sp, dtype=jnp.int32), max(sp, TOKEN_TILE))
    tab_s = _rope_table(past + jnp.arange(ss, dtype=jnp.int32), max(ss, TOKEN_TILE))

    outs = [[] for _ in range(6)]
    for i in range(depth):
        lw = _layer_weights(i, w_in, q_norm, w_uq, kv_norm, w_uk, w_uv, hg_norm, w_br_a, w_br_b, w_out,
                            ln1_g, ln1_b, w_router_group, w_router_expert, w_e_gate, w_e_up, w_e_down,
                            w_ple, w_ple_gate, ln2_g, ln2_b)
        xp, c, r, s = _trunk_layer(i, xp, p_prompt[i].reshape(bp * sp, -1), tab_p, lw, hg_lb, bp, sp, None, alpha)
        outs[0].append(c.reshape(bp, sp, -1))
        outs[1].append(r.reshape(bp, sp, -1))
        outs[2].append(s)
        cache = (cache_ckv[i], cache_krope[i], state_hgrn[i])
        xs, c, r, s = _trunk_layer(i, xs, p_sample[i].reshape(bs * ss, -1), tab_s, lw, hg_lb, bs, ss, cache, alpha)
        outs[3].append(c.reshape(bs, ss, -1))
        outs[4].append(r.reshape(bs, ss, -1))
        outs[5].append(s)
    return (xp.reshape(bp, sp, d), xs.reshape(bs, ss, d)) + tuple(jnp.stack(o) for o in outs)
```

```python
import functools

import numpy as np
import jax
import jax.numpy as jnp
from jax import lax
from jax.experimental import pallas as pl
from jax.experimental.pallas import tpu as pltpu

F32 = jnp.float32
BF16 = jnp.bfloat16

D_MODEL = 1024
CHUNK = 64
PLE_DIM = 256
MLA_HEADS = 8
NOPE_DIM = 64
ROPE_DIM = 32
V_DIM = 64
Q_LORA = 384
KV_LORA = 256
ROPE_THETA = 10000.0
ATTN_SCALE = (NOPE_DIM + ROPE_DIM) ** -0.5
HG_HEADS = 4
HG_DK = 128
HG_DV = 128
HG_FDIM = HG_HEADS * HG_DK
HG_IDIM = HG_HEADS * HG_DV
N_GROUPS = 4
EXPERTS_PER_GROUP = 8
N_EXPERTS = N_GROUPS * EXPERTS_PER_GROUP
TOP_K = 2
D_EXPERT = 512
EPS = 1e-6

LANES = 128
HEAD_PAD = LANES
QK_PAD = MLA_HEADS * HEAD_PAD
VMEM_LIMIT = 56 * 1024 * 1024

TOKEN_TILE = 512
ATTN_TILE = 256
MOE_TILE = 256
FINAL_TILE = 256
HG_SUB = 16


def _cparams(sem):
    return pltpu.CompilerParams(dimension_semantics=sem, vmem_limit_bytes=VMEM_LIMIT)


def _const_spec(shape):
    nd = len(shape)
    return pl.BlockSpec(shape, lambda *_: (0,) * nd, pipeline_mode=pl.Buffered(1))


def _dot(a, b):
    return jnp.dot(a, b, preferred_element_type=F32)


def _dot_nt(a, b):
    return lax.dot_general(a, b, (((1,), (1,)), ((), ())), preferred_element_type=F32)


def _dot_tn(a, b):
    return lax.dot_general(a, b, (((0,), (0,)), ((), ())), preferred_element_type=F32)


def _div_pow2(x, d):
    return jnp.right_shift(x, int(d).bit_length() - 1)


def _sigmoid(x):
    return 1.0 / (1.0 + jnp.exp(-x))


def _layernorm(x, g, b):
    mu = jnp.mean(x, axis=-1, keepdims=True)
    xc = x - mu
    var = jnp.mean(xc * xc, axis=-1, keepdims=True)
    return xc * lax.rsqrt(var + EPS) * g + b


def _rmsnorm(x, g):
    return x * lax.rsqrt(jnp.mean(x * x, axis=-1, keepdims=True) + EPS) * g


def _ln_kernel(x_ref, g_ref, b_ref, o_ref):
    o_ref[...] = _layernorm(x_ref[...], g_ref[...], b_ref[...])


def _ln_rows(x, g, b):
    t, d = x.shape
    tm = TOKEN_TILE
    return pl.pallas_call(
        _ln_kernel,
        grid=(t // tm,),
        in_specs=[pl.BlockSpec((tm, d), lambda i: (i, 0)), _const_spec((1, d)), _const_spec((1, d))],
        out_specs=pl.BlockSpec((tm, d), lambda i: (i, 0)),
        out_shape=jax.ShapeDtypeStruct((t, d), F32),
        compiler_params=_cparams(("parallel",)),
        name="ln0",
    )(x, g.reshape(1, d), b.reshape(1, d))


N_LAT = Q_LORA + KV_LORA
N_HG = 2 * HG_FDIM + 2 * HG_IDIM
COL_HG = N_LAT
COL_KR = N_LAT + N_HG
N_PROJ = COL_KR + 2 * LANES


def _proj_kernel(x_ref, tab_ref, wa_ref, qn_ref, kvn_ref, wq_ref, wqr_ref, wkv_ref,
                 q_ref, k_ref, va_ref, vb_ref, ckv_ref, kr_ref, f_ref, hq_ref, hv_ref, hg_ref):
    xb = x_ref[...].astype(BF16)
    cq_tab = tab_ref[:, 0:LANES]
    sq_tab = tab_ref[:, LANES:2 * LANES]
    ck_tab = tab_ref[:, 2 * LANES:3 * LANES]
    sk_tab = tab_ref[:, 3 * LANES:4 * LANES]

    lat = _dot(xb, wa_ref[:, 0:N_LAT])
    cqn = _rmsnorm(lat[:, 0:Q_LORA], qn_ref[...]).astype(BF16)
    ckvn = _rmsnorm(lat[:, Q_LORA:N_LAT], kvn_ref[...])
    ckv_ref[...] = ckvn

    qa = _dot(cqn, wq_ref[...])
    qb = _dot(cqn, wqr_ref[...])
    for h in range(MLA_HEADS):
        sl = slice(h * HEAD_PAD, (h + 1) * HEAD_PAD)
        q_ref[:, sl] = (qa[:, sl] * cq_tab + qb[:, sl] * sq_tab).astype(BF16)

    kr2 = _dot(xb, wa_ref[:, COL_KR:N_PROJ])
    kr = kr2[:, 0:LANES] * ck_tab + kr2[:, LANES:2 * LANES] * sk_tab
    kr_ref[...] = kr[:, 0:ROPE_DIM]

    kvin = jnp.concatenate([ckvn.astype(BF16), kr.astype(BF16)], axis=1)
    kv = _dot(kvin, wkv_ref[...])
    k_ref[...] = kv[:, 0:QK_PAD].astype(BF16)
    va_ref[...] = kv[:, QK_PAD:QK_PAD + 512].astype(BF16)
    vb_ref[...] = kv[:, QK_PAD + 512:QK_PAD + 1024].astype(BF16)

    f_ref[...] = _dot(xb, wa_ref[:, COL_HG:COL_HG + 512])
    hq_ref[...] = _dot(xb, wa_ref[:, COL_HG + 512:COL_HG + 1024])
    hv_ref[...] = _dot(xb, wa_ref[:, COL_HG + 1024:COL_HG + 1536]).astype(BF16)
    hg_ref[...] = _dot(xb, wa_ref[:, COL_HG + 1536:COL_HG + 2048])


def _proj(xn, tab, lw):
    t = xn.shape[0]
    tm = TOKEN_TILE
    ntab = tab.shape[0] // tm
    row = lambda i: (i, 0)
    outs = [
        (QK_PAD, BF16), (QK_PAD, BF16), (512, BF16), (512, BF16), (KV_LORA, F32), (ROPE_DIM, F32),
        (HG_FDIM, F32), (HG_FDIM, F32), (HG_IDIM, BF16), (HG_IDIM, F32),
    ]
    return pl.pallas_call(
        _proj_kernel,
        grid=(t // tm,),
        in_specs=[
            pl.BlockSpec((tm, D_MODEL), row),
            pl.BlockSpec((tm, 4 * LANES), lambda i: (i % ntab, 0)),
            _const_spec((D_MODEL, N_PROJ)),
            _const_spec((1, Q_LORA)), _const_spec((1, KV_LORA)),
            _const_spec((Q_LORA, QK_PAD)), _const_spec((Q_LORA, QK_PAD)),
            _const_spec((KV_LORA + LANES, 2 * QK_PAD)),
        ],
        out_specs=[pl.BlockSpec((tm, w), row) for w, _ in outs],
        out_shape=[jax.ShapeDtypeStruct((t, w), dt) for w, dt in outs],
        compiler_params=_cparams(("parallel",)),
        name="proj",
    )(xn, tab, lw["wa"], lw["q_norm"], lw["kv_norm"], lw["wq"], lw["wq_rot"], lw["wkv"])


LOG2E = 1.4426950408889634


def _attn_prompt_kernel(q_ref, k_ref, va_ref, vb_ref, o_ref, s_ref, m_ref, l_ref, acc_ref):
    tq = ATTN_TILE
    nh = MLA_HEADS
    qi = pl.program_id(1)
    rq = _div_pow2(lax.broadcasted_iota(jnp.int32, (tq, tq), 0), CHUNK)
    ck = _div_pow2(lax.broadcasted_iota(jnp.int32, (tq, tq), 1), CHUNK)
    diag_mask = ck <= rq
    c = ATTN_SCALE * LOG2E

    m_ref[...] = jnp.full(m_ref.shape, -jnp.inf, F32)
    l_ref[...] = jnp.zeros(l_ref.shape, F32)
    acc_ref[...] = jnp.zeros(acc_ref.shape, F32)

    def scores(kt, mask):
        ks = pl.multiple_of(kt * tq, tq)
        for h in range(nh):
            hs = slice(h * HEAD_PAD, (h + 1) * HEAD_PAD)
            s = _dot_nt(q_ref[:, hs], k_ref[pl.ds(ks, tq), hs]) * c
            if mask is not None:
                s = jnp.where(mask, s, -jnp.inf)
            s_ref[h, :, pl.ds(ks, tq)] = s
            m_ref[h] = jnp.maximum(m_ref[h], jnp.maximum(s[:, 0:LANES], s[:, LANES:2 * LANES]))

    def p1(kt, carry):
        scores(kt, None)
        return carry

    lax.fori_loop(0, qi, p1, 0)
    scores(qi, diag_mask)
    for h in range(nh):
        m_ref[h] = jnp.broadcast_to(jnp.max(m_ref[h], axis=-1, keepdims=True), (tq, LANES))

    def p2(kt, carry):
        ks = pl.multiple_of(kt * tq, tq)
        for h in range(nh):
            v_ref = va_ref if h % 2 == 0 else vb_ref
            pair = h // 2
            mb = m_ref[h]
            p_lo = jnp.exp2(s_ref[h, :, pl.ds(ks, LANES)] - mb)
            p_hi = jnp.exp2(s_ref[h, :, pl.ds(ks + LANES, LANES)] - mb)
            l_ref[h] += p_lo + p_hi
            p = jnp.concatenate([p_lo, p_hi], axis=1).astype(BF16)
            acc_ref[h] += _dot(p, v_ref[pl.ds(ks, tq), pair * LANES:(pair + 1) * LANES])
        return carry

    lax.fori_loop(0, qi + 1, p2, 0)
    for pair in range(nh // 2):
        he, ho = 2 * pair, 2 * pair + 1
        o = (acc_ref[he] / jnp.sum(l_ref[he], axis=-1, keepdims=True)
             + acc_ref[ho] / jnp.sum(l_ref[ho], axis=-1, keepdims=True))
        o_ref[:, pair * LANES:(pair + 1) * LANES] = o.astype(BF16)


def _attn_prompt(q, k, va, vb, batch, seq):
    tq = ATTN_TILE
    nq = seq // tq
    nh = MLA_HEADS
    return pl.pallas_call(
        _attn_prompt_kernel,
        grid=(batch, nq),
        in_specs=[
            pl.BlockSpec((tq, QK_PAD), lambda b, i: (b * nq + i, 0)),
            pl.BlockSpec((seq, QK_PAD), lambda b, i: (b, 0)),
            pl.BlockSpec((seq, 512), lambda b, i: (b, 0)),
            pl.BlockSpec((seq, 512), lambda b, i: (b, 0)),
        ],
        out_specs=pl.BlockSpec((tq, 512), lambda b, i: (b * nq + i, 0)),
        out_shape=jax.ShapeDtypeStruct((batch * seq, 512), BF16),
        scratch_shapes=[pltpu.VMEM((nh, tq, seq), F32), pltpu.VMEM((nh, tq, LANES), F32),
                        pltpu.VMEM((nh, tq, LANES), F32), pltpu.VMEM((nh, tq, LANES), F32)],
        compiler_params=_cparams(("parallel", "arbitrary")),
        name="attn_prompt",
    )(q, k, va, vb)


SAMPLE_KEY_TILE = 1024


def _attn_sample_kernel(q_ref, ckvn_ref, krn_ref, cc_ref, ck_ref, wukt_ref, psel_ref, wva_ref, wvb_ref,
                        o_ref, s_ref):
    ln = q_ref.shape[0]
    past = cc_ref.shape[0]
    nh = MLA_HEADS
    qlat, qrope = [], []
    for h in range(nh):
        qh = q_ref[:, h * HEAD_PAD:(h + 1) * HEAD_PAD]
        qlat.append(_dot(qh, wukt_ref[h]).astype(BF16))
        qrope.append(_dot(qh, psel_ref[...]).astype(BF16))
    qlat = jnp.concatenate(qlat, axis=0)
    qrope = jnp.concatenate(qrope, axis=0)

    ckv_new = ckvn_ref[...].astype(BF16)
    kr_new = krn_ref[...].astype(BF16)
    s_new = (_dot_nt(qlat, ckv_new) + _dot_nt(qrope, kr_new)) * ATTN_SCALE
    m = jnp.max(s_new, axis=-1, keepdims=True)
    tk = SAMPLE_KEY_TILE
    for kt in range(past // tk):
        rows = slice(kt * tk, (kt + 1) * tk)
        s = (_dot_nt(qlat, cc_ref[rows, :].astype(BF16)) + _dot_nt(qrope, ck_ref[rows, :].astype(BF16))) * ATTN_SCALE
        s_ref[:, rows] = s
        m = jnp.maximum(m, jnp.max(s, axis=-1, keepdims=True))
    p_new = jnp.exp(s_new - m)
    l = jnp.sum(p_new, axis=-1, keepdims=True)
    lat = _dot(p_new.astype(BF16), ckv_new)
    for kt in range(past // tk):
        rows = slice(kt * tk, (kt + 1) * tk)
        p = jnp.exp(s_ref[:, rows] - m)
        l = l + jnp.sum(p, axis=-1, keepdims=True)
        lat = lat + _dot(p.astype(BF16), cc_ref[rows, :].astype(BF16))
    lat = (lat / l).astype(BF16)
    for pair in range(nh // 2):
        he, ho = 2 * pair, 2 * pair + 1
        o = _dot(lat[he * ln:(he + 1) * ln], wva_ref[pair]) + _dot(lat[ho * ln:(ho + 1) * ln], wvb_ref[pair])
        o_ref[:, pair * LANES:(pair + 1) * LANES] = o.astype(BF16)


def _attn_sample(q, ckv_new, kr_new, cache_ckv, cache_kr, lw, batch, ln):
    past = cache_ckv.shape[1]
    nh = MLA_HEADS
    return pl.pallas_call(
        _attn_sample_kernel,
        grid=(batch,),
        in_specs=[
            pl.BlockSpec((ln, QK_PAD), lambda b: (b, 0)),
            pl.BlockSpec((ln, KV_LORA), lambda b: (b, 0)),
            pl.BlockSpec((ln, ROPE_DIM), lambda b: (b, 0)),
            pl.BlockSpec((None, past, KV_LORA), lambda b: (b, 0, 0)),
            pl.BlockSpec((None, past, ROPE_DIM), lambda b: (b, 0, 0)),
            _const_spec((nh, HEAD_PAD, KV_LORA)),
            _const_spec((HEAD_PAD, ROPE_DIM)),
            _const_spec((nh // 2, KV_LORA, LANES)),
            _const_spec((nh // 2, KV_LORA, LANES)),
        ],
        out_specs=pl.BlockSpec((ln, 512), lambda b: (b, 0)),
        out_shape=jax.ShapeDtypeStruct((batch * ln, 512), BF16),
        scratch_shapes=[pltpu.VMEM((nh * ln, past), F32)],
        compiler_params=_cparams(("parallel",)),
        name="attn_sample",
    )(q, ckv_new, kr_new, cache_ckv, cache_kr, lw["wuk_t"], lw["p_sel"], lw["wuv_a3"], lw["wuv_b3"])


def _hgrn_kernel(*refs, layer, chunk, has_init):
    if has_init:
        f_ref, q_ref, v_ref, g_ref, lb_ref, nrm_ref, s0_ref, o_ref, sout_ref, st_ref = refs
    else:
        f_ref, q_ref, v_ref, g_ref, lb_ref, nrm_ref, o_ref, sout_ref, st_ref = refs
        s0_ref = None
    tt = f_ref.shape[0]
    c = chunk
    ti = pl.program_id(1)

    @pl.when(ti == 0)
    def _():
        for h in range(HG_HEADS):
            if has_init:
                st_ref[h] = s0_ref[h].T
            else:
                st_ref[h] = jnp.zeros((HG_DV, HG_DK), F32)

    lbp = lb_ref[...]
    e = jnp.exp(lbp - jnp.max(lbp, axis=0, keepdims=True))
    tot = jnp.sum(e, axis=0, keepdims=True)
    part = jnp.zeros_like(tot)
    for j in range(1, layer + 1):
        part = part + e[j:j + 1]
    lb = part / tot
    log_lb = jnp.log(lb)
    log_1m = jnp.log1p(-lb)
    one_m = 1.0 - lb

    rid = lax.broadcasted_iota(jnp.int32, (c, 1), 0)
    rr = lax.broadcasted_iota(jnp.int32, (c, c), 0)
    cc = lax.broadcasted_iota(jnp.int32, (c, c), 1)
    tri = (cc <= rr).astype(BF16)
    base_mask = jnp.logical_and(_div_pow2(rr, HG_SUB) == _div_pow2(cc, HG_SUB), cc <= rr)

    for ci in range(tt // c):
        rows = slice(ci * c, (ci + 1) * c)
        z = f_ref[rows, :]
        ez = jnp.exp(-jnp.abs(z))
        rz = 1.0 / (1.0 + ez)
        log_sig = jnp.minimum(z, 0.0) - jnp.log1p(ez)
        kk = one_m * jnp.where(z >= 0, ez * rz, rz)
        y = log_1m + log_sig
        logf = jnp.maximum(log_lb, y) + jnp.log1p(jnp.exp(-jnp.abs(log_lb - y)))
        qp = q_ref[rows, :]
        qq = qp * _sigmoid(qp)

        hi = logf.astype(BF16)
        lo = (logf - hi.astype(F32)).astype(BF16)
        b = _dot(tri, hi) + _dot(tri, lo)
        bend = b[c - 1:c, :]

        qs, ks, masks = [], [], []
        g = c
        while g > HG_SUB:
            half = g // 2
            bref = b[half - 1:half, :]
            for gi in range(1, c // g):
                bref = jnp.where(rid >= gi * g, b[gi * g + half - 1:gi * g + half, :], bref)
            x = jnp.exp(-jnp.abs(b - bref))
            right = jnp.bitwise_and(rid, g - 1) >= half
            qs.append(jnp.where(right, qq * x, 0.0).astype(BF16))
            ks.append(jnp.where(right, 0.0, kk * x).astype(BF16))
            masks.append(None if g == c else (_div_pow2(rr, g) == _div_pow2(cc, g)))
            g = half
        bst = jnp.zeros_like(bend)
        for gi in range(1, c // HG_SUB):
            bst = jnp.where(rid >= gi * HG_SUB, b[gi * HG_SUB - 1:gi * HG_SUB, :], bst)
        e0 = b - bst
        qs.append((qq * jnp.exp(e0)).astype(BF16))
        ks.append((kk * jnp.exp(-e0)).astype(BF16))
        masks.append(base_mask)

        q_in = (qq * jnp.exp(b)).astype(BF16)
        k_out = (kk * jnp.exp(bend - b)).astype(BF16)
        dec = jnp.exp(bend)
        gp = g_ref[rows, :]
        gate = gp * _sigmoid(gp) * nrm_ref[...]

        for h in range(HG_HEADS):
            hs = slice(h * HG_DK, (h + 1) * HG_DK)
            a = None
            for ql, kl, mk in zip(qs, ks, masks):
                al = _dot_nt(ql[:, hs], kl[:, hs])
                if mk is not None:
                    al = jnp.where(mk, al, 0.0)
                a = al if a is None else a + al
            vh = v_ref[rows, hs]
            st = st_ref[h]
            o = _dot_nt(q_in[:, hs], st.astype(BF16)) + _dot(a.astype(BF16), vh)
            o = o * lax.rsqrt(jnp.mean(o * o, axis=-1, keepdims=True) + EPS) * gate[:, hs]
            o_ref[rows, hs] = o.astype(BF16)
            st_ref[h] = st * dec[:, hs] + _dot_tn(vh, k_out[:, hs])

    @pl.when(ti == pl.num_programs(1) - 1)
    def _():
        for h in range(HG_HEADS):
            sout_ref[h] = st_ref[h].T


def _hgrn(f_pre, q_pre, v, g_pre, hg_lb, hg_norm, s0, layer, batch, ln):
    tt = min(ln, 256)
    chunk = min(tt, CHUNK)
    nt = ln // tt
    has_init = s0 is not None
    depth = hg_lb.shape[0]
    row = lambda b, t: (b * nt + t, 0)
    st_spec = pl.BlockSpec((None, HG_HEADS, HG_DK, HG_DV), lambda b, t: (b, 0, 0, 0))
    in_specs = [
        pl.BlockSpec((tt, HG_FDIM), row), pl.BlockSpec((tt, HG_FDIM), row),
        pl.BlockSpec((tt, HG_IDIM), row), pl.BlockSpec((tt, HG_IDIM), row),
        _const_spec((depth, HG_FDIM)), _const_spec((1, HG_IDIM)),
    ]
    args = [f_pre, q_pre, v, g_pre, hg_lb, hg_norm]
    if has_init:
        in_specs.append(st_spec)
        args.append(s0)
    return pl.pallas_call(
        functools.partial(_hgrn_kernel, layer=layer, chunk=chunk, has_init=has_init),
        grid=(batch, nt),
        in_specs=in_specs,
        out_specs=[pl.BlockSpec((tt, HG_IDIM), row), st_spec],
        out_shape=[jax.ShapeDtypeStruct((batch * ln, HG_IDIM), BF16),
                   jax.ShapeDtypeStruct((batch, HG_HEADS, HG_DK, HG_DV), F32)],
        scratch_shapes=[pltpu.VMEM((HG_HEADS, HG_DV, HG_DK), F32)],
        compiler_params=_cparams(("parallel", "arbitrary")),
        name="hgrn",
    )(*args)


ROUTE_GROUP_LANE = N_EXPERTS


def _route(logits):
    lane_i = lax.broadcasted_iota(jnp.int32, logits.shape, 1)
    lane = lane_i.astype(F32)
    big = float(LANES)
    neg = -jnp.inf
    is_g = jnp.logical_and(lane_i >= ROUTE_GROUP_LANE, lane_i < ROUTE_GROUP_LANE + N_GROUPS)
    gl = jnp.where(is_g, logits, neg)
    gmax = jnp.max(gl, axis=-1, keepdims=True)
    gidx = jnp.min(jnp.where(gl == gmax, lane - ROUTE_GROUP_LANE, big), axis=-1, keepdims=True)
    g_top = 1.0 / jnp.sum(jnp.exp(gl - gmax), axis=-1, keepdims=True)
    lane_group = _div_pow2(lane_i, EXPERTS_PER_GROUP).astype(F32)
    in_group = jnp.logical_and(lane_i < N_EXPERTS, lane_group == gidx)
    el = jnp.where(in_group, logits, neg)
    m1 = jnp.max(el, axis=-1, keepdims=True)
    i1 = jnp.min(jnp.where(el == m1, lane, big), axis=-1, keepdims=True)
    el2 = jnp.where(lane == i1, neg, el)
    m2 = jnp.max(el2, axis=-1, keepdims=True)
    i2 = jnp.min(jnp.where(el2 == m2, lane, big), axis=-1, keepdims=True)
    r = jnp.exp(m2 - m1)
    w1 = g_top / (1.0 + r)
    w2 = g_top * r / (1.0 + r)
    out = jnp.where(lane_i == 0, i1, 0.0)
    out = jnp.where(lane_i == 1, i2, out)
    out = jnp.where(lane_i == 2, w1, out)
    out = jnp.where(lane_i == 3, w2, out)
    return out


def _post_kernel(x_ref, a_ref, o_ref, ple_ref, wg_ref, wa_ref, wb_ref, wo_ref, g1_ref, b1_ref,
                 wr_ref, wpg_ref, wp_ref, x1_ref, base_ref, route_ref, *, alpha):
    x = x_ref[...]
    xb = x.astype(BF16)
    ya = _dot(a_ref[...], wa_ref[...])
    merged = _sigmoid(_dot(xb, wg_ref[:, 0:D_MODEL])) * ya
    yb = _dot(o_ref[...], wb_ref[...])
    merged = merged + _sigmoid(_dot(xb, wg_ref[:, D_MODEL:2 * D_MODEL])) * yb
    mix = _dot(merged.astype(BF16), wo_ref[...])
    x1 = _layernorm(alpha * x + mix, g1_ref[...], b1_ref[...])
    x1_ref[...] = x1
    x1b = x1.astype(BF16)
    route_ref[...] = _route(_dot(x1b, wr_ref[...]))
    ple = _dot(ple_ref[...].astype(BF16), wp_ref[...])
    base_ref[...] = alpha * x1 + _sigmoid(_dot(x1b, wpg_ref[...])) * ple


def _post(xn, a, o, ple, lw, alpha):
    t = xn.shape[0]
    tm = TOKEN_TILE
    row = lambda i: (i, 0)
    d = D_MODEL
    return pl.pallas_call(
        functools.partial(_post_kernel, alpha=alpha),
        grid=(t // tm,),
        in_specs=[
            pl.BlockSpec((tm, d), row), pl.BlockSpec((tm, 512), row), pl.BlockSpec((tm, HG_IDIM), row),
            pl.BlockSpec((tm, PLE_DIM), row),
            _const_spec((d, 2 * d)), _const_spec((512, d)), _const_spec((HG_IDIM, d)), _const_spec((d, d)),
            _const_spec((1, d)), _const_spec((1, d)),
            _const_spec((d, LANES)), _const_spec((d, d)), _const_spec((PLE_DIM, d)),
        ],
        out_specs=[pl.BlockSpec((tm, d), row), pl.BlockSpec((tm, d), row), pl.BlockSpec((tm, LANES), row)],
        out_shape=[jax.ShapeDtypeStruct((t, d), F32), jax.ShapeDtypeStruct((t, d), F32),
                   jax.ShapeDtypeStruct((t, LANES), F32)],
        compiler_params=_cparams(("parallel",)),
        name="post",
    )(xn, a, o, ple, lw["w_gate"], lw["w_br_a"], lw["w_br_b"], lw["w_out"], lw["ln1_g"], lw["ln1_b"],
      lw["w_route"], lw["w_ple_gate"], lw["w_ple"])


def _issue_row_gather(idx_ref, src_hbm, dst, sem, n_rows, dst_off=0):
    def body(j, carry):
        pltpu.make_async_copy(src_hbm.at[pl.ds(idx_ref[0, j], 1)], dst.at[pl.ds(dst_off + j, 1)], sem).start()
        return carry

    lax.fori_loop(0, n_rows, body, 0, unroll=8)


def _wait_row_gather(src_hbm, dst, sem):
    pltpu.make_async_copy(src_hbm.at[pl.ds(0, dst.shape[0])], dst, sem).wait()


def _idx_specs(tile, n_blocks):
    first = pl.BlockSpec((None, 1, tile), lambda i, *_: (0, 0, 0), memory_space=pltpu.SMEM)
    nxt = pl.BlockSpec((None, 1, tile), lambda i, *_: (jnp.minimum(i + 1, n_blocks - 1), 0, 0),
                       memory_space=pltpu.SMEM)
    return first, nxt


def _moe_kernel(te_ref, nu_ref, idx0_ref, idxn_ref, x_hbm, w1_ref, w3_ref, w2_ref, y_ref, xbuf, sem):
    i = pl.program_id(0)
    n_used = nu_ref[0]
    slot = i % 2

    @pl.when(i == 0)
    def _():
        _issue_row_gather(idx0_ref, x_hbm, xbuf.at[0], sem.at[0], MOE_TILE)

    @pl.when(i + 1 < n_used)
    def _():
        _issue_row_gather(idxn_ref, x_hbm, xbuf.at[1 - slot], sem.at[1 - slot], MOE_TILE)

    @pl.when(i < n_used)
    def _():
        _wait_row_gather(x_hbm, xbuf.at[slot], sem.at[slot])
        xb = xbuf[slot].astype(BF16)
        h1 = _dot(xb, w1_ref[...])
        h = h1 * _sigmoid(h1) * _dot(xb, w3_ref[...])
        y_ref[...] = _dot(h.astype(BF16), w2_ref[...])

    @pl.when(i >= n_used)
    def _():
        y_ref[...] = jnp.zeros_like(y_ref)


def _moe(x, src, tile_e, n_used, w1, w3, w2):
    tm = MOE_TILE
    n_tiles = src.shape[0] // tm
    d = D_MODEL
    first, nxt = _idx_specs(tm, n_tiles)
    grid_spec = pltpu.PrefetchScalarGridSpec(
        num_scalar_prefetch=2,
        grid=(n_tiles,),
        in_specs=[
            first, nxt,
            pl.BlockSpec(memory_space=pl.ANY),
            pl.BlockSpec((None, d, D_EXPERT), lambda i, te, nu: (te[i], 0, 0)),
            pl.BlockSpec((None, d, D_EXPERT), lambda i, te, nu: (te[i], 0, 0)),
            pl.BlockSpec((None, D_EXPERT, d), lambda i, te, nu: (te[i], 0, 0)),
        ],
        out_specs=pl.BlockSpec((tm, d), lambda i, te, nu: (i, 0)),
        scratch_shapes=[pltpu.VMEM((2, tm, d), F32), pltpu.SemaphoreType.DMA((2,))],
    )
    src3 = src.reshape(n_tiles, 1, tm)
    return pl.pallas_call(
        _moe_kernel,
        grid_spec=grid_spec,
        out_shape=jax.ShapeDtypeStruct((n_tiles * tm, d), F32),
        compiler_params=_cparams(("arbitrary",)),
        name="moe_experts",
    )(tile_e, n_used, src3, src3, x, w1, w3, w2)


def _dispatch_plan(route, n_tiles):
    t = route.shape[0]
    a = t * TOP_K
    tm = MOE_TILE
    flat_e = route[:, 0:TOP_K].astype(jnp.int32).reshape(a)
    ar = jnp.arange(a, dtype=jnp.int32)
    key = lax.sort(flat_e * a + ar)
    sorted_e = key // a
    order = key - sorted_e * a
    counts = jnp.sum((flat_e[:, None] == jnp.arange(N_EXPERTS, dtype=jnp.int32)[None, :]).astype(jnp.int32), axis=0)
    tiles_e = (counts + tm - 1) // tm
    tile_end = jnp.cumsum(tiles_e)
    pstart = (tile_end - tiles_e) * tm
    start = jnp.cumsum(counts) - counts
    pos_sorted = (pstart - start)[sorted_e] + ar
    _, pos = lax.sort_key_val(order, pos_sorted)
    tile_e = jnp.minimum(jnp.searchsorted(tile_end, jnp.arange(n_tiles, dtype=jnp.int32), side="right"),
                         N_EXPERTS - 1).astype(jnp.int32)
    p = jnp.arange(n_tiles * tm, dtype=jnp.int32)
    pe = tile_e[p // tm]
    off = p - pstart[pe]
    valid = off < counts[pe]
    src = jnp.where(valid, order[jnp.minimum(start[pe] + off, a - 1)] // TOP_K, 0).astype(jnp.int32)
    n_used = tile_end[-1:].astype(jnp.int32)
    return src, pos.astype(jnp.int32), tile_e, n_used


def _final_kernel(p0c_ref, p1c_ref, p0n_ref, p1n_ref, base_ref, route_ref, g_ref, b_ref, y_hbm, o_ref, ybuf, sem):
    i = pl.program_id(0)
    n = pl.num_programs(0)
    tm = FINAL_TILE
    slot = i % 2

    def issue(pa_ref, pb_ref, s):
        _issue_row_gather(pa_ref, y_hbm, ybuf.at[s], sem.at[s], tm, 0)
        _issue_row_gather(pb_ref, y_hbm, ybuf.at[s], sem.at[s], tm, tm)

    @pl.when(i == 0)
    def _():
        issue(p0c_ref, p1c_ref, 0)

    @pl.when(i + 1 < n)
    def _():
        issue(p0n_ref, p1n_ref, 1 - slot)

    _wait_row_gather(y_hbm, ybuf.at[slot], sem.at[slot])
    w = route_ref[...]
    h = base_ref[...] + w[:, 2:3] * ybuf[slot, 0:tm, :] + w[:, 3:4] * ybuf[slot, tm:2 * tm, :]
    o_ref[...] = _layernorm(h, g_ref[...], b_ref[...])


def _final(base, y, pos, route, g, b):
    t, d = base.shape
    tm = FINAL_TILE
    nb = t // tm
    row = lambda i: (i, 0)
    first, nxt = _idx_specs(tm, nb)
    pos2 = pos.reshape(t, TOP_K)
    p0 = pos2[:, 0].reshape(nb, 1, tm)
    p1 = pos2[:, 1].reshape(nb, 1, tm)
    return pl.pallas_call(
        _final_kernel,
        grid=(nb,),
        in_specs=[first, first, nxt, nxt,
                  pl.BlockSpec((tm, d), row), pl.BlockSpec((tm, LANES), row),
                  _const_spec((1, d)), _const_spec((1, d)),
                  pl.BlockSpec(memory_space=pl.ANY)],
        out_specs=pl.BlockSpec((tm, d), row),
        out_shape=jax.ShapeDtypeStruct((t, d), F32),
        scratch_shapes=[pltpu.VMEM((2, 2 * tm, d), F32), pltpu.SemaphoreType.DMA((2,))],
        compiler_params=_cparams(("arbitrary",)),
        name="final_ln",
    )(p0, p1, p0, p1, base, route, g, b, y)


def _rot_cols(w):
    half = ROPE_DIM // 2
    return jnp.concatenate([-w[..., half:], w[..., :half]], axis=-1)


def _layer_weights(i, w_in, q_norm, w_uq, kv_norm, w_uk, w_uv, hg_norm, w_br_a, w_br_b, w_out, ln1_g, ln1_b,
                   w_rg, w_re, w_e_gate, w_e_up, w_e_down, w_ple, w_ple_gate, ln2_g, ln2_b):
    d = D_MODEL
    nh = MLA_HEADS
    win = w_in[i]
    c_kr = Q_LORA + KV_LORA
    c_f = c_kr + ROPE_DIM
    c_ga = c_f + N_HG
    w_kr = win[:, c_kr:c_f]
    zpad = jnp.zeros((d, LANES - ROPE_DIM), F32)
    wa = jnp.concatenate([win[:, 0:c_kr], win[:, c_f:c_ga], w_kr, zpad, _rot_cols(w_kr), zpad], axis=1)

    wq3 = w_uq[i].reshape(Q_LORA, nh, NOPE_DIM + ROPE_DIM)
    zq = jnp.zeros((Q_LORA, nh, HEAD_PAD - NOPE_DIM - ROPE_DIM), F32)
    wq = jnp.concatenate([wq3, zq], axis=-1).reshape(Q_LORA, QK_PAD)
    wq_rot = jnp.concatenate([jnp.zeros((Q_LORA, nh, NOPE_DIM), F32), _rot_cols(wq3[..., NOPE_DIM:]), zq],
                             axis=-1).reshape(Q_LORA, QK_PAD)

    wuk = w_uk[i]
    wuv = w_uv[i]
    k_nope = jnp.concatenate([wuk, jnp.zeros((KV_LORA, nh, HEAD_PAD - NOPE_DIM), F32)], axis=-1)
    eye = jnp.eye(ROPE_DIM, dtype=F32)
    k_rope = jnp.concatenate([jnp.zeros((ROPE_DIM, NOPE_DIM), F32), eye,
                              jnp.zeros((ROPE_DIM, HEAD_PAD - NOPE_DIM - ROPE_DIM), F32)], axis=-1)
    k_rope = jnp.concatenate([jnp.tile(k_rope[:, None, :], (1, nh, 1)).reshape(ROPE_DIM, QK_PAD),
                              jnp.zeros((LANES - ROPE_DIM, QK_PAD), F32)], axis=0)
    zv = jnp.zeros((KV_LORA, nh // 2, V_DIM), F32)
    wuv_a3 = jnp.concatenate([wuv[:, 0::2, :], zv], axis=-1)
    wuv_b3 = jnp.concatenate([zv, wuv[:, 1::2, :]], axis=-1)
    wkv_top = jnp.concatenate([k_nope.reshape(KV_LORA, QK_PAD), wuv_a3.reshape(KV_LORA, 512),
                               wuv_b3.reshape(KV_LORA, 512)], axis=1)
    wkv_bot = jnp.concatenate([k_rope, jnp.zeros((LANES, QK_PAD), F32)], axis=1)
    wkv = jnp.concatenate([wkv_top, wkv_bot], axis=0)

    wuk_t = jnp.concatenate([jnp.transpose(wuk, (1, 2, 0)),
                             jnp.zeros((nh, HEAD_PAD - NOPE_DIM, KV_LORA), F32)], axis=1)
    p_sel = jnp.concatenate([jnp.zeros((NOPE_DIM, ROPE_DIM), F32), eye,
                             jnp.zeros((HEAD_PAD - NOPE_DIM - ROPE_DIM, ROPE_DIM), F32)], axis=0)

    w_route = jnp.concatenate([w_re[i], w_rg[i], jnp.zeros((d, LANES - N_EXPERTS - N_GROUPS), F32)], axis=1)
    bf = lambda w: w.astype(BF16)
    return {
        "wa": bf(wa), "q_norm": q_norm[i].reshape(1, -1), "kv_norm": kv_norm[i].reshape(1, -1),
        "wq": bf(wq), "wq_rot": bf(wq_rot), "wkv": bf(wkv),
        "wuk_t": bf(wuk_t), "p_sel": bf(p_sel),
        "wuv_a3": bf(jnp.transpose(wuv_a3, (1, 0, 2))), "wuv_b3": bf(jnp.transpose(wuv_b3, (1, 0, 2))),
        "hg_norm": hg_norm[i].reshape(1, -1),
        "w_gate": bf(win[:, c_ga:c_ga + 2 * d]), "w_br_a": bf(w_br_a[i]), "w_br_b": bf(w_br_b[i]),
        "w_out": bf(w_out[i]), "ln1_g": ln1_g[i].reshape(1, -1), "ln1_b": ln1_b[i].reshape(1, -1),
        "w_route": bf(w_route), "w_ple_gate": bf(w_ple_gate[i]), "w_ple": bf(w_ple[i]),
        "w1": bf(w_e_gate[i]), "w3": bf(w_e_up[i]), "w2": bf(w_e_down[i]),
        "ln2_g": ln2_g[i].reshape(1, -1), "ln2_b": ln2_b[i].reshape(1, -1),
    }


def _rope_table(pos, rows):
    half = ROPE_DIM // 2
    inv = ROPE_THETA ** (-jnp.arange(half, dtype=F32) / half)
    ang = pos.astype(F32)[:, None] * inv[None, :]
    c2 = jnp.concatenate([jnp.cos(ang), jnp.cos(ang)], axis=1)
    s2 = jnp.concatenate([jnp.sin(ang), jnp.sin(ang)], axis=1)
    n = pos.shape[0]
    ones = jnp.ones((n, NOPE_DIM), F32)
    z = lambda w: jnp.zeros((n, w), F32)
    tab = jnp.concatenate([
        ones, c2, z(HEAD_PAD - NOPE_DIM - ROPE_DIM),
        z(NOPE_DIM), s2, z(HEAD_PAD - NOPE_DIM - ROPE_DIM),
        c2, z(LANES - ROPE_DIM),
        s2, z(LANES - ROPE_DIM)], axis=1)
    if rows > n:
        tab = jnp.tile(tab, (rows // n, 1))
    return tab


def _trunk_layer(i, xn, ple, tab, lw, hg_lb, batch, ln, cache, alpha):
    q, k, va, vb, ckv, kr, f_pre, q_pre, hv, g_pre = _proj(xn, tab, lw)
    if cache is None:
        a = _attn_prompt(q, k, va, vb, batch, ln)
        o, st = _hgrn(f_pre, q_pre, hv, g_pre, hg_lb, lw["hg_norm"], None, i, batch, ln)
    else:
        a = _attn_sample(q, ckv, kr, cache[0], cache[1], lw, batch, ln)
        o, st = _hgrn(f_pre, q_pre, hv, g_pre, hg_lb, lw["hg_norm"], cache[2], i, batch, ln)
    x1, base, route = _post(xn, a, o, ple, lw, alpha)

    t = xn.shape[0]
    n_tiles = (t * TOP_K) // MOE_TILE + N_EXPERTS
    src, pos, tile_e, n_used = _dispatch_plan(route, n_tiles)
    y = _moe(x1, src, tile_e, n_used, lw["w1"], lw["w3"], lw["w2"])
    x2 = _final(base, y, pos, route, lw["ln2_g"], lw["ln2_b"])
    return x2, ckv, kr, st


def kernel(x_prompt, x_sample, p_prompt, p_sample, cache_ckv, cache_krope, state_hgrn, ln0_g, ln0_b, w_in, q_norm, w_uq, kv_norm, w_uk, w_uv, hg_lb, hg_norm, w_br_a, w_br_b, w_out, ln1_g, ln1_b, w_router_group, w_router_expert, w_e_gate, w_e_up, w_e_down, w_ple, w_ple_gate, ln2_g, ln2_b):
    depth = w_in.shape[0]
    bp, sp, d = x_prompt.shape
    bs, ss, _ = x_sample.shape
    past = cache_ckv.shape[2]
    alpha = (2 * depth) ** 0.25

    xp = _ln_rows(x_prompt.reshape(bp * sp, d), ln0_g, ln0_b)
    xs = _ln_rows(x_sample.reshape(bs * ss, d), ln0_g, ln0_b)
    tab_p = _rope_table(jnp.arange(sp, dtype=jnp.int32), max(sp, TOKEN_TILE))
    tab_s = _rope_table(past + jnp.arange(ss, dtype=jnp.int32), max(ss, TOKEN_TILE))

    outs = [[] for _ in range(6)]
    for i in range(depth):
        lw = _layer_weights(i, w_in, q_norm, w_uq, kv_norm, w_uk, w_uv, hg_norm, w_br_a, w_br_b, w_out,
                            ln1_g, ln1_b, w_router_group, w_router_expert, w_e_gate, w_e_up, w_e_down,
                            w_ple, w_ple_gate, ln2_g, ln2_b)
        xp, c, r, s = _trunk_layer(i, xp, p_prompt[i].reshape(bp * sp, -1), tab_p, lw, hg_lb, bp, sp, None, alpha)
        outs[0].append(c.reshape(bp, sp, -1))
        outs[1].append(r.reshape(bp, sp, -1))
        outs[2].append(s)
        cache = (cache_ckv[i], cache_krope[i], state_hgrn[i])
        xs, c, r, s = _trunk_layer(i, xs, p_sample[i].reshape(bs * ss, -1), tab_s, lw, hg_lb, bs, ss, cache, alpha)
        outs[3].append(c.reshape(bs, ss, -1))
        outs[4].append(r.reshape(bs, ss, -1))
        outs[5].append(s)
    return (xp.reshape(bp, sp, d), xs.reshape(bs, ss, d)) + tuple(jnp.stack(o) for o in outs)
```

```python
import functools

import numpy as np
import jax
import jax.numpy as jnp
from jax import lax
from jax.experimental import pallas as pl
from jax.experimental.pallas import tpu as pltpu

F32 = jnp.float32
BF16 = jnp.bfloat16

D_MODEL = 1024
CHUNK = 64
PLE_DIM = 256
MLA_HEADS = 8
NOPE_DIM = 64
ROPE_DIM = 32
V_DIM = 64
Q_LORA = 384
KV_LORA = 256
ROPE_THETA = 10000.0
ATTN_SCALE = (NOPE_DIM + ROPE_DIM) ** -0.5
HG_HEADS = 4
HG_DK = 128
HG_DV = 128
HG_FDIM = HG_HEADS * HG_DK
HG_IDIM = HG_HEADS * HG_DV
N_GROUPS = 4
EXPERTS_PER_GROUP = 8
N_EXPERTS = N_GROUPS * EXPERTS_PER_GROUP
TOP_K = 2
D_EXPERT = 512
EPS = 1e-6

LANES = 128
HEAD_PAD = LANES
QK_PAD = MLA_HEADS * HEAD_PAD
VMEM_LIMIT = 56 * 1024 * 1024

TOKEN_TILE = 512
ATTN_TILE = 256
MOE_TILE = 256
FINAL_TILE = 256
HG_SUB = 16


def _cparams(sem):
    return pltpu.CompilerParams(dimension_semantics=sem, vmem_limit_bytes=VMEM_LIMIT)


def _const_spec(shape):
    nd = len(shape)
    return pl.BlockSpec(shape, lambda *_: (0,) * nd, pipeline_mode=pl.Buffered(1))


def _dot(a, b):
    return jnp.dot(a, b, preferred_element_type=F32)


def _dot_nt(a, b):
    return lax.dot_general(a, b, (((1,), (1,)), ((), ())), preferred_element_type=F32)


def _dot_tn(a, b):
    return lax.dot_general(a, b, (((0,), (0,)), ((), ())), preferred_element_type=F32)


def _div_pow2(x, d):
    return jnp.right_shift(x, int(d).bit_length() - 1)


def _sigmoid(x):
    return 1.0 / (1.0 + jnp.exp(-x))


def _layernorm(x, g, b):
    mu = jnp.mean(x, axis=-1, keepdims=True)
    xc = x - mu
    var = jnp.mean(xc * xc, axis=-1, keepdims=True)
    return xc * lax.rsqrt(var + EPS) * g + b


def _rmsnorm(x, g):
    return x * lax.rsqrt(jnp.mean(x * x, axis=-1, keepdims=True) + EPS) * g


def _ln_kernel(x_ref, g_ref, b_ref, o_ref):
    o_ref[...] = _layernorm(x_ref[...], g_ref[...], b_ref[...])


def _ln_rows(x, g, b):
    t, d = x.shape
    tm = TOKEN_TILE
    return pl.pallas_call(
        _ln_kernel,
        grid=(t // tm,),
        in_specs=[pl.BlockSpec((tm, d), lambda i: (i, 0)), _const_spec((1, d)), _const_spec((1, d))],
        out_specs=pl.BlockSpec((tm, d), lambda i: (i, 0)),
        out_shape=jax.ShapeDtypeStruct((t, d), F32),
        compiler_params=_cparams(("parallel",)),
        name="ln0",
    )(x, g.reshape(1, d), b.reshape(1, d))


N_LAT = Q_LORA + KV_LORA
N_HG = 2 * HG_FDIM + 2 * HG_IDIM
COL_HG = N_LAT
COL_KR = N_LAT + N_HG
N_PROJ = COL_KR + 2 * LANES


def _proj_kernel(x_ref, tab_ref, wa_ref, qn_ref, kvn_ref, wq_ref, wqr_ref, wkv_ref,
                 q_ref, k_ref, va_ref, vb_ref, ckv_ref, kr_ref, f_ref, hq_ref, hv_ref, hg_ref):
    xb = x_ref[...].astype(BF16)
    cq_tab = tab_ref[:, 0:LANES]
    sq_tab = tab_ref[:, LANES:2 * LANES]
    ck_tab = tab_ref[:, 2 * LANES:3 * LANES]
    sk_tab = tab_ref[:, 3 * LANES:4 * LANES]

    lat = _dot(xb, wa_ref[:, 0:N_LAT])
    cqn = _rmsnorm(lat[:, 0:Q_LORA], qn_ref[...]).astype(BF16)
    ckvn = _rmsnorm(lat[:, Q_LORA:N_LAT], kvn_ref[...])
    ckv_ref[...] = ckvn

    qa = _dot(cqn, wq_ref[...])
    qb = _dot(cqn, wqr_ref[...])
    for h in range(MLA_HEADS):
        sl = slice(h * HEAD_PAD, (h + 1) * HEAD_PAD)
        q_ref[:, sl] = (qa[:, sl] * cq_tab + qb[:, sl] * sq_tab).astype(BF16)

    kr2 = _dot(xb, wa_ref[:, COL_KR:N_PROJ])
    kr = kr2[:, 0:LANES] * ck_tab + kr2[:, LANES:2 * LANES] * sk_tab
    kr_ref[...] = kr[:, 0:ROPE_DIM]

    kvin = jnp.concatenate([ckvn.astype(BF16), kr.astype(BF16)], axis=1)
    kv = _dot(kvin, wkv_ref[...])
    k_ref[...] = kv[:, 0:QK_PAD].astype(BF16)
    va_ref[...] = kv[:, QK_PAD:QK_PAD + 512].astype(BF16)
    vb_ref[...] = kv[:, QK_PAD + 512:QK_PAD + 1024].astype(BF16)

    f_ref[...] = _dot(xb, wa_ref[:, COL_HG:COL_HG + 512])
    hq_ref[...] = _dot(xb, wa_ref[:, COL_HG + 512:COL_HG + 1024])
    hv_ref[...] = _dot(xb, wa_ref[:, COL_HG + 1024:COL_HG + 1536]).astype(BF16)
    hg_ref[...] = _dot(xb, wa_ref[:, COL_HG + 1536:COL_HG + 2048])


def _proj(xn, tab, lw):
    t = xn.shape[0]
    tm = TOKEN_TILE
    ntab = tab.shape[0] // tm
    row = lambda i: (i, 0)
    outs = [
        (QK_PAD, BF16), (QK_PAD, BF16), (512, BF16), (512, BF16), (KV_LORA, F32), (ROPE_DIM, F32),
        (HG_FDIM, F32), (HG_FDIM, F32), (HG_IDIM, BF16), (HG_IDIM, F32),
    ]
    return pl.pallas_call(
        _proj_kernel,
        grid=(t // tm,),
        in_specs=[
            pl.BlockSpec((tm, D_MODEL), row),
            pl.BlockSpec((tm, 4 * LANES), lambda i: (i % ntab, 0)),
            _const_spec((D_MODEL, N_PROJ)),
            _const_spec((1, Q_LORA)), _const_spec((1, KV_LORA)),
            _const_spec((Q_LORA, QK_PAD)), _const_spec((Q_LORA, QK_PAD)),
            _const_spec((KV_LORA + LANES, 2 * QK_PAD)),
        ],
        out_specs=[pl.BlockSpec((tm, w), row) for w, _ in outs],
        out_shape=[jax.ShapeDtypeStruct((t, w), dt) for w, dt in outs],
        compiler_params=_cparams(("parallel",)),
        name="proj",
    )(xn, tab, lw["wa"], lw["q_norm"], lw["kv_norm"], lw["wq"], lw["wq_rot"], lw["wkv"])


LOG2E = 1.4426950408889634


def _attn_prompt_kernel(q_ref, k_ref, va_ref, vb_ref, o_ref, s_ref, m_ref, l_ref, acc_ref):
    tq = ATTN_TILE
    nh = MLA_HEADS
    qi = pl.program_id(1)
    rq = _div_pow2(lax.broadcasted_iota(jnp.int32, (tq, tq), 0), CHUNK)
    ck = _div_pow2(lax.broadcasted_iota(jnp.int32, (tq, tq), 1), CHUNK)
    diag_mask = ck <= rq
    c = ATTN_SCALE * LOG2E

    m_ref[...] = jnp.full(m_ref.shape, -jnp.inf, F32)
    l_ref[...] = jnp.zeros(l_ref.shape, F32)
    acc_ref[...] = jnp.zeros(acc_ref.shape, F32)

    def scores(kt, mask):
        ks = pl.multiple_of(kt * tq, tq)
        for h in range(nh):
            hs = slice(h * HEAD_PAD, (h + 1) * HEAD_PAD)
            s = _dot_nt(q_ref[:, hs], k_ref[pl.ds(ks, tq), hs]) * c
            if mask is not None:
                s = jnp.where(mask, s, -jnp.inf)
            s_ref[h, :, pl.ds(ks, tq)] = s
            m_ref[h] = jnp.maximum(m_ref[h], jnp.maximum(s[:, 0:LANES], s[:, LANES:2 * LANES]))

    def p1(kt, carry):
        scores(kt, None)
        return carry

    lax.fori_loop(0, qi, p1, 0)
    scores(qi, diag_mask)
    for h in range(nh):
        m_ref[h] = jnp.broadcast_to(jnp.max(m_ref[h], axis=-1, keepdims=True), (tq, LANES))

    def p2(kt, carry):
        ks = pl.multiple_of(kt * tq, tq)
        for h in range(nh):
            v_ref = va_ref if h % 2 == 0 else vb_ref
            pair = h // 2
            mb = m_ref[h]
            p_lo = jnp.exp2(s_ref[h, :, pl.ds(ks, LANES)] - mb)
            p_hi = jnp.exp2(s_ref[h, :, pl.ds(ks + LANES, LANES)] - mb)
            l_ref[h] += p_lo + p_hi
            p = jnp.concatenate([p_lo, p_hi], axis=1).astype(BF16)
            acc_ref[h] += _dot(p, v_ref[pl.ds(ks, tq), pair * LANES:(pair + 1) * LANES])
        return carry

    lax.fori_loop(0, qi + 1, p2, 0)
    for pair in range(nh // 2):
        he, ho = 2 * pair, 2 * pair + 1
        o = (acc_ref[he] / jnp.sum(l_ref[he], axis=-1, keepdims=True)
             + acc_ref[ho] / jnp.sum(l_ref[ho], axis=-1, keepdims=True))
        o_ref[:, pair * LANES:(pair + 1) * LANES] = o.astype(BF16)


def _attn_prompt(q, k, va, vb, batch, seq):
    tq = ATTN_TILE
    nq = seq // tq
    nh = MLA_HEADS
    return pl.pallas_call(
        _attn_prompt_kernel,
        grid=(batch, nq),
        in_specs=[
            pl.BlockSpec((tq, QK_PAD), lambda b, i: (b * nq + i, 0)),
            pl.BlockSpec((seq, QK_PAD), lambda b, i: (b, 0)),
            pl.BlockSpec((seq, 512), lambda b, i: (b, 0)),
            pl.BlockSpec((seq, 512), lambda b, i: (b, 0)),
        ],
        out_specs=pl.BlockSpec((tq, 512), lambda b, i: (b * nq + i, 0)),
        out_shape=jax.ShapeDtypeStruct((batch * seq, 512), BF16),
        scratch_shapes=[pltpu.VMEM((nh, tq, seq), F32), pltpu.VMEM((nh, tq, LANES), F32),
                        pltpu.VMEM((nh, tq, LANES), F32), pltpu.VMEM((nh, tq, LANES), F32)],
        compiler_params=_cparams(("parallel", "arbitrary")),
        name="attn_prompt",
    )(q, k, va, vb)


SAMPLE_KEY_TILE = 1024


def _attn_sample_kernel(q_ref, ckvn_ref, krn_ref, cc_ref, ck_ref, wukt_ref, psel_ref, wva_ref, wvb_ref,
                        o_ref, s_ref):
    ln = q_ref.shape[0]
    past = cc_ref.shape[0]
    nh = MLA_HEADS
    qlat, qrope = [], []
    for h in range(nh):
        qh = q_ref[:, h * HEAD_PAD:(h + 1) * HEAD_PAD]
        qlat.append(_dot(qh, wukt_ref[h]).astype(BF16))
        qrope.append(_dot(qh, psel_ref[...]).astype(BF16))
    qlat = jnp.concatenate(qlat, axis=0)
    qrope = jnp.concatenate(qrope, axis=0)

    ckv_new = ckvn_ref[...].astype(BF16)
    kr_new = krn_ref[...].astype(BF16)
    s_new = (_dot_nt(qlat, ckv_new) + _dot_nt(qrope, kr_new)) * ATTN_SCALE
    m = jnp.max(s_new, axis=-1, keepdims=True)
    tk = SAMPLE_KEY_TILE
    for kt in range(past // tk):
        rows = slice(kt * tk, (kt + 1) * tk)
        s = (_dot_nt(qlat, cc_ref[rows, :].astype(BF16)) + _dot_nt(qrope, ck_ref[rows, :].astype(BF16))) * ATTN_SCALE
        s_ref[:, rows] = s
        m = jnp.maximum(m, jnp.max(s, axis=-1, keepdims=True))
    p_new = jnp.exp(s_new - m)
    l = jnp.sum(p_new, axis=-1, keepdims=True)
    lat = _dot(p_new.astype(BF16), ckv_new)
    for kt in range(past // tk):
        rows = slice(kt * tk, (kt + 1) * tk)
        p = jnp.exp(s_ref[:, rows] - m)
        l = l + jnp.sum(p, axis=-1, keepdims=True)
        lat = lat + _dot(p.astype(BF16), cc_ref[rows, :].astype(BF16))
    lat = (lat / l).astype(BF16)
    for pair in range(nh // 2):
        he, ho = 2 * pair, 2 * pair + 1
        o = _dot(lat[he * ln:(he + 1) * ln], wva_ref[pair]) + _dot(lat[ho * ln:(ho + 1) * ln], wvb_ref[pair])
        o_ref[:, pair * LANES:(pair + 1) * LANES] = o.astype(BF16)


def _attn_sample(q, ckv_new, kr_new, cache_ckv, cache_kr, lw, batch, ln):
    past = cache_ckv.shape[1]
    nh = MLA_HEADS
    return pl.pallas_call(
        _attn_sample_kernel,
        grid=(batch,),
        in_specs=[
            pl.BlockSpec((ln, QK_PAD), lambda b: (b, 0)),
            pl.BlockSpec((ln, KV_LORA), lambda b: (b, 0)),
            pl.BlockSpec((ln, ROPE_DIM), lambda b: (b, 0)),
            pl.BlockSpec((None, past, KV_LORA), lambda b: (b, 0, 0)),
            pl.BlockSpec((None, past, ROPE_DIM), lambda b: (b, 0, 0)),
            _const_spec((nh, HEAD_PAD, KV_LORA)),
            _const_spec((HEAD_PAD, ROPE_DIM)),
            _const_spec((nh // 2, KV_LORA, LANES)),
            _const_spec((nh // 2, KV_LORA, LANES)),
        ],
        out_specs=pl.BlockSpec((ln, 512), lambda b: (b, 0)),
        out_shape=jax.ShapeDtypeStruct((batch * ln, 512), BF16),
        scratch_shapes=[pltpu.VMEM((nh * ln, past), F32)],
        compiler_params=_cparams(("parallel",)),
        name="attn_sample",
    )(q, ckv_new, kr_new, cache_ckv, cache_kr, lw["wuk_t"], lw["p_sel"], lw["wuv_a3"], lw["wuv_b3"])


def _hgrn_kernel(*refs, layer, chunk, has_init):
    if has_init:
        f_ref, q_ref, v_ref, g_ref, lb_ref, nrm_ref, s0_ref, o_ref, sout_ref, st_ref = refs
    else:
        f_ref, q_ref, v_ref, g_ref, lb_ref, nrm_ref, o_ref, sout_ref, st_ref = refs
        s0_ref = None
    tt = f_ref.shape[0]
    c = chunk
    ti = pl.program_id(1)

    @pl.when(ti == 0)
    def _():
        for h in range(HG_HEADS):
            if has_init:
                st_ref[h] = s0_ref[h].T
            else:
                st_ref[h] = jnp.zeros((HG_DV, HG_DK), F32)

    lbp = lb_ref[...]
    e = jnp.exp(lbp - jnp.max(lbp, axis=0, keepdims=True))
    tot = jnp.sum(e, axis=0, keepdims=True)
    part = jnp.zeros_like(tot)
    for j in range(1, layer + 1):
        part = part + e[j:j + 1]
    lb = part / tot
    log_lb = jnp.log(lb)
    log_1m = jnp.log1p(-lb)
    one_m = 1.0 - lb

    rid = lax.broadcasted_iota(jnp.int32, (c, 1), 0)
    rr = lax.broadcasted_iota(jnp.int32, (c, c), 0)
    cc = lax.broadcasted_iota(jnp.int32, (c, c), 1)
    tri = (cc <= rr).astype(BF16)
    base_mask = jnp.logical_and(_div_pow2(rr, HG_SUB) == _div_pow2(cc, HG_SUB), cc <= rr)

    for ci in range(tt // c):
        rows = slice(ci * c, (ci + 1) * c)
        z = f_ref[rows, :]
        ez = jnp.exp(-jnp.abs(z))
        rz = 1.0 / (1.0 + ez)
        log_sig = jnp.minimum(z, 0.0) - jnp.log1p(ez)
        kk = one_m * jnp.where(z >= 0, ez * rz, rz)
        y = log_1m + log_sig
        logf = jnp.maximum(log_lb, y) + jnp.log1p(jnp.exp(-jnp.abs(log_lb - y)))
        qp = q_ref[rows, :]
        qq = qp * _sigmoid(qp)

        hi = logf.astype(BF16)
        lo = (logf - hi.astype(F32)).astype(BF16)
        b = _dot(tri, hi) + _dot(tri, lo)
        bend = b[c - 1:c, :]

        qs, ks, masks = [], [], []
        g = c
        while g > HG_SUB:
            half = g // 2
            bref = b[half - 1:half, :]
            for gi in range(1, c // g):
                bref = jnp.where(rid >= gi * g, b[gi * g + half - 1:gi * g + half, :], bref)
            x = jnp.exp(-jnp.abs(b - bref))
            right = jnp.bitwise_and(rid, g - 1) >= half
            qs.append(jnp.where(right, qq * x, 0.0).astype(BF16))
            ks.append(jnp.where(right, 0.0, kk * x).astype(BF16))
            masks.append(None if g == c else (_div_pow2(rr, g) == _div_pow2(cc, g)))
            g = half
        bst = jnp.zeros_like(bend)
        for gi in range(1, c // HG_SUB):
            bst = jnp.where(rid >= gi * HG_SUB, b[gi * HG_SUB - 1:gi * HG_SUB, :], bst)
        e0 = b - bst
        qs.append((qq * jnp.exp(e0)).astype(BF16))
        ks.append((kk * jnp.exp(-e0)).astype(BF16))
        masks.append(base_mask)

        q_in = (qq * jnp.exp(b)).astype(BF16)
        k_out = (kk * jnp.exp(bend - b)).astype(BF16)
        dec = jnp.exp(bend)
        gp = g_ref[rows, :]
        gate = gp * _sigmoid(gp) * nrm_ref[...]

        for h in range(HG_HEADS):
            hs = slice(h * HG_DK, (h + 1) * HG_DK)
            a = None
            for ql, kl, mk in zip(qs, ks, masks):
                al = _dot_nt(ql[:, hs], kl[:, hs])
                if mk is not None:
                    al = jnp.where(mk, al, 0.0)
                a = al if a is None else a + al
            vh = v_ref[rows, hs]
            st = st_ref[h]
            o = _dot_nt(q_in[:, hs], st.astype(BF16)) + _dot(a.astype(BF16), vh)
            o = o * lax.rsqrt(jnp.mean(o * o, axis=-1, keepdims=True) + EPS) * gate[:, hs]
            o_ref[rows, hs] = o.astype(BF16)
            st_ref[h] = st * dec[:, hs] + _dot_tn(vh, k_out[:, hs])

    @pl.when(ti == pl.num_programs(1) - 1)
    def _():
        for h in range(HG_HEADS):
            sout_ref[h] = st_ref[h].T


def _hgrn(f_pre, q_pre, v, g_pre, hg_lb, hg_norm, s0, layer, batch, ln):
    tt = min(ln, 256)
    chunk = min(tt, CHUNK)
    nt = ln // tt
    has_init = s0 is not None
    depth = hg_lb.shape[0]
    row = lambda b, t: (b * nt + t, 0)
    st_spec = pl.BlockSpec((None, HG_HEADS, HG_DK, HG_DV), lambda b, t: (b, 0, 0, 0))
    in_specs = [
        pl.BlockSpec((tt, HG_FDIM), row), pl.BlockSpec((tt, HG_FDIM), row),
        pl.BlockSpec((tt, HG_IDIM), row), pl.BlockSpec((tt, HG_IDIM), row),
        _const_spec((depth, HG_FDIM)), _const_spec((1, HG_IDIM)),
    ]
    args = [f_pre, q_pre, v, g_pre, hg_lb, hg_norm]
    if has_init:
        in_specs.append(st_spec)
        args.append(s0)
    return pl.pallas_call(
        functools.partial(_hgrn_kernel, layer=layer, chunk=chunk, has_init=has_init),
        grid=(batch, nt),
        in_specs=in_specs,
        out_specs=[pl.BlockSpec((tt, HG_IDIM), row), st_spec],
        out_shape=[jax.ShapeDtypeStruct((batch * ln, HG_IDIM), BF16),
                   jax.ShapeDtypeStruct((batch, HG_HEADS, HG_DK, HG_DV), F32)],
        scratch_shapes=[pltpu.VMEM((HG_HEADS, HG_DV, HG_DK), F32)],
        compiler_params=_cparams(("parallel", "arbitrary")),
        name="hgrn",
    )(*args)


ROUTE_GROUP_LANE = N_EXPERTS


def _route(logits):
    lane_i = lax.broadcasted_iota(jnp.int32, logits.shape, 1)
    lane = lane_i.astype(F32)
    big = float(LANES)
    neg = -jnp.inf
    is_g = jnp.logical_and(lane_i >= ROUTE_GROUP_LANE, lane_i < ROUTE_GROUP_LANE + N_GROUPS)
    gl = jnp.where(is_g, logits, neg)
    gmax = jnp.max(gl, axis=-1, keepdims=True)
    gidx = jnp.min(jnp.where(gl == gmax, lane - ROUTE_GROUP_LANE, big), axis=-1, keepdims=True)
    g_top = 1.0 / jnp.sum(jnp.exp(gl - gmax), axis=-1, keepdims=True)
    lane_group = _div_pow2(lane_i, EXPERTS_PER_GROUP).astype(F32)
    in_group = jnp.logical_and(lane_i < N_EXPERTS, lane_group == gidx)
    el = jnp.where(in_group, logits, neg)
    m1 = jnp.max(el, axis=-1, keepdims=True)
    i1 = jnp.min(jnp.where(el == m1, lane, big), axis=-1, keepdims=True)
    el2 = jnp.where(lane == i1, neg, el)
    m2 = jnp.max(el2, axis=-1, keepdims=True)
    i2 = jnp.min(jnp.where(el2 == m2, lane, big), axis=-1, keepdims=True)
    r = jnp.exp(m2 - m1)
    w1 = g_top / (1.0 + r)
    w2 = g_top * r / (1.0 + r)
    out = jnp.where(lane_i == 0, i1, 0.0)
    out = jnp.where(lane_i == 1, i2, out)
    out = jnp.where(lane_i == 2, w1, out)
    out = jnp.where(lane_i == 3, w2, out)
    return out


def _post_kernel(x_ref, a_ref, o_ref, ple_ref, wg_ref, wa_ref, wb_ref, wo_ref, g1_ref, b1_ref,
                 wr_ref, wpg_ref, wp_ref, x1_ref, base_ref, route_ref, *, alpha):
    x = x_ref[...]
    xb = x.astype(BF16)
    ya = _dot(a_ref[...], wa_ref[...])
    merged = _sigmoid(_dot(xb, wg_ref[:, 0:D_MODEL])) * ya
    yb = _dot(o_ref[...], wb_ref[...])
    merged = merged + _sigmoid(_dot(xb, wg_ref[:, D_MODEL:2 * D_MODEL])) * yb
    mix = _dot(merged.astype(BF16), wo_ref[...])
    x1 = _layernorm(alpha * x + mix, g1_ref[...], b1_ref[...])
    x1_ref[...] = x1
    x1b = x1.astype(BF16)
    route_ref[...] = _route(_dot(x1b, wr_ref[...]))
    ple = _dot(ple_ref[...].astype(BF16), wp_ref[...])
    base_ref[...] = alpha * x1 + _sigmoid(_dot(x1b, wpg_ref[...])) * ple


def _post(xn, a, o, ple, lw, alpha):
    t = xn.shape[0]
    tm = TOKEN_TILE
    row = lambda i: (i, 0)
    d = D_MODEL
    return pl.pallas_call(
        functools.partial(_post_kernel, alpha=alpha),
        grid=(t // tm,),
        in_specs=[
            pl.BlockSpec((tm, d), row), pl.BlockSpec((tm, 512), row), pl.BlockSpec((tm, HG_IDIM), row),
            pl.BlockSpec((tm, PLE_DIM), row),
            _const_spec((d, 2 * d)), _const_spec((512, d)), _const_spec((HG_IDIM, d)), _const_spec((d, d)),
            _const_spec((1, d)), _const_spec((1, d)),
            _const_spec((d, LANES)), _const_spec((d, d)), _const_spec((PLE_DIM, d)),
        ],
        out_specs=[pl.BlockSpec((tm, d), row), pl.BlockSpec((tm, d), row), pl.BlockSpec((tm, LANES), row)],
        out_shape=[jax.ShapeDtypeStruct((t, d), F32), jax.ShapeDtypeStruct((t, d), F32),
                   jax.ShapeDtypeStruct((t, LANES), F32)],
        compiler_params=_cparams(("parallel",)),
        name="post",
    )(xn, a, o, ple, lw["w_gate"], lw["w_br_a"], lw["w_br_b"], lw["w_out"], lw["ln1_g"], lw["ln1_b"],
      lw["w_route"], lw["w_ple_gate"], lw["w_ple"])


def _issue_row_gather(idx_ref, src_hbm, dst, sem, n_rows, dst_off=0):
    def body(j, carry):
        pltpu.make_async_copy(src_hbm.at[pl.ds(idx_ref[0, j], 1)], dst.at[pl.ds(dst_off + j, 1)], sem).start()
        return carry

    lax.fori_loop(0, n_rows, body, 0, unroll=8)


def _wait_row_gather(src_hbm, dst, sem):
    pltpu.make_async_copy(src_hbm.at[pl.ds(0, dst.shape[0])], dst, sem).wait()


def _idx_specs(tile, n_blocks):
    first = pl.BlockSpec((None, 1, tile), lambda i, *_: (0, 0, 0), memory_space=pltpu.SMEM)
    nxt = pl.BlockSpec((None, 1, tile), lambda i, *_: (jnp.minimum(i + 1, n_blocks - 1), 0, 0),
                       memory_space=pltpu.SMEM)
    return first, nxt


def _moe_kernel(te_ref, nu_ref, idx0_ref, idxn_ref, x_hbm, w1_ref, w3_ref, w2_ref, y_ref, xbuf, sem):
    i = pl.program_id(0)
    n_used = nu_ref[0]
    slot = i % 2

    @pl.when(i == 0)
    def _():
        _issue_row_gather(idx0_ref, x_hbm, xbuf.at[0], sem.at[0], MOE_TILE)

    @pl.when(i < n_used)
    def _():
        _wait_row_gather(x_hbm, xbuf.at[slot], sem.at[slot])
        xb = xbuf[slot].astype(BF16)
        nxt = xbuf.at[1 - slot]
        nsem = sem.at[1 - slot]
        group = MOE_TILE // 4

        def issue_group(gi):
            for j in range(gi * group, (gi + 1) * group):
                pltpu.make_async_copy(x_hbm.at[pl.ds(idxn_ref[0, j], 1)], nxt.at[pl.ds(j, 1)], nsem).start()

        hs = []
        half = D_EXPERT // 2
        for c in range(2):
            cs = slice(c * half, (c + 1) * half)
            h1 = _dot(xb, w1_ref[:, cs])
            issue_group(2 * c)
            h3 = _dot(xb, w3_ref[:, cs])
            issue_group(2 * c + 1)
            hs.append((h1 * _sigmoid(h1) * h3).astype(BF16))
        y_ref[...] = _dot(jnp.concatenate(hs, axis=1), w2_ref[...])

        @pl.when(i == pl.num_programs(0) - 1)
        def _():
            _wait_row_gather(x_hbm, nxt, nsem)

    @pl.when(i == n_used)
    def _():
        _wait_row_gather(x_hbm, xbuf.at[slot], sem.at[slot])

    @pl.when(i >= n_used)
    def _():
        y_ref[...] = jnp.zeros_like(y_ref)


def _moe(x, src, tile_e, n_used, w1, w3, w2):
    tm = MOE_TILE
    n_tiles = src.shape[0]
    d = D_MODEL
    first, nxt = _idx_specs(tm, n_tiles)
    grid_spec = pltpu.PrefetchScalarGridSpec(
        num_scalar_prefetch=2,
        grid=(n_tiles,),
        in_specs=[
            first, nxt,
            pl.BlockSpec(memory_space=pl.ANY),
            pl.BlockSpec((None, d, D_EXPERT), lambda i, te, nu: (te[i], 0, 0)),
            pl.BlockSpec((None, d, D_EXPERT), lambda i, te, nu: (te[i], 0, 0)),
            pl.BlockSpec((None, D_EXPERT, d), lambda i, te, nu: (te[i], 0, 0)),
        ],
        out_specs=pl.BlockSpec((tm, d), lambda i, te, nu: (i, 0)),
        scratch_shapes=[pltpu.VMEM((2, tm, d), F32), pltpu.SemaphoreType.DMA((2,))],
    )
    src3 = src.reshape(n_tiles, 1, tm)
    return pl.pallas_call(
        _moe_kernel,
        grid_spec=grid_spec,
        out_shape=jax.ShapeDtypeStruct((n_tiles * tm, d), F32),
        compiler_params=_cparams(("arbitrary",)),
        name="moe_experts",
    )(tile_e, n_used, src3, src3, x, w1, w3, w2)


def _dispatch_plan(route, n_tiles):
    t = route.shape[0]
    a = t * TOP_K
    tm = MOE_TILE
    flat_e = route[:, 0:TOP_K].astype(jnp.int32).reshape(a)
    ar = jnp.arange(a, dtype=jnp.int32)
    key = lax.sort(flat_e * a + ar)
    sorted_e = key // a
    order = key - sorted_e * a
    eid = jnp.arange(N_EXPERTS, dtype=jnp.int32)
    onehot = sorted_e[:, None] == eid[None, :]
    counts = jnp.sum(onehot.astype(jnp.int32), axis=0)
    tiles_e = (counts + tm - 1) // tm
    tile_end = jnp.cumsum(tiles_e)
    pstart = (tile_end - tiles_e) * tm
    start = jnp.cumsum(counts) - counts
    pos_sorted = ar + jnp.sum(jnp.where(onehot, (pstart - start)[None, :], 0), axis=1)
    _, pos = lax.sort_key_val(order, pos_sorted)
    jt = jnp.arange(n_tiles, dtype=jnp.int32)
    tile_e = jnp.minimum(jnp.sum((tile_end[None, :] <= jt[:, None]).astype(jnp.int32), axis=1), N_EXPERTS - 1)
    first = jt * tm + jnp.sum(jnp.where(tile_e[:, None] == eid[None, :], (start - pstart)[None, :], 0), axis=1)
    first = jnp.clip(first, 0, a)
    order_p = jnp.concatenate([order, jnp.zeros((tm,), jnp.int32)])
    src = jax.vmap(lambda s: lax.dynamic_slice(order_p, (s,), (tm,)))(first) // TOP_K
    n_used = tile_end[-1:].astype(jnp.int32)
    return src.astype(jnp.int32), pos.astype(jnp.int32), tile_e.astype(jnp.int32), n_used


def _final_kernel(p0c_ref, p1c_ref, p0n_ref, p1n_ref, base_ref, route_ref, g_ref, b_ref, y_hbm, o_ref, ybuf, sem):
    i = pl.program_id(0)
    n = pl.num_programs(0)
    tm = FINAL_TILE
    slot = i % 2

    def issue(pa_ref, pb_ref, s):
        _issue_row_gather(pa_ref, y_hbm, ybuf.at[s], sem.at[s], tm, 0)
        _issue_row_gather(pb_ref, y_hbm, ybuf.at[s], sem.at[s], tm, tm)

    @pl.when(i == 0)
    def _():
        issue(p0c_ref, p1c_ref, 0)

    @pl.when(i + 1 < n)
    def _():
        issue(p0n_ref, p1n_ref, 1 - slot)

    _wait_row_gather(y_hbm, ybuf.at[slot], sem.at[slot])
    w = route_ref[...]
    h = base_ref[...] + w[:, 2:3] * ybuf[slot, 0:tm, :] + w[:, 3:4] * ybuf[slot, tm:2 * tm, :]
    o_ref[...] = _layernorm(h, g_ref[...], b_ref[...])


def _final(base, y, pos, route, g, b):
    t, d = base.shape
    tm = FINAL_TILE
    nb = t // tm
    row = lambda i: (i, 0)
    first, nxt = _idx_specs(tm, nb)
    pos2 = pos.reshape(t, TOP_K)
    p0 = pos2[:, 0].reshape(nb, 1, tm)
    p1 = pos2[:, 1].reshape(nb, 1, tm)
    return pl.pallas_call(
        _final_kernel,
        grid=(nb,),
        in_specs=[first, first, nxt, nxt,
                  pl.BlockSpec((tm, d), row), pl.BlockSpec((tm, LANES), row),
                  _const_spec((1, d)), _const_spec((1, d)),
                  pl.BlockSpec(memory_space=pl.ANY)],
        out_specs=pl.BlockSpec((tm, d), row),
        out_shape=jax.ShapeDtypeStruct((t, d), F32),
        scratch_shapes=[pltpu.VMEM((2, 2 * tm, d), F32), pltpu.SemaphoreType.DMA((2,))],
        compiler_params=_cparams(("arbitrary",)),
        name="final_ln",
    )(p0, p1, p0, p1, base, route, g, b, y)


def _rot_cols(w):
    half = ROPE_DIM // 2
    return jnp.concatenate([-w[..., half:], w[..., :half]], axis=-1)


def _layer_weights(i, w_in, q_norm, w_uq, kv_norm, w_uk, w_uv, hg_norm, w_br_a, w_br_b, w_out, ln1_g, ln1_b,
                   w_rg, w_re, w_e_gate, w_e_up, w_e_down, w_ple, w_ple_gate, ln2_g, ln2_b):
    d = D_MODEL
    nh = MLA_HEADS
    win = w_in[i]
    c_kr = Q_LORA + KV_LORA
    c_f = c_kr + ROPE_DIM
    c_ga = c_f + N_HG
    w_kr = win[:, c_kr:c_f]
    zpad = jnp.zeros((d, LANES - ROPE_DIM), F32)
    wa = jnp.concatenate([win[:, 0:c_kr], win[:, c_f:c_ga], w_kr, zpad, _rot_cols(w_kr), zpad], axis=1)

    wq3 = w_uq[i].reshape(Q_LORA, nh, NOPE_DIM + ROPE_DIM)
    zq = jnp.zeros((Q_LORA, nh, HEAD_PAD - NOPE_DIM - ROPE_DIM), F32)
    wq = jnp.concatenate([wq3, zq], axis=-1).reshape(Q_LORA, QK_PAD)
    wq_rot = jnp.concatenate([jnp.zeros((Q_LORA, nh, NOPE_DIM), F32), _rot_cols(wq3[..., NOPE_DIM:]), zq],
                             axis=-1).reshape(Q_LORA, QK_PAD)

    wuk = w_uk[i]
    wuv = w_uv[i]
    k_nope = jnp.concatenate([wuk, jnp.zeros((KV_LORA, nh, HEAD_PAD - NOPE_DIM), F32)], axis=-1)
    eye = jnp.eye(ROPE_DIM, dtype=F32)
    k_rope = jnp.concatenate([jnp.zeros((ROPE_DIM, NOPE_DIM), F32), eye,
                              jnp.zeros((ROPE_DIM, HEAD_PAD - NOPE_DIM - ROPE_DIM), F32)], axis=-1)
    k_rope = jnp.concatenate([jnp.tile(k_rope[:, None, :], (1, nh, 1)).reshape(ROPE_DIM, QK_PAD),
                              jnp.zeros((LANES - ROPE_DIM, QK_PAD), F32)], axis=0)
    zv = jnp.zeros((KV_LORA, nh // 2, V_DIM), F32)
    wuv_a3 = jnp.concatenate([wuv[:, 0::2, :], zv], axis=-1)
    wuv_b3 = jnp.concatenate([zv, wuv[:, 1::2, :]], axis=-1)
    wkv_top = jnp.concatenate([k_nope.reshape(KV_LORA, QK_PAD), wuv_a3.reshape(KV_LORA, 512),
                               wuv_b3.reshape(KV_LORA, 512)], axis=1)
    wkv_bot = jnp.concatenate([k_rope, jnp.zeros((LANES, QK_PAD), F32)], axis=1)
    wkv = jnp.concatenate([wkv_top, wkv_bot], axis=0)

    wuk_t = jnp.concatenate([jnp.transpose(wuk, (1, 2, 0)),
                             jnp.zeros((nh, HEAD_PAD - NOPE_DIM, KV_LORA), F32)], axis=1)
    p_sel = jnp.concatenate([jnp.zeros((NOPE_DIM, ROPE_DIM), F32), eye,
                             jnp.zeros((HEAD_PAD - NOPE_DIM - ROPE_DIM, ROPE_DIM), F32)], axis=0)

    w_route = jnp.concatenate([w_re[i], w_rg[i], jnp.zeros((d, LANES - N_EXPERTS - N_GROUPS), F32)], axis=1)
    bf = lambda w: w.astype(BF16)
    return {
        "wa": bf(wa), "q_norm": q_norm[i].reshape(1, -1), "kv_norm": kv_norm[i].reshape(1, -1),
        "wq": bf(wq), "wq_rot": bf(wq_rot), "wkv": bf(wkv),
        "wuk_t": bf(wuk_t), "p_sel": bf(p_sel),
        "wuv_a3": bf(jnp.transpose(wuv_a3, (1, 0, 2))), "wuv_b3": bf(jnp.transpose(wuv_b3, (1, 0, 2))),
        "hg_norm": hg_norm[i].reshape(1, -1),
        "w_gate": bf(win[:, c_ga:c_ga + 2 * d]), "w_br_a": bf(w_br_a[i]), "w_br_b": bf(w_br_b[i]),
        "w_out": bf(w_out[i]), "ln1_g": ln1_g[i].reshape(1, -1), "ln1_b": ln1_b[i].reshape(1, -1),
        "w_route": bf(w_route), "w_ple_gate": bf(w_ple_gate[i]), "w_ple": bf(w_ple[i]),
        "w1": bf(w_e_gate[i]), "w3": bf(w_e_up[i]), "w2": bf(w_e_down[i]),
        "ln2_g": ln2_g[i].reshape(1, -1), "ln2_b": ln2_b[i].reshape(1, -1),
    }


def _rope_table(pos, rows):
    half = ROPE_DIM // 2
    inv = ROPE_THETA ** (-jnp.arange(half, dtype=F32) / half)
    ang = pos.astype(F32)[:, None] * inv[None, :]
    c2 = jnp.concatenate([jnp.cos(ang), jnp.cos(ang)], axis=1)
    s2 = jnp.concatenate([jnp.sin(ang), jnp.sin(ang)], axis=1)
    n = pos.shape[0]
    ones = jnp.ones((n, NOPE_DIM), F32)
    z = lambda w: jnp.zeros((n, w), F32)
    tab = jnp.concatenate([
        ones, c2, z(HEAD_PAD - NOPE_DIM - ROPE_DIM),
        z(NOPE_DIM), s2, z(HEAD_PAD - NOPE_DIM - ROPE_DIM),
        c2, z(LANES - ROPE_DIM),
        s2, z(LANES - ROPE_DIM)], axis=1)
    if rows > n:
        tab = jnp.tile(tab, (rows // n, 1))
    return tab


def _trunk_layer(i, xn, ple, tab, lw, hg_lb, batch, ln, cache, alpha):
    q, k, va, vb, ckv, kr, f_pre, q_pre, hv, g_pre = _proj(xn, tab, lw)
    if cache is None:
        a = _attn_prompt(q, k, va, vb, batch, ln)
        o, st = _hgrn(f_pre, q_pre, hv, g_pre, hg_lb, lw["hg_norm"], None, i, batch, ln)
    else:
        a = _attn_sample(q, ckv, kr, cache[0], cache[1], lw, batch, ln)
        o, st = _hgrn(f_pre, q_pre, hv, g_pre, hg_lb, lw["hg_norm"], cache[2], i, batch, ln)
    x1, base, route = _post(xn, a, o, ple, lw, alpha)

    t = xn.shape[0]
    n_tiles = (t * TOP_K) // MOE_TILE + N_EXPERTS
    src, pos, tile_e, n_used = _dispatch_plan(route, n_tiles)
    y = _moe(x1, src, tile_e, n_used, lw["w1"], lw["w3"], lw["w2"])
    x2 = _final(base, y, pos, route, lw["ln2_g"], lw["ln2_b"])
    return x2, ckv, kr, st


def kernel(x_prompt, x_sample, p_prompt, p_sample, cache_ckv, cache_krope, state_hgrn, ln0_g, ln0_b, w_in, q_norm, w_uq, kv_norm, w_uk, w_uv, hg_lb, hg_norm, w_br_a, w_br_b, w_out, ln1_g, ln1_b, w_router_group, w_router_expert, w_e_gate, w_e_up, w_e_down, w_ple, w_ple_gate, ln2_g, ln2_b):
    depth = w_in.shape[0]
    bp, sp, d = x_prompt.shape
    bs, ss, _ = x_sample.shape
    past = cache_ckv.shape[2]
    alpha = (2 * depth) ** 0.25

    xp = _ln_rows(x_prompt.reshape(bp * sp, d), ln0_g, ln0_b)
    xs = _ln_rows(x_sample.reshape(bs * ss, d), ln0_g, ln0_b)
    tab_p = _rope_table(jnp.arange(sp, dtype=jnp.int32), max(sp, TOKEN_TILE))
    tab_s = _rope_table(past + jnp.arange(ss, dtype=jnp.int32), max(ss, TOKEN_TILE))

    outs = [[] for _ in range(6)]
    for i in range(depth):
        lw = _layer_weights(i, w_in, q_norm, w_uq, kv_norm, w_uk, w_uv, hg_norm, w_br_a, w_br_b, w_out,
                            ln1_g, ln1_b, w_router_group, w_router_expert, w_e_gate, w_e_up, w_e_down,
                            w_ple, w_ple_gate, ln2_g, ln2_b)
        xp, c, r, s = _trunk_layer(i, xp, p_prompt[i].reshape(bp * sp, -1), tab_p, lw, hg_lb, bp, sp, None, alpha)
        outs[0].append(c.reshape(bp, sp, -1))
        outs[1].append(r.reshape(bp, sp, -1))
        outs[2].append(s)
        cache = (cache_ckv[i], cache_krope[i], state_hgrn[i])
        xs, c, r, s = _trunk_layer(i, xs, p_sample[i].reshape(bs * ss, -1), tab_s, lw, hg_lb, bs, ss, cache, alpha)
        outs[3].append(c.reshape(bs, ss, -1))
        outs[4].append(r.reshape(bs, ss, -1))
        outs[5].append(s)
    return (xp.reshape(bp, sp, d), xs.reshape(bs, ss, d)) + tuple(jnp.stack(o) for o in outs)
```

```python
import functools

import numpy as np
import jax
import jax.numpy as jnp
from jax import lax
from jax.experimental import pallas as pl
from jax.experimental.pallas import tpu as pltpu

F32 = jnp.float32
BF16 = jnp.bfloat16

D_MODEL = 1024
CHUNK = 64
PLE_DIM = 256
MLA_HEADS = 8
NOPE_DIM = 64
ROPE_DIM = 32
V_DIM = 64
Q_LORA = 384
KV_LORA = 256
ROPE_THETA = 10000.0
ATTN_SCALE = (NOPE_DIM + ROPE_DIM) ** -0.5
HG_HEADS = 4
HG_DK = 128
HG_DV = 128
HG_FDIM = HG_HEADS * HG_DK
HG_IDIM = HG_HEADS * HG_DV
N_GROUPS = 4
EXPERTS_PER_GROUP = 8
N_EXPERTS = N_GROUPS * EXPERTS_PER_GROUP
TOP_K = 2
D_EXPERT = 512
EPS = 1e-6

LANES = 128
HEAD_PAD = LANES
QK_PAD = MLA_HEADS * HEAD_PAD
VMEM_LIMIT = 56 * 1024 * 1024

TOKEN_TILE = 512
ATTN_TILE = 256
MOE_TILE = 256
FINAL_TILE = 256
HG_SUB = 16


def _cparams(sem):
    return pltpu.CompilerParams(dimension_semantics=sem, vmem_limit_bytes=VMEM_LIMIT)


def _const_spec(shape):
    nd = len(shape)
    return pl.BlockSpec(shape, lambda *_: (0,) * nd, pipeline_mode=pl.Buffered(1))


def _dot(a, b):
    return jnp.dot(a, b, preferred_element_type=F32)


def _dot_nt(a, b):
    return lax.dot_general(a, b, (((1,), (1,)), ((), ())), preferred_element_type=F32)


def _dot_tn(a, b):
    return lax.dot_general(a, b, (((0,), (0,)), ((), ())), preferred_element_type=F32)


def _div_pow2(x, d):
    return jnp.right_shift(x, int(d).bit_length() - 1)


def _sigmoid(x):
    return 1.0 / (1.0 + jnp.exp(-x))


def _layernorm(x, g, b):
    mu = jnp.mean(x, axis=-1, keepdims=True)
    xc = x - mu
    var = jnp.mean(xc * xc, axis=-1, keepdims=True)
    return xc * lax.rsqrt(var + EPS) * g + b


def _rmsnorm(x, g):
    return x * lax.rsqrt(jnp.mean(x * x, axis=-1, keepdims=True) + EPS) * g


def _ln_kernel(x_ref, g_ref, b_ref, o_ref):
    o_ref[...] = _layernorm(x_ref[...], g_ref[...], b_ref[...])


def _ln_rows(x, g, b):
    t, d = x.shape
    tm = TOKEN_TILE
    return pl.pallas_call(
        _ln_kernel,
        grid=(t // tm,),
        in_specs=[pl.BlockSpec((tm, d), lambda i: (i, 0)), _const_spec((1, d)), _const_spec((1, d))],
        out_specs=pl.BlockSpec((tm, d), lambda i: (i, 0)),
        out_shape=jax.ShapeDtypeStruct((t, d), F32),
        compiler_params=_cparams(("parallel",)),
        name="ln0",
    )(x, g.reshape(1, d), b.reshape(1, d))


N_LAT = Q_LORA + KV_LORA
N_HG = 2 * HG_FDIM + 2 * HG_IDIM
COL_HG = N_LAT
COL_KR = N_LAT + N_HG
N_PROJ = COL_KR + 2 * LANES


def _proj_kernel(x_ref, tab_ref, wa_ref, qn_ref, kvn_ref, wq_ref, wqr_ref, wkv_ref,
                 q_ref, k_ref, va_ref, vb_ref, ckv_ref, kr_ref, f_ref, hq_ref, hv_ref, hg_ref):
    xb = x_ref[...].astype(BF16)
    cq_tab = tab_ref[:, 0:LANES]
    sq_tab = tab_ref[:, LANES:2 * LANES]
    ck_tab = tab_ref[:, 2 * LANES:3 * LANES]
    sk_tab = tab_ref[:, 3 * LANES:4 * LANES]

    lat = _dot(xb, wa_ref[:, 0:N_LAT])
    cqn = _rmsnorm(lat[:, 0:Q_LORA], qn_ref[...]).astype(BF16)
    ckvn = _rmsnorm(lat[:, Q_LORA:N_LAT], kvn_ref[...])
    ckv_ref[...] = ckvn

    qa = _dot(cqn, wq_ref[...])
    qb = _dot(cqn, wqr_ref[...])
    for h in range(MLA_HEADS):
        sl = slice(h * HEAD_PAD, (h + 1) * HEAD_PAD)
        q_ref[:, sl] = (qa[:, sl] * cq_tab + qb[:, sl] * sq_tab).astype(BF16)

    kr2 = _dot(xb, wa_ref[:, COL_KR:N_PROJ])
    kr = kr2[:, 0:LANES] * ck_tab + kr2[:, LANES:2 * LANES] * sk_tab
    kr_ref[...] = kr[:, 0:ROPE_DIM]

    kvin = jnp.concatenate([ckvn.astype(BF16), kr.astype(BF16)], axis=1)
    kv = _dot(kvin, wkv_ref[...])
    k_ref[...] = kv[:, 0:QK_PAD].astype(BF16)
    va_ref[...] = kv[:, QK_PAD:QK_PAD + 512].astype(BF16)
    vb_ref[...] = kv[:, QK_PAD + 512:QK_PAD + 1024].astype(BF16)

    f_ref[...] = _dot(xb, wa_ref[:, COL_HG:COL_HG + 512])
    hq_ref[...] = _dot(xb, wa_ref[:, COL_HG + 512:COL_HG + 1024])
    hv_ref[...] = _dot(xb, wa_ref[:, COL_HG + 1024:COL_HG + 1536]).astype(BF16)
    hg_ref[...] = _dot(xb, wa_ref[:, COL_HG + 1536:COL_HG + 2048])


def _proj(xn, tab, lw):
    t = xn.shape[0]
    tm = TOKEN_TILE
    ntab = tab.shape[0] // tm
    row = lambda i: (i, 0)
    outs = [
        (QK_PAD, BF16), (QK_PAD, BF16), (512, BF16), (512, BF16), (KV_LORA, F32), (ROPE_DIM, F32),
        (HG_FDIM, F32), (HG_FDIM, F32), (HG_IDIM, BF16), (HG_IDIM, F32),
    ]
    return pl.pallas_call(
        _proj_kernel,
        grid=(t // tm,),
        in_specs=[
            pl.BlockSpec((tm, D_MODEL), row),
            pl.BlockSpec((tm, 4 * LANES), lambda i: (i % ntab, 0)),
            _const_spec((D_MODEL, N_PROJ)),
            _const_spec((1, Q_LORA)), _const_spec((1, KV_LORA)),
            _const_spec((Q_LORA, QK_PAD)), _const_spec((Q_LORA, QK_PAD)),
            _const_spec((KV_LORA + LANES, 2 * QK_PAD)),
        ],
        out_specs=[pl.BlockSpec((tm, w), row) for w, _ in outs],
        out_shape=[jax.ShapeDtypeStruct((t, w), dt) for w, dt in outs],
        compiler_params=_cparams(("parallel",)),
        name="proj",
    )(xn, tab, lw["wa"], lw["q_norm"], lw["kv_norm"], lw["wq"], lw["wq_rot"], lw["wkv"])


LOG2E = 1.4426950408889634


def _attn_prompt_kernel(q_ref, k_ref, va_ref, vb_ref, o_ref, s_ref, m_ref, l_ref, acc_ref):
    tq = ATTN_TILE
    nh = MLA_HEADS
    qi = pl.program_id(1)
    rq = _div_pow2(lax.broadcasted_iota(jnp.int32, (tq, tq), 0), CHUNK)
    ck = _div_pow2(lax.broadcasted_iota(jnp.int32, (tq, tq), 1), CHUNK)
    diag_mask = ck <= rq
    c = ATTN_SCALE * LOG2E

    m_ref[...] = jnp.full(m_ref.shape, -jnp.inf, F32)
    l_ref[...] = jnp.zeros(l_ref.shape, F32)
    acc_ref[...] = jnp.zeros(acc_ref.shape, F32)

    def scores(kt, mask):
        ks = pl.multiple_of(kt * tq, tq)
        for h in range(nh):
            hs = slice(h * HEAD_PAD, (h + 1) * HEAD_PAD)
            s = _dot_nt(q_ref[:, hs], k_ref[pl.ds(ks, tq), hs]) * c
            if mask is not None:
                s = jnp.where(mask, s, -jnp.inf)
            s_ref[h, :, pl.ds(ks, tq)] = s
            m_ref[h] = jnp.maximum(m_ref[h], jnp.maximum(s[:, 0:LANES], s[:, LANES:2 * LANES]))

    def p1(kt, carry):
        scores(kt, None)
        return carry

    lax.fori_loop(0, qi, p1, 0)
    scores(qi, diag_mask)
    for h in range(nh):
        m_ref[h] = jnp.broadcast_to(jnp.max(m_ref[h], axis=-1, keepdims=True), (tq, LANES))

    def p2(kt, carry):
        ks = pl.multiple_of(kt * tq, tq)
        for h in range(nh):
            v_ref = va_ref if h % 2 == 0 else vb_ref
            pair = h // 2
            mb = m_ref[h]
            p_lo = jnp.exp2(s_ref[h, :, pl.ds(ks, LANES)] - mb)
            p_hi = jnp.exp2(s_ref[h, :, pl.ds(ks + LANES, LANES)] - mb)
            l_ref[h] += p_lo + p_hi
            p = jnp.concatenate([p_lo, p_hi], axis=1).astype(BF16)
            acc_ref[h] += _dot(p, v_ref[pl.ds(ks, tq), pair * LANES:(pair + 1) * LANES])
        return carry

    lax.fori_loop(0, qi + 1, p2, 0)
    for pair in range(nh // 2):
        he, ho = 2 * pair, 2 * pair + 1
        o = (acc_ref[he] / jnp.sum(l_ref[he], axis=-1, keepdims=True)
             + acc_ref[ho] / jnp.sum(l_ref[ho], axis=-1, keepdims=True))
        o_ref[:, pair * LANES:(pair + 1) * LANES] = o.astype(BF16)


def _attn_prompt(q, k, va, vb, batch, seq):
    tq = ATTN_TILE
    nq = seq // tq
    nh = MLA_HEADS
    return pl.pallas_call(
        _attn_prompt_kernel,
        grid=(batch, nq),
        in_specs=[
            pl.BlockSpec((tq, QK_PAD), lambda b, i: (b * nq + i, 0)),
            pl.BlockSpec((seq, QK_PAD), lambda b, i: (b, 0)),
            pl.BlockSpec((seq, 512), lambda b, i: (b, 0)),
            pl.BlockSpec((seq, 512), lambda b, i: (b, 0)),
        ],
        out_specs=pl.BlockSpec((tq, 512), lambda b, i: (b * nq + i, 0)),
        out_shape=jax.ShapeDtypeStruct((batch * seq, 512), BF16),
        scratch_shapes=[pltpu.VMEM((nh, tq, seq), F32), pltpu.VMEM((nh, tq, LANES), F32),
                        pltpu.VMEM((nh, tq, LANES), F32), pltpu.VMEM((nh, tq, LANES), F32)],
        compiler_params=_cparams(("parallel", "arbitrary")),
        name="attn_prompt",
    )(q, k, va, vb)


SAMPLE_KEY_TILE = 1024


def _attn_sample_kernel(q_ref, ckvn_ref, krn_ref, cc_ref, ck_ref, wukt_ref, psel_ref, wva_ref, wvb_ref,
                        o_ref, s_ref):
    ln = q_ref.shape[0]
    past = cc_ref.shape[0]
    nh = MLA_HEADS
    qlat, qrope = [], []
    for h in range(nh):
        qh = q_ref[:, h * HEAD_PAD:(h + 1) * HEAD_PAD]
        qlat.append(_dot(qh, wukt_ref[h]).astype(BF16))
        qrope.append(_dot(qh, psel_ref[...]).astype(BF16))
    qlat = jnp.concatenate(qlat, axis=0)
    qrope = jnp.concatenate(qrope, axis=0)

    ckv_new = ckvn_ref[...].astype(BF16)
    kr_new = krn_ref[...].astype(BF16)
    s_new = (_dot_nt(qlat, ckv_new) + _dot_nt(qrope, kr_new)) * ATTN_SCALE
    m = jnp.max(s_new, axis=-1, keepdims=True)
    tk = SAMPLE_KEY_TILE
    for kt in range(past // tk):
        rows = slice(kt * tk, (kt + 1) * tk)
        s = (_dot_nt(qlat, cc_ref[rows, :].astype(BF16)) + _dot_nt(qrope, ck_ref[rows, :].astype(BF16))) * ATTN_SCALE
        s_ref[:, rows] = s
        m = jnp.maximum(m, jnp.max(s, axis=-1, keepdims=True))
    p_new = jnp.exp(s_new - m)
    l = jnp.sum(p_new, axis=-1, keepdims=True)
    lat = _dot(p_new.astype(BF16), ckv_new)
    for kt in range(past // tk):
        rows = slice(kt * tk, (kt + 1) * tk)
        p = jnp.exp(s_ref[:, rows] - m)
        l = l + jnp.sum(p, axis=-1, keepdims=True)
        lat = lat + _dot(p.astype(BF16), cc_ref[rows, :].astype(BF16))
    lat = (lat / l).astype(BF16)
    for pair in range(nh // 2):
        he, ho = 2 * pair, 2 * pair + 1
        o = _dot(lat[he * ln:(he + 1) * ln], wva_ref[pair]) + _dot(lat[ho * ln:(ho + 1) * ln], wvb_ref[pair])
        o_ref[:, pair * LANES:(pair + 1) * LANES] = o.astype(BF16)


def _attn_sample(q, ckv_new, kr_new, cache_ckv, cache_kr, lw, batch, ln):
    past = cache_ckv.shape[1]
    nh = MLA_HEADS
    return pl.pallas_call(
        _attn_sample_kernel,
        grid=(batch,),
        in_specs=[
            pl.BlockSpec((ln, QK_PAD), lambda b: (b, 0)),
            pl.BlockSpec((ln, KV_LORA), lambda b: (b, 0)),
            pl.BlockSpec((ln, ROPE_DIM), lambda b: (b, 0)),
            pl.BlockSpec((None, past, KV_LORA), lambda b: (b, 0, 0)),
            pl.BlockSpec((None, past, ROPE_DIM), lambda b: (b, 0, 0)),
            _const_spec((nh, HEAD_PAD, KV_LORA)),
            _const_spec((HEAD_PAD, ROPE_DIM)),
            _const_spec((nh // 2, KV_LORA, LANES)),
            _const_spec((nh // 2, KV_LORA, LANES)),
        ],
        out_specs=pl.BlockSpec((ln, 512), lambda b: (b, 0)),
        out_shape=jax.ShapeDtypeStruct((batch * ln, 512), BF16),
        scratch_shapes=[pltpu.VMEM((nh * ln, past), F32)],
        compiler_params=_cparams(("parallel",)),
        name="attn_sample",
    )(q, ckv_new, kr_new, cache_ckv, cache_kr, lw["wuk_t"], lw["p_sel"], lw["wuv_a3"], lw["wuv_b3"])


def _hgrn_kernel(*refs, layer, chunk, has_init):
    if has_init:
        f_ref, q_ref, v_ref, g_ref, lb_ref, nrm_ref, s0_ref, o_ref, sout_ref, st_ref = refs
    else:
        f_ref, q_ref, v_ref, g_ref, lb_ref, nrm_ref, o_ref, sout_ref, st_ref = refs
        s0_ref = None
    tt = f_ref.shape[0]
    c = chunk
    ti = pl.program_id(1)

    @pl.when(ti == 0)
    def _():
        for h in range(HG_HEADS):
            if has_init:
                st_ref[h] = s0_ref[h].T
            else:
                st_ref[h] = jnp.zeros((HG_DV, HG_DK), F32)

    lbp = lb_ref[...]
    e = jnp.exp(lbp - jnp.max(lbp, axis=0, keepdims=True))
    tot = jnp.sum(e, axis=0, keepdims=True)
    part = jnp.zeros_like(tot)
    for j in range(1, layer + 1):
        part = part + e[j:j + 1]
    lb = part / tot
    log_lb = jnp.log(lb)
    log_1m = jnp.log1p(-lb)
    one_m = 1.0 - lb

    rid = lax.broadcasted_iota(jnp.int32, (c, 1), 0)
    rr = lax.broadcasted_iota(jnp.int32, (c, c), 0)
    cc = lax.broadcasted_iota(jnp.int32, (c, c), 1)
    tri = (cc <= rr).astype(BF16)
    base_mask = jnp.logical_and(_div_pow2(rr, HG_SUB) == _div_pow2(cc, HG_SUB), cc <= rr)

    for ci in range(tt // c):
        rows = slice(ci * c, (ci + 1) * c)
        z = f_ref[rows, :]
        ez = jnp.exp(-jnp.abs(z))
        rz = 1.0 / (1.0 + ez)
        log_sig = jnp.minimum(z, 0.0) - jnp.log1p(ez)
        kk = one_m * jnp.where(z >= 0, ez * rz, rz)
        y = log_1m + log_sig
        logf = jnp.maximum(log_lb, y) + jnp.log1p(jnp.exp(-jnp.abs(log_lb - y)))
        qp = q_ref[rows, :]
        qq = qp * _sigmoid(qp)

        hi = logf.astype(BF16)
        lo = (logf - hi.astype(F32)).astype(BF16)
        b = _dot(tri, hi) + _dot(tri, lo)
        bend = b[c - 1:c, :]

        qs, ks, masks = [], [], []
        g = c
        while g > HG_SUB:
            half = g // 2
            bref = b[half - 1:half, :]
            for gi in range(1, c // g):
                bref = jnp.where(rid >= gi * g, b[gi * g + half - 1:gi * g + half, :], bref)
            x = jnp.exp(-jnp.abs(b - bref))
            right = jnp.bitwise_and(rid, g - 1) >= half
            qs.append(jnp.where(right, qq * x, 0.0).astype(BF16))
            ks.append(jnp.where(right, 0.0, kk * x).astype(BF16))
            masks.append(None if g == c else (_div_pow2(rr, g) == _div_pow2(cc, g)))
            g = half
        bst = jnp.zeros_like(bend)
        for gi in range(1, c // HG_SUB):
            bst = jnp.where(rid >= gi * HG_SUB, b[gi * HG_SUB - 1:gi * HG_SUB, :], bst)
        e0 = b - bst
        qs.append((qq * jnp.exp(e0)).astype(BF16))
        ks.append((kk * jnp.exp(-e0)).astype(BF16))
        masks.append(base_mask)

        q_in = (qq * jnp.exp(b)).astype(BF16)
        k_out = (kk * jnp.exp(bend - b)).astype(BF16)
        dec = jnp.exp(bend)
        gp = g_ref[rows, :]
        gate = gp * _sigmoid(gp) * nrm_ref[...]

        for h in range(HG_HEADS):
            hs = slice(h * HG_DK, (h + 1) * HG_DK)
            a = None
            for ql, kl, mk in zip(qs, ks, masks):
                al = _dot_nt(ql[:, hs], kl[:, hs])
                if mk is not None:
                    al = jnp.where(mk, al, 0.0)
                a = al if a is None else a + al
            vh = v_ref[rows, hs]
            st = st_ref[h]
            o = _dot_nt(q_in[:, hs], st.astype(BF16)) + _dot(a.astype(BF16), vh)
            o = o * lax.rsqrt(jnp.mean(o * o, axis=-1, keepdims=True) + EPS) * gate[:, hs]
            o_ref[rows, hs] = o.astype(BF16)
            st_ref[h] = st * dec[:, hs] + _dot_tn(vh, k_out[:, hs])

    @pl.when(ti == pl.num_programs(1) - 1)
    def _():
        for h in range(HG_HEADS):
            sout_ref[h] = st_ref[h].T


def _hgrn(f_pre, q_pre, v, g_pre, hg_lb, hg_norm, s0, layer, batch, ln):
    tt = min(ln, 256)
    chunk = min(tt, CHUNK)
    nt = ln // tt
    has_init = s0 is not None
    depth = hg_lb.shape[0]
    row = lambda b, t: (b * nt + t, 0)
    st_spec = pl.BlockSpec((None, HG_HEADS, HG_DK, HG_DV), lambda b, t: (b, 0, 0, 0))
    in_specs = [
        pl.BlockSpec((tt, HG_FDIM), row), pl.BlockSpec((tt, HG_FDIM), row),
        pl.BlockSpec((tt, HG_IDIM), row), pl.BlockSpec((tt, HG_IDIM), row),
        _const_spec((depth, HG_FDIM)), _const_spec((1, HG_IDIM)),
    ]
    args = [f_pre, q_pre, v, g_pre, hg_lb, hg_norm]
    if has_init:
        in_specs.append(st_spec)
        args.append(s0)
    return pl.pallas_call(
        functools.partial(_hgrn_kernel, layer=layer, chunk=chunk, has_init=has_init),
        grid=(batch, nt),
        in_specs=in_specs,
        out_specs=[pl.BlockSpec((tt, HG_IDIM), row), st_spec],
        out_shape=[jax.ShapeDtypeStruct((batch * ln, HG_IDIM), BF16),
                   jax.ShapeDtypeStruct((batch, HG_HEADS, HG_DK, HG_DV), F32)],
        scratch_shapes=[pltpu.VMEM((HG_HEADS, HG_DV, HG_DK), F32)],
        compiler_params=_cparams(("parallel", "arbitrary")),
        name="hgrn",
    )(*args)


ROUTE_GROUP_LANE = N_EXPERTS


def _route(logits, cnt_ref):
    lane_i = lax.broadcasted_iota(jnp.int32, logits.shape, 1)
    lane = lane_i.astype(F32)
    big = float(LANES)
    neg = -jnp.inf
    is_g = jnp.logical_and(lane_i >= ROUTE_GROUP_LANE, lane_i < ROUTE_GROUP_LANE + N_GROUPS)
    gl = jnp.where(is_g, logits, neg)
    gmax = jnp.max(gl, axis=-1, keepdims=True)
    gidx = jnp.min(jnp.where(gl == gmax, lane - ROUTE_GROUP_LANE, big), axis=-1, keepdims=True)
    g_top = 1.0 / jnp.sum(jnp.exp(gl - gmax), axis=-1, keepdims=True)
    lane_group = _div_pow2(lane_i, EXPERTS_PER_GROUP).astype(F32)
    in_group = jnp.logical_and(lane_i < N_EXPERTS, lane_group == gidx)
    el = jnp.where(in_group, logits, neg)
    m1 = jnp.max(el, axis=-1, keepdims=True)
    i1 = jnp.min(jnp.where(el == m1, lane, big), axis=-1, keepdims=True)
    el2 = jnp.where(lane == i1, neg, el)
    m2 = jnp.max(el2, axis=-1, keepdims=True)
    i2 = jnp.min(jnp.where(el2 == m2, lane, big), axis=-1, keepdims=True)
    r = jnp.exp(m2 - m1)
    w1 = g_top / (1.0 + r)
    w2 = g_top * r / (1.0 + r)
    tm = logits.shape[0]
    hit = jnp.logical_or(lane == i1, lane == i2)
    rr = lax.broadcasted_iota(jnp.int32, (tm, tm), 0)
    cc = lax.broadcasted_iota(jnp.int32, (tm, tm), 1)
    before = _dot((cc < rr).astype(BF16), jnp.where(hit, 1.0, 0.0).astype(BF16)) + cnt_ref[...]
    r1 = jnp.sum(jnp.where(lane == i1, before, 0.0), axis=-1, keepdims=True)
    r2 = jnp.sum(jnp.where(lane == i2, before, 0.0), axis=-1, keepdims=True)
    cnt_ref[...] += jnp.sum(jnp.where(hit, 1.0, 0.0), axis=0, keepdims=True)
    out = jnp.where(lane_i == 0, i1, 0.0)
    out = jnp.where(lane_i == 1, i2, out)
    out = jnp.where(lane_i == 2, w1, out)
    out = jnp.where(lane_i == 3, w2, out)
    out = jnp.where(lane_i == 4, r1, out)
    out = jnp.where(lane_i == 5, r2, out)
    return out


def _post_kernel(x_ref, a_ref, o_ref, ple_ref, wg_ref, wa_ref, wb_ref, wo_ref, g1_ref, b1_ref,
                 wr_ref, wpg_ref, wp_ref, x1_ref, base_ref, route_ref, cnt_ref, *, alpha):
    @pl.when(pl.program_id(0) == 0)
    def _():
        cnt_ref[...] = jnp.zeros_like(cnt_ref)

    x = x_ref[...]
    xb = x.astype(BF16)
    ya = _dot(a_ref[...], wa_ref[...])
    merged = _sigmoid(_dot(xb, wg_ref[:, 0:D_MODEL])) * ya
    yb = _dot(o_ref[...], wb_ref[...])
    merged = merged + _sigmoid(_dot(xb, wg_ref[:, D_MODEL:2 * D_MODEL])) * yb
    mix = _dot(merged.astype(BF16), wo_ref[...])
    x1 = _layernorm(alpha * x + mix, g1_ref[...], b1_ref[...])
    _to_token_tiles(x1_ref, x1)
    x1b = x1.astype(BF16)
    route_ref[...] = _route(_dot(x1b, wr_ref[...]), cnt_ref)
    ple = _dot(ple_ref[...].astype(BF16), wp_ref[...])
    base_ref[...] = alpha * x1 + _sigmoid(_dot(x1b, wpg_ref[...])) * ple


def _post(xn, a, o, ple, lw, alpha):
    t = xn.shape[0]
    tm = TOKEN_TILE
    row = lambda i: (i, 0)
    d = D_MODEL
    return pl.pallas_call(
        functools.partial(_post_kernel, alpha=alpha),
        grid=(t // tm,),
        in_specs=[
            pl.BlockSpec((tm, d), row), pl.BlockSpec((tm, 512), row), pl.BlockSpec((tm, HG_IDIM), row),
            pl.BlockSpec((tm, PLE_DIM), row),
            _const_spec((d, 2 * d)), _const_spec((512, d)), _const_spec((HG_IDIM, d)), _const_spec((d, d)),
            _const_spec((1, d)), _const_spec((1, d)),
            _const_spec((d, LANES)), _const_spec((d, d)), _const_spec((PLE_DIM, d)),
        ],
        out_specs=[pl.BlockSpec((tm * TILE_ROWS, LANES), row), pl.BlockSpec((tm, d), row),
                   pl.BlockSpec((tm, LANES), row), pl.BlockSpec((1, LANES), lambda i: (0, 0))],
        out_shape=[jax.ShapeDtypeStruct((t * TILE_ROWS, LANES), F32), jax.ShapeDtypeStruct((t, d), F32),
                   jax.ShapeDtypeStruct((t, LANES), F32), jax.ShapeDtypeStruct((1, LANES), F32)],
        compiler_params=_cparams(("arbitrary",)),
        name="post",
    )(xn, a, o, ple, lw["w_gate"], lw["w_br_a"], lw["w_br_b"], lw["w_out"], lw["ln1_g"], lw["ln1_b"],
      lw["w_route"], lw["w_ple_gate"], lw["w_ple"])


TILE_ROWS = D_MODEL // LANES


def _to_token_tiles(ref, x):
    n = x.shape[0]
    for c in range(TILE_ROWS):
        ref[pl.ds(c, n, stride=TILE_ROWS), :] = x[:, c * LANES:(c + 1) * LANES]


def _from_token_tiles(ref, first_token, n):
    return jnp.concatenate(
        [ref[pl.ds(first_token * TILE_ROWS + c, n, stride=TILE_ROWS), :] for c in range(TILE_ROWS)], axis=1)


def _tile_copy(idx_ref, j, src_hbm, dst, dst_token, sem):
    r = pl.multiple_of(idx_ref[0, j], TILE_ROWS)
    d = dst_token * TILE_ROWS
    if not isinstance(d, int):
        d = pl.multiple_of(d, TILE_ROWS)
    return pltpu.make_async_copy(src_hbm.at[pl.ds(r, TILE_ROWS)], dst.at[pl.ds(d, TILE_ROWS)], sem)


def _issue_tiles_loop(idx_ref, src_hbm, dst, sem, n_tokens, dst_off=0):
    def body(j, carry):
        _tile_copy(idx_ref, j, src_hbm, dst, dst_off + j, sem).start()
        return carry

    lax.fori_loop(0, n_tokens, body, 0, unroll=8)


def _issue_tiles_static(idx_ref, src_hbm, dst, sem, lo, hi, dst_off=0):
    for j in range(lo, hi):
        _tile_copy(idx_ref, j, src_hbm, dst, dst_off + j, sem).start()


def _wait_tiles(src_hbm, dst, sem):
    pltpu.make_async_copy(src_hbm.at[pl.ds(0, dst.shape[0])], dst, sem).wait()


def _idx_spec(tile, n_blocks, ahead):
    if ahead is None:
        imap = lambda i, *_: (0, 0, 0)
    else:
        imap = lambda i, *_: (jnp.minimum(i + ahead, n_blocks - 1), 0, 0)
    return pl.BlockSpec((None, 1, tile), imap, memory_space=pltpu.SMEM)


MOE_SLOTS = 3


def _moe_kernel(te_ref, idx0_ref, idx1_ref, idx2_ref, x_hbm, w1_ref, w3_ref, w2_ref, y_ref, xbuf, sem):
    i = pl.program_id(0)
    n = pl.num_programs(0)
    tm = MOE_TILE
    slot = i % MOE_SLOTS

    @pl.when(i == 0)
    def _():
        _issue_tiles_loop(idx0_ref, x_hbm, xbuf.at[0], sem.at[0], tm)
        _issue_tiles_loop(idx1_ref, x_hbm, xbuf.at[1], sem.at[1], tm)

    _wait_tiles(x_hbm, xbuf.at[slot], sem.at[slot])
    xb = _from_token_tiles(xbuf.at[slot], 0, tm).astype(BF16)
    ahead = (i + 2) % MOE_SLOTS
    nxt = xbuf.at[ahead]
    nsem = sem.at[ahead]
    group = tm // 4
    hs = []
    half = D_EXPERT // 2
    for c in range(2):
        cs = slice(c * half, (c + 1) * half)
        h1 = _dot(xb, w1_ref[:, cs])
        _issue_tiles_static(idx2_ref, x_hbm, nxt, nsem, 2 * c * group, (2 * c + 1) * group)
        h3 = _dot(xb, w3_ref[:, cs])
        _issue_tiles_static(idx2_ref, x_hbm, nxt, nsem, (2 * c + 1) * group, (2 * c + 2) * group)
        hs.append((h1 * _sigmoid(h1) * h3).astype(BF16))
    _to_token_tiles(y_ref, _dot(jnp.concatenate(hs, axis=1), w2_ref[...]))

    @pl.when(i == n - 1)
    def _():
        _wait_tiles(x_hbm, xbuf.at[(i + 1) % MOE_SLOTS], sem.at[(i + 1) % MOE_SLOTS])
        _wait_tiles(x_hbm, nxt, nsem)


def _moe(x_tiles, src, tile_e, w1, w3, w2):
    tm = MOE_TILE
    n_tiles = src.shape[0]
    d = D_MODEL
    grid_spec = pltpu.PrefetchScalarGridSpec(
        num_scalar_prefetch=1,
        grid=(n_tiles,),
        in_specs=[
            _idx_spec(tm, n_tiles, None), _idx_spec(tm, n_tiles, 1), _idx_spec(tm, n_tiles, 2),
            pl.BlockSpec(memory_space=pl.ANY),
            pl.BlockSpec((None, d, D_EXPERT), lambda i, te: (te[i], 0, 0)),
            pl.BlockSpec((None, d, D_EXPERT), lambda i, te: (te[i], 0, 0)),
            pl.BlockSpec((None, D_EXPERT, d), lambda i, te: (te[i], 0, 0)),
        ],
        out_specs=pl.BlockSpec((tm * TILE_ROWS, LANES), lambda i, te: (i, 0)),
        scratch_shapes=[pltpu.VMEM((MOE_SLOTS, tm * TILE_ROWS, LANES), F32), pltpu.SemaphoreType.DMA((MOE_SLOTS,))],
    )
    src3 = src.reshape(n_tiles, 1, tm)
    return pl.pallas_call(
        _moe_kernel,
        grid_spec=grid_spec,
        out_shape=jax.ShapeDtypeStruct((n_tiles * tm * TILE_ROWS, LANES), F32),
        compiler_params=_cparams(("arbitrary",)),
        name="moe_experts",
    )(tile_e, src3, src3, src3, x_tiles, w1, w3, w2)


def _dispatch_plan(route, cnt, n_tiles):
    t = route.shape[0]
    a = t * TOP_K
    tm = MOE_TILE
    eid = jnp.arange(N_EXPERTS, dtype=jnp.int32)
    counts = cnt[0, 0:N_EXPERTS].astype(jnp.int32)
    tiles_e = (counts + tm - 1) // tm
    tile_end = jnp.cumsum(tiles_e)
    pstart = (tile_end - tiles_e) * tm
    e = route[:, 0:TOP_K].astype(jnp.int32)
    rank = route[:, 4:4 + TOP_K].astype(jnp.int32)
    pos = rank + jnp.sum(jnp.where(e[:, :, None] == eid[None, None, :], pstart[None, None, :], 0), axis=-1)
    pad_e = tiles_e * tm - counts
    cpad = jnp.cumsum(pad_e)
    k = jnp.arange(n_tiles * tm - a, dtype=jnp.int32)
    ke = jnp.sum((cpad[None, :] <= k[:, None]).astype(jnp.int32), axis=1)
    inside = k + jnp.sum(jnp.where(jnp.minimum(ke, N_EXPERTS - 1)[:, None] == eid[None, :],
                                   (pstart + counts - (cpad - pad_e))[None, :], 0), axis=1)
    padpos = jnp.where(ke < N_EXPERTS, inside, tile_end[-1] * tm + (k - cpad[-1]))
    keys = jnp.concatenate([pos.reshape(a), padpos])
    vals = jnp.concatenate([jnp.repeat(jnp.arange(t, dtype=jnp.int32), TOP_K), jnp.zeros_like(k)])
    _, src = lax.sort_key_val(keys, vals)
    jt = jnp.arange(n_tiles, dtype=jnp.int32)
    tile_e = jnp.minimum(jnp.sum((tile_end[None, :] <= jt[:, None]).astype(jnp.int32), axis=1), N_EXPERTS - 1)
    return (src * TILE_ROWS).reshape(n_tiles, tm), pos.reshape(a) * TILE_ROWS, tile_e.astype(jnp.int32)


def _final_kernel(p0c_ref, p1c_ref, p0n_ref, p1n_ref, base_ref, route_ref, g_ref, b_ref, y_hbm, o_ref, ybuf, sem):
    i = pl.program_id(0)
    n = pl.num_programs(0)
    tm = FINAL_TILE
    slot = i % 2

    @pl.when(i == 0)
    def _():
        _issue_tiles_loop(p0c_ref, y_hbm, ybuf.at[0], sem.at[0], tm, 0)
        _issue_tiles_loop(p1c_ref, y_hbm, ybuf.at[0], sem.at[0], tm, tm)

    _wait_tiles(y_hbm, ybuf.at[slot], sem.at[slot])
    nxt = ybuf.at[1 - slot]
    nsem = sem.at[1 - slot]
    _issue_tiles_static(p0n_ref, y_hbm, nxt, nsem, 0, tm, 0)
    _issue_tiles_static(p1n_ref, y_hbm, nxt, nsem, 0, tm, tm)
    w = route_ref[...]
    cur = ybuf.at[slot]
    h = base_ref[...] + w[:, 2:3] * _from_token_tiles(cur, 0, tm) + w[:, 3:4] * _from_token_tiles(cur, tm, tm)
    o_ref[...] = _layernorm(h, g_ref[...], b_ref[...])

    @pl.when(i == n - 1)
    def _():
        _wait_tiles(y_hbm, nxt, nsem)


def _final(base, y_tiles, pos, route, g, b):
    t, d = base.shape
    tm = FINAL_TILE
    nb = t // tm
    row = lambda i: (i, 0)
    pos2 = pos.reshape(t, TOP_K)
    p0 = pos2[:, 0].reshape(nb, 1, tm)
    p1 = pos2[:, 1].reshape(nb, 1, tm)
    first, nxt = _idx_spec(tm, nb, None), _idx_spec(tm, nb, 1)
    return pl.pallas_call(
        _final_kernel,
        grid=(nb,),
        in_specs=[first, first, nxt, nxt,
                  pl.BlockSpec((tm, d), row), pl.BlockSpec((tm, LANES), row),
                  _const_spec((1, d)), _const_spec((1, d)),
                  pl.BlockSpec(memory_space=pl.ANY)],
        out_specs=pl.BlockSpec((tm, d), row),
        out_shape=jax.ShapeDtypeStruct((t, d), F32),
        scratch_shapes=[pltpu.VMEM((2, 2 * tm * TILE_ROWS, LANES), F32), pltpu.SemaphoreType.DMA((2,))],
        compiler_params=_cparams(("arbitrary",)),
        name="final_ln",
    )(p0, p1, p0, p1, base, route, g, b, y_tiles)


def _rot_cols(w):
    half = ROPE_DIM // 2
    return jnp.concatenate([-w[..., half:], w[..., :half]], axis=-1)


def _layer_weights(i, w_in, q_norm, w_uq, kv_norm, w_uk, w_uv, hg_norm, w_br_a, w_br_b, w_out, ln1_g, ln1_b,
                   w_rg, w_re, w_e_gate, w_e_up, w_e_down, w_ple, w_ple_gate, ln2_g, ln2_b):
    d = D_MODEL
    nh = MLA_HEADS
    win = w_in[i]
    c_kr = Q_LORA + KV_LORA
    c_f = c_kr + ROPE_DIM
    c_ga = c_f + N_HG
    w_kr = win[:, c_kr:c_f]
    zpad = jnp.zeros((d, LANES - ROPE_DIM), F32)
    wa = jnp.concatenate([win[:, 0:c_kr], win[:, c_f:c_ga], w_kr, zpad, _rot_cols(w_kr), zpad], axis=1)

    wq3 = w_uq[i].reshape(Q_LORA, nh, NOPE_DIM + ROPE_DIM)
    zq = jnp.zeros((Q_LORA, nh, HEAD_PAD - NOPE_DIM - ROPE_DIM), F32)
    wq = jnp.concatenate([wq3, zq], axis=-1).reshape(Q_LORA, QK_PAD)
    wq_rot = jnp.concatenate([jnp.zeros((Q_LORA, nh, NOPE_DIM), F32), _rot_cols(wq3[..., NOPE_DIM:]), zq],
                             axis=-1).reshape(Q_LORA, QK_PAD)

    wuk = w_uk[i]
    wuv = w_uv[i]
    k_nope = jnp.concatenate([wuk, jnp.zeros((KV_LORA, nh, HEAD_PAD - NOPE_DIM), F32)], axis=-1)
    eye = jnp.eye(ROPE_DIM, dtype=F32)
    k_rope = jnp.concatenate([jnp.zeros((ROPE_DIM, NOPE_DIM), F32), eye,
                              jnp.zeros((ROPE_DIM, HEAD_PAD - NOPE_DIM - ROPE_DIM), F32)], axis=-1)
    k_rope = jnp.concatenate([jnp.tile(k_rope[:, None, :], (1, nh, 1)).reshape(ROPE_DIM, QK_PAD),
                              jnp.zeros((LANES - ROPE_DIM, QK_PAD), F32)], axis=0)
    zv = jnp.zeros((KV_LORA, nh // 2, V_DIM), F32)
    wuv_a3 = jnp.concatenate([wuv[:, 0::2, :], zv], axis=-1)
    wuv_b3 = jnp.concatenate([zv, wuv[:, 1::2, :]], axis=-1)
    wkv_top = jnp.concatenate([k_nope.reshape(KV_LORA, QK_PAD), wuv_a3.reshape(KV_LORA, 512),
                               wuv_b3.reshape(KV_LORA, 512)], axis=1)
    wkv_bot = jnp.concatenate([k_rope, jnp.zeros((LANES, QK_PAD), F32)], axis=1)
    wkv = jnp.concatenate([wkv_top, wkv_bot], axis=0)

    wuk_t = jnp.concatenate([jnp.transpose(wuk, (1, 2, 0)),
                             jnp.zeros((nh, HEAD_PAD - NOPE_DIM, KV_LORA), F32)], axis=1)
    p_sel = jnp.concatenate([jnp.zeros((NOPE_DIM, ROPE_DIM), F32), eye,
                             jnp.zeros((HEAD_PAD - NOPE_DIM - ROPE_DIM, ROPE_DIM), F32)], axis=0)

    w_route = jnp.concatenate([w_re[i], w_rg[i], jnp.zeros((d, LANES - N_EXPERTS - N_GROUPS), F32)], axis=1)
    bf = lambda w: w.astype(BF16)
    return {
        "wa": bf(wa), "q_norm": q_norm[i].reshape(1, -1), "kv_norm": kv_norm[i].reshape(1, -1),
        "wq": bf(wq), "wq_rot": bf(wq_rot), "wkv": bf(wkv),
        "wuk_t": bf(wuk_t), "p_sel": bf(p_sel),
        "wuv_a3": bf(jnp.transpose(wuv_a3, (1, 0, 2))), "wuv_b3": bf(jnp.transpose(wuv_b3, (1, 0, 2))),
        "hg_norm": hg_norm[i].reshape(1, -1),
        "w_gate": bf(win[:, c_ga:c_ga + 2 * d]), "w_br_a": bf(w_br_a[i]), "w_br_b": bf(w_br_b[i]),
        "w_out": bf(w_out[i]), "ln1_g": ln1_g[i].reshape(1, -1), "ln1_b": ln1_b[i].reshape(1, -1),
        "w_route": bf(w_route), "w_ple_gate": bf(w_ple_gate[i]), "w_ple": bf(w_ple[i]),
        "w1": bf(w_e_gate[i]), "w3": bf(w_e_up[i]), "w2": bf(w_e_down[i]),
        "ln2_g": ln2_g[i].reshape(1, -1), "ln2_b": ln2_b[i].reshape(1, -1),
    }


def _rope_table(pos, rows):
    half = ROPE_DIM // 2
    inv = ROPE_THETA ** (-jnp.arange(half, dtype=F32) / half)
    ang = pos.astype(F32)[:, None] * inv[None, :]
    c2 = jnp.concatenate([jnp.cos(ang), jnp.cos(ang)], axis=1)
    s2 = jnp.concatenate([jnp.sin(ang), jnp.sin(ang)], axis=1)
    n = pos.shape[0]
    ones = jnp.ones((n, NOPE_DIM), F32)
    z = lambda w: jnp.zeros((n, w), F32)
    tab = jnp.concatenate([
        ones, c2, z(HEAD_PAD - NOPE_DIM - ROPE_DIM),
        z(NOPE_DIM), s2, z(HEAD_PAD - NOPE_DIM - ROPE_DIM),
        c2, z(LANES - ROPE_DIM),
        s2, z(LANES - ROPE_DIM)], axis=1)
    if rows > n:
        tab = jnp.tile(tab, (rows // n, 1))
    return tab


def _trunk_layer(i, xn, ple, tab, lw, hg_lb, batch, ln, cache, alpha):
    q, k, va, vb, ckv, kr, f_pre, q_pre, hv, g_pre = _proj(xn, tab, lw)
    if cache is None:
        a = _attn_prompt(q, k, va, vb, batch, ln)
        o, st = _hgrn(f_pre, q_pre, hv, g_pre, hg_lb, lw["hg_norm"], None, i, batch, ln)
    else:
        a = _attn_sample(q, ckv, kr, cache[0], cache[1], lw, batch, ln)
        o, st = _hgrn(f_pre, q_pre, hv, g_pre, hg_lb, lw["hg_norm"], cache[2], i, batch, ln)
    x1, base, route, cnt = _post(xn, a, o, ple, lw, alpha)

    t = xn.shape[0]
    n_tiles = (t * TOP_K) // MOE_TILE + N_EXPERTS
    src, pos, tile_e = _dispatch_plan(route, cnt, n_tiles)
    y = _moe(x1, src, tile_e, lw["w1"], lw["w3"], lw["w2"])
    x2 = _final(base, y, pos, route, lw["ln2_g"], lw["ln2_b"])
    return x2, ckv, kr, st


def kernel(x_prompt, x_sample, p_prompt, p_sample, cache_ckv, cache_krope, state_hgrn, ln0_g, ln0_b, w_in, q_norm, w_uq, kv_norm, w_uk, w_uv, hg_lb, hg_norm, w_br_a, w_br_b, w_out, ln1_g, ln1_b, w_router_group, w_router_expert, w_e_gate, w_e_up, w_e_down, w_ple, w_ple_gate, ln2_g, ln2_b):
    depth = w_in.shape[0]
    bp, sp, d = x_prompt.shape
    bs, ss, _ = x_sample.shape
    past = cache_ckv.shape[2]
    alpha = (2 * depth) ** 0.25

    xp = _ln_rows(x_prompt.reshape(bp * sp, d), ln0_g, ln0_b)
    xs = _ln_rows(x_sample.reshape(bs * ss, d), ln0_g, ln0_b)
    tab_p = _rope_table(jnp.arange(sp, dtype=jnp.int32), max(sp, TOKEN_TILE))
    tab_s = _rope_table(past + jnp.arange(ss, dtype=jnp.int32), max(ss, TOKEN_TILE))

    outs = [[] for _ in range(6)]
    for i in range(depth):
        lw = _layer_weights(i, w_in, q_norm, w_uq, kv_norm, w_uk, w_uv, hg_norm, w_br_a, w_br_b, w_out,
                            ln1_g, ln1_b, w_router_group, w_router_expert, w_e_gate, w_e_up, w_e_down,
                            w_ple, w_ple_gate, ln2_g, ln2_b)
        xp, c, r, s = _trunk_layer(i, xp, p_prompt[i].reshape(bp * sp, -1), tab_p, lw, hg_lb, bp, sp, None, alpha)
        outs[0].append(c.reshape(bp, sp, -1))
        outs[1].append(r.reshape(bp, sp, -1))
        outs[2].append(s)
        cache = (cache_ckv[i], cache_krope[i], state_hgrn[i])
        xs, c, r, s = _trunk_layer(i, xs, p_sample[i].reshape(bs * ss, -1), tab_s, lw, hg_lb, bs, ss, cache, alpha)
        outs[3].append(c.reshape(bs, ss, -1))
        outs[4].append(r.reshape(bs, ss, -1))
        outs[5].append(s)
    return (xp.reshape(bp, sp, d), xs.reshape(bs, ss, d)) + tuple(jnp.stack(o) for o in outs)
```

```python
import functools

import numpy as np
import jax
import jax.numpy as jnp
from jax import lax
from jax.experimental import pallas as pl
from jax.experimental.pallas import tpu as pltpu

F32 = jnp.float32
BF16 = jnp.bfloat16

D_MODEL = 1024
CHUNK = 64
PLE_DIM = 256
MLA_HEADS = 8
NOPE_DIM = 64
ROPE_DIM = 32
V_DIM = 64
Q_LORA = 384
KV_LORA = 256
ROPE_THETA = 10000.0
ATTN_SCALE = (NOPE_DIM + ROPE_DIM) ** -0.5
HG_HEADS = 4
HG_DK = 128
HG_DV = 128
HG_FDIM = HG_HEADS * HG_DK
HG_IDIM = HG_HEADS * HG_DV
N_GROUPS = 4
EXPERTS_PER_GROUP = 8
N_EXPERTS = N_GROUPS * EXPERTS_PER_GROUP
TOP_K = 2
D_EXPERT = 512
EPS = 1e-6

LANES = 128
HEAD_PAD = LANES
QK_PAD = MLA_HEADS * HEAD_PAD
VMEM_LIMIT = 56 * 1024 * 1024

TOKEN_TILE = 512
ATTN_TILE = 256
MOE_TILE = 256
FINAL_TILE = 256
HG_SUB = 16


def _cparams(sem):
    return pltpu.CompilerParams(dimension_semantics=sem, vmem_limit_bytes=VMEM_LIMIT)


def _const_spec(shape):
    nd = len(shape)
    return pl.BlockSpec(shape, lambda *_: (0,) * nd, pipeline_mode=pl.Buffered(1))


def _dot(a, b):
    return jnp.dot(a, b, preferred_element_type=F32)


def _dot_nt(a, b):
    return lax.dot_general(a, b, (((1,), (1,)), ((), ())), preferred_element_type=F32)


def _dot_tn(a, b):
    return lax.dot_general(a, b, (((0,), (0,)), ((), ())), preferred_element_type=F32)


def _div_pow2(x, d):
    return jnp.right_shift(x, int(d).bit_length() - 1)


def _sigmoid(x):
    return 1.0 / (1.0 + jnp.exp(-x))


def _layernorm(x, g, b):
    mu = jnp.mean(x, axis=-1, keepdims=True)
    xc = x - mu
    var = jnp.mean(xc * xc, axis=-1, keepdims=True)
    return xc * lax.rsqrt(var + EPS) * g + b


def _rmsnorm(x, g):
    return x * lax.rsqrt(jnp.mean(x * x, axis=-1, keepdims=True) + EPS) * g


def _ln_kernel(x_ref, g_ref, b_ref, o_ref):
    o_ref[...] = _layernorm(x_ref[...], g_ref[...], b_ref[...])


def _ln_rows(x, g, b):
    t, d = x.shape
    tm = TOKEN_TILE
    return pl.pallas_call(
        _ln_kernel,
        grid=(t // tm,),
        in_specs=[pl.BlockSpec((tm, d), lambda i: (i, 0)), _const_spec((1, d)), _const_spec((1, d))],
        out_specs=pl.BlockSpec((tm, d), lambda i: (i, 0)),
        out_shape=jax.ShapeDtypeStruct((t, d), F32),
        compiler_params=_cparams(("parallel",)),
        name="ln0",
    )(x, g.reshape(1, d), b.reshape(1, d))


N_LAT = Q_LORA + KV_LORA
N_HG = 2 * HG_FDIM + 2 * HG_IDIM
COL_HG = N_LAT
COL_KR = N_LAT + N_HG
N_PROJ = COL_KR + 2 * LANES


def _proj_kernel(x_ref, tab_ref, wa_ref, qn_ref, kvn_ref, wq_ref, wqr_ref, wkv_ref,
                 q_ref, k_ref, va_ref, vb_ref, ckv_ref, kr_ref, f_ref, hq_ref, hv_ref, hg_ref):
    xb = x_ref[...].astype(BF16)
    cq_tab = tab_ref[:, 0:LANES]
    sq_tab = tab_ref[:, LANES:2 * LANES]
    ck_tab = tab_ref[:, 2 * LANES:3 * LANES]
    sk_tab = tab_ref[:, 3 * LANES:4 * LANES]

    lat = _dot(xb, wa_ref[:, 0:N_LAT])
    cqn = _rmsnorm(lat[:, 0:Q_LORA], qn_ref[...]).astype(BF16)
    ckvn = _rmsnorm(lat[:, Q_LORA:N_LAT], kvn_ref[...])
    ckv_ref[...] = ckvn

    qa = _dot(cqn, wq_ref[...])
    qb = _dot(cqn, wqr_ref[...])
    for h in range(MLA_HEADS):
        sl = slice(h * HEAD_PAD, (h + 1) * HEAD_PAD)
        q_ref[:, sl] = (qa[:, sl] * cq_tab + qb[:, sl] * sq_tab).astype(BF16)

    kr2 = _dot(xb, wa_ref[:, COL_KR:N_PROJ])
    kr = kr2[:, 0:LANES] * ck_tab + kr2[:, LANES:2 * LANES] * sk_tab
    kr_ref[...] = kr[:, 0:ROPE_DIM]

    kvin = jnp.concatenate([ckvn.astype(BF16), kr.astype(BF16)], axis=1)
    kv = _dot(kvin, wkv_ref[...])
    k_ref[...] = kv[:, 0:QK_PAD].astype(BF16)
    va_ref[...] = kv[:, QK_PAD:QK_PAD + 512].astype(BF16)
    vb_ref[...] = kv[:, QK_PAD + 512:QK_PAD + 1024].astype(BF16)

    f_ref[...] = _dot(xb, wa_ref[:, COL_HG:COL_HG + 512])
    hq_ref[...] = _dot(xb, wa_ref[:, COL_HG + 512:COL_HG + 1024])
    hv_ref[...] = _dot(xb, wa_ref[:, COL_HG + 1024:COL_HG + 1536]).astype(BF16)
    hg_ref[...] = _dot(xb, wa_ref[:, COL_HG + 1536:COL_HG + 2048])


def _proj(xn, tab, lw):
    t = xn.shape[0]
    tm = TOKEN_TILE
    ntab = tab.shape[0] // tm
    row = lambda i: (i, 0)
    outs = [
        (QK_PAD, BF16), (QK_PAD, BF16), (512, BF16), (512, BF16), (KV_LORA, F32), (ROPE_DIM, F32),
        (HG_FDIM, F32), (HG_FDIM, F32), (HG_IDIM, BF16), (HG_IDIM, F32),
    ]
    return pl.pallas_call(
        _proj_kernel,
        grid=(t // tm,),
        in_specs=[
            pl.BlockSpec((tm, D_MODEL), row),
            pl.BlockSpec((tm, 4 * LANES), lambda i: (i % ntab, 0)),
            _const_spec((D_MODEL, N_PROJ)),
            _const_spec((1, Q_LORA)), _const_spec((1, KV_LORA)),
            _const_spec((Q_LORA, QK_PAD)), _const_spec((Q_LORA, QK_PAD)),
            _const_spec((KV_LORA + LANES, 2 * QK_PAD)),
        ],
        out_specs=[pl.BlockSpec((tm, w), row) for w, _ in outs],
        out_shape=[jax.ShapeDtypeStruct((t, w), dt) for w, dt in outs],
        compiler_params=_cparams(("parallel",)),
        name="proj",
    )(xn, tab, lw["wa"], lw["q_norm"], lw["kv_norm"], lw["wq"], lw["wq_rot"], lw["wkv"])


LOG2E = 1.4426950408889634


def _attn_prompt_kernel(q_ref, k_ref, va_ref, vb_ref, o_ref, s_ref, m_ref, l_ref, acc_ref):
    tq = ATTN_TILE
    nh = MLA_HEADS
    qi = pl.program_id(1)
    rq = _div_pow2(lax.broadcasted_iota(jnp.int32, (tq, tq), 0), CHUNK)
    ck = _div_pow2(lax.broadcasted_iota(jnp.int32, (tq, tq), 1), CHUNK)
    diag_mask = ck <= rq
    c = ATTN_SCALE * LOG2E

    m_ref[...] = jnp.full(m_ref.shape, -jnp.inf, F32)
    l_ref[...] = jnp.zeros(l_ref.shape, F32)
    acc_ref[...] = jnp.zeros(acc_ref.shape, F32)

    def scores(kt, mask):
        ks = pl.multiple_of(kt * tq, tq)
        for h in range(nh):
            hs = slice(h * HEAD_PAD, (h + 1) * HEAD_PAD)
            s = _dot_nt(q_ref[:, hs], k_ref[pl.ds(ks, tq), hs]) * c
            if mask is not None:
                s = jnp.where(mask, s, -jnp.inf)
            s_ref[h, :, pl.ds(ks, tq)] = s
            m_ref[h] = jnp.maximum(m_ref[h], jnp.maximum(s[:, 0:LANES], s[:, LANES:2 * LANES]))

    def p1(kt, carry):
        scores(kt, None)
        return carry

    lax.fori_loop(0, qi, p1, 0)
    scores(qi, diag_mask)
    for h in range(nh):
        m_ref[h] = jnp.broadcast_to(jnp.max(m_ref[h], axis=-1, keepdims=True), (tq, LANES))

    def p2(kt, carry):
        ks = pl.multiple_of(kt * tq, tq)
        for h in range(nh):
            v_ref = va_ref if h % 2 == 0 else vb_ref
            pair = h // 2
            mb = m_ref[h]
            p_lo = jnp.exp2(s_ref[h, :, pl.ds(ks, LANES)] - mb)
            p_hi = jnp.exp2(s_ref[h, :, pl.ds(ks + LANES, LANES)] - mb)
            l_ref[h] += p_lo + p_hi
            p = jnp.concatenate([p_lo, p_hi], axis=1).astype(BF16)
            acc_ref[h] += _dot(p, v_ref[pl.ds(ks, tq), pair * LANES:(pair + 1) * LANES])
        return carry

    lax.fori_loop(0, qi + 1, p2, 0)
    for pair in range(nh // 2):
        he, ho = 2 * pair, 2 * pair + 1
        o = (acc_ref[he] / jnp.sum(l_ref[he], axis=-1, keepdims=True)
             + acc_ref[ho] / jnp.sum(l_ref[ho], axis=-1, keepdims=True))
        o_ref[:, pair * LANES:(pair + 1) * LANES] = o.astype(BF16)


def _attn_prompt(q, k, va, vb, batch, seq):
    tq = ATTN_TILE
    nq = seq // tq
    nh = MLA_HEADS
    return pl.pallas_call(
        _attn_prompt_kernel,
        grid=(batch, nq),
        in_specs=[
            pl.BlockSpec((tq, QK_PAD), lambda b, i: (b * nq + i, 0)),
            pl.BlockSpec((seq, QK_PAD), lambda b, i: (b, 0)),
            pl.BlockSpec((seq, 512), lambda b, i: (b, 0)),
            pl.BlockSpec((seq, 512), lambda b, i: (b, 0)),
        ],
        out_specs=pl.BlockSpec((tq, 512), lambda b, i: (b * nq + i, 0)),
        out_shape=jax.ShapeDtypeStruct((batch * seq, 512), BF16),
        scratch_shapes=[pltpu.VMEM((nh, tq, seq), F32), pltpu.VMEM((nh, tq, LANES), F32),
                        pltpu.VMEM((nh, tq, LANES), F32), pltpu.VMEM((nh, tq, LANES), F32)],
        compiler_params=_cparams(("parallel", "arbitrary")),
        name="attn_prompt",
    )(q, k, va, vb)


SAMPLE_KEY_TILE = 1024


def _attn_sample_kernel(q_ref, ckvn_ref, krn_ref, cc_ref, ck_ref, wukt_ref, psel_ref, wva_ref, wvb_ref,
                        o_ref, s_ref):
    ln = q_ref.shape[0]
    past = cc_ref.shape[0]
    nh = MLA_HEADS
    qlat, qrope = [], []
    for h in range(nh):
        qh = q_ref[:, h * HEAD_PAD:(h + 1) * HEAD_PAD]
        qlat.append(_dot(qh, wukt_ref[h]).astype(BF16))
        qrope.append(_dot(qh, psel_ref[...]).astype(BF16))
    qlat = jnp.concatenate(qlat, axis=0)
    qrope = jnp.concatenate(qrope, axis=0)

    ckv_new = ckvn_ref[...].astype(BF16)
    kr_new = krn_ref[...].astype(BF16)
    s_new = (_dot_nt(qlat, ckv_new) + _dot_nt(qrope, kr_new)) * ATTN_SCALE
    m = jnp.max(s_new, axis=-1, keepdims=True)
    tk = SAMPLE_KEY_TILE
    for kt in range(past // tk):
        rows = slice(kt * tk, (kt + 1) * tk)
        s = (_dot_nt(qlat, cc_ref[rows, :].astype(BF16)) + _dot_nt(qrope, ck_ref[rows, :].astype(BF16))) * ATTN_SCALE
        s_ref[:, rows] = s
        m = jnp.maximum(m, jnp.max(s, axis=-1, keepdims=True))
    p_new = jnp.exp(s_new - m)
    l = jnp.sum(p_new, axis=-1, keepdims=True)
    lat = _dot(p_new.astype(BF16), ckv_new)
    for kt in range(past // tk):
        rows = slice(kt * tk, (kt + 1) * tk)
        p = jnp.exp(s_ref[:, rows] - m)
        l = l + jnp.sum(p, axis=-1, keepdims=True)
        lat = lat + _dot(p.astype(BF16), cc_ref[rows, :].astype(BF16))
    lat = (lat / l).astype(BF16)
    for pair in range(nh // 2):
        he, ho = 2 * pair, 2 * pair + 1
        o = _dot(lat[he * ln:(he + 1) * ln], wva_ref[pair]) + _dot(lat[ho * ln:(ho + 1) * ln], wvb_ref[pair])
        o_ref[:, pair * LANES:(pair + 1) * LANES] = o.astype(BF16)


def _attn_sample(q, ckv_new, kr_new, cache_ckv, cache_kr, lw, batch, ln):
    past = cache_ckv.shape[1]
    nh = MLA_HEADS
    return pl.pallas_call(
        _attn_sample_kernel,
        grid=(batch,),
        in_specs=[
            pl.BlockSpec((ln, QK_PAD), lambda b: (b, 0)),
            pl.BlockSpec((ln, KV_LORA), lambda b: (b, 0)),
            pl.BlockSpec((ln, ROPE_DIM), lambda b: (b, 0)),
            pl.BlockSpec((None, past, KV_LORA), lambda b: (b, 0, 0)),
            pl.BlockSpec((None, past, ROPE_DIM), lambda b: (b, 0, 0)),
            _const_spec((nh, HEAD_PAD, KV_LORA)),
            _const_spec((HEAD_PAD, ROPE_DIM)),
            _const_spec((nh // 2, KV_LORA, LANES)),
            _const_spec((nh // 2, KV_LORA, LANES)),
        ],
        out_specs=pl.BlockSpec((ln, 512), lambda b: (b, 0)),
        out_shape=jax.ShapeDtypeStruct((batch * ln, 512), BF16),
        scratch_shapes=[pltpu.VMEM((nh * ln, past), F32)],
        compiler_params=_cparams(("parallel",)),
        name="attn_sample",
    )(q, ckv_new, kr_new, cache_ckv, cache_kr, lw["wuk_t"], lw["p_sel"], lw["wuv_a3"], lw["wuv_b3"])


def _hgrn_kernel(*refs, layer, chunk, has_init):
    if has_init:
        f_ref, q_ref, v_ref, g_ref, lb_ref, nrm_ref, s0_ref, o_ref, sout_ref, st_ref = refs
    else:
        f_ref, q_ref, v_ref, g_ref, lb_ref, nrm_ref, o_ref, sout_ref, st_ref = refs
        s0_ref = None
    tt = f_ref.shape[0]
    c = chunk
    ti = pl.program_id(1)

    @pl.when(ti == 0)
    def _():
        for h in range(HG_HEADS):
            if has_init:
                st_ref[h] = s0_ref[h].T
            else:
                st_ref[h] = jnp.zeros((HG_DV, HG_DK), F32)

    lbp = lb_ref[...]
    e = jnp.exp(lbp - jnp.max(lbp, axis=0, keepdims=True))
    tot = jnp.sum(e, axis=0, keepdims=True)
    part = jnp.zeros_like(tot)
    for j in range(1, layer + 1):
        part = part + e[j:j + 1]
    lb = part / tot
    log_lb = jnp.log(lb)
    log_1m = jnp.log1p(-lb)
    one_m = 1.0 - lb

    rid = lax.broadcasted_iota(jnp.int32, (c, 1), 0)
    rr = lax.broadcasted_iota(jnp.int32, (c, c), 0)
    cc = lax.broadcasted_iota(jnp.int32, (c, c), 1)
    tri = (cc <= rr).astype(BF16)
    base_mask = jnp.logical_and(_div_pow2(rr, HG_SUB) == _div_pow2(cc, HG_SUB), cc <= rr)

    for ci in range(tt // c):
        rows = slice(ci * c, (ci + 1) * c)
        z = f_ref[rows, :]
        ez = jnp.exp(-jnp.abs(z))
        rz = 1.0 / (1.0 + ez)
        log_sig = jnp.minimum(z, 0.0) - jnp.log1p(ez)
        kk = one_m * jnp.where(z >= 0, ez * rz, rz)
        y = log_1m + log_sig
        logf = jnp.maximum(log_lb, y) + jnp.log1p(jnp.exp(-jnp.abs(log_lb - y)))
        qp = q_ref[rows, :]
        qq = qp * _sigmoid(qp)

        hi = logf.astype(BF16)
        lo = (logf - hi.astype(F32)).astype(BF16)
        b = _dot(tri, hi) + _dot(tri, lo)
        bend = b[c - 1:c, :]

        qs, ks, masks = [], [], []
        g = c
        while g > HG_SUB:
            half = g // 2
            bref = b[half - 1:half, :]
            for gi in range(1, c // g):
                bref = jnp.where(rid >= gi * g, b[gi * g + half - 1:gi * g + half, :], bref)
            x = jnp.exp(-jnp.abs(b - bref))
            right = jnp.bitwise_and(rid, g - 1) >= half
            qs.append(jnp.where(right, qq * x, 0.0).astype(BF16))
            ks.append(jnp.where(right, 0.0, kk * x).astype(BF16))
            masks.append(None if g == c else (_div_pow2(rr, g) == _div_pow2(cc, g)))
            g = half
        bst = jnp.zeros_like(bend)
        for gi in range(1, c // HG_SUB):
            bst = jnp.where(rid >= gi * HG_SUB, b[gi * HG_SUB - 1:gi * HG_SUB, :], bst)
        e0 = b - bst
        qs.append((qq * jnp.exp(e0)).astype(BF16))
        ks.append((kk * jnp.exp(-e0)).astype(BF16))
        masks.append(base_mask)

        q_in = (qq * jnp.exp(b)).astype(BF16)
        k_out = (kk * jnp.exp(bend - b)).astype(BF16)
        dec = jnp.exp(bend)
        gp = g_ref[rows, :]
        gate = gp * _sigmoid(gp) * nrm_ref[...]

        for h in range(HG_HEADS):
            hs = slice(h * HG_DK, (h + 1) * HG_DK)
            a = None
            for ql, kl, mk in zip(qs, ks, masks):
                al = _dot_nt(ql[:, hs], kl[:, hs])
                if mk is not None:
                    al = jnp.where(mk, al, 0.0)
                a = al if a is None else a + al
            vh = v_ref[rows, hs]
            st = st_ref[h]
            o = _dot_nt(q_in[:, hs], st.astype(BF16)) + _dot(a.astype(BF16), vh)
            o = o * lax.rsqrt(jnp.mean(o * o, axis=-1, keepdims=True) + EPS) * gate[:, hs]
            o_ref[rows, hs] = o.astype(BF16)
            st_ref[h] = st * dec[:, hs] + _dot_tn(vh, k_out[:, hs])

    @pl.when(ti == pl.num_programs(1) - 1)
    def _():
        for h in range(HG_HEADS):
            sout_ref[h] = st_ref[h].T


def _hgrn(f_pre, q_pre, v, g_pre, hg_lb, hg_norm, s0, layer, batch, ln):
    tt = min(ln, 256)
    chunk = min(tt, CHUNK)
    nt = ln // tt
    has_init = s0 is not None
    depth = hg_lb.shape[0]
    row = lambda b, t: (b * nt + t, 0)
    st_spec = pl.BlockSpec((None, HG_HEADS, HG_DK, HG_DV), lambda b, t: (b, 0, 0, 0))
    in_specs = [
        pl.BlockSpec((tt, HG_FDIM), row), pl.BlockSpec((tt, HG_FDIM), row),
        pl.BlockSpec((tt, HG_IDIM), row), pl.BlockSpec((tt, HG_IDIM), row),
        _const_spec((depth, HG_FDIM)), _const_spec((1, HG_IDIM)),
    ]
    args = [f_pre, q_pre, v, g_pre, hg_lb, hg_norm]
    if has_init:
        in_specs.append(st_spec)
        args.append(s0)
    return pl.pallas_call(
        functools.partial(_hgrn_kernel, layer=layer, chunk=chunk, has_init=has_init),
        grid=(batch, nt),
        in_specs=in_specs,
        out_specs=[pl.BlockSpec((tt, HG_IDIM), row), st_spec],
        out_shape=[jax.ShapeDtypeStruct((batch * ln, HG_IDIM), BF16),
                   jax.ShapeDtypeStruct((batch, HG_HEADS, HG_DK, HG_DV), F32)],
        scratch_shapes=[pltpu.VMEM((HG_HEADS, HG_DV, HG_DK), F32)],
        compiler_params=_cparams(("parallel", "arbitrary")),
        name="hgrn",
    )(*args)


ROUTE_GROUP_LANE = N_EXPERTS


def _route(logits, cnt_ref):
    lane_i = lax.broadcasted_iota(jnp.int32, logits.shape, 1)
    lane = lane_i.astype(F32)
    big = float(LANES)
    neg = -jnp.inf
    is_g = jnp.logical_and(lane_i >= ROUTE_GROUP_LANE, lane_i < ROUTE_GROUP_LANE + N_GROUPS)
    gl = jnp.where(is_g, logits, neg)
    gmax = jnp.max(gl, axis=-1, keepdims=True)
    gidx = jnp.min(jnp.where(gl == gmax, lane - ROUTE_GROUP_LANE, big), axis=-1, keepdims=True)
    g_top = 1.0 / jnp.sum(jnp.exp(gl - gmax), axis=-1, keepdims=True)
    lane_group = _div_pow2(lane_i, EXPERTS_PER_GROUP).astype(F32)
    in_group = jnp.logical_and(lane_i < N_EXPERTS, lane_group == gidx)
    el = jnp.where(in_group, logits, neg)
    m1 = jnp.max(el, axis=-1, keepdims=True)
    i1 = jnp.min(jnp.where(el == m1, lane, big), axis=-1, keepdims=True)
    el2 = jnp.where(lane == i1, neg, el)
    m2 = jnp.max(el2, axis=-1, keepdims=True)
    i2 = jnp.min(jnp.where(el2 == m2, lane, big), axis=-1, keepdims=True)
    r = jnp.exp(m2 - m1)
    w1 = g_top / (1.0 + r)
    w2 = g_top * r / (1.0 + r)
    tm = logits.shape[0]
    hit = jnp.logical_or(lane == i1, lane == i2)
    rr = lax.broadcasted_iota(jnp.int32, (tm, tm), 0)
    cc = lax.broadcasted_iota(jnp.int32, (tm, tm), 1)
    before = _dot((cc < rr).astype(BF16), jnp.where(hit, 1.0, 0.0).astype(BF16)) + cnt_ref[...]
    r1 = jnp.sum(jnp.where(lane == i1, before, 0.0), axis=-1, keepdims=True)
    r2 = jnp.sum(jnp.where(lane == i2, before, 0.0), axis=-1, keepdims=True)
    cnt_ref[...] += jnp.sum(jnp.where(hit, 1.0, 0.0), axis=0, keepdims=True)
    out = jnp.where(lane_i == 0, i1, 0.0)
    out = jnp.where(lane_i == 1, i2, out)
    out = jnp.where(lane_i == 2, w1, out)
    out = jnp.where(lane_i == 3, w2, out)
    out = jnp.where(lane_i == 4, r1, out)
    out = jnp.where(lane_i == 5, r2, out)
    return out


def _post_kernel(x_ref, a_ref, o_ref, ple_ref, wg_ref, wa_ref, wb_ref, wo_ref, g1_ref, b1_ref,
                 wr_ref, wpg_ref, wp_ref, x1_ref, base_ref, route_ref, cnt_ref, *, alpha):
    @pl.when(pl.program_id(0) == 0)
    def _():
        cnt_ref[...] = jnp.zeros_like(cnt_ref)

    x = x_ref[...]
    xb = x.astype(BF16)
    ya = _dot(a_ref[...], wa_ref[...])
    merged = _sigmoid(_dot(xb, wg_ref[:, 0:D_MODEL])) * ya
    yb = _dot(o_ref[...], wb_ref[...])
    merged = merged + _sigmoid(_dot(xb, wg_ref[:, D_MODEL:2 * D_MODEL])) * yb
    mix = _dot(merged.astype(BF16), wo_ref[...])
    x1 = _layernorm(alpha * x + mix, g1_ref[...], b1_ref[...])
    _to_token_tiles(x1_ref, x1)
    x1b = x1.astype(BF16)
    route_ref[...] = _route(_dot(x1b, wr_ref[...]), cnt_ref)
    ple = _dot(ple_ref[...].astype(BF16), wp_ref[...])
    base_ref[...] = alpha * x1 + _sigmoid(_dot(x1b, wpg_ref[...])) * ple


def _post(xn, a, o, ple, lw, alpha):
    t = xn.shape[0]
    tm = TOKEN_TILE
    row = lambda i: (i, 0)
    d = D_MODEL
    return pl.pallas_call(
        functools.partial(_post_kernel, alpha=alpha),
        grid=(t // tm,),
        in_specs=[
            pl.BlockSpec((tm, d), row), pl.BlockSpec((tm, 512), row), pl.BlockSpec((tm, HG_IDIM), row),
            pl.BlockSpec((tm, PLE_DIM), row),
            _const_spec((d, 2 * d)), _const_spec((512, d)), _const_spec((HG_IDIM, d)), _const_spec((d, d)),
            _const_spec((1, d)), _const_spec((1, d)),
            _const_spec((d, LANES)), _const_spec((d, d)), _const_spec((PLE_DIM, d)),
        ],
        out_specs=[pl.BlockSpec((tm * TILE_ROWS, LANES), row), pl.BlockSpec((tm, d), row),
                   pl.BlockSpec((tm, LANES), row), pl.BlockSpec((1, LANES), lambda i: (0, 0))],
        out_shape=[jax.ShapeDtypeStruct((t * TILE_ROWS, LANES), F32), jax.ShapeDtypeStruct((t, d), F32),
                   jax.ShapeDtypeStruct((t, LANES), F32), jax.ShapeDtypeStruct((1, LANES), F32)],
        compiler_params=_cparams(("arbitrary",)),
        name="post",
    )(xn, a, o, ple, lw["w_gate"], lw["w_br_a"], lw["w_br_b"], lw["w_out"], lw["ln1_g"], lw["ln1_b"],
      lw["w_route"], lw["w_ple_gate"], lw["w_ple"])


TILE_ROWS = D_MODEL // LANES
GATHER_DMA_PRIORITY = 1


def _to_token_tiles(ref, x):
    n = x.shape[0]
    for c in range(TILE_ROWS):
        ref[pl.ds(c, n, stride=TILE_ROWS), :] = x[:, c * LANES:(c + 1) * LANES]


def _from_token_tiles(ref, first_token, n):
    return jnp.concatenate(
        [ref[pl.ds(first_token * TILE_ROWS + c, n, stride=TILE_ROWS), :] for c in range(TILE_ROWS)], axis=1)


def _tile_copy(idx_ref, j, src_hbm, dst, dst_token, sem):
    r = pl.multiple_of(idx_ref[0, j], TILE_ROWS)
    d = dst_token * TILE_ROWS
    if not isinstance(d, int):
        d = pl.multiple_of(d, TILE_ROWS)
    return pltpu.make_async_copy(src_hbm.at[pl.ds(r, TILE_ROWS)], dst.at[pl.ds(d, TILE_ROWS)], sem)


def _issue_tiles_loop(idx_ref, src_hbm, dst, sem, n_tokens, dst_off=0):
    def body(j, carry):
        _tile_copy(idx_ref, j, src_hbm, dst, dst_off + j, sem).start(priority=GATHER_DMA_PRIORITY)
        return carry

    lax.fori_loop(0, n_tokens, body, 0, unroll=8)


def _issue_tiles_static(idx_ref, src_hbm, dst, sem, lo, hi, dst_off=0):
    for j in range(lo, hi):
        _tile_copy(idx_ref, j, src_hbm, dst, dst_off + j, sem).start(priority=GATHER_DMA_PRIORITY)


def _wait_tiles(src_hbm, dst, sem):
    pltpu.make_async_copy(src_hbm.at[pl.ds(0, dst.shape[0])], dst, sem).wait()


def _idx_spec(tile, n_blocks, ahead):
    if ahead is None:
        imap = lambda i, *_: (0, 0, 0)
    else:
        imap = lambda i, *_: (jnp.minimum(i + ahead, n_blocks - 1), 0, 0)
    return pl.BlockSpec((None, 1, tile), imap, memory_space=pltpu.SMEM)


MOE_SLOTS = 3


def _moe_kernel(te_ref, idx0_ref, idx1_ref, idx2_ref, x_hbm, w1_ref, w3_ref, w2_ref, y_ref, xbuf, sem):
    i = pl.program_id(0)
    n = pl.num_programs(0)
    tm = MOE_TILE
    slot = i % MOE_SLOTS

    @pl.when(i == 0)
    def _():
        _issue_tiles_loop(idx0_ref, x_hbm, xbuf.at[0], sem.at[0], tm)
        _issue_tiles_loop(idx1_ref, x_hbm, xbuf.at[1], sem.at[1], tm)

    _wait_tiles(x_hbm, xbuf.at[slot], sem.at[slot])
    xb = _from_token_tiles(xbuf.at[slot], 0, tm).astype(BF16)
    ahead = (i + 2) % MOE_SLOTS
    nxt = xbuf.at[ahead]
    nsem = sem.at[ahead]
    group = tm // 4
    hs = []
    half = D_EXPERT // 2
    for c in range(2):
        cs = slice(c * half, (c + 1) * half)
        h1 = _dot(xb, w1_ref[:, cs])
        _issue_tiles_static(idx2_ref, x_hbm, nxt, nsem, 2 * c * group, (2 * c + 1) * group)
        h3 = _dot(xb, w3_ref[:, cs])
        _issue_tiles_static(idx2_ref, x_hbm, nxt, nsem, (2 * c + 1) * group, (2 * c + 2) * group)
        hs.append((h1 * _sigmoid(h1) * h3).astype(BF16))
    _to_token_tiles(y_ref, _dot(jnp.concatenate(hs, axis=1), w2_ref[...]))

    @pl.when(i == n - 1)
    def _():
        _wait_tiles(x_hbm, xbuf.at[(i + 1) % MOE_SLOTS], sem.at[(i + 1) % MOE_SLOTS])
        _wait_tiles(x_hbm, nxt, nsem)


def _moe(x_tiles, src, tile_e, w1, w3, w2):
    tm = MOE_TILE
    n_tiles = src.shape[0]
    d = D_MODEL
    grid_spec = pltpu.PrefetchScalarGridSpec(
        num_scalar_prefetch=1,
        grid=(n_tiles,),
        in_specs=[
            _idx_spec(tm, n_tiles, None), _idx_spec(tm, n_tiles, 1), _idx_spec(tm, n_tiles, 2),
            pl.BlockSpec(memory_space=pl.ANY),
            pl.BlockSpec((None, d, D_EXPERT), lambda i, te: (te[i], 0, 0)),
            pl.BlockSpec((None, d, D_EXPERT), lambda i, te: (te[i], 0, 0)),
            pl.BlockSpec((None, D_EXPERT, d), lambda i, te: (te[i], 0, 0)),
        ],
        out_specs=pl.BlockSpec((tm * TILE_ROWS, LANES), lambda i, te: (i, 0)),
        scratch_shapes=[pltpu.VMEM((MOE_SLOTS, tm * TILE_ROWS, LANES), F32), pltpu.SemaphoreType.DMA((MOE_SLOTS,))],
    )
    src3 = src.reshape(n_tiles, 1, tm)
    return pl.pallas_call(
        _moe_kernel,
        grid_spec=grid_spec,
        out_shape=jax.ShapeDtypeStruct((n_tiles * tm * TILE_ROWS, LANES), F32),
        compiler_params=_cparams(("arbitrary",)),
        name="moe_experts",
    )(tile_e, src3, src3, src3, x_tiles, w1, w3, w2)


def _dispatch_plan(route, cnt, n_tiles):
    t = route.shape[0]
    a = t * TOP_K
    tm = MOE_TILE
    eid = jnp.arange(N_EXPERTS, dtype=jnp.int32)
    counts = cnt[0, 0:N_EXPERTS].astype(jnp.int32)
    tiles_e = (counts + tm - 1) // tm
    tile_end = jnp.cumsum(tiles_e)
    pstart = (tile_end - tiles_e) * tm
    e = route[:, 0:TOP_K].astype(jnp.int32)
    rank = route[:, 4:4 + TOP_K].astype(jnp.int32)
    pos = rank + jnp.sum(jnp.where(e[:, :, None] == eid[None, None, :], pstart[None, None, :], 0), axis=-1)
    pad_e = tiles_e * tm - counts
    cpad = jnp.cumsum(pad_e)
    k = jnp.arange(n_tiles * tm - a, dtype=jnp.int32)
    ke = jnp.sum((cpad[None, :] <= k[:, None]).astype(jnp.int32), axis=1)
    inside = k + jnp.sum(jnp.where(jnp.minimum(ke, N_EXPERTS - 1)[:, None] == eid[None, :],
                                   (pstart + counts - (cpad - pad_e))[None, :], 0), axis=1)
    padpos = jnp.where(ke < N_EXPERTS, inside, tile_end[-1] * tm + (k - cpad[-1]))
    keys = jnp.concatenate([pos.reshape(a), padpos])
    vals = jnp.concatenate([jnp.repeat(jnp.arange(t, dtype=jnp.int32), TOP_K), jnp.zeros_like(k)])
    _, src = lax.sort_key_val(keys, vals)
    jt = jnp.arange(n_tiles, dtype=jnp.int32)
    tile_e = jnp.minimum(jnp.sum((tile_end[None, :] <= jt[:, None]).astype(jnp.int32), axis=1), N_EXPERTS - 1)
    return (src * TILE_ROWS).reshape(n_tiles, tm), pos.reshape(a) * TILE_ROWS, tile_e.astype(jnp.int32)


def _final_kernel(p0c_ref, p1c_ref, p0n_ref, p1n_ref, base_ref, route_ref, g_ref, b_ref, y_hbm, o_ref, ybuf, sem):
    i = pl.program_id(0)
    n = pl.num_programs(0)
    tm = FINAL_TILE
    slot = i % 2

    @pl.when(i == 0)
    def _():
        _issue_tiles_loop(p0c_ref, y_hbm, ybuf.at[0], sem.at[0], tm, 0)
        _issue_tiles_loop(p1c_ref, y_hbm, ybuf.at[0], sem.at[0], tm, tm)

    _wait_tiles(y_hbm, ybuf.at[slot], sem.at[slot])
    nxt = ybuf.at[1 - slot]
    nsem = sem.at[1 - slot]
    _issue_tiles_static(p0n_ref, y_hbm, nxt, nsem, 0, tm, 0)
    _issue_tiles_static(p1n_ref, y_hbm, nxt, nsem, 0, tm, tm)
    w = route_ref[...]
    cur = ybuf.at[slot]
    h = base_ref[...] + w[:, 2:3] * _from_token_tiles(cur, 0, tm) + w[:, 3:4] * _from_token_tiles(cur, tm, tm)
    o_ref[...] = _layernorm(h, g_ref[...], b_ref[...])

    @pl.when(i == n - 1)
    def _():
        _wait_tiles(y_hbm, nxt, nsem)


def _final(base, y_tiles, pos, route, g, b):
    t, d = base.shape
    tm = FINAL_TILE
    nb = t // tm
    row = lambda i: (i, 0)
    pos2 = pos.reshape(t, TOP_K)
    p0 = pos2[:, 0].reshape(nb, 1, tm)
    p1 = pos2[:, 1].reshape(nb, 1, tm)
    first, nxt = _idx_spec(tm, nb, None), _idx_spec(tm, nb, 1)
    return pl.pallas_call(
        _final_kernel,
        grid=(nb,),
        in_specs=[first, first, nxt, nxt,
                  pl.BlockSpec((tm, d), row), pl.BlockSpec((tm, LANES), row),
                  _const_spec((1, d)), _const_spec((1, d)),
                  pl.BlockSpec(memory_space=pl.ANY)],
        out_specs=pl.BlockSpec((tm, d), row),
        out_shape=jax.ShapeDtypeStruct((t, d), F32),
        scratch_shapes=[pltpu.VMEM((2, 2 * tm * TILE_ROWS, LANES), F32), pltpu.SemaphoreType.DMA((2,))],
        compiler_params=_cparams(("arbitrary",)),
        name="final_ln",
    )(p0, p1, p0, p1, base, route, g, b, y_tiles)


def _rot_cols(w):
    half = ROPE_DIM // 2
    return jnp.concatenate([-w[..., half:], w[..., :half]], axis=-1)


def _layer_weights(i, w_in, q_norm, w_uq, kv_norm, w_uk, w_uv, hg_norm, w_br_a, w_br_b, w_out, ln1_g, ln1_b,
                   w_rg, w_re, w_e_gate, w_e_up, w_e_down, w_ple, w_ple_gate, ln2_g, ln2_b):
    d = D_MODEL
    nh = MLA_HEADS
    win = w_in[i]
    c_kr = Q_LORA + KV_LORA
    c_f = c_kr + ROPE_DIM
    c_ga = c_f + N_HG
    w_kr = win[:, c_kr:c_f]
    zpad = jnp.zeros((d, LANES - ROPE_DIM), F32)
    wa = jnp.concatenate([win[:, 0:c_kr], win[:, c_f:c_ga], w_kr, zpad, _rot_cols(w_kr), zpad], axis=1)

    wq3 = w_uq[i].reshape(Q_LORA, nh, NOPE_DIM + ROPE_DIM)
    zq = jnp.zeros((Q_LORA, nh, HEAD_PAD - NOPE_DIM - ROPE_DIM), F32)
    wq = jnp.concatenate([wq3, zq], axis=-1).reshape(Q_LORA, QK_PAD)
    wq_rot = jnp.concatenate([jnp.zeros((Q_LORA, nh, NOPE_DIM), F32), _rot_cols(wq3[..., NOPE_DIM:]), zq],
                             axis=-1).reshape(Q_LORA, QK_PAD)

    wuk = w_uk[i]
    wuv = w_uv[i]
    k_nope = jnp.concatenate([wuk, jnp.zeros((KV_LORA, nh, HEAD_PAD - NOPE_DIM), F32)], axis=-1)
    eye = jnp.eye(ROPE_DIM, dtype=F32)
    k_rope = jnp.concatenate([jnp.zeros((ROPE_DIM, NOPE_DIM), F32), eye,
                              jnp.zeros((ROPE_DIM, HEAD_PAD - NOPE_DIM - ROPE_DIM), F32)], axis=-1)
    k_rope = jnp.concatenate([jnp.tile(k_rope[:, None, :], (1, nh, 1)).reshape(ROPE_DIM, QK_PAD),
                              jnp.zeros((LANES - ROPE_DIM, QK_PAD), F32)], axis=0)
    zv = jnp.zeros((KV_LORA, nh // 2, V_DIM), F32)
    wuv_a3 = jnp.concatenate([wuv[:, 0::2, :], zv], axis=-1)
    wuv_b3 = jnp.concatenate([zv, wuv[:, 1::2, :]], axis=-1)
    wkv_top = jnp.concatenate([k_nope.reshape(KV_LORA, QK_PAD), wuv_a3.reshape(KV_LORA, 512),
                               wuv_b3.reshape(KV_LORA, 512)], axis=1)
    wkv_bot = jnp.concatenate([k_rope, jnp.zeros((LANES, QK_PAD), F32)], axis=1)
    wkv = jnp.concatenate([wkv_top, wkv_bot], axis=0)

    wuk_t = jnp.concatenate([jnp.transpose(wuk, (1, 2, 0)),
                             jnp.zeros((nh, HEAD_PAD - NOPE_DIM, KV_LORA), F32)], axis=1)
    p_sel = jnp.concatenate([jnp.zeros((NOPE_DIM, ROPE_DIM), F32), eye,
                             jnp.zeros((HEAD_PAD - NOPE_DIM - ROPE_DIM, ROPE_DIM), F32)], axis=0)

    w_route = jnp.concatenate([w_re[i], w_rg[i], jnp.zeros((d, LANES - N_EXPERTS - N_GROUPS), F32)], axis=1)
    bf = lambda w: w.astype(BF16)
    return {
        "wa": bf(wa), "q_norm": q_norm[i].reshape(1, -1), "kv_norm": kv_norm[i].reshape(1, -1),
        "wq": bf(wq), "wq_rot": bf(wq_rot), "wkv": bf(wkv),
        "wuk_t": bf(wuk_t), "p_sel": bf(p_sel),
        "wuv_a3": bf(jnp.transpose(wuv_a3, (1, 0, 2))), "wuv_b3": bf(jnp.transpose(wuv_b3, (1, 0, 2))),
        "hg_norm": hg_norm[i].reshape(1, -1),
        "w_gate": bf(win[:, c_ga:c_ga + 2 * d]), "w_br_a": bf(w_br_a[i]), "w_br_b": bf(w_br_b[i]),
        "w_out": bf(w_out[i]), "ln1_g": ln1_g[i].reshape(1, -1), "ln1_b": ln1_b[i].reshape(1, -1),
        "w_route": bf(w_route), "w_ple_gate": bf(w_ple_gate[i]), "w_ple": bf(w_ple[i]),
        "w1": bf(w_e_gate[i]), "w3": bf(w_e_up[i]), "w2": bf(w_e_down[i]),
        "ln2_g": ln2_g[i].reshape(1, -1), "ln2_b": ln2_b[i].reshape(1, -1),
    }


def _rope_table(pos, rows):
    half = ROPE_DIM // 2
    inv = ROPE_THETA ** (-jnp.arange(half, dtype=F32) / half)
    ang = pos.astype(F32)[:, None] * inv[None, :]
    c2 = jnp.concatenate([jnp.cos(ang), jnp.cos(ang)], axis=1)
    s2 = jnp.concatenate([jnp.sin(ang), jnp.sin(ang)], axis=1)
    n = pos.shape[0]
    ones = jnp.ones((n, NOPE_DIM), F32)
    z = lambda w: jnp.zeros((n, w), F32)
    tab = jnp.concatenate([
        ones, c2, z(HEAD_PAD - NOPE_DIM - ROPE_DIM),
        z(NOPE_DIM), s2, z(HEAD_PAD - NOPE_DIM - ROPE_DIM),
        c2, z(LANES - ROPE_DIM),
        s2, z(LANES - ROPE_DIM)], axis=1)
    if rows > n:
        tab = jnp.tile(tab, (rows // n, 1))
    return tab


def _trunk_layer(i, xn, ple, tab, lw, hg_lb, batch, ln, cache, alpha):
    q, k, va, vb, ckv, kr, f_pre, q_pre, hv, g_pre = _proj(xn, tab, lw)
    if cache is None:
        a = _attn_prompt(q, k, va, vb, batch, ln)
        o, st = _hgrn(f_pre, q_pre, hv, g_pre, hg_lb, lw["hg_norm"], None, i, batch, ln)
    else:
        a = _attn_sample(q, ckv, kr, cache[0], cache[1], lw, batch, ln)
        o, st = _hgrn(f_pre, q_pre, hv, g_pre, hg_lb, lw["hg_norm"], cache[2], i, batch, ln)
    x1, base, route, cnt = _post(xn, a, o, ple, lw, alpha)

    t = xn.shape[0]
    n_tiles = (t * TOP_K) // MOE_TILE + N_EXPERTS
    src, pos, tile_e = _dispatch_plan(route, cnt, n_tiles)
    y = _moe(x1, src, tile_e, lw["w1"], lw["w3"], lw["w2"])
    x2 = _final(base, y, pos, route, lw["ln2_g"], lw["ln2_b"])
    return x2, ckv, kr, st


def kernel(x_prompt, x_sample, p_prompt, p_sample, cache_ckv, cache_krope, state_hgrn, ln0_g, ln0_b, w_in, q_norm, w_uq, kv_norm, w_uk, w_uv, hg_lb, hg_norm, w_br_a, w_br_b, w_out, ln1_g, ln1_b, w_router_group, w_router_expert, w_e_gate, w_e_up, w_e_down, w_ple, w_ple_gate, ln2_g, ln2_b):
    depth = w_in.shape[0]
    bp, sp, d = x_prompt.shape
    bs, ss, _ = x_sample.shape
    past = cache_ckv.shape[2]
    alpha = (2 * depth) ** 0.25

    xp = _ln_rows(x_prompt.reshape(bp * sp, d), ln0_g, ln0_b)
    xs = _ln_rows(x_sample.reshape(bs * ss, d), ln0_g, ln0_b)
    tab_p = _rope_table(jnp.arange(sp, dtype=jnp.int32), max(sp, TOKEN_TILE))
    tab_s = _rope_table(past + jnp.arange(ss, dtype=jnp.int32), max(ss, TOKEN_TILE))

    outs = [[] for _ in range(6)]
    for i in range(depth):
        lw = _layer_weights(i, w_in, q_norm, w_uq, kv_norm, w_uk, w_uv, hg_norm, w_br_a, w_br_b, w_out,
                            ln1_g, ln1_b, w_router_group, w_router_expert, w_e_gate, w_e_up, w_e_down,
                            w_ple, w_ple_gate, ln2_g, ln2_b)
        xp, c, r, s = _trunk_layer(i, xp, p_prompt[i].reshape(bp * sp, -1), tab_p, lw, hg_lb, bp, sp, None, alpha)
        outs[0].append(c.reshape(bp, sp, -1))
        outs[1].append(r.reshape(bp, sp, -1))
        outs[2].append(s)
        cache = (cache_ckv[i], cache_krope[i], state_hgrn[i])
        xs, c, r, s = _trunk_layer(i, xs, p_sample[i].reshape(bs * ss, -1), tab_s, lw, hg_lb, bs, ss, cache, alpha)
        outs[3].append(c.reshape(bs, ss, -1))
        outs[4].append(r.reshape(bs, ss, -1))
        outs[5].append(s)
    return (xp.reshape(bp, sp, d), xs.reshape(bs, ss, d)) + tuple(jnp.stack(o) for o in outs)
```

```python
import functools

import numpy as np
import jax
import jax.numpy as jnp
from jax import lax
from jax.experimental import pallas as pl
from jax.experimental.pallas import tpu as pltpu

F32 = jnp.float32
BF16 = jnp.bfloat16

D_MODEL = 1024
CHUNK = 64
PLE_DIM = 256
MLA_HEADS = 8
NOPE_DIM = 64
ROPE_DIM = 32
V_DIM = 64
Q_LORA = 384
KV_LORA = 256
ROPE_THETA = 10000.0
ATTN_SCALE = (NOPE_DIM + ROPE_DIM) ** -0.5
HG_HEADS = 4
HG_DK = 128
HG_DV = 128
HG_FDIM = HG_HEADS * HG_DK
HG_IDIM = HG_HEADS * HG_DV
N_GROUPS = 4
EXPERTS_PER_GROUP = 8
N_EXPERTS = N_GROUPS * EXPERTS_PER_GROUP
TOP_K = 2
D_EXPERT = 512
EPS = 1e-6

LANES = 128
HEAD_PAD = LANES
QK_PAD = MLA_HEADS * HEAD_PAD
VMEM_LIMIT = 56 * 1024 * 1024

TOKEN_TILE = 512
ATTN_TILE = 256
MOE_TILE = 256
FINAL_TILE = 256
HG_SUB = 16


def _cparams(sem):
    return pltpu.CompilerParams(dimension_semantics=sem, vmem_limit_bytes=VMEM_LIMIT)


def _const_spec(shape):
    nd = len(shape)
    return pl.BlockSpec(shape, lambda *_: (0,) * nd, pipeline_mode=pl.Buffered(1))


def _dot(a, b):
    return jnp.dot(a, b, preferred_element_type=F32)


def _dot_nt(a, b):
    return lax.dot_general(a, b, (((1,), (1,)), ((), ())), preferred_element_type=F32)


def _dot_tn(a, b):
    return lax.dot_general(a, b, (((0,), (0,)), ((), ())), preferred_element_type=F32)


def _div_pow2(x, d):
    return jnp.right_shift(x, int(d).bit_length() - 1)


def _sigmoid(x):
    return 1.0 / (1.0 + jnp.exp(-x))


def _layernorm(x, g, b):
    mu = jnp.mean(x, axis=-1, keepdims=True)
    xc = x - mu
    var = jnp.mean(xc * xc, axis=-1, keepdims=True)
    return xc * lax.rsqrt(var + EPS) * g + b


def _rmsnorm(x, g):
    return x * lax.rsqrt(jnp.mean(x * x, axis=-1, keepdims=True) + EPS) * g


def _ln_kernel(x_ref, g_ref, b_ref, o_ref):
    o_ref[...] = _layernorm(x_ref[...], g_ref[...], b_ref[...])


def _ln_rows(x, g, b):
    t, d = x.shape
    tm = TOKEN_TILE
    return pl.pallas_call(
        _ln_kernel,
        grid=(t // tm,),
        in_specs=[pl.BlockSpec((tm, d), lambda i: (i, 0)), _const_spec((1, d)), _const_spec((1, d))],
        out_specs=pl.BlockSpec((tm, d), lambda i: (i, 0)),
        out_shape=jax.ShapeDtypeStruct((t, d), F32),
        compiler_params=_cparams(("parallel",)),
        name="ln0",
    )(x, g.reshape(1, d), b.reshape(1, d))


N_LAT = Q_LORA + KV_LORA
N_HG = 2 * HG_FDIM + 2 * HG_IDIM
COL_HG = N_LAT
COL_KR = N_LAT + N_HG
N_PROJ = COL_KR + 2 * LANES


def _proj_kernel(x_ref, tab_ref, wa_ref, qn_ref, kvn_ref, wq_ref, wqr_ref, wkv_ref,
                 q_ref, k_ref, va_ref, vb_ref, ckv_ref, kr_ref, f_ref, hq_ref, hv_ref, hg_ref):
    xb = x_ref[...].astype(BF16)
    cq_tab = tab_ref[:, 0:LANES]
    sq_tab = tab_ref[:, LANES:2 * LANES]
    ck_tab = tab_ref[:, 2 * LANES:3 * LANES]
    sk_tab = tab_ref[:, 3 * LANES:4 * LANES]

    lat = _dot(xb, wa_ref[:, 0:N_LAT])
    cqn = _rmsnorm(lat[:, 0:Q_LORA], qn_ref[...]).astype(BF16)
    ckvn = _rmsnorm(lat[:, Q_LORA:N_LAT], kvn_ref[...])
    ckv_ref[...] = ckvn

    qa = _dot(cqn, wq_ref[...])
    qb = _dot(cqn, wqr_ref[...])
    for h in range(MLA_HEADS):
        sl = slice(h * HEAD_PAD, (h + 1) * HEAD_PAD)
        q_ref[:, sl] = (qa[:, sl] * cq_tab + qb[:, sl] * sq_tab).astype(BF16)

    kr2 = _dot(xb, wa_ref[:, COL_KR:N_PROJ])
    kr = kr2[:, 0:LANES] * ck_tab + kr2[:, LANES:2 * LANES] * sk_tab
    kr_ref[...] = kr[:, 0:ROPE_DIM]

    kvin = jnp.concatenate([ckvn.astype(BF16), kr.astype(BF16)], axis=1)
    kv = _dot(kvin, wkv_ref[...])
    k_ref[...] = kv[:, 0:QK_PAD].astype(BF16)
    va_ref[...] = kv[:, QK_PAD:QK_PAD + 512].astype(BF16)
    vb_ref[...] = kv[:, QK_PAD + 512:QK_PAD + 1024].astype(BF16)

    f_ref[...] = _dot(xb, wa_ref[:, COL_HG:COL_HG + 512])
    hq_ref[...] = _dot(xb, wa_ref[:, COL_HG + 512:COL_HG + 1024])
    hv_ref[...] = _dot(xb, wa_ref[:, COL_HG + 1024:COL_HG + 1536]).astype(BF16)
    hg_ref[...] = _dot(xb, wa_ref[:, COL_HG + 1536:COL_HG + 2048])


def _proj(xn, tab, lw):
    t = xn.shape[0]
    tm = TOKEN_TILE
    ntab = tab.shape[0] // tm
    row = lambda i: (i, 0)
    outs = [
        (QK_PAD, BF16), (QK_PAD, BF16), (512, BF16), (512, BF16), (KV_LORA, F32), (ROPE_DIM, F32),
        (HG_FDIM, F32), (HG_FDIM, F32), (HG_IDIM, BF16), (HG_IDIM, F32),
    ]
    return pl.pallas_call(
        _proj_kernel,
        grid=(t // tm,),
        in_specs=[
            pl.BlockSpec((tm, D_MODEL), row),
            pl.BlockSpec((tm, 4 * LANES), lambda i: (i % ntab, 0)),
            _const_spec((D_MODEL, N_PROJ)),
            _const_spec((1, Q_LORA)), _const_spec((1, KV_LORA)),
            _const_spec((Q_LORA, QK_PAD)), _const_spec((Q_LORA, QK_PAD)),
            _const_spec((KV_LORA + LANES, 2 * QK_PAD)),
        ],
        out_specs=[pl.BlockSpec((tm, w), row) for w, _ in outs],
        out_shape=[jax.ShapeDtypeStruct((t, w), dt) for w, dt in outs],
        compiler_params=_cparams(("parallel",)),
        name="proj",
    )(xn, tab, lw["wa"], lw["q_norm"], lw["kv_norm"], lw["wq"], lw["wq_rot"], lw["wkv"])


LOG2E = 1.4426950408889634


def _attn_prompt_kernel(q_ref, k_ref, va_ref, vb_ref, o_ref, s_ref, m_ref, l_ref, acc_ref):
    tq = ATTN_TILE
    nh = MLA_HEADS
    qi = pl.program_id(1)
    rq = _div_pow2(lax.broadcasted_iota(jnp.int32, (tq, tq), 0), CHUNK)
    ck = _div_pow2(lax.broadcasted_iota(jnp.int32, (tq, tq), 1), CHUNK)
    diag_mask = ck <= rq
    c = ATTN_SCALE * LOG2E

    m_ref[...] = jnp.full(m_ref.shape, -jnp.inf, F32)
    l_ref[...] = jnp.zeros(l_ref.shape, F32)
    acc_ref[...] = jnp.zeros(acc_ref.shape, F32)

    def scores(kt, mask):
        ks = pl.multiple_of(kt * tq, tq)
        for h in range(nh):
            hs = slice(h * HEAD_PAD, (h + 1) * HEAD_PAD)
            s = _dot_nt(q_ref[:, hs], k_ref[pl.ds(ks, tq), hs]) * c
            if mask is not None:
                s = jnp.where(mask, s, -jnp.inf)
            s_ref[h, :, pl.ds(ks, tq)] = s
            m_ref[h] = jnp.maximum(m_ref[h], jnp.maximum(s[:, 0:LANES], s[:, LANES:2 * LANES]))

    def p1(kt, carry):
        scores(kt, None)
        return carry

    lax.fori_loop(0, qi, p1, 0)
    scores(qi, diag_mask)
    for h in range(nh):
        m_ref[h] = jnp.broadcast_to(jnp.max(m_ref[h], axis=-1, keepdims=True), (tq, LANES))

    def p2(kt, carry):
        ks = pl.multiple_of(kt * tq, tq)
        for h in range(nh):
            v_ref = va_ref if h % 2 == 0 else vb_ref
            pair = h // 2
            mb = m_ref[h]
            p_lo = jnp.exp2(s_ref[h, :, pl.ds(ks, LANES)] - mb)
            p_hi = jnp.exp2(s_ref[h, :, pl.ds(ks + LANES, LANES)] - mb)
            l_ref[h] += p_lo + p_hi
            p = jnp.concatenate([p_lo, p_hi], axis=1).astype(BF16)
            acc_ref[h] += _dot(p, v_ref[pl.ds(ks, tq), pair * LANES:(pair + 1) * LANES])
        return carry

    lax.fori_loop(0, qi + 1, p2, 0)
    for pair in range(nh // 2):
        he, ho = 2 * pair, 2 * pair + 1
        o = (acc_ref[he] / jnp.sum(l_ref[he], axis=-1, keepdims=True)
             + acc_ref[ho] / jnp.sum(l_ref[ho], axis=-1, keepdims=True))
        o_ref[:, pair * LANES:(pair + 1) * LANES] = o.astype(BF16)


def _attn_prompt(q, k, va, vb, batch, seq):
    tq = ATTN_TILE
    nq = seq // tq
    nh = MLA_HEADS
    return pl.pallas_call(
        _attn_prompt_kernel,
        grid=(batch, nq),
        in_specs=[
            pl.BlockSpec((tq, QK_PAD), lambda b, i: (b * nq + i, 0)),
            pl.BlockSpec((seq, QK_PAD), lambda b, i: (b, 0)),
            pl.BlockSpec((seq, 512), lambda b, i: (b, 0)),
            pl.BlockSpec((seq, 512), lambda b, i: (b, 0)),
        ],
        out_specs=pl.BlockSpec((tq, 512), lambda b, i: (b * nq + i, 0)),
        out_shape=jax.ShapeDtypeStruct((batch * seq, 512), BF16),
        scratch_shapes=[pltpu.VMEM((nh, tq, seq), F32), pltpu.VMEM((nh, tq, LANES), F32),
                        pltpu.VMEM((nh, tq, LANES), F32), pltpu.VMEM((nh, tq, LANES), F32)],
        compiler_params=_cparams(("parallel", "arbitrary")),
        name="attn_prompt",
    )(q, k, va, vb)


SAMPLE_KEY_TILE = 1024


def _attn_sample_kernel(q_ref, ckvn_ref, krn_ref, cc_ref, ck_ref, wukt_ref, psel_ref, wva_ref, wvb_ref,
                        o_ref, s_ref):
    ln = q_ref.shape[0]
    past = cc_ref.shape[0]
    nh = MLA_HEADS
    qlat, qrope = [], []
    for h in range(nh):
        qh = q_ref[:, h * HEAD_PAD:(h + 1) * HEAD_PAD]
        qlat.append(_dot(qh, wukt_ref[h]).astype(BF16))
        qrope.append(_dot(qh, psel_ref[...]).astype(BF16))
    qlat = jnp.concatenate(qlat, axis=0)
    qrope = jnp.concatenate(qrope, axis=0)

    ckv_new = ckvn_ref[...].astype(BF16)
    kr_new = krn_ref[...].astype(BF16)
    s_new = (_dot_nt(qlat, ckv_new) + _dot_nt(qrope, kr_new)) * ATTN_SCALE
    m = jnp.max(s_new, axis=-1, keepdims=True)
    tk = SAMPLE_KEY_TILE
    for kt in range(past // tk):
        rows = slice(kt * tk, (kt + 1) * tk)
        s = (_dot_nt(qlat, cc_ref[rows, :].astype(BF16)) + _dot_nt(qrope, ck_ref[rows, :].astype(BF16))) * ATTN_SCALE
        s_ref[:, rows] = s
        m = jnp.maximum(m, jnp.max(s, axis=-1, keepdims=True))
    p_new = jnp.exp(s_new - m)
    l = jnp.sum(p_new, axis=-1, keepdims=True)
    lat = _dot(p_new.astype(BF16), ckv_new)
    for kt in range(past // tk):
        rows = slice(kt * tk, (kt + 1) * tk)
        p = jnp.exp(s_ref[:, rows] - m)
        l = l + jnp.sum(p, axis=-1, keepdims=True)
        lat = lat + _dot(p.astype(BF16), cc_ref[rows, :].astype(BF16))
    lat = (lat / l).astype(BF16)
    for pair in range(nh // 2):
        he, ho = 2 * pair, 2 * pair + 1
        o = _dot(lat[he * ln:(he + 1) * ln], wva_ref[pair]) + _dot(lat[ho * ln:(ho + 1) * ln], wvb_ref[pair])
        o_ref[:, pair * LANES:(pair + 1) * LANES] = o.astype(BF16)


def _attn_sample(q, ckv_new, kr_new, cache_ckv, cache_kr, lw, batch, ln):
    past = cache_ckv.shape[1]
    nh = MLA_HEADS
    return pl.pallas_call(
        _attn_sample_kernel,
        grid=(batch,),
        in_specs=[
            pl.BlockSpec((ln, QK_PAD), lambda b: (b, 0)),
            pl.BlockSpec((ln, KV_LORA), lambda b: (b, 0)),
            pl.BlockSpec((ln, ROPE_DIM), lambda b: (b, 0)),
            pl.BlockSpec((None, past, KV_LORA), lambda b: (b, 0, 0)),
            pl.BlockSpec((None, past, ROPE_DIM), lambda b: (b, 0, 0)),
            _const_spec((nh, HEAD_PAD, KV_LORA)),
            _const_spec((HEAD_PAD, ROPE_DIM)),
            _const_spec((nh // 2, KV_LORA, LANES)),
            _const_spec((nh // 2, KV_LORA, LANES)),
        ],
        out_specs=pl.BlockSpec((ln, 512), lambda b: (b, 0)),
        out_shape=jax.ShapeDtypeStruct((batch * ln, 512), BF16),
        scratch_shapes=[pltpu.VMEM((nh * ln, past), F32)],
        compiler_params=_cparams(("parallel",)),
        name="attn_sample",
    )(q, ckv_new, kr_new, cache_ckv, cache_kr, lw["wuk_t"], lw["p_sel"], lw["wuv_a3"], lw["wuv_b3"])


def _hgrn_kernel(*refs, layer, chunk, has_init):
    if has_init:
        f_ref, q_ref, v_ref, g_ref, lb_ref, nrm_ref, s0_ref, o_ref, sout_ref, st_ref = refs
    else:
        f_ref, q_ref, v_ref, g_ref, lb_ref, nrm_ref, o_ref, sout_ref, st_ref = refs
        s0_ref = None
    tt = f_ref.shape[0]
    c = chunk
    ti = pl.program_id(1)

    @pl.when(ti == 0)
    def _():
        for h in range(HG_HEADS):
            if has_init:
                st_ref[h] = s0_ref[h].T
            else:
                st_ref[h] = jnp.zeros((HG_DV, HG_DK), F32)

    lbp = lb_ref[...]
    e = jnp.exp(lbp - jnp.max(lbp, axis=0, keepdims=True))
    tot = jnp.sum(e, axis=0, keepdims=True)
    part = jnp.zeros_like(tot)
    for j in range(1, layer + 1):
        part = part + e[j:j + 1]
    lb = part / tot
    log_lb = jnp.log(lb)
    log_1m = jnp.log1p(-lb)
    one_m = 1.0 - lb

    rid = lax.broadcasted_iota(jnp.int32, (c, 1), 0)
    rr = lax.broadcasted_iota(jnp.int32, (c, c), 0)
    cc = lax.broadcasted_iota(jnp.int32, (c, c), 1)
    tri = (cc <= rr).astype(BF16)
    base_mask = jnp.logical_and(_div_pow2(rr, HG_SUB) == _div_pow2(cc, HG_SUB), cc <= rr)

    for ci in range(tt // c):
        rows = slice(ci * c, (ci + 1) * c)
        z = f_ref[rows, :]
        ez = jnp.exp(-jnp.abs(z))
        rz = 1.0 / (1.0 + ez)
        log_sig = jnp.minimum(z, 0.0) - jnp.log1p(ez)
        kk = one_m * jnp.where(z >= 0, ez * rz, rz)
        y = log_1m + log_sig
        logf = jnp.maximum(log_lb, y) + jnp.log1p(jnp.exp(-jnp.abs(log_lb - y)))
        qp = q_ref[rows, :]
        qq = qp * _sigmoid(qp)

        hi = logf.astype(BF16)
        lo = (logf - hi.astype(F32)).astype(BF16)
        b = _dot(tri, hi) + _dot(tri, lo)
        bend = b[c - 1:c, :]

        qs, ks, masks = [], [], []
        g = c
        while g > HG_SUB:
            half = g // 2
            bref = b[half - 1:half, :]
            for gi in range(1, c // g):
                bref = jnp.where(rid >= gi * g, b[gi * g + half - 1:gi * g + half, :], bref)
            x = jnp.exp(-jnp.abs(b - bref))
            right = jnp.bitwise_and(rid, g - 1) >= half
            qs.append(jnp.where(right, qq * x, 0.0).astype(BF16))
            ks.append(jnp.where(right, 0.0, kk * x).astype(BF16))
            masks.append(None if g == c else (_div_pow2(rr, g) == _div_pow2(cc, g)))
            g = half
        bst = jnp.zeros_like(bend)
        for gi in range(1, c // HG_SUB):
            bst = jnp.where(rid >= gi * HG_SUB, b[gi * HG_SUB - 1:gi * HG_SUB, :], bst)
        e0 = b - bst
        qs.append((qq * jnp.exp(e0)).astype(BF16))
        ks.append((kk * jnp.exp(-e0)).astype(BF16))
        masks.append(base_mask)

        q_in = (qq * jnp.exp(b)).astype(BF16)
        k_out = (kk * jnp.exp(bend - b)).astype(BF16)
        dec = jnp.exp(bend)
        gp = g_ref[rows, :]
        gate = gp * _sigmoid(gp) * nrm_ref[...]

        for h in range(HG_HEADS):
            hs = slice(h * HG_DK, (h + 1) * HG_DK)
            a = None
            for ql, kl, mk in zip(qs, ks, masks):
                al = _dot_nt(ql[:, hs], kl[:, hs])
                if mk is not None:
                    al = jnp.where(mk, al, 0.0)
                a = al if a is None else a + al
            vh = v_ref[rows, hs]
            st = st_ref[h]
            o = _dot_nt(q_in[:, hs], st.astype(BF16)) + _dot(a.astype(BF16), vh)
            o = o * lax.rsqrt(jnp.mean(o * o, axis=-1, keepdims=True) + EPS) * gate[:, hs]
            o_ref[rows, hs] = o.astype(BF16)
            st_ref[h] = st * dec[:, hs] + _dot_tn(vh, k_out[:, hs])

    @pl.when(ti == pl.num_programs(1) - 1)
    def _():
        for h in range(HG_HEADS):
            sout_ref[h] = st_ref[h].T


def _hgrn(f_pre, q_pre, v, g_pre, hg_lb, hg_norm, s0, layer, batch, ln):
    tt = min(ln, 256)
    chunk = min(tt, CHUNK)
    nt = ln // tt
    has_init = s0 is not None
    depth = hg_lb.shape[0]
    row = lambda b, t: (b * nt + t, 0)
    st_spec = pl.BlockSpec((None, HG_HEADS, HG_DK, HG_DV), lambda b, t: (b, 0, 0, 0))
    in_specs = [
        pl.BlockSpec((tt, HG_FDIM), row), pl.BlockSpec((tt, HG_FDIM), row),
        pl.BlockSpec((tt, HG_IDIM), row), pl.BlockSpec((tt, HG_IDIM), row),
        _const_spec((depth, HG_FDIM)), _const_spec((1, HG_IDIM)),
    ]
    args = [f_pre, q_pre, v, g_pre, hg_lb, hg_norm]
    if has_init:
        in_specs.append(st_spec)
        args.append(s0)
    return pl.pallas_call(
        functools.partial(_hgrn_kernel, layer=layer, chunk=chunk, has_init=has_init),
        grid=(batch, nt),
        in_specs=in_specs,
        out_specs=[pl.BlockSpec((tt, HG_IDIM), row), st_spec],
        out_shape=[jax.ShapeDtypeStruct((batch * ln, HG_IDIM), BF16),
                   jax.ShapeDtypeStruct((batch, HG_HEADS, HG_DK, HG_DV), F32)],
        scratch_shapes=[pltpu.VMEM((HG_HEADS, HG_DV, HG_DK), F32)],
        compiler_params=_cparams(("parallel", "arbitrary")),
        name="hgrn",
    )(*args)


ROUTE_GROUP_LANE = N_EXPERTS


def _route(logits, cnt_ref):
    lane_i = lax.broadcasted_iota(jnp.int32, logits.shape, 1)
    lane = lane_i.astype(F32)
    big = float(LANES)
    neg = -jnp.inf
    is_g = jnp.logical_and(lane_i >= ROUTE_GROUP_LANE, lane_i < ROUTE_GROUP_LANE + N_GROUPS)
    gl = jnp.where(is_g, logits, neg)
    gmax = jnp.max(gl, axis=-1, keepdims=True)
    gidx = jnp.min(jnp.where(gl == gmax, lane - ROUTE_GROUP_LANE, big), axis=-1, keepdims=True)
    g_top = 1.0 / jnp.sum(jnp.exp(gl - gmax), axis=-1, keepdims=True)
    lane_group = _div_pow2(lane_i, EXPERTS_PER_GROUP).astype(F32)
    in_group = jnp.logical_and(lane_i < N_EXPERTS, lane_group == gidx)
    el = jnp.where(in_group, logits, neg)
    m1 = jnp.max(el, axis=-1, keepdims=True)
    i1 = jnp.min(jnp.where(el == m1, lane, big), axis=-1, keepdims=True)
    el2 = jnp.where(lane == i1, neg, el)
    m2 = jnp.max(el2, axis=-1, keepdims=True)
    i2 = jnp.min(jnp.where(el2 == m2, lane, big), axis=-1, keepdims=True)
    r = jnp.exp(m2 - m1)
    w1 = g_top / (1.0 + r)
    w2 = g_top * r / (1.0 + r)
    tm = logits.shape[0]
    hit = jnp.logical_or(lane == i1, lane == i2)
    rr = lax.broadcasted_iota(jnp.int32, (tm, tm), 0)
    cc = lax.broadcasted_iota(jnp.int32, (tm, tm), 1)
    before = _dot((cc < rr).astype(BF16), jnp.where(hit, 1.0, 0.0).astype(BF16)) + cnt_ref[...]
    r1 = jnp.sum(jnp.where(lane == i1, before, 0.0), axis=-1, keepdims=True)
    r2 = jnp.sum(jnp.where(lane == i2, before, 0.0), axis=-1, keepdims=True)
    cnt_ref[...] += jnp.sum(jnp.where(hit, 1.0, 0.0), axis=0, keepdims=True)
    out = jnp.where(lane_i == 0, i1, 0.0)
    out = jnp.where(lane_i == 1, i2, out)
    out = jnp.where(lane_i == 2, w1, out)
    out = jnp.where(lane_i == 3, w2, out)
    out = jnp.where(lane_i == 4, r1, out)
    out = jnp.where(lane_i == 5, r2, out)
    return out


def _post_kernel(x_ref, a_ref, o_ref, ple_ref, wg_ref, wa_ref, wb_ref, wo_ref, g1_ref, b1_ref,
                 wr_ref, wpg_ref, wp_ref, x1_ref, base_ref, route_ref, cnt_ref, *, alpha):
    @pl.when(pl.program_id(0) == 0)
    def _():
        cnt_ref[...] = jnp.zeros_like(cnt_ref)

    x = x_ref[...]
    xb = x.astype(BF16)
    ya = _dot(a_ref[...], wa_ref[...])
    merged = _sigmoid(_dot(xb, wg_ref[:, 0:D_MODEL])) * ya
    yb = _dot(o_ref[...], wb_ref[...])
    merged = merged + _sigmoid(_dot(xb, wg_ref[:, D_MODEL:2 * D_MODEL])) * yb
    mix = _dot(merged.astype(BF16), wo_ref[...])
    x1 = _layernorm(alpha * x + mix, g1_ref[...], b1_ref[...])
    _to_token_tiles(x1_ref, x1)
    x1b = x1.astype(BF16)
    route_ref[...] = _route(_dot(x1b, wr_ref[...]), cnt_ref)
    ple = _dot(ple_ref[...].astype(BF16), wp_ref[...])
    base_ref[...] = alpha * x1 + _sigmoid(_dot(x1b, wpg_ref[...])) * ple


def _post(xn, a, o, ple, lw, alpha):
    t = xn.shape[0]
    tm = TOKEN_TILE
    row = lambda i: (i, 0)
    d = D_MODEL
    return pl.pallas_call(
        functools.partial(_post_kernel, alpha=alpha),
        grid=(t // tm,),
        in_specs=[
            pl.BlockSpec((tm, d), row), pl.BlockSpec((tm, 512), row), pl.BlockSpec((tm, HG_IDIM), row),
            pl.BlockSpec((tm, PLE_DIM), row),
            _const_spec((d, 2 * d)), _const_spec((512, d)), _const_spec((HG_IDIM, d)), _const_spec((d, d)),
            _const_spec((1, d)), _const_spec((1, d)),
            _const_spec((d, LANES)), _const_spec((d, d)), _const_spec((PLE_DIM, d)),
        ],
        out_specs=[pl.BlockSpec((tm * TILE_ROWS, LANES), row), pl.BlockSpec((tm, d), row),
                   pl.BlockSpec((tm, LANES), row), pl.BlockSpec((1, LANES), lambda i: (0, 0))],
        out_shape=[jax.ShapeDtypeStruct((t * TILE_ROWS, LANES), F32), jax.ShapeDtypeStruct((t, d), F32),
                   jax.ShapeDtypeStruct((t, LANES), F32), jax.ShapeDtypeStruct((1, LANES), F32)],
        compiler_params=_cparams(("arbitrary",)),
        name="post",
    )(xn, a, o, ple, lw["w_gate"], lw["w_br_a"], lw["w_br_b"], lw["w_out"], lw["ln1_g"], lw["ln1_b"],
      lw["w_route"], lw["w_ple_gate"], lw["w_ple"])


TILE_ROWS = D_MODEL // LANES


def _to_token_tiles(ref, x):
    n = x.shape[0]
    for c in range(TILE_ROWS):
        ref[pl.ds(c, n, stride=TILE_ROWS), :] = x[:, c * LANES:(c + 1) * LANES]


def _from_token_tiles(ref, first_token, n):
    return jnp.concatenate(
        [ref[pl.ds(first_token * TILE_ROWS + c, n, stride=TILE_ROWS), :] for c in range(TILE_ROWS)], axis=1)


def _tile_copy(idx_ref, j, src_hbm, dst, dst_token, sem):
    r = pl.multiple_of(idx_ref[0, j], TILE_ROWS)
    d = dst_token * TILE_ROWS
    if not isinstance(d, int):
        d = pl.multiple_of(d, TILE_ROWS)
    return pltpu.make_async_copy(src_hbm.at[pl.ds(r, TILE_ROWS)], dst.at[pl.ds(d, TILE_ROWS)], sem)


def _issue_tiles_loop(idx_ref, src_hbm, dst, sem, n_tokens, dst_off=0):
    def body(j, carry):
        _tile_copy(idx_ref, j, src_hbm, dst, dst_off + j, sem).start()
        return carry

    lax.fori_loop(0, n_tokens, body, 0, unroll=8)


def _issue_tiles_static(idx_ref, src_hbm, dst, sem, lo, hi, dst_off=0):
    for j in range(lo, hi):
        _tile_copy(idx_ref, j, src_hbm, dst, dst_off + j, sem).start(priority=j % 2)


def _wait_tiles(src_hbm, dst, sem):
    pltpu.make_async_copy(src_hbm.at[pl.ds(0, dst.shape[0])], dst, sem).wait()


def _idx_spec(tile, n_blocks, ahead):
    if ahead is None:
        imap = lambda i, *_: (0, 0, 0)
    else:
        imap = lambda i, *_: (jnp.minimum(i + ahead, n_blocks - 1), 0, 0)
    return pl.BlockSpec((None, 1, tile), imap, memory_space=pltpu.SMEM)


MOE_SLOTS = 3


def _moe_kernel(te_ref, idx0_ref, idx1_ref, idx2_ref, x_hbm, w1_ref, w3_ref, w2_ref, y_ref, xbuf, sem):
    i = pl.program_id(0)
    n = pl.num_programs(0)
    tm = MOE_TILE
    slot = i % MOE_SLOTS

    @pl.when(i == 0)
    def _():
        _issue_tiles_loop(idx0_ref, x_hbm, xbuf.at[0], sem.at[0], tm)
        _issue_tiles_loop(idx1_ref, x_hbm, xbuf.at[1], sem.at[1], tm)

    _wait_tiles(x_hbm, xbuf.at[slot], sem.at[slot])
    xb = _from_token_tiles(xbuf.at[slot], 0, tm).astype(BF16)
    ahead = (i + 2) % MOE_SLOTS
    nxt = xbuf.at[ahead]
    nsem = sem.at[ahead]
    group = tm // 4
    hs = []
    half = D_EXPERT // 2
    for c in range(2):
        cs = slice(c * half, (c + 1) * half)
        h1 = _dot(xb, w1_ref[:, cs])
        _issue_tiles_static(idx2_ref, x_hbm, nxt, nsem, 2 * c * group, (2 * c + 1) * group)
        h3 = _dot(xb, w3_ref[:, cs])
        _issue_tiles_static(idx2_ref, x_hbm, nxt, nsem, (2 * c + 1) * group, (2 * c + 2) * group)
        hs.append((h1 * _sigmoid(h1) * h3).astype(BF16))
    _to_token_tiles(y_ref, _dot(jnp.concatenate(hs, axis=1), w2_ref[...]))

    @pl.when(i == n - 1)
    def _():
        _wait_tiles(x_hbm, xbuf.at[(i + 1) % MOE_SLOTS], sem.at[(i + 1) % MOE_SLOTS])
        _wait_tiles(x_hbm, nxt, nsem)


def _moe(x_tiles, src, tile_e, w1, w3, w2):
    tm = MOE_TILE
    n_tiles = src.shape[0]
    d = D_MODEL
    grid_spec = pltpu.PrefetchScalarGridSpec(
        num_scalar_prefetch=1,
        grid=(n_tiles,),
        in_specs=[
            _idx_spec(tm, n_tiles, None), _idx_spec(tm, n_tiles, 1), _idx_spec(tm, n_tiles, 2),
            pl.BlockSpec(memory_space=pl.ANY),
            pl.BlockSpec((None, d, D_EXPERT), lambda i, te: (te[i], 0, 0)),
            pl.BlockSpec((None, d, D_EXPERT), lambda i, te: (te[i], 0, 0)),
            pl.BlockSpec((None, D_EXPERT, d), lambda i, te: (te[i], 0, 0)),
        ],
        out_specs=pl.BlockSpec((tm * TILE_ROWS, LANES), lambda i, te: (i, 0)),
        scratch_shapes=[pltpu.VMEM((MOE_SLOTS, tm * TILE_ROWS, LANES), F32), pltpu.SemaphoreType.DMA((MOE_SLOTS,))],
    )
    src3 = src.reshape(n_tiles, 1, tm)
    return pl.pallas_call(
        _moe_kernel,
        grid_spec=grid_spec,
        out_shape=jax.ShapeDtypeStruct((n_tiles * tm * TILE_ROWS, LANES), F32),
        compiler_params=_cparams(("arbitrary",)),
        name="moe_experts",
    )(tile_e, src3, src3, src3, x_tiles, w1, w3, w2)


def _dispatch_plan(route, cnt, n_tiles):
    t = route.shape[0]
    a = t * TOP_K
    tm = MOE_TILE
    eid = jnp.arange(N_EXPERTS, dtype=jnp.int32)
    counts = cnt[0, 0:N_EXPERTS].astype(jnp.int32)
    tiles_e = (counts + tm - 1) // tm
    tile_end = jnp.cumsum(tiles_e)
    pstart = (tile_end - tiles_e) * tm
    e = route[:, 0:TOP_K].astype(jnp.int32)
    rank = route[:, 4:4 + TOP_K].astype(jnp.int32)
    pos = rank + jnp.sum(jnp.where(e[:, :, None] == eid[None, None, :], pstart[None, None, :], 0), axis=-1)
    pad_e = tiles_e * tm - counts
    cpad = jnp.cumsum(pad_e)
    k = jnp.arange(n_tiles * tm - a, dtype=jnp.int32)
    ke = jnp.sum((cpad[None, :] <= k[:, None]).astype(jnp.int32), axis=1)
    inside = k + jnp.sum(jnp.where(jnp.minimum(ke, N_EXPERTS - 1)[:, None] == eid[None, :],
                                   (pstart + counts - (cpad - pad_e))[None, :], 0), axis=1)
    padpos = jnp.where(ke < N_EXPERTS, inside, tile_end[-1] * tm + (k - cpad[-1]))
    keys = jnp.concatenate([pos.reshape(a), padpos])
    vals = jnp.concatenate([jnp.repeat(jnp.arange(t, dtype=jnp.int32), TOP_K), jnp.zeros_like(k)])
    _, src = lax.sort_key_val(keys, vals)
    jt = jnp.arange(n_tiles, dtype=jnp.int32)
    tile_e = jnp.minimum(jnp.sum((tile_end[None, :] <= jt[:, None]).astype(jnp.int32), axis=1), N_EXPERTS - 1)
    return (src * TILE_ROWS).reshape(n_tiles, tm), pos.reshape(a) * TILE_ROWS, tile_e.astype(jnp.int32)


def _final_kernel(p0c_ref, p1c_ref, p0n_ref, p1n_ref, base_ref, route_ref, g_ref, b_ref, y_hbm, o_ref, ybuf, sem):
    i = pl.program_id(0)
    n = pl.num_programs(0)
    tm = FINAL_TILE
    slot = i % 2

    @pl.when(i == 0)
    def _():
        _issue_tiles_loop(p0c_ref, y_hbm, ybuf.at[0], sem.at[0], tm, 0)
        _issue_tiles_loop(p1c_ref, y_hbm, ybuf.at[0], sem.at[0], tm, tm)

    _wait_tiles(y_hbm, ybuf.at[slot], sem.at[slot])
    nxt = ybuf.at[1 - slot]
    nsem = sem.at[1 - slot]
    _issue_tiles_static(p0n_ref, y_hbm, nxt, nsem, 0, tm, 0)
    _issue_tiles_static(p1n_ref, y_hbm, nxt, nsem, 0, tm, tm)
    w = route_ref[...]
    cur = ybuf.at[slot]
    h = base_ref[...] + w[:, 2:3] * _from_token_tiles(cur, 0, tm) + w[:, 3:4] * _from_token_tiles(cur, tm, tm)
    o_ref[...] = _layernorm(h, g_ref[...], b_ref[...])

    @pl.when(i == n - 1)
    def _():
        _wait_tiles(y_hbm, nxt, nsem)


def _final(base, y_tiles, pos, route, g, b):
    t, d = base.shape
    tm = FINAL_TILE
    nb = t // tm
    row = lambda i: (i, 0)
    pos2 = pos.reshape(t, TOP_K)
    p0 = pos2[:, 0].reshape(nb, 1, tm)
    p1 = pos2[:, 1].reshape(nb, 1, tm)
    first, nxt = _idx_spec(tm, nb, None), _idx_spec(tm, nb, 1)
    return pl.pallas_call(
        _final_kernel,
        grid=(nb,),
        in_specs=[first, first, nxt, nxt,
                  pl.BlockSpec((tm, d), row), pl.BlockSpec((tm, LANES), row),
                  _const_spec((1, d)), _const_spec((1, d)),
                  pl.BlockSpec(memory_space=pl.ANY)],
        out_specs=pl.BlockSpec((tm, d), row),
        out_shape=jax.ShapeDtypeStruct((t, d), F32),
        scratch_shapes=[pltpu.VMEM((2, 2 * tm * TILE_ROWS, LANES), F32), pltpu.SemaphoreType.DMA((2,))],
        compiler_params=_cparams(("arbitrary",)),
        name="final_ln",
    )(p0, p1, p0, p1, base, route, g, b, y_tiles)


def _rot_cols(w):
    half = ROPE_DIM // 2
    return jnp.concatenate([-w[..., half:], w[..., :half]], axis=-1)


def _layer_weights(i, w_in, q_norm, w_uq, kv_norm, w_uk, w_uv, hg_norm, w_br_a, w_br_b, w_out, ln1_g, ln1_b,
                   w_rg, w_re, w_e_gate, w_e_up, w_e_down, w_ple, w_ple_gate, ln2_g, ln2_b):
    d = D_MODEL
    nh = MLA_HEADS
    win = w_in[i]
    c_kr = Q_LORA + KV_LORA
    c_f = c_kr + ROPE_DIM
    c_ga = c_f + N_HG
    w_kr = win[:, c_kr:c_f]
    zpad = jnp.zeros((d, LANES - ROPE_DIM), F32)
    wa = jnp.concatenate([win[:, 0:c_kr], win[:, c_f:c_ga], w_kr, zpad, _rot_cols(w_kr), zpad], axis=1)

    wq3 = w_uq[i].reshape(Q_LORA, nh, NOPE_DIM + ROPE_DIM)
    zq = jnp.zeros((Q_LORA, nh, HEAD_PAD - NOPE_DIM - ROPE_DIM), F32)
    wq = jnp.concatenate([wq3, zq], axis=-1).reshape(Q_LORA, QK_PAD)
    wq_rot = jnp.concatenate([jnp.zeros((Q_LORA, nh, NOPE_DIM), F32), _rot_cols(wq3[..., NOPE_DIM:]), zq],
                             axis=-1).reshape(Q_LORA, QK_PAD)

    wuk = w_uk[i]
    wuv = w_uv[i]
    k_nope = jnp.concatenate([wuk, jnp.zeros((KV_LORA, nh, HEAD_PAD - NOPE_DIM), F32)], axis=-1)
    eye = jnp.eye(ROPE_DIM, dtype=F32)
    k_rope = jnp.concatenate([jnp.zeros((ROPE_DIM, NOPE_DIM), F32), eye,
                              jnp.zeros((ROPE_DIM, HEAD_PAD - NOPE_DIM - ROPE_DIM), F32)], axis=-1)
    k_rope = jnp.concatenate([jnp.tile(k_rope[:, None, :], (1, nh, 1)).reshape(ROPE_DIM, QK_PAD),
                              jnp.zeros((LANES - ROPE_DIM, QK_PAD), F32)], axis=0)
    zv = jnp.zeros((KV_LORA, nh // 2, V_DIM), F32)
    wuv_a3 = jnp.concatenate([wuv[:, 0::2, :], zv], axis=-1)
    wuv_b3 = jnp.concatenate([zv, wuv[:, 1::2, :]], axis=-1)
    wkv_top = jnp.concatenate([k_nope.reshape(KV_LORA, QK_PAD), wuv_a3.reshape(KV_LORA, 512),
                               wuv_b3.reshape(KV_LORA, 512)], axis=1)
    wkv_bot = jnp.concatenate([k_rope, jnp.zeros((LANES, QK_PAD), F32)], axis=1)
    wkv = jnp.concatenate([wkv_top, wkv_bot], axis=0)

    wuk_t = jnp.concatenate([jnp.transpose(wuk, (1, 2, 0)),
                             jnp.zeros((nh, HEAD_PAD - NOPE_DIM, KV_LORA), F32)], axis=1)
    p_sel = jnp.concatenate([jnp.zeros((NOPE_DIM, ROPE_DIM), F32), eye,
                             jnp.zeros((HEAD_PAD - NOPE_DIM - ROPE_DIM, ROPE_DIM), F32)], axis=0)

    w_route = jnp.concatenate([w_re[i], w_rg[i], jnp.zeros((d, LANES - N_EXPERTS - N_GROUPS), F32)], axis=1)
    bf = lambda w: w.astype(BF16)
    return {
        "wa": bf(wa), "q_norm": q_norm[i].reshape(1, -1), "kv_norm": kv_norm[i].reshape(1, -1),
        "wq": bf(wq), "wq_rot": bf(wq_rot), "wkv": bf(wkv),
        "wuk_t": bf(wuk_t), "p_sel": bf(p_sel),
        "wuv_a3": bf(jnp.transpose(wuv_a3, (1, 0, 2))), "wuv_b3": bf(jnp.transpose(wuv_b3, (1, 0, 2))),
        "hg_norm": hg_norm[i].reshape(1, -1),
        "w_gate": bf(win[:, c_ga:c_ga + 2 * d]), "w_br_a": bf(w_br_a[i]), "w_br_b": bf(w_br_b[i]),
        "w_out": bf(w_out[i]), "ln1_g": ln1_g[i].reshape(1, -1), "ln1_b": ln1_b[i].reshape(1, -1),
        "w_route": bf(w_route), "w_ple_gate": bf(w_ple_gate[i]), "w_ple": bf(w_ple[i]),
        "w1": bf(w_e_gate[i]), "w3": bf(w_e_up[i]), "w2": bf(w_e_down[i]),
        "ln2_g": ln2_g[i].reshape(1, -1), "ln2_b": ln2_b[i].reshape(1, -1),
    }


def _rope_table(pos, rows):
    half = ROPE_DIM // 2
    inv = ROPE_THETA ** (-jnp.arange(half, dtype=F32) / half)
    ang = pos.astype(F32)[:, None] * inv[None, :]
    c2 = jnp.concatenate([jnp.cos(ang), jnp.cos(ang)], axis=1)
    s2 = jnp.concatenate([jnp.sin(ang), jnp.sin(ang)], axis=1)
    n = pos.shape[0]
    ones = jnp.ones((n, NOPE_DIM), F32)
    z = lambda w: jnp.zeros((n, w), F32)
    tab = jnp.concatenate([
        ones, c2, z(HEAD_PAD - NOPE_DIM - ROPE_DIM),
        z(NOPE_DIM), s2, z(HEAD_PAD - NOPE_DIM - ROPE_DIM),
        c2, z(LANES - ROPE_DIM),
        s2, z(LANES - ROPE_DIM)], axis=1)
    if rows > n:
        tab = jnp.tile(tab, (rows // n, 1))
    return tab


def _trunk_layer(i, xn, ple, tab, lw, hg_lb, batch, ln, cache, alpha):
    q, k, va, vb, ckv, kr, f_pre, q_pre, hv, g_pre = _proj(xn, tab, lw)
    if cache is None:
        a = _attn_prompt(q, k, va, vb, batch, ln)
        o, st = _hgrn(f_pre, q_pre, hv, g_pre, hg_lb, lw["hg_norm"], None, i, batch, ln)
    else:
        a = _attn_sample(q, ckv, kr, cache[0], cache[1], lw, batch, ln)
        o, st = _hgrn(f_pre, q_pre, hv, g_pre, hg_lb, lw["hg_norm"], cache[2], i, batch, ln)
    x1, base, route, cnt = _post(xn, a, o, ple, lw, alpha)

    t = xn.shape[0]
    n_tiles = (t * TOP_K) // MOE_TILE + N_EXPERTS
    src, pos, tile_e = _dispatch_plan(route, cnt, n_tiles)
    y = _moe(x1, src, tile_e, lw["w1"], lw["w3"], lw["w2"])
    x2 = _final(base, y, pos, route, lw["ln2_g"], lw["ln2_b"])
    return x2, ckv, kr, st


def kernel(x_prompt, x_sample, p_prompt, p_sample, cache_ckv, cache_krope, state_hgrn, ln0_g, ln0_b, w_in, q_norm, w_uq, kv_norm, w_uk, w_uv, hg_lb, hg_norm, w_br_a, w_br_b, w_out, ln1_g, ln1_b, w_router_group, w_router_expert, w_e_gate, w_e_up, w_e_down, w_ple, w_ple_gate, ln2_g, ln2_b):
    depth = w_in.shape[0]
    bp, sp, d = x_prompt.shape
    bs, ss, _ = x_sample.shape
    past = cache_ckv.shape[2]
    alpha = (2 * depth) ** 0.25

    xp = _ln_rows(x_prompt.reshape(bp * sp, d), ln0_g, ln0_b)
    xs = _ln_rows(x_sample.reshape(bs * ss, d), ln0_g, ln0_b)
    tab_p = _rope_table(jnp.arange(sp, dtype=jnp.int32), max(sp, TOKEN_TILE))
    tab_s = _rope_table(past + jnp.arange(ss, dtype=jnp.int32), max(ss, TOKEN_TILE))

    outs = [[] for _ in range(6)]
    for i in range(depth):
        lw = _layer_weights(i, w_in, q_norm, w_uq, kv_norm, w_uk, w_uv, hg_norm, w_br_a, w_br_b, w_out,
                            ln1_g, ln1_b, w_router_group, w_router_expert, w_e_gate, w_e_up, w_e_down,
                            w_ple, w_ple_gate, ln2_g, ln2_b)
        xp, c, r, s = _trunk_layer(i, xp, p_prompt[i].reshape(bp * sp, -1), tab_p, lw, hg_lb, bp, sp, None, alpha)
        outs[0].append(c.reshape(bp, sp, -1))
        outs[1].append(r.reshape(bp, sp, -1))
        outs[2].append(s)
        cache = (cache_ckv[i], cache_krope[i], state_hgrn[i])
        xs, c, r, s = _trunk_layer(i, xs, p_sample[i].reshape(bs * ss, -1), tab_s, lw, hg_lb, bs, ss, cache, alpha)
        outs[3].append(c.reshape(bs, ss, -1))
        outs[4].append(r.reshape(bs, ss, -1))
        outs[5].append(s)
    return (xp.reshape(bp, sp, d), xs.reshape(bs, ss, d)) + tuple(jnp.stack(o) for o in outs)
```

```python
import functools

import numpy as np
import jax
import jax.numpy as jnp
from jax import lax
from jax.experimental import pallas as pl
from jax.experimental.pallas import tpu as pltpu

F32 = jnp.float32
BF16 = jnp.bfloat16

D_MODEL = 1024
CHUNK = 64
PLE_DIM = 256
MLA_HEADS = 8
NOPE_DIM = 64
ROPE_DIM = 32
V_DIM = 64
Q_LORA = 384
KV_LORA = 256
ROPE_THETA = 10000.0
ATTN_SCALE = (NOPE_DIM + ROPE_DIM) ** -0.5
HG_HEADS = 4
HG_DK = 128
HG_DV = 128
HG_FDIM = HG_HEADS * HG_DK
HG_IDIM = HG_HEADS * HG_DV
N_GROUPS = 4
EXPERTS_PER_GROUP = 8
N_EXPERTS = N_GROUPS * EXPERTS_PER_GROUP
TOP_K = 2
D_EXPERT = 512
EPS = 1e-6

LANES = 128
HEAD_PAD = LANES
QK_PAD = MLA_HEADS * HEAD_PAD
VMEM_LIMIT = 56 * 1024 * 1024

TOKEN_TILE = 512
ATTN_TILE = 256
MOE_TILE = 256
FINAL_TILE = 256


def _cparams(sem):
    return pltpu.CompilerParams(dimension_semantics=sem, vmem_limit_bytes=VMEM_LIMIT)


def _const_spec(shape):
    nd = len(shape)
    return pl.BlockSpec(shape, lambda *_: (0,) * nd, pipeline_mode=pl.Buffered(1))


def _dot(a, b):
    return jnp.dot(a, b, preferred_element_type=F32)


def _dot_nt(a, b):
    return lax.dot_general(a, b, (((1,), (1,)), ((), ())), preferred_element_type=F32)


def _dot_tn(a, b):
    return lax.dot_general(a, b, (((0,), (0,)), ((), ())), preferred_element_type=F32)


def _div_pow2(x, d):
    return jnp.right_shift(x, int(d).bit_length() - 1)


def _sigmoid(x):
    return 1.0 / (1.0 + jnp.exp(-x))


def _layernorm(x, g, b):
    mu = jnp.mean(x, axis=-1, keepdims=True)
    xc = x - mu
    var = jnp.mean(xc * xc, axis=-1, keepdims=True)
    return xc * lax.rsqrt(var + EPS) * g + b


def _rmsnorm(x, g):
    return x * lax.rsqrt(jnp.mean(x * x, axis=-1, keepdims=True) + EPS) * g


def _ln_kernel(x_ref, g_ref, b_ref, o_ref):
    o_ref[...] = _layernorm(x_ref[...], g_ref[...], b_ref[...])


def _ln_rows(x, g, b):
    t, d = x.shape
    tm = TOKEN_TILE
    return pl.pallas_call(
        _ln_kernel,
        grid=(t // tm,),
        in_specs=[pl.BlockSpec((tm, d), lambda i: (i, 0)), _const_spec((1, d)), _const_spec((1, d))],
        out_specs=pl.BlockSpec((tm, d), lambda i: (i, 0)),
        out_shape=jax.ShapeDtypeStruct((t, d), F32),
        compiler_params=_cparams(("parallel",)),
        name="ln0",
    )(x, g.reshape(1, d), b.reshape(1, d))


N_LAT = Q_LORA + KV_LORA
N_HG = 2 * HG_FDIM + 2 * HG_IDIM
COL_HG = N_LAT
COL_KR = N_LAT + N_HG
N_PROJ = COL_KR + 2 * LANES


def _proj_kernel(x_ref, tab_ref, wa_ref, qn_ref, kvn_ref, wq_ref, wqr_ref, wkv_ref,
                 q_ref, k_ref, va_ref, vb_ref, ckv_ref, kr_ref, f_ref, hq_ref, hv_ref, hg_ref):
    xb = x_ref[...].astype(BF16)
    cq_tab = tab_ref[:, 0:LANES]
    sq_tab = tab_ref[:, LANES:2 * LANES]
    ck_tab = tab_ref[:, 2 * LANES:3 * LANES]
    sk_tab = tab_ref[:, 3 * LANES:4 * LANES]

    lat = _dot(xb, wa_ref[:, 0:N_LAT])
    cqn = _rmsnorm(lat[:, 0:Q_LORA], qn_ref[...]).astype(BF16)
    ckvn = _rmsnorm(lat[:, Q_LORA:N_LAT], kvn_ref[...])
    ckv_ref[...] = ckvn

    qa = _dot(cqn, wq_ref[...])
    qb = _dot(cqn, wqr_ref[...])
    for h in range(MLA_HEADS):
        sl = slice(h * HEAD_PAD, (h + 1) * HEAD_PAD)
        q_ref[:, sl] = (qa[:, sl] * cq_tab + qb[:, sl] * sq_tab).astype(BF16)

    kr2 = _dot(xb, wa_ref[:, COL_KR:N_PROJ])
    kr = kr2[:, 0:LANES] * ck_tab + kr2[:, LANES:2 * LANES] * sk_tab
    kr_ref[...] = kr[:, 0:ROPE_DIM]

    kvin = jnp.concatenate([ckvn.astype(BF16), kr.astype(BF16)], axis=1)
    kv = _dot(kvin, wkv_ref[...])
    k_ref[...] = kv[:, 0:QK_PAD].astype(BF16)
    va_ref[...] = kv[:, QK_PAD:QK_PAD + 512].astype(BF16)
    vb_ref[...] = kv[:, QK_PAD + 512:QK_PAD + 1024].astype(BF16)

    f_ref[...] = _dot(xb, wa_ref[:, COL_HG:COL_HG + 512])
    hq_ref[...] = _dot(xb, wa_ref[:, COL_HG + 512:COL_HG + 1024])
    hv_ref[...] = _dot(xb, wa_ref[:, COL_HG + 1024:COL_HG + 1536]).astype(BF16)
    hg_ref[...] = _dot(xb, wa_ref[:, COL_HG + 1536:COL_HG + 2048])


def _proj(xn, tab, lw):
    t = xn.shape[0]
    tm = TOKEN_TILE
    ntab = tab.shape[0] // tm
    row = lambda i: (i, 0)
    outs = [
        (QK_PAD, BF16), (QK_PAD, BF16), (512, BF16), (512, BF16), (KV_LORA, F32), (ROPE_DIM, F32),
        (HG_FDIM, F32), (HG_FDIM, F32), (HG_IDIM, BF16), (HG_IDIM, F32),
    ]
    return pl.pallas_call(
        _proj_kernel,
        grid=(t // tm,),
        in_specs=[
            pl.BlockSpec((tm, D_MODEL), row),
            pl.BlockSpec((tm, 4 * LANES), lambda i: (i % ntab, 0)),
            _const_spec((D_MODEL, N_PROJ)),
            _const_spec((1, Q_LORA)), _const_spec((1, KV_LORA)),
            _const_spec((Q_LORA, QK_PAD)), _const_spec((Q_LORA, QK_PAD)),
            _const_spec((KV_LORA + LANES, 2 * QK_PAD)),
        ],
        out_specs=[pl.BlockSpec((tm, w), row) for w, _ in outs],
        out_shape=[jax.ShapeDtypeStruct((t, w), dt) for w, dt in outs],
        compiler_params=_cparams(("parallel",)),
        name="proj",
    )(xn, tab, lw["wa"], lw["q_norm"], lw["kv_norm"], lw["wq"], lw["wq_rot"], lw["wkv"])


LOG2E = 1.4426950408889634


def _attn_prompt_kernel(q_ref, k_ref, va_ref, vb_ref, o_ref, s_ref, m_ref, l_ref, acc_ref):
    tq = ATTN_TILE
    nh = MLA_HEADS
    qi = pl.program_id(1)
    rq = _div_pow2(lax.broadcasted_iota(jnp.int32, (tq, tq), 0), CHUNK)
    ck = _div_pow2(lax.broadcasted_iota(jnp.int32, (tq, tq), 1), CHUNK)
    diag_mask = ck <= rq
    c = ATTN_SCALE * LOG2E

    m_ref[...] = jnp.full(m_ref.shape, -jnp.inf, F32)
    l_ref[...] = jnp.zeros(l_ref.shape, F32)
    acc_ref[...] = jnp.zeros(acc_ref.shape, F32)

    def scores(kt, mask):
        ks = pl.multiple_of(kt * tq, tq)
        for h in range(nh):
            hs = slice(h * HEAD_PAD, (h + 1) * HEAD_PAD)
            s = _dot_nt(q_ref[:, hs], k_ref[pl.ds(ks, tq), hs]) * c
            if mask is not None:
                s = jnp.where(mask, s, -jnp.inf)
            s_ref[h, :, pl.ds(ks, tq)] = s
            m_ref[h] = jnp.maximum(m_ref[h], jnp.maximum(s[:, 0:LANES], s[:, LANES:2 * LANES]))

    def p1(kt, carry):
        scores(kt, None)
        return carry

    lax.fori_loop(0, qi, p1, 0)
    scores(qi, diag_mask)
    for h in range(nh):
        m_ref[h] = jnp.broadcast_to(jnp.max(m_ref[h], axis=-1, keepdims=True), (tq, LANES))

    def p2(kt, carry):
        ks = pl.multiple_of(kt * tq, tq)
        for h in range(nh):
            v_ref = va_ref if h % 2 == 0 else vb_ref
            pair = h // 2
            mb = m_ref[h]
            p_lo = jnp.exp2(s_ref[h, :, pl.ds(ks, LANES)] - mb)
            p_hi = jnp.exp2(s_ref[h, :, pl.ds(ks + LANES, LANES)] - mb)
            l_ref[h] += p_lo + p_hi
            p = jnp.concatenate([p_lo, p_hi], axis=1).astype(BF16)
            acc_ref[h] += _dot(p, v_ref[pl.ds(ks, tq), pair * LANES:(pair + 1) * LANES])
        return carry

    lax.fori_loop(0, qi + 1, p2, 0)
    for pair in range(nh // 2):
        he, ho = 2 * pair, 2 * pair + 1
        o = (acc_ref[he] / jnp.sum(l_ref[he], axis=-1, keepdims=True)
             + acc_ref[ho] / jnp.sum(l_ref[ho], axis=-1, keepdims=True))
        o_ref[:, pair * LANES:(pair + 1) * LANES] = o.astype(BF16)


def _attn_prompt(q, k, va, vb, batch, seq):
    tq = ATTN_TILE
    nq = seq // tq
    nh = MLA_HEADS
    return pl.pallas_call(
        _attn_prompt_kernel,
        grid=(batch, nq),
        in_specs=[
            pl.BlockSpec((tq, QK_PAD), lambda b, i: (b * nq + i, 0)),
            pl.BlockSpec((seq, QK_PAD), lambda b, i: (b, 0)),
            pl.BlockSpec((seq, 512), lambda b, i: (b, 0)),
            pl.BlockSpec((seq, 512), lambda b, i: (b, 0)),
        ],
        out_specs=pl.BlockSpec((tq, 512), lambda b, i: (b * nq + i, 0)),
        out_shape=jax.ShapeDtypeStruct((batch * seq, 512), BF16),
        scratch_shapes=[pltpu.VMEM((nh, tq, seq), F32), pltpu.VMEM((nh, tq, LANES), F32),
                        pltpu.VMEM((nh, tq, LANES), F32), pltpu.VMEM((nh, tq, LANES), F32)],
        compiler_params=_cparams(("parallel", "arbitrary")),
        name="attn_prompt",
    )(q, k, va, vb)


SAMPLE_KEY_TILE = 1024


def _attn_sample_kernel(q_ref, ckvn_ref, krn_ref, cc_ref, ck_ref, wukt_ref, psel_ref, wva_ref, wvb_ref,
                        o_ref, s_ref):
    ln = q_ref.shape[0]
    past = cc_ref.shape[0]
    nh = MLA_HEADS
    qlat, qrope = [], []
    for h in range(nh):
        qh = q_ref[:, h * HEAD_PAD:(h + 1) * HEAD_PAD]
        qlat.append(_dot(qh, wukt_ref[h]).astype(BF16))
        qrope.append(_dot(qh, psel_ref[...]).astype(BF16))
    qlat = jnp.concatenate(qlat, axis=0)
    qrope = jnp.concatenate(qrope, axis=0)

    ckv_new = ckvn_ref[...].astype(BF16)
    kr_new = krn_ref[...].astype(BF16)
    s_new = (_dot_nt(qlat, ckv_new) + _dot_nt(qrope, kr_new)) * ATTN_SCALE
    m = jnp.max(s_new, axis=-1, keepdims=True)
    tk = SAMPLE_KEY_TILE
    for kt in range(past // tk):
        rows = slice(kt * tk, (kt + 1) * tk)
        s = (_dot_nt(qlat, cc_ref[rows, :].astype(BF16)) + _dot_nt(qrope, ck_ref[rows, :].astype(BF16))) * ATTN_SCALE
        s_ref[:, rows] = s
        m = jnp.maximum(m, jnp.max(s, axis=-1, keepdims=True))
    p_new = jnp.exp(s_new - m)
    l = jnp.sum(p_new, axis=-1, keepdims=True)
    lat = _dot(p_new.astype(BF16), ckv_new)
    for kt in range(past // tk):
        rows = slice(kt * tk, (kt + 1) * tk)
        p = jnp.exp(s_ref[:, rows] - m)
        l = l + jnp.sum(p, axis=-1, keepdims=True)
        lat = lat + _dot(p.astype(BF16), cc_ref[rows, :].astype(BF16))
    lat = (lat / l).astype(BF16)
    for pair in range(nh // 2):
        he, ho = 2 * pair, 2 * pair + 1
        o = _dot(lat[he * ln:(he + 1) * ln], wva_ref[pair]) + _dot(lat[ho * ln:(ho + 1) * ln], wvb_ref[pair])
        o_ref[:, pair * LANES:(pair + 1) * LANES] = o.astype(BF16)


def _attn_sample(q, ckv_new, kr_new, cache_ckv, cache_kr, lw, batch, ln):
    past = cache_ckv.shape[1]
    nh = MLA_HEADS
    return pl.pallas_call(
        _attn_sample_kernel,
        grid=(batch,),
        in_specs=[
            pl.BlockSpec((ln, QK_PAD), lambda b: (b, 0)),
            pl.BlockSpec((ln, KV_LORA), lambda b: (b, 0)),
            pl.BlockSpec((ln, ROPE_DIM), lambda b: (b, 0)),
            pl.BlockSpec((None, past, KV_LORA), lambda b: (b, 0, 0)),
            pl.BlockSpec((None, past, ROPE_DIM), lambda b: (b, 0, 0)),
            _const_spec((nh, HEAD_PAD, KV_LORA)),
            _const_spec((HEAD_PAD, ROPE_DIM)),
            _const_spec((nh // 2, KV_LORA, LANES)),
            _const_spec((nh // 2, KV_LORA, LANES)),
        ],
        out_specs=pl.BlockSpec((ln, 512), lambda b: (b, 0)),
        out_shape=jax.ShapeDtypeStruct((batch * ln, 512), BF16),
        scratch_shapes=[pltpu.VMEM((nh * ln, past), F32)],
        compiler_params=_cparams(("parallel",)),
        name="attn_sample",
    )(q, ckv_new, kr_new, cache_ckv, cache_kr, lw["wuk_t"], lw["p_sel"], lw["wuv_a3"], lw["wuv_b3"])


def _hgrn_kernel(*refs, layer, chunk, has_init):
    if has_init:
        f_ref, q_ref, v_ref, g_ref, lb_ref, nrm_ref, s0_ref, o_ref, sout_ref, st_ref = refs
    else:
        f_ref, q_ref, v_ref, g_ref, lb_ref, nrm_ref, o_ref, sout_ref, st_ref = refs
        s0_ref = None
    tt = f_ref.shape[0]
    c = chunk
    ti = pl.program_id(1)

    @pl.when(ti == 0)
    def _():
        for h in range(HG_HEADS):
            if has_init:
                st_ref[h] = s0_ref[h].T
            else:
                st_ref[h] = jnp.zeros((HG_DV, HG_DK), F32)

    lbp = lb_ref[...]
    e = jnp.exp(lbp - jnp.max(lbp, axis=0, keepdims=True))
    tot = jnp.sum(e, axis=0, keepdims=True)
    part = jnp.zeros_like(tot)
    for j in range(1, layer + 1):
        part = part + e[j:j + 1]
    lb = part / tot
    log_lb = jnp.log(lb)
    log_1m = jnp.log1p(-lb)
    one_m = 1.0 - lb

    rid = lax.broadcasted_iota(jnp.int32, (c, 1), 0)
    rr = lax.broadcasted_iota(jnp.int32, (c, c), 0)
    cc = lax.broadcasted_iota(jnp.int32, (c, c), 1)
    levels = [c >> k for k in range(c.bit_length() - 1)]
    sums = [cc <= rr, cc > rr]
    for g in levels:
        ref_row = jnp.bitwise_and(rr, -g) + (g // 2 - 1)
        is_right = jnp.bitwise_and(rr, g - 1) >= g // 2
        lo_row = jnp.where(is_right, ref_row, rr)
        hi_row = jnp.where(is_right, rr, ref_row)
        sums.append(jnp.logical_and(cc > lo_row, cc <= hi_row))
    sum_mat = jnp.concatenate([m.astype(BF16) for m in sums], axis=0)
    pair_masks = [None] + [_div_pow2(rr, g) == _div_pow2(cc, g) for g in levels[1:]] + [rr == cc]

    for ci in range(tt // c):
        rows = slice(ci * c, (ci + 1) * c)
        z = f_ref[rows, :]
        ez = jnp.exp(-jnp.abs(z))
        rz = 1.0 / (1.0 + ez)
        log_sig = jnp.minimum(z, 0.0) - jnp.log1p(ez)
        kk = one_m * jnp.where(z >= 0, ez * rz, rz)
        y = log_1m + log_sig
        logf = jnp.maximum(log_lb, y) + jnp.log1p(jnp.exp(-jnp.abs(log_lb - y)))
        qp = q_ref[rows, :]
        qq = qp * _sigmoid(qp)

        nl = -logf
        hi = nl.astype(BF16)
        lo = (nl - hi.astype(F32)).astype(BF16)
        acc = _dot(sum_mat, hi) + _dot(sum_mat, lo)
        q_in = (qq * jnp.exp(-acc[0:c])).astype(BF16)
        dec = jnp.exp(-acc[c - 1:c])
        k_out = (kk * jnp.exp(-acc[c:2 * c])).astype(BF16)
        qs, ks = [], []
        for li, g in enumerate(levels):
            x = jnp.exp(-acc[(2 + li) * c:(3 + li) * c])
            right = jnp.bitwise_and(rid, g - 1) >= g // 2
            qs.append(jnp.where(right, qq * x, 0.0).astype(BF16))
            ks.append(jnp.where(right, 0.0, kk * x).astype(BF16))
        qs.append(qq.astype(BF16))
        ks.append(kk.astype(BF16))

        gp = g_ref[rows, :]
        gate = gp * _sigmoid(gp) * nrm_ref[...]

        for h in range(HG_HEADS):
            hs = slice(h * HG_DK, (h + 1) * HG_DK)
            a = None
            for ql, kl, mk in zip(qs, ks, pair_masks):
                al = _dot_nt(ql[:, hs], kl[:, hs])
                if mk is not None:
                    al = jnp.where(mk, al, 0.0)
                a = al if a is None else a + al
            vh = v_ref[rows, hs]
            st = st_ref[h]
            o = _dot_nt(q_in[:, hs], st.astype(BF16)) + _dot(a.astype(BF16), vh)
            o = o * lax.rsqrt(jnp.mean(o * o, axis=-1, keepdims=True) + EPS) * gate[:, hs]
            o_ref[rows, hs] = o.astype(BF16)
            st_ref[h] = st * dec[:, hs] + _dot_tn(vh, k_out[:, hs])

    @pl.when(ti == pl.num_programs(1) - 1)
    def _():
        for h in range(HG_HEADS):
            sout_ref[h] = st_ref[h].T


def _hgrn(f_pre, q_pre, v, g_pre, hg_lb, hg_norm, s0, layer, batch, ln):
    tt = min(ln, 256)
    chunk = min(tt, CHUNK)
    nt = ln // tt
    has_init = s0 is not None
    depth = hg_lb.shape[0]
    row = lambda b, t: (b * nt + t, 0)
    st_spec = pl.BlockSpec((None, HG_HEADS, HG_DK, HG_DV), lambda b, t: (b, 0, 0, 0))
    in_specs = [
        pl.BlockSpec((tt, HG_FDIM), row), pl.BlockSpec((tt, HG_FDIM), row),
        pl.BlockSpec((tt, HG_IDIM), row), pl.BlockSpec((tt, HG_IDIM), row),
        _const_spec((depth, HG_FDIM)), _const_spec((1, HG_IDIM)),
    ]
    args = [f_pre, q_pre, v, g_pre, hg_lb, hg_norm]
    if has_init:
        in_specs.append(st_spec)
        args.append(s0)
    return pl.pallas_call(
        functools.partial(_hgrn_kernel, layer=layer, chunk=chunk, has_init=has_init),
        grid=(batch, nt),
        in_specs=in_specs,
        out_specs=[pl.BlockSpec((tt, HG_IDIM), row), st_spec],
        out_shape=[jax.ShapeDtypeStruct((batch * ln, HG_IDIM), BF16),
                   jax.ShapeDtypeStruct((batch, HG_HEADS, HG_DK, HG_DV), F32)],
        scratch_shapes=[pltpu.VMEM((HG_HEADS, HG_DV, HG_DK), F32)],
        compiler_params=_cparams(("parallel", "arbitrary")),
        name="hgrn",
    )(*args)


ROUTE_GROUP_LANE = N_EXPERTS


def _route(logits, cnt_ref):
    lane_i = lax.broadcasted_iota(jnp.int32, logits.shape, 1)
    lane = lane_i.astype(F32)
    big = float(LANES)
    neg = -jnp.inf
    is_g = jnp.logical_and(lane_i >= ROUTE_GROUP_LANE, lane_i < ROUTE_GROUP_LANE + N_GROUPS)
    gl = jnp.where(is_g, logits, neg)
    gmax = jnp.max(gl, axis=-1, keepdims=True)
    gidx = jnp.min(jnp.where(gl == gmax, lane - ROUTE_GROUP_LANE, big), axis=-1, keepdims=True)
    g_top = 1.0 / jnp.sum(jnp.exp(gl - gmax), axis=-1, keepdims=True)
    lane_group = _div_pow2(lane_i, EXPERTS_PER_GROUP).astype(F32)
    in_group = jnp.logical_and(lane_i < N_EXPERTS, lane_group == gidx)
    el = jnp.where(in_group, logits, neg)
    m1 = jnp.max(el, axis=-1, keepdims=True)
    i1 = jnp.min(jnp.where(el == m1, lane, big), axis=-1, keepdims=True)
    el2 = jnp.where(lane == i1, neg, el)
    m2 = jnp.max(el2, axis=-1, keepdims=True)
    i2 = jnp.min(jnp.where(el2 == m2, lane, big), axis=-1, keepdims=True)
    r = jnp.exp(m2 - m1)
    w1 = g_top / (1.0 + r)
    w2 = g_top * r / (1.0 + r)
    tm = logits.shape[0]
    hit = jnp.logical_or(lane == i1, lane == i2)
    rr = lax.broadcasted_iota(jnp.int32, (tm, tm), 0)
    cc = lax.broadcasted_iota(jnp.int32, (tm, tm), 1)
    before = _dot((cc < rr).astype(BF16), jnp.where(hit, 1.0, 0.0).astype(BF16)) + cnt_ref[...]
    r1 = jnp.sum(jnp.where(lane == i1, before, 0.0), axis=-1, keepdims=True)
    r2 = jnp.sum(jnp.where(lane == i2, before, 0.0), axis=-1, keepdims=True)
    cnt_ref[...] += jnp.sum(jnp.where(hit, 1.0, 0.0), axis=0, keepdims=True)
    out = jnp.where(lane_i == 0, i1, 0.0)
    out = jnp.where(lane_i == 1, i2, out)
    out = jnp.where(lane_i == 2, w1, out)
    out = jnp.where(lane_i == 3, w2, out)
    out = jnp.where(lane_i == 4, r1, out)
    out = jnp.where(lane_i == 5, r2, out)
    return out


def _post_kernel(x_ref, a_ref, o_ref, ple_ref, wg_ref, wa_ref, wb_ref, wo_ref, g1_ref, b1_ref,
                 wr_ref, wpg_ref, wp_ref, x1_ref, base_ref, route_ref, cnt_ref, *, alpha):
    @pl.when(pl.program_id(0) == 0)
    def _():
        cnt_ref[...] = jnp.zeros_like(cnt_ref)

    x = x_ref[...]
    xb = x.astype(BF16)
    ya = _dot(a_ref[...], wa_ref[...])
    merged = _sigmoid(_dot(xb, wg_ref[:, 0:D_MODEL])) * ya
    yb = _dot(o_ref[...], wb_ref[...])
    merged = merged + _sigmoid(_dot(xb, wg_ref[:, D_MODEL:2 * D_MODEL])) * yb
    mix = _dot(merged.astype(BF16), wo_ref[...])
    x1 = _layernorm(alpha * x + mix, g1_ref[...], b1_ref[...])
    _to_token_tiles(x1_ref, x1)
    x1b = x1.astype(BF16)
    route_ref[...] = _route(_dot(x1b, wr_ref[...]), cnt_ref)
    ple = _dot(ple_ref[...].astype(BF16), wp_ref[...])
    base_ref[...] = alpha * x1 + _sigmoid(_dot(x1b, wpg_ref[...])) * ple


def _post(xn, a, o, ple, lw, alpha):
    t = xn.shape[0]
    tm = TOKEN_TILE
    row = lambda i: (i, 0)
    d = D_MODEL
    return pl.pallas_call(
        functools.partial(_post_kernel, alpha=alpha),
        grid=(t // tm,),
        in_specs=[
            pl.BlockSpec((tm, d), row), pl.BlockSpec((tm, 512), row), pl.BlockSpec((tm, HG_IDIM), row),
            pl.BlockSpec((tm, PLE_DIM), row),
            _const_spec((d, 2 * d)), _const_spec((512, d)), _const_spec((HG_IDIM, d)), _const_spec((d, d)),
            _const_spec((1, d)), _const_spec((1, d)),
            _const_spec((d, LANES)), _const_spec((d, d)), _const_spec((PLE_DIM, d)),
        ],
        out_specs=[pl.BlockSpec((tm * TILE_ROWS, LANES), row), pl.BlockSpec((tm, d), row),
                   pl.BlockSpec((tm, LANES), row), pl.BlockSpec((1, LANES), lambda i: (0, 0))],
        out_shape=[jax.ShapeDtypeStruct((t * TILE_ROWS, LANES), F32), jax.ShapeDtypeStruct((t, d), F32),
                   jax.ShapeDtypeStruct((t, LANES), F32), jax.ShapeDtypeStruct((1, LANES), F32)],
        compiler_params=_cparams(("arbitrary",)),
        name="post",
    )(xn, a, o, ple, lw["w_gate"], lw["w_br_a"], lw["w_br_b"], lw["w_out"], lw["ln1_g"], lw["ln1_b"],
      lw["w_route"], lw["w_ple_gate"], lw["w_ple"])


TILE_ROWS = D_MODEL // LANES


def _to_token_tiles(ref, x):
    n = x.shape[0]
    for c in range(TILE_ROWS):
        ref[pl.ds(c, n, stride=TILE_ROWS), :] = x[:, c * LANES:(c + 1) * LANES]


def _from_token_tiles(ref, first_token, n):
    return jnp.concatenate(
        [ref[pl.ds(first_token * TILE_ROWS + c, n, stride=TILE_ROWS), :] for c in range(TILE_ROWS)], axis=1)


def _tile_copy(idx_ref, j, src_hbm, dst, dst_token, sem):
    r = pl.multiple_of(idx_ref[0, j], TILE_ROWS)
    d = dst_token * TILE_ROWS
    if not isinstance(d, int):
        d = pl.multiple_of(d, TILE_ROWS)
    return pltpu.make_async_copy(src_hbm.at[pl.ds(r, TILE_ROWS)], dst.at[pl.ds(d, TILE_ROWS)], sem)


def _issue_tiles_loop(idx_ref, src_hbm, dst, sem, n_tokens, dst_off=0):
    def body(j, carry):
        _tile_copy(idx_ref, j, src_hbm, dst, dst_off + j, sem).start()
        return carry

    lax.fori_loop(0, n_tokens, body, 0, unroll=8)


def _issue_tiles_static(idx_ref, src_hbm, dst, sem, lo, hi, dst_off=0):
    for j in range(lo, hi):
        _tile_copy(idx_ref, j, src_hbm, dst, dst_off + j, sem).start(priority=j % 2)


def _wait_tiles(src_hbm, dst, sem):
    pltpu.make_async_copy(src_hbm.at[pl.ds(0, dst.shape[0])], dst, sem).wait()


def _idx_spec(tile, n_blocks, ahead):
    if ahead is None:
        imap = lambda i, *_: (0, 0, 0)
    else:
        imap = lambda i, *_: (jnp.minimum(i + ahead, n_blocks - 1), 0, 0)
    return pl.BlockSpec((None, 1, tile), imap, memory_space=pltpu.SMEM)


MOE_SLOTS = 3


def _moe_kernel(te_ref, idx0_ref, idx1_ref, idx2_ref, x_hbm, w1_ref, w3_ref, w2_ref, y_ref, xbuf, sem):
    i = pl.program_id(0)
    n = pl.num_programs(0)
    tm = MOE_TILE
    slot = i % MOE_SLOTS

    @pl.when(i == 0)
    def _():
        _issue_tiles_loop(idx0_ref, x_hbm, xbuf.at[0], sem.at[0], tm)
        _issue_tiles_loop(idx1_ref, x_hbm, xbuf.at[1], sem.at[1], tm)

    _wait_tiles(x_hbm, xbuf.at[slot], sem.at[slot])
    xb = _from_token_tiles(xbuf.at[slot], 0, tm).astype(BF16)
    ahead = (i + 2) % MOE_SLOTS
    nxt = xbuf.at[ahead]
    nsem = sem.at[ahead]
    group = tm // 4
    hs = []
    half = D_EXPERT // 2
    for c in range(2):
        cs = slice(c * half, (c + 1) * half)
        h1 = _dot(xb, w1_ref[:, cs])
        _issue_tiles_static(idx2_ref, x_hbm, nxt, nsem, 2 * c * group, (2 * c + 1) * group)
        h3 = _dot(xb, w3_ref[:, cs])
        _issue_tiles_static(idx2_ref, x_hbm, nxt, nsem, (2 * c + 1) * group, (2 * c + 2) * group)
        hs.append((h1 * _sigmoid(h1) * h3).astype(BF16))
    _to_token_tiles(y_ref, _dot(jnp.concatenate(hs, axis=1), w2_ref[...]))

    @pl.when(i == n - 1)
    def _():
        _wait_tiles(x_hbm, xbuf.at[(i + 1) % MOE_SLOTS], sem.at[(i + 1) % MOE_SLOTS])
        _wait_tiles(x_hbm, nxt, nsem)


def _moe(x_tiles, src, tile_e, w1, w3, w2):
    tm = MOE_TILE
    n_tiles = src.shape[0]
    d = D_MODEL
    grid_spec = pltpu.PrefetchScalarGridSpec(
        num_scalar_prefetch=1,
        grid=(n_tiles,),
        in_specs=[
            _idx_spec(tm, n_tiles, None), _idx_spec(tm, n_tiles, 1), _idx_spec(tm, n_tiles, 2),
            pl.BlockSpec(memory_space=pl.ANY),
            pl.BlockSpec((None, d, D_EXPERT), lambda i, te: (te[i], 0, 0)),
            pl.BlockSpec((None, d, D_EXPERT), lambda i, te: (te[i], 0, 0)),
            pl.BlockSpec((None, D_EXPERT, d), lambda i, te: (te[i], 0, 0)),
        ],
        out_specs=pl.BlockSpec((tm * TILE_ROWS, LANES), lambda i, te: (i, 0)),
        scratch_shapes=[pltpu.VMEM((MOE_SLOTS, tm * TILE_ROWS, LANES), F32), pltpu.SemaphoreType.DMA((MOE_SLOTS,))],
    )
    src3 = src.reshape(n_tiles, 1, tm)
    return pl.pallas_call(
        _moe_kernel,
        grid_spec=grid_spec,
        out_shape=jax.ShapeDtypeStruct((n_tiles * tm * TILE_ROWS, LANES), F32),
        compiler_params=_cparams(("arbitrary",)),
        name="moe_experts",
    )(tile_e, src3, src3, src3, x_tiles, w1, w3, w2)


def _dispatch_plan(route, cnt, n_tiles):
    t = route.shape[0]
    a = t * TOP_K
    tm = MOE_TILE
    eid = jnp.arange(N_EXPERTS, dtype=jnp.int32)
    counts = cnt[0, 0:N_EXPERTS].astype(jnp.int32)
    tiles_e = (counts + tm - 1) // tm
    tile_end = jnp.cumsum(tiles_e)
    pstart = (tile_end - tiles_e) * tm
    e = route[:, 0:TOP_K].astype(jnp.int32)
    rank = route[:, 4:4 + TOP_K].astype(jnp.int32)
    pos = rank + jnp.sum(jnp.where(e[:, :, None] == eid[None, None, :], pstart[None, None, :], 0), axis=-1)
    pad_e = tiles_e * tm - counts
    cpad = jnp.cumsum(pad_e)
    k = jnp.arange(n_tiles * tm - a, dtype=jnp.int32)
    ke = jnp.sum((cpad[None, :] <= k[:, None]).astype(jnp.int32), axis=1)
    inside = k + jnp.sum(jnp.where(jnp.minimum(ke, N_EXPERTS - 1)[:, None] == eid[None, :],
                                   (pstart + counts - (cpad - pad_e))[None, :], 0), axis=1)
    padpos = jnp.where(ke < N_EXPERTS, inside, tile_end[-1] * tm + (k - cpad[-1]))
    keys = jnp.concatenate([pos.reshape(a), padpos])
    vals = jnp.concatenate([jnp.repeat(jnp.arange(t, dtype=jnp.int32), TOP_K), jnp.zeros_like(k)])
    _, src = lax.sort_key_val(keys, vals)
    jt = jnp.arange(n_tiles, dtype=jnp.int32)
    tile_e = jnp.minimum(jnp.sum((tile_end[None, :] <= jt[:, None]).astype(jnp.int32), axis=1), N_EXPERTS - 1)
    return (src * TILE_ROWS).reshape(n_tiles, tm), pos.reshape(a) * TILE_ROWS, tile_e.astype(jnp.int32)


def _final_kernel(p0c_ref, p1c_ref, p0n_ref, p1n_ref, base_ref, route_ref, g_ref, b_ref, y_hbm, o_ref, ybuf, sem):
    i = pl.program_id(0)
    n = pl.num_programs(0)
    tm = FINAL_TILE
    slot = i % 2

    @pl.when(i == 0)
    def _():
        _issue_tiles_loop(p0c_ref, y_hbm, ybuf.at[0], sem.at[0], tm, 0)
        _issue_tiles_loop(p1c_ref, y_hbm, ybuf.at[0], sem.at[0], tm, tm)

    _wait_tiles(y_hbm, ybuf.at[slot], sem.at[slot])
    nxt = ybuf.at[1 - slot]
    nsem = sem.at[1 - slot]
    _issue_tiles_static(p0n_ref, y_hbm, nxt, nsem, 0, tm, 0)
    _issue_tiles_static(p1n_ref, y_hbm, nxt, nsem, 0, tm, tm)
    w = route_ref[...]
    cur = ybuf.at[slot]
    h = base_ref[...] + w[:, 2:3] * _from_token_tiles(cur, 0, tm) + w[:, 3:4] * _from_token_tiles(cur, tm, tm)
    o_ref[...] = _layernorm(h, g_ref[...], b_ref[...])

    @pl.when(i == n - 1)
    def _():
        _wait_tiles(y_hbm, nxt, nsem)


def _final(base, y_tiles, pos, route, g, b):
    t, d = base.shape
    tm = FINAL_TILE
    nb = t // tm
    row = lambda i: (i, 0)
    pos2 = pos.reshape(t, TOP_K)
    p0 = pos2[:, 0].reshape(nb, 1, tm)
    p1 = pos2[:, 1].reshape(nb, 1, tm)
    first, nxt = _idx_spec(tm, nb, None), _idx_spec(tm, nb, 1)
    return pl.pallas_call(
        _final_kernel,
        grid=(nb,),
        in_specs=[first, first, nxt, nxt,
                  pl.BlockSpec((tm, d), row), pl.BlockSpec((tm, LANES), row),
                  _const_spec((1, d)), _const_spec((1, d)),
                  pl.BlockSpec(memory_space=pl.ANY)],
        out_specs=pl.BlockSpec((tm, d), row),
        out_shape=jax.ShapeDtypeStruct((t, d), F32),
        scratch_shapes=[pltpu.VMEM((2, 2 * tm * TILE_ROWS, LANES), F32), pltpu.SemaphoreType.DMA((2,))],
        compiler_params=_cparams(("arbitrary",)),
        name="final_ln",
    )(p0, p1, p0, p1, base, route, g, b, y_tiles)


def _rot_cols(w):
    half = ROPE_DIM // 2
    return jnp.concatenate([-w[..., half:], w[..., :half]], axis=-1)


def _layer_weights(i, w_in, q_norm, w_uq, kv_norm, w_uk, w_uv, hg_norm, w_br_a, w_br_b, w_out, ln1_g, ln1_b,
                   w_rg, w_re, w_e_gate, w_e_up, w_e_down, w_ple, w_ple_gate, ln2_g, ln2_b):
    d = D_MODEL
    nh = MLA_HEADS
    win = w_in[i]
    c_kr = Q_LORA + KV_LORA
    c_f = c_kr + ROPE_DIM
    c_ga = c_f + N_HG
    w_kr = win[:, c_kr:c_f]
    zpad = jnp.zeros((d, LANES - ROPE_DIM), F32)
    wa = jnp.concatenate([win[:, 0:c_kr], win[:, c_f:c_ga], w_kr, zpad, _rot_cols(w_kr), zpad], axis=1)

    wq3 = w_uq[i].reshape(Q_LORA, nh, NOPE_DIM + ROPE_DIM)
    zq = jnp.zeros((Q_LORA, nh, HEAD_PAD - NOPE_DIM - ROPE_DIM), F32)
    wq = jnp.concatenate([wq3, zq], axis=-1).reshape(Q_LORA, QK_PAD)
    wq_rot = jnp.concatenate([jnp.zeros((Q_LORA, nh, NOPE_DIM), F32), _rot_cols(wq3[..., NOPE_DIM:]), zq],
                             axis=-1).reshape(Q_LORA, QK_PAD)

    wuk = w_uk[i]
    wuv = w_uv[i]
    k_nope = jnp.concatenate([wuk, jnp.zeros((KV_LORA, nh, HEAD_PAD - NOPE_DIM), F32)], axis=-1)
    eye = jnp.eye(ROPE_DIM, dtype=F32)
    k_rope = jnp.concatenate([jnp.zeros((ROPE_DIM, NOPE_DIM), F32), eye,
                              jnp.zeros((ROPE_DIM, HEAD_PAD - NOPE_DIM - ROPE_DIM), F32)], axis=-1)
    k_rope = jnp.concatenate([jnp.tile(k_rope[:, None, :], (1, nh, 1)).reshape(ROPE_DIM, QK_PAD),
                              jnp.zeros((LANES - ROPE_DIM, QK_PAD), F32)], axis=0)
    zv = jnp.zeros((KV_LORA, nh // 2, V_DIM), F32)
    wuv_a3 = jnp.concatenate([wuv[:, 0::2, :], zv], axis=-1)
    wuv_b3 = jnp.concatenate([zv, wuv[:, 1::2, :]], axis=-1)
    wkv_top = jnp.concatenate([k_nope.reshape(KV_LORA, QK_PAD), wuv_a3.reshape(KV_LORA, 512),
                               wuv_b3.reshape(KV_LORA, 512)], axis=1)
    wkv_bot = jnp.concatenate([k_rope, jnp.zeros((LANES, QK_PAD), F32)], axis=1)
    wkv = jnp.concatenate([wkv_top, wkv_bot], axis=0)

    wuk_t = jnp.concatenate([jnp.transpose(wuk, (1, 2, 0)),
                             jnp.zeros((nh, HEAD_PAD - NOPE_DIM, KV_LORA), F32)], axis=1)
    p_sel = jnp.concatenate([jnp.zeros((NOPE_DIM, ROPE_DIM), F32), eye,
                             jnp.zeros((HEAD_PAD - NOPE_DIM - ROPE_DIM, ROPE_DIM), F32)], axis=0)

    w_route = jnp.concatenate([w_re[i], w_rg[i], jnp.zeros((d, LANES - N_EXPERTS - N_GROUPS), F32)], axis=1)
    bf = lambda w: w.astype(BF16)
    return {
        "wa": bf(wa), "q_norm": q_norm[i].reshape(1, -1), "kv_norm": kv_norm[i].reshape(1, -1),
        "wq": bf(wq), "wq_rot": bf(wq_rot), "wkv": bf(wkv),
        "wuk_t": bf(wuk_t), "p_sel": bf(p_sel),
        "wuv_a3": bf(jnp.transpose(wuv_a3, (1, 0, 2))), "wuv_b3": bf(jnp.transpose(wuv_b3, (1, 0, 2))),
        "hg_norm": hg_norm[i].reshape(1, -1),
        "w_gate": bf(win[:, c_ga:c_ga + 2 * d]), "w_br_a": bf(w_br_a[i]), "w_br_b": bf(w_br_b[i]),
        "w_out": bf(w_out[i]), "ln1_g": ln1_g[i].reshape(1, -1), "ln1_b": ln1_b[i].reshape(1, -1),
        "w_route": bf(w_route), "w_ple_gate": bf(w_ple_gate[i]), "w_ple": bf(w_ple[i]),
        "w1": bf(w_e_gate[i]), "w3": bf(w_e_up[i]), "w2": bf(w_e_down[i]),
        "ln2_g": ln2_g[i].reshape(1, -1), "ln2_b": ln2_b[i].reshape(1, -1),
    }


def _rope_table(pos, rows):
    half = ROPE_DIM // 2
    inv = ROPE_THETA ** (-jnp.arange(half, dtype=F32) / half)
    ang = pos.astype(F32)[:, None] * inv[None, :]
    c2 = jnp.concatenate([jnp.cos(ang), jnp.cos(ang)], axis=1)
    s2 = jnp.concatenate([jnp.sin(ang), jnp.sin(ang)], axis=1)
    n = pos.shape[0]
    ones = jnp.ones((n, NOPE_DIM), F32)
    z = lambda w: jnp.zeros((n, w), F32)
    tab = jnp.concatenate([
        ones, c2, z(HEAD_PAD - NOPE_DIM - ROPE_DIM),
        z(NOPE_DIM), s2, z(HEAD_PAD - NOPE_DIM - ROPE_DIM),
        c2, z(LANES - ROPE_DIM),
        s2, z(LANES - ROPE_DIM)], axis=1)
    if rows > n:
        tab = jnp.tile(tab, (rows // n, 1))
    return tab


def _trunk_layer(i, xn, ple, tab, lw, hg_lb, batch, ln, cache, alpha):
    q, k, va, vb, ckv, kr, f_pre, q_pre, hv, g_pre = _proj(xn, tab, lw)
    if cache is None:
        a = _attn_prompt(q, k, va, vb, batch, ln)
        o, st = _hgrn(f_pre, q_pre, hv, g_pre, hg_lb, lw["hg_norm"], None, i, batch, ln)
    else:
        a = _attn_sample(q, ckv, kr, cache[0], cache[1], lw, batch, ln)
        o, st = _hgrn(f_pre, q_pre, hv, g_pre, hg_lb, lw["hg_norm"], cache[2], i, batch, ln)
    x1, base, route, cnt = _post(xn, a, o, ple, lw, alpha)

    t = xn.shape[0]
    n_tiles = (t * TOP_K) // MOE_TILE + N_EXPERTS
    src, pos, tile_e = _dispatch_plan(route, cnt, n_tiles)
    y = _moe(x1, src, tile_e, lw["w1"], lw["w3"], lw["w2"])
    x2 = _final(base, y, pos, route, lw["ln2_g"], lw["ln2_b"])
    return x2, ckv, kr, st


def kernel(x_prompt, x_sample, p_prompt, p_sample, cache_ckv, cache_krope, state_hgrn, ln0_g, ln0_b, w_in, q_norm, w_uq, kv_norm, w_uk, w_uv, hg_lb, hg_norm, w_br_a, w_br_b, w_out, ln1_g, ln1_b, w_router_group, w_router_expert, w_e_gate, w_e_up, w_e_down, w_ple, w_ple_gate, ln2_g, ln2_b):
    depth = w_in.shape[0]
    bp, sp, d = x_prompt.shape
    bs, ss, _ = x_sample.shape
    past = cache_ckv.shape[2]
    alpha = (2 * depth) ** 0.25

    xp = _ln_rows(x_prompt.reshape(bp * sp, d), ln0_g, ln0_b)
    xs = _ln_rows(x_sample.reshape(bs * ss, d), ln0_g, ln0_b)
    tab_p = _rope_table(jnp.arange(sp, dtype=jnp.int32), max(sp, TOKEN_TILE))
    tab_s = _rope_table(past + jnp.arange(ss, dtype=jnp.int32), max(ss, TOKEN_TILE))

    outs = [[] for _ in range(6)]
    for i in range(depth):
        lw = _layer_weights(i, w_in, q_norm, w_uq, kv_norm, w_uk, w_uv, hg_norm, w_br_a, w_br_b, w_out,
                            ln1_g, ln1_b, w_router_group, w_router_expert, w_e_gate, w_e_up, w_e_down,
                            w_ple, w_ple_gate, ln2_g, ln2_b)
        xp, c, r, s = _trunk_layer(i, xp, p_prompt[i].reshape(bp * sp, -1), tab_p, lw, hg_lb, bp, sp, None, alpha)
        outs[0].append(c.reshape(bp, sp, -1))
        outs[1].append(r.reshape(bp, sp, -1))
        outs[2].append(s)
        cache = (cache_ckv[i], cache_krope[i], state_hgrn[i])
        xs, c, r, s = _trunk_layer(i, xs, p_sample[i].reshape(bs * ss, -1), tab_s, lw, hg_lb, bs, ss, cache, alpha)
        outs[3].append(c.reshape(bs, ss, -1))
        outs[4].append(r.reshape(bs, ss, -1))
        outs[5].append(s)
    return (xp.reshape(bp, sp, d), xs.reshape(bs, ss, d)) + tuple(jnp.stack(o) for o in outs)
```

```python
import functools

import numpy as np
import jax
import jax.numpy as jnp
from jax import lax
from jax.experimental import pallas as pl
from jax.experimental.pallas import tpu as pltpu

F32 = jnp.float32
BF16 = jnp.bfloat16

D_MODEL = 1024
CHUNK = 64
PLE_DIM = 256
MLA_HEADS = 8
NOPE_DIM = 64
ROPE_DIM = 32
V_DIM = 64
Q_LORA = 384
KV_LORA = 256
ROPE_THETA = 10000.0
ATTN_SCALE = (NOPE_DIM + ROPE_DIM) ** -0.5
HG_HEADS = 4
HG_DK = 128
HG_DV = 128
HG_FDIM = HG_HEADS * HG_DK
HG_IDIM = HG_HEADS * HG_DV
N_GROUPS = 4
EXPERTS_PER_GROUP = 8
N_EXPERTS = N_GROUPS * EXPERTS_PER_GROUP
TOP_K = 2
D_EXPERT = 512
EPS = 1e-6

LANES = 128
HEAD_PAD = LANES
QK_PAD = MLA_HEADS * HEAD_PAD
VMEM_LIMIT = 56 * 1024 * 1024

TOKEN_TILE = 512
ATTN_TILE = 256
MOE_TILE = 256
HG_CHUNK = 128
FINAL_TILE = 256


def _cparams(sem):
    return pltpu.CompilerParams(dimension_semantics=sem, vmem_limit_bytes=VMEM_LIMIT)


def _const_spec(shape):
    nd = len(shape)
    return pl.BlockSpec(shape, lambda *_: (0,) * nd, pipeline_mode=pl.Buffered(1))


def _dot(a, b):
    return jnp.dot(a, b, preferred_element_type=F32)


def _dot_nt(a, b):
    return lax.dot_general(a, b, (((1,), (1,)), ((), ())), preferred_element_type=F32)


def _dot_tn(a, b):
    return lax.dot_general(a, b, (((0,), (0,)), ((), ())), preferred_element_type=F32)


def _div_pow2(x, d):
    return jnp.right_shift(x, int(d).bit_length() - 1)


def _sigmoid(x):
    return 1.0 / (1.0 + jnp.exp(-x))


def _layernorm(x, g, b):
    mu = jnp.mean(x, axis=-1, keepdims=True)
    xc = x - mu
    var = jnp.mean(xc * xc, axis=-1, keepdims=True)
    return xc * lax.rsqrt(var + EPS) * g + b


def _rmsnorm(x, g):
    return x * lax.rsqrt(jnp.mean(x * x, axis=-1, keepdims=True) + EPS) * g


def _ln_kernel(x_ref, g_ref, b_ref, o_ref):
    o_ref[...] = _layernorm(x_ref[...], g_ref[...], b_ref[...])


def _ln_rows(x, g, b):
    t, d = x.shape
    tm = TOKEN_TILE
    return pl.pallas_call(
        _ln_kernel,
        grid=(t // tm,),
        in_specs=[pl.BlockSpec((tm, d), lambda i: (i, 0)), _const_spec((1, d)), _const_spec((1, d))],
        out_specs=pl.BlockSpec((tm, d), lambda i: (i, 0)),
        out_shape=jax.ShapeDtypeStruct((t, d), F32),
        compiler_params=_cparams(("parallel",)),
        name="ln0",
    )(x, g.reshape(1, d), b.reshape(1, d))


N_LAT = Q_LORA + KV_LORA
N_HG = 2 * HG_FDIM + 2 * HG_IDIM
COL_HG = N_LAT
COL_KR = N_LAT + N_HG
N_PROJ = COL_KR + 2 * LANES


def _proj_kernel(x_ref, tab_ref, wa_ref, qn_ref, kvn_ref, wq_ref, wqr_ref, wkv_ref,
                 q_ref, k_ref, va_ref, vb_ref, ckv_ref, kr_ref, f_ref, hq_ref, hv_ref, hg_ref):
    xb = x_ref[...].astype(BF16)
    cq_tab = tab_ref[:, 0:LANES]
    sq_tab = tab_ref[:, LANES:2 * LANES]
    ck_tab = tab_ref[:, 2 * LANES:3 * LANES]
    sk_tab = tab_ref[:, 3 * LANES:4 * LANES]

    lat = _dot(xb, wa_ref[:, 0:N_LAT])
    cqn = _rmsnorm(lat[:, 0:Q_LORA], qn_ref[...]).astype(BF16)
    ckvn = _rmsnorm(lat[:, Q_LORA:N_LAT], kvn_ref[...])
    ckv_ref[...] = ckvn

    qa = _dot(cqn, wq_ref[...])
    qb = _dot(cqn, wqr_ref[...])
    for h in range(MLA_HEADS):
        sl = slice(h * HEAD_PAD, (h + 1) * HEAD_PAD)
        q_ref[:, sl] = (qa[:, sl] * cq_tab + qb[:, sl] * sq_tab).astype(BF16)

    kr2 = _dot(xb, wa_ref[:, COL_KR:N_PROJ])
    kr = kr2[:, 0:LANES] * ck_tab + kr2[:, LANES:2 * LANES] * sk_tab
    kr_ref[...] = kr[:, 0:ROPE_DIM]

    kvin = jnp.concatenate([ckvn.astype(BF16), kr.astype(BF16)], axis=1)
    kv = _dot(kvin, wkv_ref[...])
    k_ref[...] = kv[:, 0:QK_PAD].astype(BF16)
    va_ref[...] = kv[:, QK_PAD:QK_PAD + 512].astype(BF16)
    vb_ref[...] = kv[:, QK_PAD + 512:QK_PAD + 1024].astype(BF16)

    f_ref[...] = _dot(xb, wa_ref[:, COL_HG:COL_HG + 512])
    hq_ref[...] = _dot(xb, wa_ref[:, COL_HG + 512:COL_HG + 1024])
    hv_ref[...] = _dot(xb, wa_ref[:, COL_HG + 1024:COL_HG + 1536]).astype(BF16)
    hg_ref[...] = _dot(xb, wa_ref[:, COL_HG + 1536:COL_HG + 2048])


def _proj(xn, tab, lw):
    t = xn.shape[0]
    tm = TOKEN_TILE
    ntab = tab.shape[0] // tm
    row = lambda i: (i, 0)
    outs = [
        (QK_PAD, BF16), (QK_PAD, BF16), (512, BF16), (512, BF16), (KV_LORA, F32), (ROPE_DIM, F32),
        (HG_FDIM, F32), (HG_FDIM, F32), (HG_IDIM, BF16), (HG_IDIM, F32),
    ]
    return pl.pallas_call(
        _proj_kernel,
        grid=(t // tm,),
        in_specs=[
            pl.BlockSpec((tm, D_MODEL), row),
            pl.BlockSpec((tm, 4 * LANES), lambda i: (i % ntab, 0)),
            _const_spec((D_MODEL, N_PROJ)),
            _const_spec((1, Q_LORA)), _const_spec((1, KV_LORA)),
            _const_spec((Q_LORA, QK_PAD)), _const_spec((Q_LORA, QK_PAD)),
            _const_spec((KV_LORA + LANES, 2 * QK_PAD)),
        ],
        out_specs=[pl.BlockSpec((tm, w), row) for w, _ in outs],
        out_shape=[jax.ShapeDtypeStruct((t, w), dt) for w, dt in outs],
        compiler_params=_cparams(("parallel",)),
        name="proj",
    )(xn, tab, lw["wa"], lw["q_norm"], lw["kv_norm"], lw["wq"], lw["wq_rot"], lw["wkv"])


LOG2E = 1.4426950408889634


def _attn_prompt_kernel(q_ref, k_ref, va_ref, vb_ref, o_ref, s_ref, m_ref, l_ref, acc_ref):
    tq = ATTN_TILE
    nh = MLA_HEADS
    qi = pl.program_id(1)
    rq = _div_pow2(lax.broadcasted_iota(jnp.int32, (tq, tq), 0), CHUNK)
    ck = _div_pow2(lax.broadcasted_iota(jnp.int32, (tq, tq), 1), CHUNK)
    diag_mask = ck <= rq
    c = ATTN_SCALE * LOG2E

    m_ref[...] = jnp.full(m_ref.shape, -jnp.inf, F32)
    l_ref[...] = jnp.zeros(l_ref.shape, F32)
    acc_ref[...] = jnp.zeros(acc_ref.shape, F32)

    def scores(kt, mask):
        ks = pl.multiple_of(kt * tq, tq)
        for h in range(nh):
            hs = slice(h * HEAD_PAD, (h + 1) * HEAD_PAD)
            s = _dot_nt(q_ref[:, hs], k_ref[pl.ds(ks, tq), hs]) * c
            if mask is not None:
                s = jnp.where(mask, s, -jnp.inf)
            s_ref[h, :, pl.ds(ks, tq)] = s
            m_ref[h] = jnp.maximum(m_ref[h], jnp.maximum(s[:, 0:LANES], s[:, LANES:2 * LANES]))

    def p1(kt, carry):
        scores(kt, None)
        return carry

    lax.fori_loop(0, qi, p1, 0)
    scores(qi, diag_mask)
    for h in range(nh):
        m_ref[h] = jnp.broadcast_to(jnp.max(m_ref[h], axis=-1, keepdims=True), (tq, LANES))

    def p2(kt, carry):
        ks = pl.multiple_of(kt * tq, tq)
        for h in range(nh):
            v_ref = va_ref if h % 2 == 0 else vb_ref
            pair = h // 2
            mb = m_ref[h]
            p_lo = jnp.exp2(s_ref[h, :, pl.ds(ks, LANES)] - mb)
            p_hi = jnp.exp2(s_ref[h, :, pl.ds(ks + LANES, LANES)] - mb)
            l_ref[h] += p_lo + p_hi
            p = jnp.concatenate([p_lo, p_hi], axis=1).astype(BF16)
            acc_ref[h] += _dot(p, v_ref[pl.ds(ks, tq), pair * LANES:(pair + 1) * LANES])
        return carry

    lax.fori_loop(0, qi + 1, p2, 0)
    for pair in range(nh // 2):
        he, ho = 2 * pair, 2 * pair + 1
        o = (acc_ref[he] / jnp.sum(l_ref[he], axis=-1, keepdims=True)
             + acc_ref[ho] / jnp.sum(l_ref[ho], axis=-1, keepdims=True))
        o_ref[:, pair * LANES:(pair + 1) * LANES] = o.astype(BF16)


def _attn_prompt(q, k, va, vb, batch, seq):
    tq = ATTN_TILE
    nq = seq // tq
    nh = MLA_HEADS
    return pl.pallas_call(
        _attn_prompt_kernel,
        grid=(batch, nq),
        in_specs=[
            pl.BlockSpec((tq, QK_PAD), lambda b, i: (b * nq + i, 0)),
            pl.BlockSpec((seq, QK_PAD), lambda b, i: (b, 0)),
            pl.BlockSpec((seq, 512), lambda b, i: (b, 0)),
            pl.BlockSpec((seq, 512), lambda b, i: (b, 0)),
        ],
        out_specs=pl.BlockSpec((tq, 512), lambda b, i: (b * nq + i, 0)),
        out_shape=jax.ShapeDtypeStruct((batch * seq, 512), BF16),
        scratch_shapes=[pltpu.VMEM((nh, tq, seq), F32), pltpu.VMEM((nh, tq, LANES), F32),
                        pltpu.VMEM((nh, tq, LANES), F32), pltpu.VMEM((nh, tq, LANES), F32)],
        compiler_params=_cparams(("parallel", "arbitrary")),
        name="attn_prompt",
    )(q, k, va, vb)


SAMPLE_KEY_TILE = 1024


def _attn_sample_kernel(q_ref, ckvn_ref, krn_ref, cc_ref, ck_ref, wukt_ref, psel_ref, wva_ref, wvb_ref,
                        o_ref, s_ref):
    ln = q_ref.shape[0]
    past = cc_ref.shape[0]
    nh = MLA_HEADS
    qlat, qrope = [], []
    for h in range(nh):
        qh = q_ref[:, h * HEAD_PAD:(h + 1) * HEAD_PAD]
        qlat.append(_dot(qh, wukt_ref[h]).astype(BF16))
        qrope.append(_dot(qh, psel_ref[...]).astype(BF16))
    qlat = jnp.concatenate(qlat, axis=0)
    qrope = jnp.concatenate(qrope, axis=0)

    ckv_new = ckvn_ref[...].astype(BF16)
    kr_new = krn_ref[...].astype(BF16)
    s_new = (_dot_nt(qlat, ckv_new) + _dot_nt(qrope, kr_new)) * ATTN_SCALE
    m = jnp.max(s_new, axis=-1, keepdims=True)
    tk = SAMPLE_KEY_TILE
    for kt in range(past // tk):
        rows = slice(kt * tk, (kt + 1) * tk)
        s = (_dot_nt(qlat, cc_ref[rows, :].astype(BF16)) + _dot_nt(qrope, ck_ref[rows, :].astype(BF16))) * ATTN_SCALE
        s_ref[:, rows] = s
        m = jnp.maximum(m, jnp.max(s, axis=-1, keepdims=True))
    p_new = jnp.exp(s_new - m)
    l = jnp.sum(p_new, axis=-1, keepdims=True)
    lat = _dot(p_new.astype(BF16), ckv_new)
    for kt in range(past // tk):
        rows = slice(kt * tk, (kt + 1) * tk)
        p = jnp.exp(s_ref[:, rows] - m)
        l = l + jnp.sum(p, axis=-1, keepdims=True)
        lat = lat + _dot(p.astype(BF16), cc_ref[rows, :].astype(BF16))
    lat = (lat / l).astype(BF16)
    for pair in range(nh // 2):
        he, ho = 2 * pair, 2 * pair + 1
        o = _dot(lat[he * ln:(he + 1) * ln], wva_ref[pair]) + _dot(lat[ho * ln:(ho + 1) * ln], wvb_ref[pair])
        o_ref[:, pair * LANES:(pair + 1) * LANES] = o.astype(BF16)


def _attn_sample(q, ckv_new, kr_new, cache_ckv, cache_kr, lw, batch, ln):
    past = cache_ckv.shape[1]
    nh = MLA_HEADS
    return pl.pallas_call(
        _attn_sample_kernel,
        grid=(batch,),
        in_specs=[
            pl.BlockSpec((ln, QK_PAD), lambda b: (b, 0)),
            pl.BlockSpec((ln, KV_LORA), lambda b: (b, 0)),
            pl.BlockSpec((ln, ROPE_DIM), lambda b: (b, 0)),
            pl.BlockSpec((None, past, KV_LORA), lambda b: (b, 0, 0)),
            pl.BlockSpec((None, past, ROPE_DIM), lambda b: (b, 0, 0)),
            _const_spec((nh, HEAD_PAD, KV_LORA)),
            _const_spec((HEAD_PAD, ROPE_DIM)),
            _const_spec((nh // 2, KV_LORA, LANES)),
            _const_spec((nh // 2, KV_LORA, LANES)),
        ],
        out_specs=pl.BlockSpec((ln, 512), lambda b: (b, 0)),
        out_shape=jax.ShapeDtypeStruct((batch * ln, 512), BF16),
        scratch_shapes=[pltpu.VMEM((nh * ln, past), F32)],
        compiler_params=_cparams(("parallel",)),
        name="attn_sample",
    )(q, ckv_new, kr_new, cache_ckv, cache_kr, lw["wuk_t"], lw["p_sel"], lw["wuv_a3"], lw["wuv_b3"])


def _hgrn_kernel(*refs, layer, chunk, has_init):
    if has_init:
        f_ref, q_ref, v_ref, g_ref, lb_ref, nrm_ref, s0_ref, o_ref, sout_ref, st_ref = refs
    else:
        f_ref, q_ref, v_ref, g_ref, lb_ref, nrm_ref, o_ref, sout_ref, st_ref = refs
        s0_ref = None
    tt = f_ref.shape[0]
    c = chunk
    ti = pl.program_id(1)

    @pl.when(ti == 0)
    def _():
        for h in range(HG_HEADS):
            if has_init:
                st_ref[h] = s0_ref[h].T
            else:
                st_ref[h] = jnp.zeros((HG_DV, HG_DK), F32)

    lbp = lb_ref[...]
    e = jnp.exp(lbp - jnp.max(lbp, axis=0, keepdims=True))
    tot = jnp.sum(e, axis=0, keepdims=True)
    part = jnp.zeros_like(tot)
    for j in range(1, layer + 1):
        part = part + e[j:j + 1]
    lb = part / tot
    log_lb = jnp.log(lb)
    log_1m = jnp.log1p(-lb)
    one_m = 1.0 - lb

    rid = lax.broadcasted_iota(jnp.int32, (c, 1), 0)
    rr = lax.broadcasted_iota(jnp.int32, (c, c), 0)
    cc = lax.broadcasted_iota(jnp.int32, (c, c), 1)
    levels = [c >> k for k in range(c.bit_length() - 1)]
    sums = [cc <= rr, cc > rr]
    for g in levels:
        ref_row = jnp.bitwise_and(rr, -g) + (g // 2 - 1)
        is_right = jnp.bitwise_and(rr, g - 1) >= g // 2
        lo_row = jnp.where(is_right, ref_row, rr)
        hi_row = jnp.where(is_right, rr, ref_row)
        sums.append(jnp.logical_and(cc > lo_row, cc <= hi_row))
    sum_mat = jnp.concatenate([m.astype(BF16) for m in sums], axis=0)
    pair_masks = [None] + [_div_pow2(rr, g) == _div_pow2(cc, g) for g in levels[1:]] + [rr == cc]

    for ci in range(tt // c):
        rows = slice(ci * c, (ci + 1) * c)
        z = f_ref[rows, :]
        ez = jnp.exp(-jnp.abs(z))
        rz = 1.0 / (1.0 + ez)
        log_sig = jnp.minimum(z, 0.0) - jnp.log1p(ez)
        kk = one_m * jnp.where(z >= 0, ez * rz, rz)
        y = log_1m + log_sig
        logf = jnp.maximum(log_lb, y) + jnp.log1p(jnp.exp(-jnp.abs(log_lb - y)))
        qp = q_ref[rows, :]
        qq = qp * _sigmoid(qp)

        nl = -logf
        hi = nl.astype(BF16)
        lo = (nl - hi.astype(F32)).astype(BF16)
        acc = _dot(sum_mat, hi) + _dot(sum_mat, lo)
        q_in = (qq * jnp.exp(-acc[0:c])).astype(BF16)
        dec = jnp.exp(-acc[c - 1:c])
        k_out = (kk * jnp.exp(-acc[c:2 * c])).astype(BF16)
        qs, ks = [], []
        for li, g in enumerate(levels):
            x = jnp.exp(-acc[(2 + li) * c:(3 + li) * c])
            right = jnp.bitwise_and(rid, g - 1) >= g // 2
            qs.append(jnp.where(right, qq * x, 0.0).astype(BF16))
            ks.append(jnp.where(right, 0.0, kk * x).astype(BF16))
        qs.append(qq.astype(BF16))
        ks.append(kk.astype(BF16))

        gp = g_ref[rows, :]
        gate = gp * _sigmoid(gp) * nrm_ref[...]

        for h in range(HG_HEADS):
            hs = slice(h * HG_DK, (h + 1) * HG_DK)
            a = None
            for ql, kl, mk in zip(qs, ks, pair_masks):
                al = _dot_nt(ql[:, hs], kl[:, hs])
                if mk is not None:
                    al = jnp.where(mk, al, 0.0)
                a = al if a is None else a + al
            vh = v_ref[rows, hs]
            st = st_ref[h]
            o = _dot_nt(q_in[:, hs], st.astype(BF16)) + _dot(a.astype(BF16), vh)
            o = o * lax.rsqrt(jnp.mean(o * o, axis=-1, keepdims=True) + EPS) * gate[:, hs]
            o_ref[rows, hs] = o.astype(BF16)
            st_ref[h] = st * dec[:, hs] + _dot_tn(vh, k_out[:, hs])

    @pl.when(ti == pl.num_programs(1) - 1)
    def _():
        for h in range(HG_HEADS):
            sout_ref[h] = st_ref[h].T


def _hgrn(f_pre, q_pre, v, g_pre, hg_lb, hg_norm, s0, layer, batch, ln):
    tt = min(ln, 256)
    chunk = min(tt, HG_CHUNK)
    nt = ln // tt
    has_init = s0 is not None
    depth = hg_lb.shape[0]
    row = lambda b, t: (b * nt + t, 0)
    st_spec = pl.BlockSpec((None, HG_HEADS, HG_DK, HG_DV), lambda b, t: (b, 0, 0, 0))
    in_specs = [
        pl.BlockSpec((tt, HG_FDIM), row), pl.BlockSpec((tt, HG_FDIM), row),
        pl.BlockSpec((tt, HG_IDIM), row), pl.BlockSpec((tt, HG_IDIM), row),
        _const_spec((depth, HG_FDIM)), _const_spec((1, HG_IDIM)),
    ]
    args = [f_pre, q_pre, v, g_pre, hg_lb, hg_norm]
    if has_init:
        in_specs.append(st_spec)
        args.append(s0)
    return pl.pallas_call(
        functools.partial(_hgrn_kernel, layer=layer, chunk=chunk, has_init=has_init),
        grid=(batch, nt),
        in_specs=in_specs,
        out_specs=[pl.BlockSpec((tt, HG_IDIM), row), st_spec],
        out_shape=[jax.ShapeDtypeStruct((batch * ln, HG_IDIM), BF16),
                   jax.ShapeDtypeStruct((batch, HG_HEADS, HG_DK, HG_DV), F32)],
        scratch_shapes=[pltpu.VMEM((HG_HEADS, HG_DV, HG_DK), F32)],
        compiler_params=_cparams(("parallel", "arbitrary")),
        name="hgrn",
    )(*args)


ROUTE_GROUP_LANE = N_EXPERTS


def _route(logits, cnt_ref):
    lane_i = lax.broadcasted_iota(jnp.int32, logits.shape, 1)
    lane = lane_i.astype(F32)
    big = float(LANES)
    neg = -jnp.inf
    is_g = jnp.logical_and(lane_i >= ROUTE_GROUP_LANE, lane_i < ROUTE_GROUP_LANE + N_GROUPS)
    gl = jnp.where(is_g, logits, neg)
    gmax = jnp.max(gl, axis=-1, keepdims=True)
    gidx = jnp.min(jnp.where(gl == gmax, lane - ROUTE_GROUP_LANE, big), axis=-1, keepdims=True)
    g_top = 1.0 / jnp.sum(jnp.exp(gl - gmax), axis=-1, keepdims=True)
    lane_group = _div_pow2(lane_i, EXPERTS_PER_GROUP).astype(F32)
    in_group = jnp.logical_and(lane_i < N_EXPERTS, lane_group == gidx)
    el = jnp.where(in_group, logits, neg)
    m1 = jnp.max(el, axis=-1, keepdims=True)
    i1 = jnp.min(jnp.where(el == m1, lane, big), axis=-1, keepdims=True)
    el2 = jnp.where(lane == i1, neg, el)
    m2 = jnp.max(el2, axis=-1, keepdims=True)
    i2 = jnp.min(jnp.where(el2 == m2, lane, big), axis=-1, keepdims=True)
    r = jnp.exp(m2 - m1)
    w1 = g_top / (1.0 + r)
    w2 = g_top * r / (1.0 + r)
    tm = logits.shape[0]
    hit = jnp.logical_or(lane == i1, lane == i2)
    rr = lax.broadcasted_iota(jnp.int32, (tm, tm), 0)
    cc = lax.broadcasted_iota(jnp.int32, (tm, tm), 1)
    before = _dot((cc < rr).astype(BF16), jnp.where(hit, 1.0, 0.0).astype(BF16)) + cnt_ref[...]
    r1 = jnp.sum(jnp.where(lane == i1, before, 0.0), axis=-1, keepdims=True)
    r2 = jnp.sum(jnp.where(lane == i2, before, 0.0), axis=-1, keepdims=True)
    cnt_ref[...] += jnp.sum(jnp.where(hit, 1.0, 0.0), axis=0, keepdims=True)
    out = jnp.where(lane_i == 0, i1, 0.0)
    out = jnp.where(lane_i == 1, i2, out)
    out = jnp.where(lane_i == 2, w1, out)
    out = jnp.where(lane_i == 3, w2, out)
    out = jnp.where(lane_i == 4, r1, out)
    out = jnp.where(lane_i == 5, r2, out)
    return out


def _post_kernel(x_ref, a_ref, o_ref, ple_ref, wg_ref, wa_ref, wb_ref, wo_ref, g1_ref, b1_ref,
                 wr_ref, wpg_ref, wp_ref, x1_ref, base_ref, route_ref, cnt_ref, *, alpha):
    @pl.when(pl.program_id(0) == 0)
    def _():
        cnt_ref[...] = jnp.zeros_like(cnt_ref)

    x = x_ref[...]
    xb = x.astype(BF16)
    ya = _dot(a_ref[...], wa_ref[...])
    merged = _sigmoid(_dot(xb, wg_ref[:, 0:D_MODEL])) * ya
    yb = _dot(o_ref[...], wb_ref[...])
    merged = merged + _sigmoid(_dot(xb, wg_ref[:, D_MODEL:2 * D_MODEL])) * yb
    mix = _dot(merged.astype(BF16), wo_ref[...])
    x1 = _layernorm(alpha * x + mix, g1_ref[...], b1_ref[...])
    _to_token_tiles(x1_ref, x1)
    x1b = x1.astype(BF16)
    route_ref[...] = _route(_dot(x1b, wr_ref[...]), cnt_ref)
    ple = _dot(ple_ref[...].astype(BF16), wp_ref[...])
    base_ref[...] = alpha * x1 + _sigmoid(_dot(x1b, wpg_ref[...])) * ple


def _post(xn, a, o, ple, lw, alpha):
    t = xn.shape[0]
    tm = TOKEN_TILE
    row = lambda i: (i, 0)
    d = D_MODEL
    return pl.pallas_call(
        functools.partial(_post_kernel, alpha=alpha),
        grid=(t // tm,),
        in_specs=[
            pl.BlockSpec((tm, d), row), pl.BlockSpec((tm, 512), row), pl.BlockSpec((tm, HG_IDIM), row),
            pl.BlockSpec((tm, PLE_DIM), row),
            _const_spec((d, 2 * d)), _const_spec((512, d)), _const_spec((HG_IDIM, d)), _const_spec((d, d)),
            _const_spec((1, d)), _const_spec((1, d)),
            _const_spec((d, LANES)), _const_spec((d, d)), _const_spec((PLE_DIM, d)),
        ],
        out_specs=[pl.BlockSpec((tm * TILE_ROWS, LANES), row), pl.BlockSpec((tm, d), row),
                   pl.BlockSpec((tm, LANES), row), pl.BlockSpec((1, LANES), lambda i: (0, 0))],
        out_shape=[jax.ShapeDtypeStruct((t * TILE_ROWS, LANES), F32), jax.ShapeDtypeStruct((t, d), F32),
                   jax.ShapeDtypeStruct((t, LANES), F32), jax.ShapeDtypeStruct((1, LANES), F32)],
        compiler_params=_cparams(("arbitrary",)),
        name="post",
    )(xn, a, o, ple, lw["w_gate"], lw["w_br_a"], lw["w_br_b"], lw["w_out"], lw["ln1_g"], lw["ln1_b"],
      lw["w_route"], lw["w_ple_gate"], lw["w_ple"])


TILE_ROWS = D_MODEL // LANES


def _to_token_tiles(ref, x):
    n = x.shape[0]
    for c in range(TILE_ROWS):
        ref[pl.ds(c, n, stride=TILE_ROWS), :] = x[:, c * LANES:(c + 1) * LANES]


def _from_token_tiles(ref, first_token, n):
    return jnp.concatenate(
        [ref[pl.ds(first_token * TILE_ROWS + c, n, stride=TILE_ROWS), :] for c in range(TILE_ROWS)], axis=1)


def _tile_copy(idx_ref, j, src_hbm, dst, dst_token, sem):
    r = pl.multiple_of(idx_ref[0, j], TILE_ROWS)
    d = dst_token * TILE_ROWS
    if not isinstance(d, int):
        d = pl.multiple_of(d, TILE_ROWS)
    return pltpu.make_async_copy(src_hbm.at[pl.ds(r, TILE_ROWS)], dst.at[pl.ds(d, TILE_ROWS)], sem)


def _issue_tiles_loop(idx_ref, src_hbm, dst, sem, n_tokens, dst_off=0):
    def body(j, carry):
        _tile_copy(idx_ref, j, src_hbm, dst, dst_off + j, sem).start()
        return carry

    lax.fori_loop(0, n_tokens, body, 0, unroll=8)


def _issue_tiles_static(idx_ref, src_hbm, dst, sem, lo, hi, dst_off=0):
    for j in range(lo, hi):
        _tile_copy(idx_ref, j, src_hbm, dst, dst_off + j, sem).start(priority=j % 2)


def _wait_tiles(src_hbm, dst, sem):
    pltpu.make_async_copy(src_hbm.at[pl.ds(0, dst.shape[0])], dst, sem).wait()


def _idx_spec(tile, n_blocks, ahead):
    if ahead is None:
        imap = lambda i, *_: (0, 0, 0)
    else:
        imap = lambda i, *_: (jnp.minimum(i + ahead, n_blocks - 1), 0, 0)
    return pl.BlockSpec((None, 1, tile), imap, memory_space=pltpu.SMEM)


MOE_SLOTS = 3


def _moe_kernel(te_ref, idx0_ref, idx1_ref, idx2_ref, x_hbm, w1_ref, w3_ref, w2_ref, y_ref, xbuf, sem):
    i = pl.program_id(0)
    n = pl.num_programs(0)
    tm = MOE_TILE
    slot = i % MOE_SLOTS

    @pl.when(i == 0)
    def _():
        _issue_tiles_loop(idx0_ref, x_hbm, xbuf.at[0], sem.at[0], tm)
        _issue_tiles_loop(idx1_ref, x_hbm, xbuf.at[1], sem.at[1], tm)

    _wait_tiles(x_hbm, xbuf.at[slot], sem.at[slot])
    xb = _from_token_tiles(xbuf.at[slot], 0, tm).astype(BF16)
    ahead = (i + 2) % MOE_SLOTS
    nxt = xbuf.at[ahead]
    nsem = sem.at[ahead]
    group = tm // 4
    hs = []
    half = D_EXPERT // 2
    for c in range(2):
        cs = slice(c * half, (c + 1) * half)
        h1 = _dot(xb, w1_ref[:, cs])
        _issue_tiles_static(idx2_ref, x_hbm, nxt, nsem, 2 * c * group, (2 * c + 1) * group)
        h3 = _dot(xb, w3_ref[:, cs])
        _issue_tiles_static(idx2_ref, x_hbm, nxt, nsem, (2 * c + 1) * group, (2 * c + 2) * group)
        hs.append((h1 * _sigmoid(h1) * h3).astype(BF16))
    _to_token_tiles(y_ref, _dot(jnp.concatenate(hs, axis=1), w2_ref[...]))

    @pl.when(i == n - 1)
    def _():
        _wait_tiles(x_hbm, xbuf.at[(i + 1) % MOE_SLOTS], sem.at[(i + 1) % MOE_SLOTS])
        _wait_tiles(x_hbm, nxt, nsem)


def _moe(x_tiles, src, tile_e, w1, w3, w2):
    tm = MOE_TILE
    n_tiles = src.shape[0]
    d = D_MODEL
    grid_spec = pltpu.PrefetchScalarGridSpec(
        num_scalar_prefetch=1,
        grid=(n_tiles,),
        in_specs=[
            _idx_spec(tm, n_tiles, None), _idx_spec(tm, n_tiles, 1), _idx_spec(tm, n_tiles, 2),
            pl.BlockSpec(memory_space=pl.ANY),
            pl.BlockSpec((None, d, D_EXPERT), lambda i, te: (te[i], 0, 0)),
            pl.BlockSpec((None, d, D_EXPERT), lambda i, te: (te[i], 0, 0)),
            pl.BlockSpec((None, D_EXPERT, d), lambda i, te: (te[i], 0, 0)),
        ],
        out_specs=pl.BlockSpec((tm * TILE_ROWS, LANES), lambda i, te: (i, 0)),
        scratch_shapes=[pltpu.VMEM((MOE_SLOTS, tm * TILE_ROWS, LANES), F32), pltpu.SemaphoreType.DMA((MOE_SLOTS,))],
    )
    src3 = src.reshape(n_tiles, 1, tm)
    return pl.pallas_call(
        _moe_kernel,
        grid_spec=grid_spec,
        out_shape=jax.ShapeDtypeStruct((n_tiles * tm * TILE_ROWS, LANES), F32),
        compiler_params=_cparams(("arbitrary",)),
        name="moe_experts",
    )(tile_e, src3, src3, src3, x_tiles, w1, w3, w2)


def _dispatch_plan(route, cnt, n_tiles):
    t = route.shape[0]
    a = t * TOP_K
    tm = MOE_TILE
    eid = jnp.arange(N_EXPERTS, dtype=jnp.int32)
    counts = cnt[0, 0:N_EXPERTS].astype(jnp.int32)
    tiles_e = (counts + tm - 1) // tm
    tile_end = jnp.cumsum(tiles_e)
    pstart = (tile_end - tiles_e) * tm
    e = route[:, 0:TOP_K].astype(jnp.int32)
    rank = route[:, 4:4 + TOP_K].astype(jnp.int32)
    pos = rank + jnp.sum(jnp.where(e[:, :, None] == eid[None, None, :], pstart[None, None, :], 0), axis=-1)
    pad_e = tiles_e * tm - counts
    cpad = jnp.cumsum(pad_e)
    k = jnp.arange(n_tiles * tm - a, dtype=jnp.int32)
    ke = jnp.sum((cpad[None, :] <= k[:, None]).astype(jnp.int32), axis=1)
    inside = k + jnp.sum(jnp.where(jnp.minimum(ke, N_EXPERTS - 1)[:, None] == eid[None, :],
                                   (pstart + counts - (cpad - pad_e))[None, :], 0), axis=1)
    padpos = jnp.where(ke < N_EXPERTS, inside, tile_end[-1] * tm + (k - cpad[-1]))
    keys = jnp.concatenate([pos.reshape(a), padpos])
    vals = jnp.concatenate([jnp.repeat(jnp.arange(t, dtype=jnp.int32), TOP_K), jnp.zeros_like(k)])
    _, src = lax.sort_key_val(keys, vals)
    jt = jnp.arange(n_tiles, dtype=jnp.int32)
    tile_e = jnp.minimum(jnp.sum((tile_end[None, :] <= jt[:, None]).astype(jnp.int32), axis=1), N_EXPERTS - 1)
    return (src * TILE_ROWS).reshape(n_tiles, tm), pos.reshape(a) * TILE_ROWS, tile_e.astype(jnp.int32)


def _final_kernel(p0c_ref, p1c_ref, p0n_ref, p1n_ref, base_ref, route_ref, g_ref, b_ref, y_hbm, o_ref, ybuf, sem):
    i = pl.program_id(0)
    n = pl.num_programs(0)
    tm = FINAL_TILE
    slot = i % 2

    @pl.when(i == 0)
    def _():
        _issue_tiles_loop(p0c_ref, y_hbm, ybuf.at[0], sem.at[0], tm, 0)
        _issue_tiles_loop(p1c_ref, y_hbm, ybuf.at[0], sem.at[0], tm, tm)

    _wait_tiles(y_hbm, ybuf.at[slot], sem.at[slot])
    nxt = ybuf.at[1 - slot]
    nsem = sem.at[1 - slot]
    _issue_tiles_static(p0n_ref, y_hbm, nxt, nsem, 0, tm, 0)
    _issue_tiles_static(p1n_ref, y_hbm, nxt, nsem, 0, tm, tm)
    w = route_ref[...]
    cur = ybuf.at[slot]
    h = base_ref[...] + w[:, 2:3] * _from_token_tiles(cur, 0, tm) + w[:, 3:4] * _from_token_tiles(cur, tm, tm)
    o_ref[...] = _layernorm(h, g_ref[...], b_ref[...])

    @pl.when(i == n - 1)
    def _():
        _wait_tiles(y_hbm, nxt, nsem)


def _final(base, y_tiles, pos, route, g, b):
    t, d = base.shape
    tm = FINAL_TILE
    nb = t // tm
    row = lambda i: (i, 0)
    pos2 = pos.reshape(t, TOP_K)
    p0 = pos2[:, 0].reshape(nb, 1, tm)
    p1 = pos2[:, 1].reshape(nb, 1, tm)
    first, nxt = _idx_spec(tm, nb, None), _idx_spec(tm, nb, 1)
    return pl.pallas_call(
        _final_kernel,
        grid=(nb,),
        in_specs=[first, first, nxt, nxt,
                  pl.BlockSpec((tm, d), row), pl.BlockSpec((tm, LANES), row),
                  _const_spec((1, d)), _const_spec((1, d)),
                  pl.BlockSpec(memory_space=pl.ANY)],
        out_specs=pl.BlockSpec((tm, d), row),
        out_shape=jax.ShapeDtypeStruct((t, d), F32),
        scratch_shapes=[pltpu.VMEM((2, 2 * tm * TILE_ROWS, LANES), F32), pltpu.SemaphoreType.DMA((2,))],
        compiler_params=_cparams(("arbitrary",)),
        name="final_ln",
    )(p0, p1, p0, p1, base, route, g, b, y_tiles)


def _rot_cols(w):
    half = ROPE_DIM // 2
    return jnp.concatenate([-w[..., half:], w[..., :half]], axis=-1)


def _layer_weights(i, w_in, q_norm, w_uq, kv_norm, w_uk, w_uv, hg_norm, w_br_a, w_br_b, w_out, ln1_g, ln1_b,
                   w_rg, w_re, w_e_gate, w_e_up, w_e_down, w_ple, w_ple_gate, ln2_g, ln2_b):
    d = D_MODEL
    nh = MLA_HEADS
    win = w_in[i]
    c_kr = Q_LORA + KV_LORA
    c_f = c_kr + ROPE_DIM
    c_ga = c_f + N_HG
    w_kr = win[:, c_kr:c_f]
    zpad = jnp.zeros((d, LANES - ROPE_DIM), F32)
    wa = jnp.concatenate([win[:, 0:c_kr], win[:, c_f:c_ga], w_kr, zpad, _rot_cols(w_kr), zpad], axis=1)

    wq3 = w_uq[i].reshape(Q_LORA, nh, NOPE_DIM + ROPE_DIM)
    zq = jnp.zeros((Q_LORA, nh, HEAD_PAD - NOPE_DIM - ROPE_DIM), F32)
    wq = jnp.concatenate([wq3, zq], axis=-1).reshape(Q_LORA, QK_PAD)
    wq_rot = jnp.concatenate([jnp.zeros((Q_LORA, nh, NOPE_DIM), F32), _rot_cols(wq3[..., NOPE_DIM:]), zq],
                             axis=-1).reshape(Q_LORA, QK_PAD)

    wuk = w_uk[i]
    wuv = w_uv[i]
    k_nope = jnp.concatenate([wuk, jnp.zeros((KV_LORA, nh, HEAD_PAD - NOPE_DIM), F32)], axis=-1)
    eye = jnp.eye(ROPE_DIM, dtype=F32)
    k_rope = jnp.concatenate([jnp.zeros((ROPE_DIM, NOPE_DIM), F32), eye,
                              jnp.zeros((ROPE_DIM, HEAD_PAD - NOPE_DIM - ROPE_DIM), F32)], axis=-1)
    k_rope = jnp.concatenate([jnp.tile(k_rope[:, None, :], (1, nh, 1)).reshape(ROPE_DIM, QK_PAD),
                              jnp.zeros((LANES - ROPE_DIM, QK_PAD), F32)], axis=0)
    zv = jnp.zeros((KV_LORA, nh // 2, V_DIM), F32)
    wuv_a3 = jnp.concatenate([wuv[:, 0::2, :], zv], axis=-1)
    wuv_b3 = jnp.concatenate([zv, wuv[:, 1::2, :]], axis=-1)
    wkv_top = jnp.concatenate([k_nope.reshape(KV_LORA, QK_PAD), wuv_a3.reshape(KV_LORA, 512),
                               wuv_b3.reshape(KV_LORA, 512)], axis=1)
    wkv_bot = jnp.concatenate([k_rope, jnp.zeros((LANES, QK_PAD), F32)], axis=1)
    wkv = jnp.concatenate([wkv_top, wkv_bot], axis=0)

    wuk_t = jnp.concatenate([jnp.transpose(wuk, (1, 2, 0)),
                             jnp.zeros((nh, HEAD_PAD - NOPE_DIM, KV_LORA), F32)], axis=1)
    p_sel = jnp.concatenate([jnp.zeros((NOPE_DIM, ROPE_DIM), F32), eye,
                             jnp.zeros((HEAD_PAD - NOPE_DIM - ROPE_DIM, ROPE_DIM), F32)], axis=0)

    w_route = jnp.concatenate([w_re[i], w_rg[i], jnp.zeros((d, LANES - N_EXPERTS - N_GROUPS), F32)], axis=1)
    bf = lambda w: w.astype(BF16)
    return {
        "wa": bf(wa), "q_norm": q_norm[i].reshape(1, -1), "kv_norm": kv_norm[i].reshape(1, -1),
        "wq": bf(wq), "wq_rot": bf(wq_rot), "wkv": bf(wkv),
        "wuk_t": bf(wuk_t), "p_sel": bf(p_sel),
        "wuv_a3": bf(jnp.transpose(wuv_a3, (1, 0, 2))), "wuv_b3": bf(jnp.transpose(wuv_b3, (1, 0, 2))),
        "hg_norm": hg_norm[i].reshape(1, -1),
        "w_gate": bf(win[:, c_ga:c_ga + 2 * d]), "w_br_a": bf(w_br_a[i]), "w_br_b": bf(w_br_b[i]),
        "w_out": bf(w_out[i]), "ln1_g": ln1_g[i].reshape(1, -1), "ln1_b": ln1_b[i].reshape(1, -1),
        "w_route": bf(w_route), "w_ple_gate": bf(w_ple_gate[i]), "w_ple": bf(w_ple[i]),
        "w1": bf(w_e_gate[i]), "w3": bf(w_e_up[i]), "w2": bf(w_e_down[i]),
        "ln2_g": ln2_g[i].reshape(1, -1), "ln2_b": ln2_b[i].reshape(1, -1),
    }


def _rope_table(pos, rows):
    half = ROPE_DIM // 2
    inv = ROPE_THETA ** (-jnp.arange(half, dtype=F32) / half)
    ang = pos.astype(F32)[:, None] * inv[None, :]
    c2 = jnp.concatenate([jnp.cos(ang), jnp.cos(ang)], axis=1)
    s2 = jnp.concatenate([jnp.sin(ang), jnp.sin(ang)], axis=1)
    n = pos.shape[0]
    ones = jnp.ones((n, NOPE_DIM), F32)
    z = lambda w: jnp.zeros((n, w), F32)
    tab = jnp.concatenate([
        ones, c2, z(HEAD_PAD - NOPE_DIM - ROPE_DIM),
        z(NOPE_DIM), s2, z(HEAD_PAD - NOPE_DIM - ROPE_DIM),
        c2, z(LANES - ROPE_DIM),
        s2, z(LANES - ROPE_DIM)], axis=1)
    if rows > n:
        tab = jnp.tile(tab, (rows // n, 1))
    return tab


def _trunk_layer(i, xn, ple, tab, lw, hg_lb, batch, ln, cache, alpha):
    q, k, va, vb, ckv, kr, f_pre, q_pre, hv, g_pre = _proj(xn, tab, lw)
    if cache is None:
        a = _attn_prompt(q, k, va, vb, batch, ln)
        o, st = _hgrn(f_pre, q_pre, hv, g_pre, hg_lb, lw["hg_norm"], None, i, batch, ln)
    else:
        a = _attn_sample(q, ckv, kr, cache[0], cache[1], lw, batch, ln)
        o, st = _hgrn(f_pre, q_pre, hv, g_pre, hg_lb, lw["hg_norm"], cache[2], i, batch, ln)
    x1, base, route, cnt = _post(xn, a, o, ple, lw, alpha)

    t = xn.shape[0]
    n_tiles = (t * TOP_K) // MOE_TILE + N_EXPERTS
    src, pos, tile_e = _dispatch_plan(route, cnt, n_tiles)
    y = _moe(x1, src, tile_e, lw["w1"], lw["w3"], lw["w2"])
    x2 = _final(base, y, pos, route, lw["ln2_g"], lw["ln2_b"])
    return x2, ckv, kr, st


def kernel(x_prompt, x_sample, p_prompt, p_sample, cache_ckv, cache_krope, state_hgrn, ln0_g, ln0_b, w_in, q_norm, w_uq, kv_norm, w_uk, w_uv, hg_lb, hg_norm, w_br_a, w_br_b, w_out, ln1_g, ln1_b, w_router_group, w_router_expert, w_e_gate, w_e_up, w_e_down, w_ple, w_ple_gate, ln2_g, ln2_b):
    depth = w_in.shape[0]
    bp, sp, d = x_prompt.shape
    bs, ss, _ = x_sample.shape
    past = cache_ckv.shape[2]
    alpha = (2 * depth) ** 0.25

    xp = _ln_rows(x_prompt.reshape(bp * sp, d), ln0_g, ln0_b)
    xs = _ln_rows(x_sample.reshape(bs * ss, d), ln0_g, ln0_b)
    tab_p = _rope_table(jnp.arange(sp, dtype=jnp.int32), max(sp, TOKEN_TILE))
    tab_s = _rope_table(past + jnp.arange(ss, dtype=jnp.int32), max(ss, TOKEN_TILE))

    outs = [[] for _ in range(6)]
    for i in range(depth):
        lw = _layer_weights(i, w_in, q_norm, w_uq, kv_norm, w_uk, w_uv, hg_norm, w_br_a, w_br_b, w_out,
                            ln1_g, ln1_b, w_router_group, w_router_expert, w_e_gate, w_e_up, w_e_down,
                            w_ple, w_ple_gate, ln2_g, ln2_b)
        xp, c, r, s = _trunk_layer(i, xp, p_prompt[i].reshape(bp * sp, -1), tab_p, lw, hg_lb, bp, sp, None, alpha)
        outs[0].append(c.reshape(bp, sp, -1))
        outs[1].append(r.reshape(bp, sp, -1))
        outs[2].append(s)
        cache = (cache_ckv[i], cache_krope[i], state_hgrn[i])
        xs, c, r, s = _trunk_layer(i, xs, p_sample[i].reshape(bs * ss, -1), tab_s, lw, hg_lb, bs, ss, cache, alpha)
        outs[3].append(c.reshape(bs, ss, -1))
        outs[4].append(r.reshape(bs, ss, -1))
        outs[5].append(s)
    return (xp.reshape(bp, sp, d), xs.reshape(bs, ss, d)) + tuple(jnp.stack(o) for o in outs)
```

```python
import functools

import numpy as np
import jax
import jax.numpy as jnp
from jax import lax
from jax.experimental import pallas as pl
from jax.experimental.pallas import tpu as pltpu

F32 = jnp.float32
BF16 = jnp.bfloat16

D_MODEL = 1024
CHUNK = 64
PLE_DIM = 256
MLA_HEADS = 8
NOPE_DIM = 64
ROPE_DIM = 32
V_DIM = 64
Q_LORA = 384
KV_LORA = 256
ROPE_THETA = 10000.0
ATTN_SCALE = (NOPE_DIM + ROPE_DIM) ** -0.5
HG_HEADS = 4
HG_DK = 128
HG_DV = 128
HG_FDIM = HG_HEADS * HG_DK
HG_IDIM = HG_HEADS * HG_DV
N_GROUPS = 4
EXPERTS_PER_GROUP = 8
N_EXPERTS = N_GROUPS * EXPERTS_PER_GROUP
TOP_K = 2
D_EXPERT = 512
EPS = 1e-6

LANES = 128
HEAD_PAD = LANES
QK_PAD = MLA_HEADS * HEAD_PAD
VMEM_LIMIT = 56 * 1024 * 1024

TOKEN_TILE = 512
ATTN_TILE = 256
MOE_TILE = 256
HG_CHUNK = 128
FINAL_TILE = 256


def _cparams(sem):
    return pltpu.CompilerParams(dimension_semantics=sem, vmem_limit_bytes=VMEM_LIMIT)


def _const_spec(shape):
    nd = len(shape)
    return pl.BlockSpec(shape, lambda *_: (0,) * nd, pipeline_mode=pl.Buffered(1))


def _dot(a, b):
    return jnp.dot(a, b, preferred_element_type=F32)


def _dot_nt(a, b):
    return lax.dot_general(a, b, (((1,), (1,)), ((), ())), preferred_element_type=F32)


def _dot_tn(a, b):
    return lax.dot_general(a, b, (((0,), (0,)), ((), ())), preferred_element_type=F32)


def _div_pow2(x, d):
    return jnp.right_shift(x, int(d).bit_length() - 1)


def _sigmoid(x):
    return 1.0 / (1.0 + jnp.exp(-x))


def _layernorm(x, g, b):
    mu = jnp.mean(x, axis=-1, keepdims=True)
    xc = x - mu
    var = jnp.mean(xc * xc, axis=-1, keepdims=True)
    return xc * lax.rsqrt(var + EPS) * g + b


def _rmsnorm(x, g):
    return x * lax.rsqrt(jnp.mean(x * x, axis=-1, keepdims=True) + EPS) * g


def _ln_kernel(x_ref, g_ref, b_ref, o_ref):
    o_ref[...] = _layernorm(x_ref[...], g_ref[...], b_ref[...])


def _ln_rows(x, g, b):
    t, d = x.shape
    tm = TOKEN_TILE
    return pl.pallas_call(
        _ln_kernel,
        grid=(t // tm,),
        in_specs=[pl.BlockSpec((tm, d), lambda i: (i, 0)), _const_spec((1, d)), _const_spec((1, d))],
        out_specs=pl.BlockSpec((tm, d), lambda i: (i, 0)),
        out_shape=jax.ShapeDtypeStruct((t, d), F32),
        compiler_params=_cparams(("parallel",)),
        name="ln0",
    )(x, g.reshape(1, d), b.reshape(1, d))


N_LAT = Q_LORA + KV_LORA
N_HG = 2 * HG_FDIM + 2 * HG_IDIM
COL_HG = N_LAT
COL_KR = N_LAT + N_HG
N_PROJ = COL_KR + 2 * LANES


def _proj_kernel(x_ref, tab_ref, wa_ref, qn_ref, kvn_ref, wq_ref, wqr_ref, wkv_ref,
                 q_ref, k_ref, va_ref, vb_ref, ckv_ref, kr_ref, f_ref, hq_ref, hv_ref, hg_ref):
    xb = x_ref[...].astype(BF16)
    cq_tab = tab_ref[:, 0:LANES]
    sq_tab = tab_ref[:, LANES:2 * LANES]
    ck_tab = tab_ref[:, 2 * LANES:3 * LANES]
    sk_tab = tab_ref[:, 3 * LANES:4 * LANES]

    lat = _dot(xb, wa_ref[:, 0:N_LAT])
    cqn = _rmsnorm(lat[:, 0:Q_LORA], qn_ref[...]).astype(BF16)
    ckvn = _rmsnorm(lat[:, Q_LORA:N_LAT], kvn_ref[...])
    ckv_ref[...] = ckvn

    qa = _dot(cqn, wq_ref[...])
    qb = _dot(cqn, wqr_ref[...])
    for h in range(MLA_HEADS):
        sl = slice(h * HEAD_PAD, (h + 1) * HEAD_PAD)
        q_ref[:, sl] = (qa[:, sl] * cq_tab + qb[:, sl] * sq_tab).astype(BF16)

    kr2 = _dot(xb, wa_ref[:, COL_KR:N_PROJ])
    kr = kr2[:, 0:LANES] * ck_tab + kr2[:, LANES:2 * LANES] * sk_tab
    kr_ref[...] = kr[:, 0:ROPE_DIM]

    kvin = jnp.concatenate([ckvn.astype(BF16), kr.astype(BF16)], axis=1)
    kv = _dot(kvin, wkv_ref[...])
    k_ref[...] = kv[:, 0:QK_PAD].astype(BF16)
    va_ref[...] = kv[:, QK_PAD:QK_PAD + 512].astype(BF16)
    vb_ref[...] = kv[:, QK_PAD + 512:QK_PAD + 1024].astype(BF16)

    f_ref[...] = _dot(xb, wa_ref[:, COL_HG:COL_HG + 512])
    hq_ref[...] = _dot(xb, wa_ref[:, COL_HG + 512:COL_HG + 1024])
    hv_ref[...] = _dot(xb, wa_ref[:, COL_HG + 1024:COL_HG + 1536]).astype(BF16)
    hg_ref[...] = _dot(xb, wa_ref[:, COL_HG + 1536:COL_HG + 2048])


def _proj(xn, tab, lw):
    t = xn.shape[0]
    tm = TOKEN_TILE
    ntab = tab.shape[0] // tm
    row = lambda i: (i, 0)
    outs = [
        (QK_PAD, BF16), (QK_PAD, BF16), (512, BF16), (512, BF16), (KV_LORA, F32), (ROPE_DIM, F32),
        (HG_FDIM, F32), (HG_FDIM, F32), (HG_IDIM, BF16), (HG_IDIM, F32),
    ]
    return pl.pallas_call(
        _proj_kernel,
        grid=(t // tm,),
        in_specs=[
            pl.BlockSpec((tm, D_MODEL), row),
            pl.BlockSpec((tm, 4 * LANES), lambda i: (i % ntab, 0)),
            _const_spec((D_MODEL, N_PROJ)),
            _const_spec((1, Q_LORA)), _const_spec((1, KV_LORA)),
            _const_spec((Q_LORA, QK_PAD)), _const_spec((Q_LORA, QK_PAD)),
            _const_spec((KV_LORA + LANES, 2 * QK_PAD)),
        ],
        out_specs=[pl.BlockSpec((tm, w), row) for w, _ in outs],
        out_shape=[jax.ShapeDtypeStruct((t, w), dt) for w, dt in outs],
        compiler_params=_cparams(("parallel",)),
        name="proj",
    )(xn, tab, lw["wa"], lw["q_norm"], lw["kv_norm"], lw["wq"], lw["wq_rot"], lw["wkv"])


LOG2E = 1.4426950408889634


def _attn_prompt_kernel(q_ref, k_ref, va_ref, vb_ref, o_ref, s_ref, m_ref, l_ref, acc_ref):
    tq = ATTN_TILE
    nh = MLA_HEADS
    qi = pl.program_id(1)
    rq = _div_pow2(lax.broadcasted_iota(jnp.int32, (tq, tq), 0), CHUNK)
    ck = _div_pow2(lax.broadcasted_iota(jnp.int32, (tq, tq), 1), CHUNK)
    diag_mask = ck <= rq
    c = ATTN_SCALE * LOG2E

    m_ref[...] = jnp.full(m_ref.shape, -jnp.inf, F32)
    l_ref[...] = jnp.zeros(l_ref.shape, F32)
    acc_ref[...] = jnp.zeros(acc_ref.shape, F32)

    def scores(kt, mask):
        ks = pl.multiple_of(kt * tq, tq)
        for h in range(nh):
            hs = slice(h * HEAD_PAD, (h + 1) * HEAD_PAD)
            s = _dot_nt(q_ref[:, hs], k_ref[pl.ds(ks, tq), hs]) * c
            if mask is not None:
                s = jnp.where(mask, s, -jnp.inf)
            s_ref[h, :, pl.ds(ks, tq)] = s
            m_ref[h] = jnp.maximum(m_ref[h], jnp.maximum(s[:, 0:LANES], s[:, LANES:2 * LANES]))

    def p1(kt, carry):
        scores(kt, None)
        return carry

    lax.fori_loop(0, qi, p1, 0)
    scores(qi, diag_mask)
    for h in range(nh):
        m_ref[h] = jnp.broadcast_to(jnp.max(m_ref[h], axis=-1, keepdims=True), (tq, LANES))

    def p2(kt, carry):
        ks = pl.multiple_of(kt * tq, tq)
        for h in range(nh):
            v_ref = va_ref if h % 2 == 0 else vb_ref
            pair = h // 2
            mb = m_ref[h]
            p_lo = jnp.exp2(s_ref[h, :, pl.ds(ks, LANES)] - mb)
            p_hi = jnp.exp2(s_ref[h, :, pl.ds(ks + LANES, LANES)] - mb)
            l_ref[h] += p_lo + p_hi
            p = jnp.concatenate([p_lo, p_hi], axis=1).astype(BF16)
            acc_ref[h] += _dot(p, v_ref[pl.ds(ks, tq), pair * LANES:(pair + 1) * LANES])
        return carry

    lax.fori_loop(0, qi + 1, p2, 0)
    for pair in range(nh // 2):
        he, ho = 2 * pair, 2 * pair + 1
        o = (acc_ref[he] / jnp.sum(l_ref[he], axis=-1, keepdims=True)
             + acc_ref[ho] / jnp.sum(l_ref[ho], axis=-1, keepdims=True))
        o_ref[:, pair * LANES:(pair + 1) * LANES] = o.astype(BF16)


def _attn_prompt(q, k, va, vb, batch, seq):
    tq = ATTN_TILE
    nq = seq // tq
    nh = MLA_HEADS
    return pl.pallas_call(
        _attn_prompt_kernel,
        grid=(batch, nq),
        in_specs=[
            pl.BlockSpec((tq, QK_PAD), lambda b, i: (b * nq + i, 0)),
            pl.BlockSpec((seq, QK_PAD), lambda b, i: (b, 0)),
            pl.BlockSpec((seq, 512), lambda b, i: (b, 0)),
            pl.BlockSpec((seq, 512), lambda b, i: (b, 0)),
        ],
        out_specs=pl.BlockSpec((tq, 512), lambda b, i: (b * nq + i, 0)),
        out_shape=jax.ShapeDtypeStruct((batch * seq, 512), BF16),
        scratch_shapes=[pltpu.VMEM((nh, tq, seq), F32), pltpu.VMEM((nh, tq, LANES), F32),
                        pltpu.VMEM((nh, tq, LANES), F32), pltpu.VMEM((nh, tq, LANES), F32)],
        compiler_params=_cparams(("parallel", "arbitrary")),
        name="attn_prompt",
    )(q, k, va, vb)


SAMPLE_KEY_TILE = 1024


def _attn_sample_kernel(q_ref, ckvn_ref, krn_ref, cc_ref, ck_ref, wukt_ref, psel_ref, wva_ref, wvb_ref,
                        o_ref, s_ref):
    ln = q_ref.shape[0]
    past = cc_ref.shape[0]
    nh = MLA_HEADS
    qlat, qrope = [], []
    for h in range(nh):
        qh = q_ref[:, h * HEAD_PAD:(h + 1) * HEAD_PAD]
        qlat.append(_dot(qh, wukt_ref[h]).astype(BF16))
        qrope.append(_dot(qh, psel_ref[...]).astype(BF16))
    qlat = jnp.concatenate(qlat, axis=0)
    qrope = jnp.concatenate(qrope, axis=0)

    ckv_new = ckvn_ref[...].astype(BF16)
    kr_new = krn_ref[...].astype(BF16)
    s_new = (_dot_nt(qlat, ckv_new) + _dot_nt(qrope, kr_new)) * ATTN_SCALE
    m = jnp.max(s_new, axis=-1, keepdims=True)
    tk = SAMPLE_KEY_TILE
    for kt in range(past // tk):
        rows = slice(kt * tk, (kt + 1) * tk)
        s = (_dot_nt(qlat, cc_ref[rows, :].astype(BF16)) + _dot_nt(qrope, ck_ref[rows, :].astype(BF16))) * ATTN_SCALE
        s_ref[:, rows] = s
        m = jnp.maximum(m, jnp.max(s, axis=-1, keepdims=True))
    p_new = jnp.exp(s_new - m)
    l = jnp.sum(p_new, axis=-1, keepdims=True)
    lat = _dot(p_new.astype(BF16), ckv_new)
    for kt in range(past // tk):
        rows = slice(kt * tk, (kt + 1) * tk)
        p = jnp.exp(s_ref[:, rows] - m)
        l = l + jnp.sum(p, axis=-1, keepdims=True)
        lat = lat + _dot(p.astype(BF16), cc_ref[rows, :].astype(BF16))
    lat = (lat / l).astype(BF16)
    for pair in range(nh // 2):
        he, ho = 2 * pair, 2 * pair + 1
        o = _dot(lat[he * ln:(he + 1) * ln], wva_ref[pair]) + _dot(lat[ho * ln:(ho + 1) * ln], wvb_ref[pair])
        o_ref[:, pair * LANES:(pair + 1) * LANES] = o.astype(BF16)


def _attn_sample(q, ckv_new, kr_new, cache_ckv, cache_kr, lw, batch, ln):
    past = cache_ckv.shape[1]
    nh = MLA_HEADS
    return pl.pallas_call(
        _attn_sample_kernel,
        grid=(batch,),
        in_specs=[
            pl.BlockSpec((ln, QK_PAD), lambda b: (b, 0)),
            pl.BlockSpec((ln, KV_LORA), lambda b: (b, 0)),
            pl.BlockSpec((ln, ROPE_DIM), lambda b: (b, 0)),
            pl.BlockSpec((None, past, KV_LORA), lambda b: (b, 0, 0)),
            pl.BlockSpec((None, past, ROPE_DIM), lambda b: (b, 0, 0)),
            _const_spec((nh, HEAD_PAD, KV_LORA)),
            _const_spec((HEAD_PAD, ROPE_DIM)),
            _const_spec((nh // 2, KV_LORA, LANES)),
            _const_spec((nh // 2, KV_LORA, LANES)),
        ],
        out_specs=pl.BlockSpec((ln, 512), lambda b: (b, 0)),
        out_shape=jax.ShapeDtypeStruct((batch * ln, 512), BF16),
        scratch_shapes=[pltpu.VMEM((nh * ln, past), F32)],
        compiler_params=_cparams(("parallel",)),
        name="attn_sample",
    )(q, ckv_new, kr_new, cache_ckv, cache_kr, lw["wuk_t"], lw["p_sel"], lw["wuv_a3"], lw["wuv_b3"])


def _hgrn_kernel(*refs, layer, chunk, has_init):
    if has_init:
        f_ref, q_ref, v_ref, g_ref, lb_ref, nrm_ref, s0_ref, o_ref, sout_ref, st_ref = refs
    else:
        f_ref, q_ref, v_ref, g_ref, lb_ref, nrm_ref, o_ref, sout_ref, st_ref = refs
        s0_ref = None
    tt = f_ref.shape[0]
    c = chunk
    ti = pl.program_id(1)

    @pl.when(ti == 0)
    def _():
        for h in range(HG_HEADS):
            if has_init:
                st_ref[h] = s0_ref[h].T
            else:
                st_ref[h] = jnp.zeros((HG_DV, HG_DK), F32)

    lbp = lb_ref[...]
    e = jnp.exp(lbp - jnp.max(lbp, axis=0, keepdims=True))
    tot = jnp.sum(e, axis=0, keepdims=True)
    part = jnp.zeros_like(tot)
    for j in range(1, layer + 1):
        part = part + e[j:j + 1]
    lb = part / tot
    log_lb = jnp.log(lb)
    log_1m = jnp.log1p(-lb)
    one_m = 1.0 - lb

    rid = lax.broadcasted_iota(jnp.int32, (c, 1), 0)
    rr = lax.broadcasted_iota(jnp.int32, (c, c), 0)
    cc = lax.broadcasted_iota(jnp.int32, (c, c), 1)
    levels = [c >> k for k in range(c.bit_length() - 1)]
    sums = [cc <= rr, cc > rr]
    for g in levels:
        ref_row = jnp.bitwise_and(rr, -g) + (g // 2 - 1)
        is_right = jnp.bitwise_and(rr, g - 1) >= g // 2
        lo_row = jnp.where(is_right, ref_row, rr)
        hi_row = jnp.where(is_right, rr, ref_row)
        sums.append(jnp.logical_and(cc > lo_row, cc <= hi_row))
    sum_mat = jnp.concatenate([m.astype(BF16) for m in sums], axis=0)
    pair_masks = [None] + [_div_pow2(rr, g) == _div_pow2(cc, g) for g in levels[1:]] + [rr == cc]

    for ci in range(tt // c):
        rows = slice(ci * c, (ci + 1) * c)
        z = f_ref[rows, :]
        ez = jnp.exp(-jnp.abs(z))
        rz = 1.0 / (1.0 + ez)
        log_sig = jnp.minimum(z, 0.0) - jnp.log1p(ez)
        kk = one_m * jnp.where(z >= 0, ez * rz, rz)
        y = log_1m + log_sig
        logf = jnp.maximum(log_lb, y) + jnp.log1p(jnp.exp(-jnp.abs(log_lb - y)))
        qp = q_ref[rows, :]
        qq = qp * _sigmoid(qp)

        nl = -logf
        hi = nl.astype(BF16)
        lo = (nl - hi.astype(F32)).astype(BF16)
        acc = _dot(sum_mat, hi) + _dot(sum_mat, lo)
        q_in = (qq * jnp.exp(-acc[0:c])).astype(BF16)
        dec = jnp.exp(-acc[c - 1:c])
        k_out = (kk * jnp.exp(-acc[c:2 * c])).astype(BF16)
        qs, ks = [], []
        for li, g in enumerate(levels):
            x = jnp.exp(-acc[(2 + li) * c:(3 + li) * c])
            right = jnp.bitwise_and(rid, g - 1) >= g // 2
            qs.append(jnp.where(right, qq * x, 0.0).astype(BF16))
            ks.append(jnp.where(right, 0.0, kk * x).astype(BF16))
        qs.append(qq.astype(BF16))
        ks.append(kk.astype(BF16))

        gp = g_ref[rows, :]
        gate = gp * _sigmoid(gp) * nrm_ref[...]

        for h in range(HG_HEADS):
            hs = slice(h * HG_DK, (h + 1) * HG_DK)
            a = None
            for ql, kl, mk in zip(qs, ks, pair_masks):
                al = _dot_nt(ql[:, hs], kl[:, hs])
                if mk is not None:
                    al = jnp.where(mk, al, 0.0)
                a = al if a is None else a + al
            vh = v_ref[rows, hs]
            st = st_ref[h]
            o = _dot_nt(q_in[:, hs], st.astype(BF16)) + _dot(a.astype(BF16), vh)
            o = o * lax.rsqrt(jnp.mean(o * o, axis=-1, keepdims=True) + EPS) * gate[:, hs]
            o_ref[rows, hs] = o.astype(BF16)
            st_ref[h] = st * dec[:, hs] + _dot_tn(vh, k_out[:, hs])

    @pl.when(ti == pl.num_programs(1) - 1)
    def _():
        for h in range(HG_HEADS):
            sout_ref[h] = st_ref[h].T


def _hgrn(f_pre, q_pre, v, g_pre, hg_lb, hg_norm, s0, layer, batch, ln):
    tt = min(ln, 256)
    chunk = min(tt, HG_CHUNK)
    nt = ln // tt
    has_init = s0 is not None
    depth = hg_lb.shape[0]
    row = lambda b, t: (b * nt + t, 0)
    st_spec = pl.BlockSpec((None, HG_HEADS, HG_DK, HG_DV), lambda b, t: (b, 0, 0, 0))
    in_specs = [
        pl.BlockSpec((tt, HG_FDIM), row), pl.BlockSpec((tt, HG_FDIM), row),
        pl.BlockSpec((tt, HG_IDIM), row), pl.BlockSpec((tt, HG_IDIM), row),
        _const_spec((depth, HG_FDIM)), _const_spec((1, HG_IDIM)),
    ]
    args = [f_pre, q_pre, v, g_pre, hg_lb, hg_norm]
    if has_init:
        in_specs.append(st_spec)
        args.append(s0)
    return pl.pallas_call(
        functools.partial(_hgrn_kernel, layer=layer, chunk=chunk, has_init=has_init),
        grid=(batch, nt),
        in_specs=in_specs,
        out_specs=[pl.BlockSpec((tt, HG_IDIM), row), st_spec],
        out_shape=[jax.ShapeDtypeStruct((batch * ln, HG_IDIM), BF16),
                   jax.ShapeDtypeStruct((batch, HG_HEADS, HG_DK, HG_DV), F32)],
        scratch_shapes=[pltpu.VMEM((HG_HEADS, HG_DV, HG_DK), F32)],
        compiler_params=_cparams(("parallel", "arbitrary")),
        name="hgrn",
    )(*args)


ROUTE_GROUP_LANE = N_EXPERTS


def _route(logits, cnt_ref):
    lane_i = lax.broadcasted_iota(jnp.int32, logits.shape, 1)
    lane = lane_i.astype(F32)
    big = float(LANES)
    neg = -jnp.inf
    is_g = jnp.logical_and(lane_i >= ROUTE_GROUP_LANE, lane_i < ROUTE_GROUP_LANE + N_GROUPS)
    gl = jnp.where(is_g, logits, neg)
    gmax = jnp.max(gl, axis=-1, keepdims=True)
    gidx = jnp.min(jnp.where(gl == gmax, lane - ROUTE_GROUP_LANE, big), axis=-1, keepdims=True)
    g_top = 1.0 / jnp.sum(jnp.exp(gl - gmax), axis=-1, keepdims=True)
    lane_group = _div_pow2(lane_i, EXPERTS_PER_GROUP).astype(F32)
    in_group = jnp.logical_and(lane_i < N_EXPERTS, lane_group == gidx)
    el = jnp.where(in_group, logits, neg)
    m1 = jnp.max(el, axis=-1, keepdims=True)
    i1 = jnp.min(jnp.where(el == m1, lane, big), axis=-1, keepdims=True)
    el2 = jnp.where(lane == i1, neg, el)
    m2 = jnp.max(el2, axis=-1, keepdims=True)
    i2 = jnp.min(jnp.where(el2 == m2, lane, big), axis=-1, keepdims=True)
    r = jnp.exp(m2 - m1)
    w1 = g_top / (1.0 + r)
    w2 = g_top * r / (1.0 + r)
    tm = logits.shape[0]
    hit = jnp.logical_or(lane == i1, lane == i2)
    rr = lax.broadcasted_iota(jnp.int32, (tm, tm), 0)
    cc = lax.broadcasted_iota(jnp.int32, (tm, tm), 1)
    before = _dot((cc < rr).astype(BF16), jnp.where(hit, 1.0, 0.0).astype(BF16)) + cnt_ref[...]
    r1 = jnp.sum(jnp.where(lane == i1, before, 0.0), axis=-1, keepdims=True)
    r2 = jnp.sum(jnp.where(lane == i2, before, 0.0), axis=-1, keepdims=True)
    cnt_ref[...] += jnp.sum(jnp.where(hit, 1.0, 0.0), axis=0, keepdims=True)
    out = jnp.where(lane_i == 0, i1, 0.0)
    out = jnp.where(lane_i == 1, i2, out)
    out = jnp.where(lane_i == 2, w1, out)
    out = jnp.where(lane_i == 3, w2, out)
    out = jnp.where(lane_i == 4, r1, out)
    out = jnp.where(lane_i == 5, r2, out)
    return out


def _post_kernel(x_ref, a_ref, o_ref, ple_ref, wg_ref, wa_ref, wb_ref, wo_ref, g1_ref, b1_ref,
                 wr_ref, wpg_ref, wp_ref, x1_ref, base_ref, route_ref, cnt_ref, *, alpha):
    @pl.when(pl.program_id(0) == 0)
    def _():
        cnt_ref[...] = jnp.zeros_like(cnt_ref)

    x = x_ref[...]
    xb = x.astype(BF16)
    ya = _dot(a_ref[...], wa_ref[...])
    merged = _sigmoid(_dot(xb, wg_ref[:, 0:D_MODEL])) * ya
    yb = _dot(o_ref[...], wb_ref[...])
    merged = merged + _sigmoid(_dot(xb, wg_ref[:, D_MODEL:2 * D_MODEL])) * yb
    mix = _dot(merged.astype(BF16), wo_ref[...])
    x1 = _layernorm(alpha * x + mix, g1_ref[...], b1_ref[...])
    _to_token_tiles(x1_ref, x1)
    x1b = x1.astype(BF16)
    route_ref[...] = _route(_dot(x1b, wr_ref[...]), cnt_ref)
    ple = _dot(ple_ref[...].astype(BF16), wp_ref[...])
    base_ref[...] = alpha * x1 + _sigmoid(_dot(x1b, wpg_ref[...])) * ple


def _post(xn, a, o, ple, lw, alpha):
    t = xn.shape[0]
    tm = TOKEN_TILE
    row = lambda i: (i, 0)
    d = D_MODEL
    return pl.pallas_call(
        functools.partial(_post_kernel, alpha=alpha),
        grid=(t // tm,),
        in_specs=[
            pl.BlockSpec((tm, d), row), pl.BlockSpec((tm, 512), row), pl.BlockSpec((tm, HG_IDIM), row),
            pl.BlockSpec((tm, PLE_DIM), row),
            _const_spec((d, 2 * d)), _const_spec((512, d)), _const_spec((HG_IDIM, d)), _const_spec((d, d)),
            _const_spec((1, d)), _const_spec((1, d)),
            _const_spec((d, LANES)), _const_spec((d, d)), _const_spec((PLE_DIM, d)),
        ],
        out_specs=[pl.BlockSpec((tm * TILE_ROWS, LANES), row), pl.BlockSpec((tm, d), row),
                   pl.BlockSpec((tm, LANES), row), pl.BlockSpec((1, LANES), lambda i: (0, 0))],
        out_shape=[jax.ShapeDtypeStruct((t * TILE_ROWS, LANES), F32), jax.ShapeDtypeStruct((t, d), F32),
                   jax.ShapeDtypeStruct((t, LANES), F32), jax.ShapeDtypeStruct((1, LANES), F32)],
        compiler_params=_cparams(("arbitrary",)),
        name="post",
    )(xn, a, o, ple, lw["w_gate"], lw["w_br_a"], lw["w_br_b"], lw["w_out"], lw["ln1_g"], lw["ln1_b"],
      lw["w_route"], lw["w_ple_gate"], lw["w_ple"])


TILE_ROWS = D_MODEL // LANES


def _to_token_tiles(ref, x):
    n = x.shape[0]
    for c in range(TILE_ROWS):
        ref[pl.ds(c, n, stride=TILE_ROWS), :] = x[:, c * LANES:(c + 1) * LANES]


def _from_token_tiles(ref, first_token, n):
    return jnp.concatenate(
        [ref[pl.ds(first_token * TILE_ROWS + c, n, stride=TILE_ROWS), :] for c in range(TILE_ROWS)], axis=1)


def _tile_copy(idx_ref, j, src_hbm, dst, dst_token, sem):
    r = pl.multiple_of(idx_ref[0, j], TILE_ROWS)
    d = dst_token * TILE_ROWS
    if not isinstance(d, int):
        d = pl.multiple_of(d, TILE_ROWS)
    return pltpu.make_async_copy(src_hbm.at[pl.ds(r, TILE_ROWS)], dst.at[pl.ds(d, TILE_ROWS)], sem)


def _issue_tiles_loop(idx_ref, src_hbm, dst, sem, n_tokens, dst_off=0):
    def body(j, carry):
        _tile_copy(idx_ref, j, src_hbm, dst, dst_off + j, sem).start()
        return carry

    lax.fori_loop(0, n_tokens, body, 0, unroll=8)


def _issue_tiles_static(idx_ref, src_hbm, dst, sem, lo, hi, dst_off=0):
    for j in range(lo, hi):
        _tile_copy(idx_ref, j, src_hbm, dst, dst_off + j, sem).start(priority=j % 2)


def _wait_tiles(src_hbm, dst, sem):
    pltpu.make_async_copy(src_hbm.at[pl.ds(0, dst.shape[0])], dst, sem).wait()


def _idx_spec(tile, n_blocks, ahead):
    if ahead is None:
        imap = lambda i, *_: (0, 0, 0)
    else:
        imap = lambda i, *_: (jnp.minimum(i + ahead, n_blocks - 1), 0, 0)
    return pl.BlockSpec((None, 1, tile), imap, memory_space=pltpu.SMEM)


MOE_SLOTS = 3


def _moe_kernel(te_ref, idx0_ref, idx1_ref, idx2_ref, x_hbm, w1_ref, w3_ref, w2_ref, y_ref, xbuf, sem):
    i = pl.program_id(0)
    n = pl.num_programs(0)
    tm = MOE_TILE
    slot = i % MOE_SLOTS

    @pl.when(i == 0)
    def _():
        _issue_tiles_loop(idx0_ref, x_hbm, xbuf.at[0], sem.at[0], tm)
        _issue_tiles_loop(idx1_ref, x_hbm, xbuf.at[1], sem.at[1], tm)

    _wait_tiles(x_hbm, xbuf.at[slot], sem.at[slot])
    xb = _from_token_tiles(xbuf.at[slot], 0, tm).astype(BF16)
    ahead = (i + 2) % MOE_SLOTS
    nxt = xbuf.at[ahead]
    nsem = sem.at[ahead]
    group = tm // 4
    hs = []
    half = D_EXPERT // 2
    for c in range(2):
        cs = slice(c * half, (c + 1) * half)
        h1 = _dot(xb, w1_ref[:, cs])
        _issue_tiles_static(idx2_ref, x_hbm, nxt, nsem, 2 * c * group, (2 * c + 1) * group)
        h3 = _dot(xb, w3_ref[:, cs])
        _issue_tiles_static(idx2_ref, x_hbm, nxt, nsem, (2 * c + 1) * group, (2 * c + 2) * group)
        hs.append((h1 * _sigmoid(h1) * h3).astype(BF16))
    _to_token_tiles(y_ref, _dot(jnp.concatenate(hs, axis=1), w2_ref[...]))

    @pl.when(i == n - 1)
    def _():
        _wait_tiles(x_hbm, xbuf.at[(i + 1) % MOE_SLOTS], sem.at[(i + 1) % MOE_SLOTS])
        _wait_tiles(x_hbm, nxt, nsem)


def _moe(x_tiles, src, tile_e, w1, w3, w2):
    tm = MOE_TILE
    n_tiles = src.shape[0]
    d = D_MODEL
    grid_spec = pltpu.PrefetchScalarGridSpec(
        num_scalar_prefetch=1,
        grid=(n_tiles,),
        in_specs=[
            _idx_spec(tm, n_tiles, None), _idx_spec(tm, n_tiles, 1), _idx_spec(tm, n_tiles, 2),
            pl.BlockSpec(memory_space=pl.ANY),
            pl.BlockSpec((None, d, D_EXPERT), lambda i, te: (te[i], 0, 0)),
            pl.BlockSpec((None, d, D_EXPERT), lambda i, te: (te[i], 0, 0)),
            pl.BlockSpec((None, D_EXPERT, d), lambda i, te: (te[i], 0, 0)),
        ],
        out_specs=pl.BlockSpec((tm * TILE_ROWS, LANES), lambda i, te: (i, 0)),
        scratch_shapes=[pltpu.VMEM((MOE_SLOTS, tm * TILE_ROWS, LANES), F32), pltpu.SemaphoreType.DMA((MOE_SLOTS,))],
    )
    src3 = src.reshape(n_tiles, 1, tm)
    return pl.pallas_call(
        _moe_kernel,
        grid_spec=grid_spec,
        out_shape=jax.ShapeDtypeStruct((n_tiles * tm * TILE_ROWS, LANES), F32),
        compiler_params=_cparams(("arbitrary",)),
        name="moe_experts",
    )(tile_e, src3, src3, src3, x_tiles, w1, w3, w2)


def _dispatch_plan(route, cnt, n_tiles):
    t = route.shape[0]
    a = t * TOP_K
    tm = MOE_TILE
    eid = jnp.arange(N_EXPERTS, dtype=jnp.int32)
    counts = cnt[0, 0:N_EXPERTS].astype(jnp.int32)
    tiles_e = (counts + tm - 1) // tm
    tile_end = jnp.cumsum(tiles_e)
    pstart = (tile_end - tiles_e) * tm
    e = route[:, 0:TOP_K].astype(jnp.int32)
    rank = route[:, 4:4 + TOP_K].astype(jnp.int32)
    pos = rank + jnp.sum(jnp.where(e[:, :, None] == eid[None, None, :], pstart[None, None, :], 0), axis=-1)
    pad_e = tiles_e * tm - counts
    cpad = jnp.cumsum(pad_e)
    k = jnp.arange(n_tiles * tm - a, dtype=jnp.int32)
    ke = jnp.sum((cpad[None, :] <= k[:, None]).astype(jnp.int32), axis=1)
    inside = k + jnp.sum(jnp.where(jnp.minimum(ke, N_EXPERTS - 1)[:, None] == eid[None, :],
                                   (pstart + counts - (cpad - pad_e))[None, :], 0), axis=1)
    padpos = jnp.where(ke < N_EXPERTS, inside, tile_end[-1] * tm + (k - cpad[-1]))
    keys = jnp.concatenate([pos.reshape(a), padpos])
    vals = jnp.concatenate([jnp.repeat(jnp.arange(t, dtype=jnp.int32), TOP_K), k % t])
    _, src = lax.sort_key_val(keys, vals)
    jt = jnp.arange(n_tiles, dtype=jnp.int32)
    tile_e = jnp.minimum(jnp.sum((tile_end[None, :] <= jt[:, None]).astype(jnp.int32), axis=1), N_EXPERTS - 1)
    return (src * TILE_ROWS).reshape(n_tiles, tm), pos.reshape(a) * TILE_ROWS, tile_e.astype(jnp.int32)


def _final_kernel(p0c_ref, p1c_ref, p0n_ref, p1n_ref, base_ref, route_ref, g_ref, b_ref, y_hbm, o_ref, ybuf, sem):
    i = pl.program_id(0)
    n = pl.num_programs(0)
    tm = FINAL_TILE
    slot = i % 2

    @pl.when(i == 0)
    def _():
        _issue_tiles_loop(p0c_ref, y_hbm, ybuf.at[0], sem.at[0], tm, 0)
        _issue_tiles_loop(p1c_ref, y_hbm, ybuf.at[0], sem.at[0], tm, tm)

    _wait_tiles(y_hbm, ybuf.at[slot], sem.at[slot])
    nxt = ybuf.at[1 - slot]
    nsem = sem.at[1 - slot]
    _issue_tiles_static(p0n_ref, y_hbm, nxt, nsem, 0, tm, 0)
    _issue_tiles_static(p1n_ref, y_hbm, nxt, nsem, 0, tm, tm)
    w = route_ref[...]
    cur = ybuf.at[slot]
    h = base_ref[...] + w[:, 2:3] * _from_token_tiles(cur, 0, tm) + w[:, 3:4] * _from_token_tiles(cur, tm, tm)
    o_ref[...] = _layernorm(h, g_ref[...], b_ref[...])

    @pl.when(i == n - 1)
    def _():
        _wait_tiles(y_hbm, nxt, nsem)


def _final(base, y_tiles, pos, route, g, b):
    t, d = base.shape
    tm = FINAL_TILE
    nb = t // tm
    row = lambda i: (i, 0)
    pos2 = pos.reshape(t, TOP_K)
    p0 = pos2[:, 0].reshape(nb, 1, tm)
    p1 = pos2[:, 1].reshape(nb, 1, tm)
    first, nxt = _idx_spec(tm, nb, None), _idx_spec(tm, nb, 1)
    return pl.pallas_call(
        _final_kernel,
        grid=(nb,),
        in_specs=[first, first, nxt, nxt,
                  pl.BlockSpec((tm, d), row), pl.BlockSpec((tm, LANES), row),
                  _const_spec((1, d)), _const_spec((1, d)),
                  pl.BlockSpec(memory_space=pl.ANY)],
        out_specs=pl.BlockSpec((tm, d), row),
        out_shape=jax.ShapeDtypeStruct((t, d), F32),
        scratch_shapes=[pltpu.VMEM((2, 2 * tm * TILE_ROWS, LANES), F32), pltpu.SemaphoreType.DMA((2,))],
        compiler_params=_cparams(("arbitrary",)),
        name="final_ln",
    )(p0, p1, p0, p1, base, route, g, b, y_tiles)


def _rot_cols(w):
    half = ROPE_DIM // 2
    return jnp.concatenate([-w[..., half:], w[..., :half]], axis=-1)


def _layer_weights(i, w_in, q_norm, w_uq, kv_norm, w_uk, w_uv, hg_norm, w_br_a, w_br_b, w_out, ln1_g, ln1_b,
                   w_rg, w_re, w_e_gate, w_e_up, w_e_down, w_ple, w_ple_gate, ln2_g, ln2_b):
    d = D_MODEL
    nh = MLA_HEADS
    win = w_in[i]
    c_kr = Q_LORA + KV_LORA
    c_f = c_kr + ROPE_DIM
    c_ga = c_f + N_HG
    w_kr = win[:, c_kr:c_f]
    zpad = jnp.zeros((d, LANES - ROPE_DIM), F32)
    wa = jnp.concatenate([win[:, 0:c_kr], win[:, c_f:c_ga], w_kr, zpad, _rot_cols(w_kr), zpad], axis=1)

    wq3 = w_uq[i].reshape(Q_LORA, nh, NOPE_DIM + ROPE_DIM)
    zq = jnp.zeros((Q_LORA, nh, HEAD_PAD - NOPE_DIM - ROPE_DIM), F32)
    wq = jnp.concatenate([wq3, zq], axis=-1).reshape(Q_LORA, QK_PAD)
    wq_rot = jnp.concatenate([jnp.zeros((Q_LORA, nh, NOPE_DIM), F32), _rot_cols(wq3[..., NOPE_DIM:]), zq],
                             axis=-1).reshape(Q_LORA, QK_PAD)

    wuk = w_uk[i]
    wuv = w_uv[i]
    k_nope = jnp.concatenate([wuk, jnp.zeros((KV_LORA, nh, HEAD_PAD - NOPE_DIM), F32)], axis=-1)
    eye = jnp.eye(ROPE_DIM, dtype=F32)
    k_rope = jnp.concatenate([jnp.zeros((ROPE_DIM, NOPE_DIM), F32), eye,
                              jnp.zeros((ROPE_DIM, HEAD_PAD - NOPE_DIM - ROPE_DIM), F32)], axis=-1)
    k_rope = jnp.concatenate([jnp.tile(k_rope[:, None, :], (1, nh, 1)).reshape(ROPE_DIM, QK_PAD),
                              jnp.zeros((LANES - ROPE_DIM, QK_PAD), F32)], axis=0)
    zv = jnp.zeros((KV_LORA, nh // 2, V_DIM), F32)
    wuv_a3 = jnp.concatenate([wuv[:, 0::2, :], zv], axis=-1)
    wuv_b3 = jnp.concatenate([zv, wuv[:, 1::2, :]], axis=-1)
    wkv_top = jnp.concatenate([k_nope.reshape(KV_LORA, QK_PAD), wuv_a3.reshape(KV_LORA, 512),
                               wuv_b3.reshape(KV_LORA, 512)], axis=1)
    wkv_bot = jnp.concatenate([k_rope, jnp.zeros((LANES, QK_PAD), F32)], axis=1)
    wkv = jnp.concatenate([wkv_top, wkv_bot], axis=0)

    wuk_t = jnp.concatenate([jnp.transpose(wuk, (1, 2, 0)),
                             jnp.zeros((nh, HEAD_PAD - NOPE_DIM, KV_LORA), F32)], axis=1)
    p_sel = jnp.concatenate([jnp.zeros((NOPE_DIM, ROPE_DIM), F32), eye,
                             jnp.zeros((HEAD_PAD - NOPE_DIM - ROPE_DIM, ROPE_DIM), F32)], axis=0)

    w_route = jnp.concatenate([w_re[i], w_rg[i], jnp.zeros((d, LANES - N_EXPERTS - N_GROUPS), F32)], axis=1)
    bf = lambda w: w.astype(BF16)
    return {
        "wa": bf(wa), "q_norm": q_norm[i].reshape(1, -1), "kv_norm": kv_norm[i].reshape(1, -1),
        "wq": bf(wq), "wq_rot": bf(wq_rot), "wkv": bf(wkv),
        "wuk_t": bf(wuk_t), "p_sel": bf(p_sel),
        "wuv_a3": bf(jnp.transpose(wuv_a3, (1, 0, 2))), "wuv_b3": bf(jnp.transpose(wuv_b3, (1, 0, 2))),
        "hg_norm": hg_norm[i].reshape(1, -1),
        "w_gate": bf(win[:, c_ga:c_ga + 2 * d]), "w_br_a": bf(w_br_a[i]), "w_br_b": bf(w_br_b[i]),
        "w_out": bf(w_out[i]), "ln1_g": ln1_g[i].reshape(1, -1), "ln1_b": ln1_b[i].reshape(1, -1),
        "w_route": bf(w_route), "w_ple_gate": bf(w_ple_gate[i]), "w_ple": bf(w_ple[i]),
        "w1": bf(w_e_gate[i]), "w3": bf(w_e_up[i]), "w2": bf(w_e_down[i]),
        "ln2_g": ln2_g[i].reshape(1, -1), "ln2_b": ln2_b[i].reshape(1, -1),
    }


def _rope_table(pos, rows):
    half = ROPE_DIM // 2
    inv = ROPE_THETA ** (-jnp.arange(half, dtype=F32) / half)
    ang = pos.astype(F32)[:, None] * inv[None, :]
    c2 = jnp.concatenate([jnp.cos(ang), jnp.cos(ang)], axis=1)
    s2 = jnp.concatenate([jnp.sin(ang), jnp.sin(ang)], axis=1)
    n = pos.shape[0]
    ones = jnp.ones((n, NOPE_DIM), F32)
    z = lambda w: jnp.zeros((n, w), F32)
    tab = jnp.concatenate([
        ones, c2, z(HEAD_PAD - NOPE_DIM - ROPE_DIM),
        z(NOPE_DIM), s2, z(HEAD_PAD - NOPE_DIM - ROPE_DIM),
        c2, z(LANES - ROPE_DIM),
        s2, z(LANES - ROPE_DIM)], axis=1)
    if rows > n:
        tab = jnp.tile(tab, (rows // n, 1))
    return tab


def _trunk_layer(i, xn, ple, tab, lw, hg_lb, batch, ln, cache, alpha):
    q, k, va, vb, ckv, kr, f_pre, q_pre, hv, g_pre = _proj(xn, tab, lw)
    if cache is None:
        a = _attn_prompt(q, k, va, vb, batch, ln)
        o, st = _hgrn(f_pre, q_pre, hv, g_pre, hg_lb, lw["hg_norm"], None, i, batch, ln)
    else:
        a = _attn_sample(q, ckv, kr, cache[0], cache[1], lw, batch, ln)
        o, st = _hgrn(f_pre, q_pre, hv, g_pre, hg_lb, lw["hg_norm"], cache[2], i, batch, ln)
    x1, base, route, cnt = _post(xn, a, o, ple, lw, alpha)

    t = xn.shape[0]
    n_tiles = (t * TOP_K) // MOE_TILE + N_EXPERTS
    src, pos, tile_e = _dispatch_plan(route, cnt, n_tiles)
    y = _moe(x1, src, tile_e, lw["w1"], lw["w3"], lw["w2"])
    x2 = _final(base, y, pos, route, lw["ln2_g"], lw["ln2_b"])
    return x2, ckv, kr, st


def kernel(x_prompt, x_sample, p_prompt, p_sample, cache_ckv, cache_krope, state_hgrn, ln0_g, ln0_b, w_in, q_norm, w_uq, kv_norm, w_uk, w_uv, hg_lb, hg_norm, w_br_a, w_br_b, w_out, ln1_g, ln1_b, w_router_group, w_router_expert, w_e_gate, w_e_up, w_e_down, w_ple, w_ple_gate, ln2_g, ln2_b):
    depth = w_in.shape[0]
    bp, sp, d = x_prompt.shape
    bs, ss, _ = x_sample.shape
    past = cache_ckv.shape[2]
    alpha = (2 * depth) ** 0.25

    xp = _ln_rows(x_prompt.reshape(bp * sp, d), ln0_g, ln0_b)
    xs = _ln_rows(x_sample.reshape(bs * ss, d), ln0_g, ln0_b)
    tab_p = _rope_table(jnp.arange(sp, dtype=jnp.int32), max(sp, TOKEN_TILE))
    tab_s = _rope_table(past + jnp.arange(ss, dtype=jnp.int32), max(ss, TOKEN_TILE))

    outs = [[] for _ in range(6)]
    for i in range(depth):
        lw = _layer_weights(i, w_in, q_norm, w_uq, kv_norm, w_uk, w_uv, hg_norm, w_br_a, w_br_b, w_out,
                            ln1_g, ln1_b, w_router_group, w_router_expert, w_e_gate, w_e_up, w_e_down,
                            w_ple, w_ple_gate, ln2_g, ln2_b)
        xp, c, r, s = _trunk_layer(i, xp, p_prompt[i].reshape(bp * sp, -1), tab_p, lw, hg_lb, bp, sp, None, alpha)
        outs[0].append(c.reshape(bp, sp, -1))
        outs[1].append(r.reshape(bp, sp, -1))
        outs[2].append(s)
        cache = (cache_ckv[i], cache_krope[i], state_hgrn[i])
        xs, c, r, s = _trunk_layer(i, xs, p_sample[i].reshape(bs * ss, -1), tab_s, lw, hg_lb, bs, ss, cache, alpha)
        outs[3].append(c.reshape(bs, ss, -1))
        outs[4].append(r.reshape(bs, ss, -1))
        outs[5].append(s)
    return (xp.reshape(bp, sp, d), xs.reshape(bs, ss, d)) + tuple(jnp.stack(o) for o in outs)
```

```python
import functools

import numpy as np
import jax
import jax.numpy as jnp
from jax import lax
from jax.experimental import pallas as pl
from jax.experimental.pallas import tpu as pltpu

F32 = jnp.float32
BF16 = jnp.bfloat16

D_MODEL = 1024
CHUNK = 64
PLE_DIM = 256
MLA_HEADS = 8
NOPE_DIM = 64
ROPE_DIM = 32
V_DIM = 64
Q_LORA = 384
KV_LORA = 256
ROPE_THETA = 10000.0
ATTN_SCALE = (NOPE_DIM + ROPE_DIM) ** -0.5
HG_HEADS = 4
HG_DK = 128
HG_DV = 128
HG_FDIM = HG_HEADS * HG_DK
HG_IDIM = HG_HEADS * HG_DV
N_GROUPS = 4
EXPERTS_PER_GROUP = 8
N_EXPERTS = N_GROUPS * EXPERTS_PER_GROUP
TOP_K = 2
D_EXPERT = 512
EPS = 1e-6

LANES = 128
HEAD_PAD = LANES
QK_PAD = MLA_HEADS * HEAD_PAD
VMEM_LIMIT = 56 * 1024 * 1024

TOKEN_TILE = 512
ATTN_TILE = 256
MOE_TILE = 256
HG_CHUNK = 128
HG_SEQS_PER_STEP = 2
FINAL_TILE = 256


def _cparams(sem):
    return pltpu.CompilerParams(dimension_semantics=sem, vmem_limit_bytes=VMEM_LIMIT)


def _const_spec(shape):
    nd = len(shape)
    return pl.BlockSpec(shape, lambda *_: (0,) * nd, pipeline_mode=pl.Buffered(1))


def _dot(a, b):
    return jnp.dot(a, b, preferred_element_type=F32)


def _dot_nt(a, b):
    return lax.dot_general(a, b, (((1,), (1,)), ((), ())), preferred_element_type=F32)


def _dot_tn(a, b):
    return lax.dot_general(a, b, (((0,), (0,)), ((), ())), preferred_element_type=F32)


def _div_pow2(x, d):
    return jnp.right_shift(x, int(d).bit_length() - 1)


def _sigmoid(x):
    return 1.0 / (1.0 + jnp.exp(-x))


def _layernorm(x, g, b):
    mu = jnp.mean(x, axis=-1, keepdims=True)
    xc = x - mu
    var = jnp.mean(xc * xc, axis=-1, keepdims=True)
    return xc * lax.rsqrt(var + EPS) * g + b


def _rmsnorm(x, g):
    return x * lax.rsqrt(jnp.mean(x * x, axis=-1, keepdims=True) + EPS) * g


def _ln_kernel(x_ref, g_ref, b_ref, o_ref):
    o_ref[...] = _layernorm(x_ref[...], g_ref[...], b_ref[...])


def _ln_rows(x, g, b):
    t, d = x.shape
    tm = TOKEN_TILE
    return pl.pallas_call(
        _ln_kernel,
        grid=(t // tm,),
        in_specs=[pl.BlockSpec((tm, d), lambda i: (i, 0)), _const_spec((1, d)), _const_spec((1, d))],
        out_specs=pl.BlockSpec((tm, d), lambda i: (i, 0)),
        out_shape=jax.ShapeDtypeStruct((t, d), F32),
        compiler_params=_cparams(("parallel",)),
        name="ln0",
    )(x, g.reshape(1, d), b.reshape(1, d))


N_LAT = Q_LORA + KV_LORA
N_HG = 2 * HG_FDIM + 2 * HG_IDIM
COL_HG = N_LAT
COL_KR = N_LAT + N_HG
N_PROJ = COL_KR + 2 * LANES


def _proj_kernel(x_ref, tab_ref, wa_ref, qn_ref, kvn_ref, wq_ref, wqr_ref, wkv_ref,
                 q_ref, k_ref, va_ref, vb_ref, ckv_ref, kr_ref, f_ref, hq_ref, hv_ref, hg_ref):
    xb = x_ref[...].astype(BF16)
    cq_tab = tab_ref[:, 0:LANES]
    sq_tab = tab_ref[:, LANES:2 * LANES]
    ck_tab = tab_ref[:, 2 * LANES:3 * LANES]
    sk_tab = tab_ref[:, 3 * LANES:4 * LANES]

    lat = _dot(xb, wa_ref[:, 0:N_LAT])
    cqn = _rmsnorm(lat[:, 0:Q_LORA], qn_ref[...]).astype(BF16)
    ckvn = _rmsnorm(lat[:, Q_LORA:N_LAT], kvn_ref[...])
    ckv_ref[...] = ckvn

    qa = _dot(cqn, wq_ref[...])
    qb = _dot(cqn, wqr_ref[...])
    for h in range(MLA_HEADS):
        sl = slice(h * HEAD_PAD, (h + 1) * HEAD_PAD)
        q_ref[:, sl] = (qa[:, sl] * cq_tab + qb[:, sl] * sq_tab).astype(BF16)

    kr2 = _dot(xb, wa_ref[:, COL_KR:N_PROJ])
    kr = kr2[:, 0:LANES] * ck_tab + kr2[:, LANES:2 * LANES] * sk_tab
    kr_ref[...] = kr[:, 0:ROPE_DIM]

    kvin = jnp.concatenate([ckvn.astype(BF16), kr.astype(BF16)], axis=1)
    kv = _dot(kvin, wkv_ref[...])
    k_ref[...] = kv[:, 0:QK_PAD].astype(BF16)
    va_ref[...] = kv[:, QK_PAD:QK_PAD + 512].astype(BF16)
    vb_ref[...] = kv[:, QK_PAD + 512:QK_PAD + 1024].astype(BF16)

    f_ref[...] = _dot(xb, wa_ref[:, COL_HG:COL_HG + 512])
    hq_ref[...] = _dot(xb, wa_ref[:, COL_HG + 512:COL_HG + 1024])
    hv_ref[...] = _dot(xb, wa_ref[:, COL_HG + 1024:COL_HG + 1536]).astype(BF16)
    hg_ref[...] = _dot(xb, wa_ref[:, COL_HG + 1536:COL_HG + 2048])


def _proj(xn, tab, lw):
    t = xn.shape[0]
    tm = TOKEN_TILE
    ntab = tab.shape[0] // tm
    row = lambda i: (i, 0)
    outs = [
        (QK_PAD, BF16), (QK_PAD, BF16), (512, BF16), (512, BF16), (KV_LORA, F32), (ROPE_DIM, F32),
        (HG_FDIM, F32), (HG_FDIM, F32), (HG_IDIM, BF16), (HG_IDIM, F32),
    ]
    return pl.pallas_call(
        _proj_kernel,
        grid=(t // tm,),
        in_specs=[
            pl.BlockSpec((tm, D_MODEL), row),
            pl.BlockSpec((tm, 4 * LANES), lambda i: (i % ntab, 0)),
            _const_spec((D_MODEL, N_PROJ)),
            _const_spec((1, Q_LORA)), _const_spec((1, KV_LORA)),
            _const_spec((Q_LORA, QK_PAD)), _const_spec((Q_LORA, QK_PAD)),
            _const_spec((KV_LORA + LANES, 2 * QK_PAD)),
        ],
        out_specs=[pl.BlockSpec((tm, w), row) for w, _ in outs],
        out_shape=[jax.ShapeDtypeStruct((t, w), dt) for w, dt in outs],
        compiler_params=_cparams(("parallel",)),
        name="proj",
    )(xn, tab, lw["wa"], lw["q_norm"], lw["kv_norm"], lw["wq"], lw["wq_rot"], lw["wkv"])


LOG2E = 1.4426950408889634


def _attn_prompt_kernel(q_ref, k_ref, va_ref, vb_ref, o_ref, s_ref, m_ref, l_ref, acc_ref):
    tq = ATTN_TILE
    nh = MLA_HEADS
    qi = pl.program_id(1)
    rq = _div_pow2(lax.broadcasted_iota(jnp.int32, (tq, tq), 0), CHUNK)
    ck = _div_pow2(lax.broadcasted_iota(jnp.int32, (tq, tq), 1), CHUNK)
    diag_mask = ck <= rq
    c = ATTN_SCALE * LOG2E

    m_ref[...] = jnp.full(m_ref.shape, -jnp.inf, F32)
    l_ref[...] = jnp.zeros(l_ref.shape, F32)
    acc_ref[...] = jnp.zeros(acc_ref.shape, F32)

    def scores(kt, mask):
        ks = pl.multiple_of(kt * tq, tq)
        for h in range(nh):
            hs = slice(h * HEAD_PAD, (h + 1) * HEAD_PAD)
            s = _dot_nt(q_ref[:, hs], k_ref[pl.ds(ks, tq), hs]) * c
            if mask is not None:
                s = jnp.where(mask, s, -jnp.inf)
            s_ref[h, :, pl.ds(ks, tq)] = s
            m_ref[h] = jnp.maximum(m_ref[h], jnp.maximum(s[:, 0:LANES], s[:, LANES:2 * LANES]))

    def p1(kt, carry):
        scores(kt, None)
        return carry

    lax.fori_loop(0, qi, p1, 0)
    scores(qi, diag_mask)
    for h in range(nh):
        m_ref[h] = jnp.broadcast_to(jnp.max(m_ref[h], axis=-1, keepdims=True), (tq, LANES))

    def p2(kt, carry):
        ks = pl.multiple_of(kt * tq, tq)
        for h in range(nh):
            v_ref = va_ref if h % 2 == 0 else vb_ref
            pair = h // 2
            mb = m_ref[h]
            p_lo = jnp.exp2(s_ref[h, :, pl.ds(ks, LANES)] - mb)
            p_hi = jnp.exp2(s_ref[h, :, pl.ds(ks + LANES, LANES)] - mb)
            l_ref[h] += p_lo + p_hi
            p = jnp.concatenate([p_lo, p_hi], axis=1).astype(BF16)
            acc_ref[h] += _dot(p, v_ref[pl.ds(ks, tq), pair * LANES:(pair + 1) * LANES])
        return carry

    lax.fori_loop(0, qi + 1, p2, 0)
    for pair in range(nh // 2):
        he, ho = 2 * pair, 2 * pair + 1
        o = (acc_ref[he] / jnp.sum(l_ref[he], axis=-1, keepdims=True)
             + acc_ref[ho] / jnp.sum(l_ref[ho], axis=-1, keepdims=True))
        o_ref[:, pair * LANES:(pair + 1) * LANES] = o.astype(BF16)


def _attn_prompt(q, k, va, vb, batch, seq):
    tq = ATTN_TILE
    nq = seq // tq
    nh = MLA_HEADS
    return pl.pallas_call(
        _attn_prompt_kernel,
        grid=(batch, nq),
        in_specs=[
            pl.BlockSpec((tq, QK_PAD), lambda b, i: (b * nq + i, 0)),
            pl.BlockSpec((seq, QK_PAD), lambda b, i: (b, 0)),
            pl.BlockSpec((seq, 512), lambda b, i: (b, 0)),
            pl.BlockSpec((seq, 512), lambda b, i: (b, 0)),
        ],
        out_specs=pl.BlockSpec((tq, 512), lambda b, i: (b * nq + i, 0)),
        out_shape=jax.ShapeDtypeStruct((batch * seq, 512), BF16),
        scratch_shapes=[pltpu.VMEM((nh, tq, seq), F32), pltpu.VMEM((nh, tq, LANES), F32),
                        pltpu.VMEM((nh, tq, LANES), F32), pltpu.VMEM((nh, tq, LANES), F32)],
        compiler_params=_cparams(("parallel", "arbitrary")),
        name="attn_prompt",
    )(q, k, va, vb)


SAMPLE_KEY_TILE = 1024


def _attn_sample_kernel(q_ref, ckvn_ref, krn_ref, cc_ref, ck_ref, wukt_ref, psel_ref, wva_ref, wvb_ref,
                        o_ref, s_ref):
    ln = q_ref.shape[0]
    past = cc_ref.shape[0]
    nh = MLA_HEADS
    qlat, qrope = [], []
    for h in range(nh):
        qh = q_ref[:, h * HEAD_PAD:(h + 1) * HEAD_PAD]
        qlat.append(_dot(qh, wukt_ref[h]).astype(BF16))
        qrope.append(_dot(qh, psel_ref[...]).astype(BF16))
    qlat = jnp.concatenate(qlat, axis=0)
    qrope = jnp.concatenate(qrope, axis=0)

    ckv_new = ckvn_ref[...].astype(BF16)
    kr_new = krn_ref[...].astype(BF16)
    s_new = (_dot_nt(qlat, ckv_new) + _dot_nt(qrope, kr_new)) * ATTN_SCALE
    m = jnp.max(s_new, axis=-1, keepdims=True)
    tk = SAMPLE_KEY_TILE
    for kt in range(past // tk):
        rows = slice(kt * tk, (kt + 1) * tk)
        s = (_dot_nt(qlat, cc_ref[rows, :].astype(BF16)) + _dot_nt(qrope, ck_ref[rows, :].astype(BF16))) * ATTN_SCALE
        s_ref[:, rows] = s
        m = jnp.maximum(m, jnp.max(s, axis=-1, keepdims=True))
    p_new = jnp.exp(s_new - m)
    l = jnp.sum(p_new, axis=-1, keepdims=True)
    lat = _dot(p_new.astype(BF16), ckv_new)
    for kt in range(past // tk):
        rows = slice(kt * tk, (kt + 1) * tk)
        p = jnp.exp(s_ref[:, rows] - m)
        l = l + jnp.sum(p, axis=-1, keepdims=True)
        lat = lat + _dot(p.astype(BF16), cc_ref[rows, :].astype(BF16))
    lat = (lat / l).astype(BF16)
    for pair in range(nh // 2):
        he, ho = 2 * pair, 2 * pair + 1
        o = _dot(lat[he * ln:(he + 1) * ln], wva_ref[pair]) + _dot(lat[ho * ln:(ho + 1) * ln], wvb_ref[pair])
        o_ref[:, pair * LANES:(pair + 1) * LANES] = o.astype(BF16)


def _attn_sample(q, ckv_new, kr_new, cache_ckv, cache_kr, lw, batch, ln):
    past = cache_ckv.shape[1]
    nh = MLA_HEADS
    return pl.pallas_call(
        _attn_sample_kernel,
        grid=(batch,),
        in_specs=[
            pl.BlockSpec((ln, QK_PAD), lambda b: (b, 0)),
            pl.BlockSpec((ln, KV_LORA), lambda b: (b, 0)),
            pl.BlockSpec((ln, ROPE_DIM), lambda b: (b, 0)),
            pl.BlockSpec((None, past, KV_LORA), lambda b: (b, 0, 0)),
            pl.BlockSpec((None, past, ROPE_DIM), lambda b: (b, 0, 0)),
            _const_spec((nh, HEAD_PAD, KV_LORA)),
            _const_spec((HEAD_PAD, ROPE_DIM)),
            _const_spec((nh // 2, KV_LORA, LANES)),
            _const_spec((nh // 2, KV_LORA, LANES)),
        ],
        out_specs=pl.BlockSpec((ln, 512), lambda b: (b, 0)),
        out_shape=jax.ShapeDtypeStruct((batch * ln, 512), BF16),
        scratch_shapes=[pltpu.VMEM((nh * ln, past), F32)],
        compiler_params=_cparams(("parallel",)),
        name="attn_sample",
    )(q, ckv_new, kr_new, cache_ckv, cache_kr, lw["wuk_t"], lw["p_sel"], lw["wuv_a3"], lw["wuv_b3"])


def _hgrn_kernel(*refs, layer, chunk, has_init):
    if has_init:
        f_ref, q_ref, v_ref, g_ref, lb_ref, nrm_ref, s0_ref, o_ref, sout_ref, st_ref = refs
    else:
        f_ref, q_ref, v_ref, g_ref, lb_ref, nrm_ref, o_ref, sout_ref, st_ref = refs
        s0_ref = None
    nseq, tt = f_ref.shape[0], f_ref.shape[1]
    c = chunk
    ti = pl.program_id(1)

    @pl.when(ti == 0)
    def _():
        for sq in range(nseq):
            for h in range(HG_HEADS):
                if has_init:
                    st_ref[sq, h] = s0_ref[sq, h].T
                else:
                    st_ref[sq, h] = jnp.zeros((HG_DV, HG_DK), F32)

    lbp = lb_ref[...]
    e = jnp.exp(lbp - jnp.max(lbp, axis=0, keepdims=True))
    tot = jnp.sum(e, axis=0, keepdims=True)
    part = jnp.zeros_like(tot)
    for j in range(1, layer + 1):
        part = part + e[j:j + 1]
    lb = part / tot
    log_lb = jnp.log(lb)
    log_1m = jnp.log1p(-lb)
    one_m = 1.0 - lb

    rid = lax.broadcasted_iota(jnp.int32, (c, 1), 0)
    rr = lax.broadcasted_iota(jnp.int32, (c, c), 0)
    cc = lax.broadcasted_iota(jnp.int32, (c, c), 1)
    levels = [c >> k for k in range(c.bit_length() - 1)]
    sums = [cc <= rr, cc > rr]
    for g in levels:
        ref_row = jnp.bitwise_and(rr, -g) + (g // 2 - 1)
        is_right = jnp.bitwise_and(rr, g - 1) >= g // 2
        lo_row = jnp.where(is_right, ref_row, rr)
        hi_row = jnp.where(is_right, rr, ref_row)
        sums.append(jnp.logical_and(cc > lo_row, cc <= hi_row))
    sum_mat = jnp.concatenate([m.astype(BF16) for m in sums], axis=0)
    pair_masks = [None] + [_div_pow2(rr, g) == _div_pow2(cc, g) for g in levels[1:]] + [rr == cc]

    for ci, sq in [(ci, sq) for ci in range(tt // c) for sq in range(nseq)]:
        rows = slice(ci * c, (ci + 1) * c)
        z = f_ref[sq, rows, :]
        ez = jnp.exp(-jnp.abs(z))
        rz = 1.0 / (1.0 + ez)
        log_sig = jnp.minimum(z, 0.0) - jnp.log1p(ez)
        kk = one_m * jnp.where(z >= 0, ez * rz, rz)
        y = log_1m + log_sig
        logf = jnp.maximum(log_lb, y) + jnp.log1p(jnp.exp(-jnp.abs(log_lb - y)))
        qp = q_ref[sq, rows, :]
        qq = qp * _sigmoid(qp)

        nl = -logf
        hi = nl.astype(BF16)
        lo = (nl - hi.astype(F32)).astype(BF16)
        acc = _dot(sum_mat, hi) + _dot(sum_mat, lo)
        q_in = (qq * jnp.exp(-acc[0:c])).astype(BF16)
        dec = jnp.exp(-acc[c - 1:c])
        k_out = (kk * jnp.exp(-acc[c:2 * c])).astype(BF16)
        qs, ks = [], []
        for li, g in enumerate(levels):
            x = jnp.exp(-acc[(2 + li) * c:(3 + li) * c])
            right = jnp.bitwise_and(rid, g - 1) >= g // 2
            qs.append(jnp.where(right, qq * x, 0.0).astype(BF16))
            ks.append(jnp.where(right, 0.0, kk * x).astype(BF16))
        qs.append(qq.astype(BF16))
        ks.append(kk.astype(BF16))

        gp = g_ref[sq, rows, :]
        gate = gp * _sigmoid(gp) * nrm_ref[...]

        for h in range(HG_HEADS):
            hs = slice(h * HG_DK, (h + 1) * HG_DK)
            a = None
            for ql, kl, mk in zip(qs, ks, pair_masks):
                al = _dot_nt(ql[:, hs], kl[:, hs])
                if mk is not None:
                    al = jnp.where(mk, al, 0.0)
                a = al if a is None else a + al
            vh = v_ref[sq, rows, hs]
            st = st_ref[sq, h]
            o = _dot_nt(q_in[:, hs], st.astype(BF16)) + _dot(a.astype(BF16), vh)
            o = o * lax.rsqrt(jnp.mean(o * o, axis=-1, keepdims=True) + EPS) * gate[:, hs]
            o_ref[sq, rows, hs] = o.astype(BF16)
            st_ref[sq, h] = st * dec[:, hs] + _dot_tn(vh, k_out[:, hs])

    @pl.when(ti == pl.num_programs(1) - 1)
    def _():
        for sq in range(nseq):
            for h in range(HG_HEADS):
                sout_ref[sq, h] = st_ref[sq, h].T


def _hgrn(f_pre, q_pre, v, g_pre, hg_lb, hg_norm, s0, layer, batch, ln):
    tt = min(ln, 256)
    chunk = min(tt, HG_CHUNK)
    nt = ln // tt
    nseq = HG_SEQS_PER_STEP
    has_init = s0 is not None
    depth = hg_lb.shape[0]
    blk = lambda b, t: (b, t, 0)
    st_spec = pl.BlockSpec((nseq, HG_HEADS, HG_DK, HG_DV), lambda b, t: (b, 0, 0, 0))
    in_specs = [
        pl.BlockSpec((nseq, tt, HG_FDIM), blk), pl.BlockSpec((nseq, tt, HG_FDIM), blk),
        pl.BlockSpec((nseq, tt, HG_IDIM), blk), pl.BlockSpec((nseq, tt, HG_IDIM), blk),
        _const_spec((depth, HG_FDIM)), _const_spec((1, HG_IDIM)),
    ]
    args = [x.reshape(batch, ln, x.shape[-1]) for x in (f_pre, q_pre, v, g_pre)] + [hg_lb, hg_norm]
    if has_init:
        in_specs.append(st_spec)
        args.append(s0)
    o, st = pl.pallas_call(
        functools.partial(_hgrn_kernel, layer=layer, chunk=chunk, has_init=has_init),
        grid=(batch // nseq, nt),
        in_specs=in_specs,
        out_specs=[pl.BlockSpec((nseq, tt, HG_IDIM), blk), st_spec],
        out_shape=[jax.ShapeDtypeStruct((batch, ln, HG_IDIM), BF16),
                   jax.ShapeDtypeStruct((batch, HG_HEADS, HG_DK, HG_DV), F32)],
        scratch_shapes=[pltpu.VMEM((nseq, HG_HEADS, HG_DV, HG_DK), F32)],
        compiler_params=_cparams(("parallel", "arbitrary")),
        name="hgrn",
    )(*args)
    return o.reshape(batch * ln, HG_IDIM), st


ROUTE_GROUP_LANE = N_EXPERTS


def _route(logits, cnt_ref):
    lane_i = lax.broadcasted_iota(jnp.int32, logits.shape, 1)
    lane = lane_i.astype(F32)
    big = float(LANES)
    neg = -jnp.inf
    is_g = jnp.logical_and(lane_i >= ROUTE_GROUP_LANE, lane_i < ROUTE_GROUP_LANE + N_GROUPS)
    gl = jnp.where(is_g, logits, neg)
    gmax = jnp.max(gl, axis=-1, keepdims=True)
    gidx = jnp.min(jnp.where(gl == gmax, lane - ROUTE_GROUP_LANE, big), axis=-1, keepdims=True)
    g_top = 1.0 / jnp.sum(jnp.exp(gl - gmax), axis=-1, keepdims=True)
    lane_group = _div_pow2(lane_i, EXPERTS_PER_GROUP).astype(F32)
    in_group = jnp.logical_and(lane_i < N_EXPERTS, lane_group == gidx)
    el = jnp.where(in_group, logits, neg)
    m1 = jnp.max(el, axis=-1, keepdims=True)
    i1 = jnp.min(jnp.where(el == m1, lane, big), axis=-1, keepdims=True)
    el2 = jnp.where(lane == i1, neg, el)
    m2 = jnp.max(el2, axis=-1, keepdims=True)
    i2 = jnp.min(jnp.where(el2 == m2, lane, big), axis=-1, keepdims=True)
    r = jnp.exp(m2 - m1)
    w1 = g_top / (1.0 + r)
    w2 = g_top * r / (1.0 + r)
    tm = logits.shape[0]
    hit = jnp.logical_or(lane == i1, lane == i2)
    rr = lax.broadcasted_iota(jnp.int32, (tm, tm), 0)
    cc = lax.broadcasted_iota(jnp.int32, (tm, tm), 1)
    before = _dot((cc < rr).astype(BF16), jnp.where(hit, 1.0, 0.0).astype(BF16)) + cnt_ref[...]
    r1 = jnp.sum(jnp.where(lane == i1, before, 0.0), axis=-1, keepdims=True)
    r2 = jnp.sum(jnp.where(lane == i2, before, 0.0), axis=-1, keepdims=True)
    cnt_ref[...] += jnp.sum(jnp.where(hit, 1.0, 0.0), axis=0, keepdims=True)
    out = jnp.where(lane_i == 0, i1, 0.0)
    out = jnp.where(lane_i == 1, i2, out)
    out = jnp.where(lane_i == 2, w1, out)
    out = jnp.where(lane_i == 3, w2, out)
    out = jnp.where(lane_i == 4, r1, out)
    out = jnp.where(lane_i == 5, r2, out)
    return out


def _post_kernel(x_ref, a_ref, o_ref, ple_ref, wg_ref, wa_ref, wb_ref, wo_ref, g1_ref, b1_ref,
                 wr_ref, wpg_ref, wp_ref, x1_ref, base_ref, route_ref, cnt_ref, *, alpha):
    @pl.when(pl.program_id(0) == 0)
    def _():
        cnt_ref[...] = jnp.zeros_like(cnt_ref)

    x = x_ref[...]
    xb = x.astype(BF16)
    ya = _dot(a_ref[...], wa_ref[...])
    merged = _sigmoid(_dot(xb, wg_ref[:, 0:D_MODEL])) * ya
    yb = _dot(o_ref[...], wb_ref[...])
    merged = merged + _sigmoid(_dot(xb, wg_ref[:, D_MODEL:2 * D_MODEL])) * yb
    mix = _dot(merged.astype(BF16), wo_ref[...])
    x1 = _layernorm(alpha * x + mix, g1_ref[...], b1_ref[...])
    _to_token_tiles(x1_ref, x1)
    x1b = x1.astype(BF16)
    route_ref[...] = _route(_dot(x1b, wr_ref[...]), cnt_ref)
    ple = _dot(ple_ref[...].astype(BF16), wp_ref[...])
    base_ref[...] = alpha * x1 + _sigmoid(_dot(x1b, wpg_ref[...])) * ple


def _post(xn, a, o, ple, lw, alpha):
    t = xn.shape[0]
    tm = TOKEN_TILE
    row = lambda i: (i, 0)
    d = D_MODEL
    return pl.pallas_call(
        functools.partial(_post_kernel, alpha=alpha),
        grid=(t // tm,),
        in_specs=[
            pl.BlockSpec((tm, d), row), pl.BlockSpec((tm, 512), row), pl.BlockSpec((tm, HG_IDIM), row),
            pl.BlockSpec((tm, PLE_DIM), row),
            _const_spec((d, 2 * d)), _const_spec((512, d)), _const_spec((HG_IDIM, d)), _const_spec((d, d)),
            _const_spec((1, d)), _const_spec((1, d)),
            _const_spec((d, LANES)), _const_spec((d, d)), _const_spec((PLE_DIM, d)),
        ],
        out_specs=[pl.BlockSpec((tm * TILE_ROWS, LANES), row), pl.BlockSpec((tm, d), row),
                   pl.BlockSpec((tm, LANES), row), pl.BlockSpec((1, LANES), lambda i: (0, 0))],
        out_shape=[jax.ShapeDtypeStruct((t * TILE_ROWS, LANES), F32), jax.ShapeDtypeStruct((t, d), F32),
                   jax.ShapeDtypeStruct((t, LANES), F32), jax.ShapeDtypeStruct((1, LANES), F32)],
        compiler_params=_cparams(("arbitrary",)),
        name="post",
    )(xn, a, o, ple, lw["w_gate"], lw["w_br_a"], lw["w_br_b"], lw["w_out"], lw["ln1_g"], lw["ln1_b"],
      lw["w_route"], lw["w_ple_gate"], lw["w_ple"])


TILE_ROWS = D_MODEL // LANES


def _to_token_tiles(ref, x):
    n = x.shape[0]
    for c in range(TILE_ROWS):
        ref[pl.ds(c, n, stride=TILE_ROWS), :] = x[:, c * LANES:(c + 1) * LANES]


def _from_token_tiles(ref, first_token, n):
    return jnp.concatenate(
        [ref[pl.ds(first_token * TILE_ROWS + c, n, stride=TILE_ROWS), :] for c in range(TILE_ROWS)], axis=1)


def _tile_copy(idx_ref, j, src_hbm, dst, dst_token, sem):
    r = pl.multiple_of(idx_ref[0, j], TILE_ROWS)
    d = dst_token * TILE_ROWS
    if not isinstance(d, int):
        d = pl.multiple_of(d, TILE_ROWS)
    return pltpu.make_async_copy(src_hbm.at[pl.ds(r, TILE_ROWS)], dst.at[pl.ds(d, TILE_ROWS)], sem)


def _issue_tiles_loop(idx_ref, src_hbm, dst, sem, n_tokens, dst_off=0):
    def body(j, carry):
        _tile_copy(idx_ref, j, src_hbm, dst, dst_off + j, sem).start()
        return carry

    lax.fori_loop(0, n_tokens, body, 0, unroll=8)


def _issue_tiles_static(idx_ref, src_hbm, dst, sem, lo, hi, dst_off=0):
    for j in range(lo, hi):
        _tile_copy(idx_ref, j, src_hbm, dst, dst_off + j, sem).start(priority=j % 2)


def _wait_tiles(src_hbm, dst, sem):
    pltpu.make_async_copy(src_hbm.at[pl.ds(0, dst.shape[0])], dst, sem).wait()


def _idx_spec(tile, n_blocks, ahead):
    if ahead is None:
        imap = lambda i, *_: (0, 0, 0)
    else:
        imap = lambda i, *_: (jnp.minimum(i + ahead, n_blocks - 1), 0, 0)
    return pl.BlockSpec((None, 1, tile), imap, memory_space=pltpu.SMEM)


MOE_SLOTS = 3


def _moe_kernel(te_ref, idx0_ref, idx1_ref, idx2_ref, x_hbm, w1_ref, w3_ref, w2_ref, y_ref, xbuf, sem):
    i = pl.program_id(0)
    n = pl.num_programs(0)
    tm = MOE_TILE
    slot = i % MOE_SLOTS

    @pl.when(i == 0)
    def _():
        _issue_tiles_loop(idx0_ref, x_hbm, xbuf.at[0], sem.at[0], tm)
        _issue_tiles_loop(idx1_ref, x_hbm, xbuf.at[1], sem.at[1], tm)

    _wait_tiles(x_hbm, xbuf.at[slot], sem.at[slot])
    xb = _from_token_tiles(xbuf.at[slot], 0, tm).astype(BF16)
    ahead = (i + 2) % MOE_SLOTS
    nxt = xbuf.at[ahead]
    nsem = sem.at[ahead]
    group = tm // 4
    hs = []
    half = D_EXPERT // 2
    for c in range(2):
        cs = slice(c * half, (c + 1) * half)
        h1 = _dot(xb, w1_ref[:, cs])
        _issue_tiles_static(idx2_ref, x_hbm, nxt, nsem, 2 * c * group, (2 * c + 1) * group)
        h3 = _dot(xb, w3_ref[:, cs])
        _issue_tiles_static(idx2_ref, x_hbm, nxt, nsem, (2 * c + 1) * group, (2 * c + 2) * group)
        hs.append((h1 * _sigmoid(h1) * h3).astype(BF16))
    _to_token_tiles(y_ref, _dot(jnp.concatenate(hs, axis=1), w2_ref[...]))

    @pl.when(i == n - 1)
    def _():
        _wait_tiles(x_hbm, xbuf.at[(i + 1) % MOE_SLOTS], sem.at[(i + 1) % MOE_SLOTS])
        _wait_tiles(x_hbm, nxt, nsem)


def _moe(x_tiles, src, tile_e, w1, w3, w2):
    tm = MOE_TILE
    n_tiles = src.shape[0]
    d = D_MODEL
    grid_spec = pltpu.PrefetchScalarGridSpec(
        num_scalar_prefetch=1,
        grid=(n_tiles,),
        in_specs=[
            _idx_spec(tm, n_tiles, None), _idx_spec(tm, n_tiles, 1), _idx_spec(tm, n_tiles, 2),
            pl.BlockSpec(memory_space=pl.ANY),
            pl.BlockSpec((None, d, D_EXPERT), lambda i, te: (te[i], 0, 0)),
            pl.BlockSpec((None, d, D_EXPERT), lambda i, te: (te[i], 0, 0)),
            pl.BlockSpec((None, D_EXPERT, d), lambda i, te: (te[i], 0, 0)),
        ],
        out_specs=pl.BlockSpec((tm * TILE_ROWS, LANES), lambda i, te: (i, 0)),
        scratch_shapes=[pltpu.VMEM((MOE_SLOTS, tm * TILE_ROWS, LANES), F32), pltpu.SemaphoreType.DMA((MOE_SLOTS,))],
    )
    src3 = src.reshape(n_tiles, 1, tm)
    return pl.pallas_call(
        _moe_kernel,
        grid_spec=grid_spec,
        out_shape=jax.ShapeDtypeStruct((n_tiles * tm * TILE_ROWS, LANES), F32),
        compiler_params=_cparams(("arbitrary",)),
        name="moe_experts",
    )(tile_e, src3, src3, src3, x_tiles, w1, w3, w2)


def _dispatch_plan(route, cnt, n_tiles):
    t = route.shape[0]
    a = t * TOP_K
    tm = MOE_TILE
    eid = jnp.arange(N_EXPERTS, dtype=jnp.int32)
    counts = cnt[0, 0:N_EXPERTS].astype(jnp.int32)
    tiles_e = (counts + tm - 1) // tm
    tile_end = jnp.cumsum(tiles_e)
    pstart = (tile_end - tiles_e) * tm
    e = route[:, 0:TOP_K].astype(jnp.int32)
    rank = route[:, 4:4 + TOP_K].astype(jnp.int32)
    pos = rank + jnp.sum(jnp.where(e[:, :, None] == eid[None, None, :], pstart[None, None, :], 0), axis=-1)
    pad_e = tiles_e * tm - counts
    cpad = jnp.cumsum(pad_e)
    k = jnp.arange(n_tiles * tm - a, dtype=jnp.int32)
    ke = jnp.sum((cpad[None, :] <= k[:, None]).astype(jnp.int32), axis=1)
    inside = k + jnp.sum(jnp.where(jnp.minimum(ke, N_EXPERTS - 1)[:, None] == eid[None, :],
                                   (pstart + counts - (cpad - pad_e))[None, :], 0), axis=1)
    padpos = jnp.where(ke < N_EXPERTS, inside, tile_end[-1] * tm + (k - cpad[-1]))
    keys = jnp.concatenate([pos.reshape(a), padpos])
    vals = jnp.concatenate([jnp.repeat(jnp.arange(t, dtype=jnp.int32), TOP_K), k % t])
    _, src = lax.sort_key_val(keys, vals)
    jt = jnp.arange(n_tiles, dtype=jnp.int32)
    tile_e = jnp.minimum(jnp.sum((tile_end[None, :] <= jt[:, None]).astype(jnp.int32), axis=1), N_EXPERTS - 1)
    return (src * TILE_ROWS).reshape(n_tiles, tm), pos.reshape(a) * TILE_ROWS, tile_e.astype(jnp.int32)


def _final_kernel(p0c_ref, p1c_ref, p0n_ref, p1n_ref, base_ref, route_ref, g_ref, b_ref, y_hbm, o_ref, ybuf, sem):
    i = pl.program_id(0)
    n = pl.num_programs(0)
    tm = FINAL_TILE
    slot = i % 2

    @pl.when(i == 0)
    def _():
        _issue_tiles_loop(p0c_ref, y_hbm, ybuf.at[0], sem.at[0], tm, 0)
        _issue_tiles_loop(p1c_ref, y_hbm, ybuf.at[0], sem.at[0], tm, tm)

    _wait_tiles(y_hbm, ybuf.at[slot], sem.at[slot])
    nxt = ybuf.at[1 - slot]
    nsem = sem.at[1 - slot]
    _issue_tiles_static(p0n_ref, y_hbm, nxt, nsem, 0, tm, 0)
    _issue_tiles_static(p1n_ref, y_hbm, nxt, nsem, 0, tm, tm)
    w = route_ref[...]
    cur = ybuf.at[slot]
    h = base_ref[...] + w[:, 2:3] * _from_token_tiles(cur, 0, tm) + w[:, 3:4] * _from_token_tiles(cur, tm, tm)
    o_ref[...] = _layernorm(h, g_ref[...], b_ref[...])

    @pl.when(i == n - 1)
    def _():
        _wait_tiles(y_hbm, nxt, nsem)


def _final(base, y_tiles, pos, route, g, b):
    t, d = base.shape
    tm = FINAL_TILE
    nb = t // tm
    row = lambda i: (i, 0)
    pos2 = pos.reshape(t, TOP_K)
    p0 = pos2[:, 0].reshape(nb, 1, tm)
    p1 = pos2[:, 1].reshape(nb, 1, tm)
    first, nxt = _idx_spec(tm, nb, None), _idx_spec(tm, nb, 1)
    return pl.pallas_call(
        _final_kernel,
        grid=(nb,),
        in_specs=[first, first, nxt, nxt,
                  pl.BlockSpec((tm, d), row), pl.BlockSpec((tm, LANES), row),
                  _const_spec((1, d)), _const_spec((1, d)),
                  pl.BlockSpec(memory_space=pl.ANY)],
        out_specs=pl.BlockSpec((tm, d), row),
        out_shape=jax.ShapeDtypeStruct((t, d), F32),
        scratch_shapes=[pltpu.VMEM((2, 2 * tm * TILE_ROWS, LANES), F32), pltpu.SemaphoreType.DMA((2,))],
        compiler_params=_cparams(("arbitrary",)),
        name="final_ln",
    )(p0, p1, p0, p1, base, route, g, b, y_tiles)


def _rot_cols(w):
    half = ROPE_DIM // 2
    return jnp.concatenate([-w[..., half:], w[..., :half]], axis=-1)


def _layer_weights(i, w_in, q_norm, w_uq, kv_norm, w_uk, w_uv, hg_norm, w_br_a, w_br_b, w_out, ln1_g, ln1_b,
                   w_rg, w_re, w_e_gate, w_e_up, w_e_down, w_ple, w_ple_gate, ln2_g, ln2_b):
    d = D_MODEL
    nh = MLA_HEADS
    win = w_in[i]
    c_kr = Q_LORA + KV_LORA
    c_f = c_kr + ROPE_DIM
    c_ga = c_f + N_HG
    w_kr = win[:, c_kr:c_f]
    zpad = jnp.zeros((d, LANES - ROPE_DIM), F32)
    wa = jnp.concatenate([win[:, 0:c_kr], win[:, c_f:c_ga], w_kr, zpad, _rot_cols(w_kr), zpad], axis=1)

    wq3 = w_uq[i].reshape(Q_LORA, nh, NOPE_DIM + ROPE_DIM)
    zq = jnp.zeros((Q_LORA, nh, HEAD_PAD - NOPE_DIM - ROPE_DIM), F32)
    wq = jnp.concatenate([wq3, zq], axis=-1).reshape(Q_LORA, QK_PAD)
    wq_rot = jnp.concatenate([jnp.zeros((Q_LORA, nh, NOPE_DIM), F32), _rot_cols(wq3[..., NOPE_DIM:]), zq],
                             axis=-1).reshape(Q_LORA, QK_PAD)

    wuk = w_uk[i]
    wuv = w_uv[i]
    k_nope = jnp.concatenate([wuk, jnp.zeros((KV_LORA, nh, HEAD_PAD - NOPE_DIM), F32)], axis=-1)
    eye = jnp.eye(ROPE_DIM, dtype=F32)
    k_rope = jnp.concatenate([jnp.zeros((ROPE_DIM, NOPE_DIM), F32), eye,
                              jnp.zeros((ROPE_DIM, HEAD_PAD - NOPE_DIM - ROPE_DIM), F32)], axis=-1)
    k_rope = jnp.concatenate([jnp.tile(k_rope[:, None, :], (1, nh, 1)).reshape(ROPE_DIM, QK_PAD),
                              jnp.zeros((LANES - ROPE_DIM, QK_PAD), F32)], axis=0)
    zv = jnp.zeros((KV_LORA, nh // 2, V_DIM), F32)
    wuv_a3 = jnp.concatenate([wuv[:, 0::2, :], zv], axis=-1)
    wuv_b3 = jnp.concatenate([zv, wuv[:, 1::2, :]], axis=-1)
    wkv_top = jnp.concatenate([k_nope.reshape(KV_LORA, QK_PAD), wuv_a3.reshape(KV_LORA, 512),
                               wuv_b3.reshape(KV_LORA, 512)], axis=1)
    wkv_bot = jnp.concatenate([k_rope, jnp.zeros((LANES, QK_PAD), F32)], axis=1)
    wkv = jnp.concatenate([wkv_top, wkv_bot], axis=0)

    wuk_t = jnp.concatenate([jnp.transpose(wuk, (1, 2, 0)),
                             jnp.zeros((nh, HEAD_PAD - NOPE_DIM, KV_LORA), F32)], axis=1)
    p_sel = jnp.concatenate([jnp.zeros((NOPE_DIM, ROPE_DIM), F32), eye,
                             jnp.zeros((HEAD_PAD - NOPE_DIM - ROPE_DIM, ROPE_DIM), F32)], axis=0)

    w_route = jnp.concatenate([w_re[i], w_rg[i], jnp.zeros((d, LANES - N_EXPERTS - N_GROUPS), F32)], axis=1)
    bf = lambda w: w.astype(BF16)
    return {
        "wa": bf(wa), "q_norm": q_norm[i].reshape(1, -1), "kv_norm": kv_norm[i].reshape(1, -1),
        "wq": bf(wq), "wq_rot": bf(wq_rot), "wkv": bf(wkv),
        "wuk_t": bf(wuk_t), "p_sel": bf(p_sel),
        "wuv_a3": bf(jnp.transpose(wuv_a3, (1, 0, 2))), "wuv_b3": bf(jnp.transpose(wuv_b3, (1, 0, 2))),
        "hg_norm": hg_norm[i].reshape(1, -1),
        "w_gate": bf(win[:, c_ga:c_ga + 2 * d]), "w_br_a": bf(w_br_a[i]), "w_br_b": bf(w_br_b[i]),
        "w_out": bf(w_out[i]), "ln1_g": ln1_g[i].reshape(1, -1), "ln1_b": ln1_b[i].reshape(1, -1),
        "w_route": bf(w_route), "w_ple_gate": bf(w_ple_gate[i]), "w_ple": bf(w_ple[i]),
        "w1": bf(w_e_gate[i]), "w3": bf(w_e_up[i]), "w2": bf(w_e_down[i]),
        "ln2_g": ln2_g[i].reshape(1, -1), "ln2_b": ln2_b[i].reshape(1, -1),
    }


def _rope_table(pos, rows):
    half = ROPE_DIM // 2
    inv = ROPE_THETA ** (-jnp.arange(half, dtype=F32) / half)
    ang = pos.astype(F32)[:, None] * inv[None, :]
    c2 = jnp.concatenate([jnp.cos(ang), jnp.cos(ang)], axis=1)
    s2 = jnp.concatenate([jnp.sin(ang), jnp.sin(ang)], axis=1)
    n = pos.shape[0]
    ones = jnp.ones((n, NOPE_DIM), F32)
    z = lambda w: jnp.zeros((n, w), F32)
    tab = jnp.concatenate([
        ones, c2, z(HEAD_PAD - NOPE_DIM - ROPE_DIM),
        z(NOPE_DIM), s2, z(HEAD_PAD - NOPE_DIM - ROPE_DIM),
        c2, z(LANES - ROPE_DIM),
        s2, z(LANES - ROPE_DIM)], axis=1)
    if rows > n:
        tab = jnp.tile(tab, (rows // n, 1))
    return tab


def _trunk_layer(i, xn, ple, tab, lw, hg_lb, batch, ln, cache, alpha):
    q, k, va, vb, ckv, kr, f_pre, q_pre, hv, g_pre = _proj(xn, tab, lw)
    if cache is None:
        a = _attn_prompt(q, k, va, vb, batch, ln)
        o, st = _hgrn(f_pre, q_pre, hv, g_pre, hg_lb, lw["hg_norm"], None, i, batch, ln)
    else:
        a = _attn_sample(q, ckv, kr, cache[0], cache[1], lw, batch, ln)
        o, st = _hgrn(f_pre, q_pre, hv, g_pre, hg_lb, lw["hg_norm"], cache[2], i, batch, ln)
    x1, base, route, cnt = _post(xn, a, o, ple, lw, alpha)

    t = xn.shape[0]
    n_tiles = (t * TOP_K) // MOE_TILE + N_EXPERTS
    src, pos, tile_e = _dispatch_plan(route, cnt, n_tiles)
    y = _moe(x1, src, tile_e, lw["w1"], lw["w3"], lw["w2"])
    x2 = _final(base, y, pos, route, lw["ln2_g"], lw["ln2_b"])
    return x2, ckv, kr, st


def kernel(x_prompt, x_sample, p_prompt, p_sample, cache_ckv, cache_krope, state_hgrn, ln0_g, ln0_b, w_in, q_norm, w_uq, kv_norm, w_uk, w_uv, hg_lb, hg_norm, w_br_a, w_br_b, w_out, ln1_g, ln1_b, w_router_group, w_router_expert, w_e_gate, w_e_up, w_e_down, w_ple, w_ple_gate, ln2_g, ln2_b):
    depth = w_in.shape[0]
    bp, sp, d = x_prompt.shape
    bs, ss, _ = x_sample.shape
    past = cache_ckv.shape[2]
    alpha = (2 * depth) ** 0.25

    xp = _ln_rows(x_prompt.reshape(bp * sp, d), ln0_g, ln0_b)
    xs = _ln_rows(x_sample.reshape(bs * ss, d), ln0_g, ln0_b)
    tab_p = _rope_table(jnp.arange(sp, dtype=jnp.int32), max(sp, TOKEN_TILE))
    tab_s = _rope_table(past + jnp.arange(ss, dtype=jnp.int32), max(ss, TOKEN_TILE))

    outs = [[] for _ in range(6)]
    for i in range(depth):
        lw = _layer_weights(i, w_in, q_norm, w_uq, kv_norm, w_uk, w_uv, hg_norm, w_br_a, w_br_b, w_out,
                            ln1_g, ln1_b, w_router_group, w_router_expert, w_e_gate, w_e_up, w_e_down,
                            w_ple, w_ple_gate, ln2_g, ln2_b)
        xp, c, r, s = _trunk_layer(i, xp, p_prompt[i].reshape(bp * sp, -1), tab_p, lw, hg_lb, bp, sp, None, alpha)
        outs[0].append(c.reshape(bp, sp, -1))
        outs[1].append(r.reshape(bp, sp, -1))
        outs[2].append(s)
        cache = (cache_ckv[i], cache_krope[i], state_hgrn[i])
        xs, c, r, s = _trunk_layer(i, xs, p_sample[i].reshape(bs * ss, -1), tab_s, lw, hg_lb, bs, ss, cache, alpha)
        outs[3].append(c.reshape(bs, ss, -1))
        outs[4].append(r.reshape(bs, ss, -1))
        outs[5].append(s)
    return (xp.reshape(bp, sp, d), xs.reshape(bs, ss, d)) + tuple(jnp.stack(o) for o in outs)
```

```python
import functools

import numpy as np
import jax
import jax.numpy as jnp
from jax import lax
from jax.experimental import pallas as pl
from jax.experimental.pallas import tpu as pltpu

F32 = jnp.float32
BF16 = jnp.bfloat16

D_MODEL = 1024
CHUNK = 64
PLE_DIM = 256
MLA_HEADS = 8
NOPE_DIM = 64
ROPE_DIM = 32
V_DIM = 64
Q_LORA = 384
KV_LORA = 256
ROPE_THETA = 10000.0
ATTN_SCALE = (NOPE_DIM + ROPE_DIM) ** -0.5
HG_HEADS = 4
HG_DK = 128
HG_DV = 128
HG_FDIM = HG_HEADS * HG_DK
HG_IDIM = HG_HEADS * HG_DV
N_GROUPS = 4
EXPERTS_PER_GROUP = 8
N_EXPERTS = N_GROUPS * EXPERTS_PER_GROUP
TOP_K = 2
D_EXPERT = 512
EPS = 1e-6

LANES = 128
HEAD_PAD = LANES
QK_PAD = MLA_HEADS * HEAD_PAD
VMEM_LIMIT = 56 * 1024 * 1024

TOKEN_TILE = 512
ATTN_TILE = 256
MOE_TILE = 256
HG_CHUNK = 128
HG_SEQS_PER_STEP = 2
FINAL_TILE = 256


def _cparams(sem):
    return pltpu.CompilerParams(dimension_semantics=sem, vmem_limit_bytes=VMEM_LIMIT)


def _const_spec(shape):
    nd = len(shape)
    return pl.BlockSpec(shape, lambda *_: (0,) * nd, pipeline_mode=pl.Buffered(1))


def _dot(a, b):
    return jnp.dot(a, b, preferred_element_type=F32)


def _dot_nt(a, b):
    return lax.dot_general(a, b, (((1,), (1,)), ((), ())), preferred_element_type=F32)


def _dot_tn(a, b):
    return lax.dot_general(a, b, (((0,), (0,)), ((), ())), preferred_element_type=F32)


def _div_pow2(x, d):
    return jnp.right_shift(x, int(d).bit_length() - 1)


def _sigmoid(x):
    return 1.0 / (1.0 + jnp.exp(-x))


def _layernorm(x, g, b):
    mu = jnp.mean(x, axis=-1, keepdims=True)
    xc = x - mu
    var = jnp.mean(xc * xc, axis=-1, keepdims=True)
    return xc * lax.rsqrt(var + EPS) * g + b


def _rmsnorm(x, g):
    return x * lax.rsqrt(jnp.mean(x * x, axis=-1, keepdims=True) + EPS) * g


N_LAT = Q_LORA + KV_LORA
N_HG = 2 * HG_FDIM + 2 * HG_IDIM
COL_HG = N_LAT
COL_KR = N_LAT + N_HG
N_PROJ = COL_KR + 2 * LANES


def _input_rows(refs, pre_ln):
    if pre_ln:
        x_ref, g_ref, b_ref = refs[:3]
        return _layernorm(x_ref[...], g_ref[...], b_ref[...]), refs[3:]
    return refs[0][...], refs[1:]


def _proj_kernel(*refs, pre_ln):
    x, refs = _input_rows(refs, pre_ln)
    (tab_ref, wa_ref, qn_ref, kvn_ref, wq_ref, wqr_ref, wkv_ref,
     q_ref, k_ref, va_ref, vb_ref, ckv_ref, kr_ref, f_ref, hq_ref, hv_ref, hg_ref) = refs
    xb = x.astype(BF16)
    cq_tab = tab_ref[:, 0:LANES]
    sq_tab = tab_ref[:, LANES:2 * LANES]
    ck_tab = tab_ref[:, 2 * LANES:3 * LANES]
    sk_tab = tab_ref[:, 3 * LANES:4 * LANES]

    lat = _dot(xb, wa_ref[:, 0:N_LAT])
    cqn = _rmsnorm(lat[:, 0:Q_LORA], qn_ref[...]).astype(BF16)
    ckvn = _rmsnorm(lat[:, Q_LORA:N_LAT], kvn_ref[...])
    ckv_ref[...] = ckvn

    qa = _dot(cqn, wq_ref[...])
    qb = _dot(cqn, wqr_ref[...])
    for h in range(MLA_HEADS):
        sl = slice(h * HEAD_PAD, (h + 1) * HEAD_PAD)
        q_ref[:, sl] = (qa[:, sl] * cq_tab + qb[:, sl] * sq_tab).astype(BF16)

    kr2 = _dot(xb, wa_ref[:, COL_KR:N_PROJ])
    kr = kr2[:, 0:LANES] * ck_tab + kr2[:, LANES:2 * LANES] * sk_tab
    kr_ref[...] = kr[:, 0:ROPE_DIM]

    kvin = jnp.concatenate([ckvn.astype(BF16), kr.astype(BF16)], axis=1)
    kv = _dot(kvin, wkv_ref[...])
    k_ref[...] = kv[:, 0:QK_PAD].astype(BF16)
    va_ref[...] = kv[:, QK_PAD:QK_PAD + 512].astype(BF16)
    vb_ref[...] = kv[:, QK_PAD + 512:QK_PAD + 1024].astype(BF16)

    f_ref[...] = _dot(xb, wa_ref[:, COL_HG:COL_HG + 512])
    hq_ref[...] = _dot(xb, wa_ref[:, COL_HG + 512:COL_HG + 1024])
    hv_ref[...] = _dot(xb, wa_ref[:, COL_HG + 1024:COL_HG + 1536]).astype(BF16)
    hg_ref[...] = _dot(xb, wa_ref[:, COL_HG + 1536:COL_HG + 2048])


def _proj(xn, tab, lw, ln0):
    t = xn.shape[0]
    pre_ln = ln0 is not None
    x_specs = [pl.BlockSpec((TOKEN_TILE, D_MODEL), lambda i: (i, 0))]
    x_args = [xn]
    if pre_ln:
        x_specs += [_const_spec((1, D_MODEL)), _const_spec((1, D_MODEL))]
        x_args += [ln0[0].reshape(1, D_MODEL), ln0[1].reshape(1, D_MODEL)]
    tm = TOKEN_TILE
    ntab = tab.shape[0] // tm
    row = lambda i: (i, 0)
    outs = [
        (QK_PAD, BF16), (QK_PAD, BF16), (512, BF16), (512, BF16), (KV_LORA, F32), (ROPE_DIM, F32),
        (HG_FDIM, F32), (HG_FDIM, F32), (HG_IDIM, BF16), (HG_IDIM, F32),
    ]
    return pl.pallas_call(
        functools.partial(_proj_kernel, pre_ln=pre_ln),
        grid=(t // tm,),
        in_specs=x_specs + [
            pl.BlockSpec((tm, 4 * LANES), lambda i: (i % ntab, 0)),
            _const_spec((D_MODEL, N_PROJ)),
            _const_spec((1, Q_LORA)), _const_spec((1, KV_LORA)),
            _const_spec((Q_LORA, QK_PAD)), _const_spec((Q_LORA, QK_PAD)),
            _const_spec((KV_LORA + LANES, 2 * QK_PAD)),
        ],
        out_specs=[pl.BlockSpec((tm, w), row) for w, _ in outs],
        out_shape=[jax.ShapeDtypeStruct((t, w), dt) for w, dt in outs],
        compiler_params=_cparams(("parallel",)),
        name="proj",
    )(*x_args, tab, lw["wa"], lw["q_norm"], lw["kv_norm"], lw["wq"], lw["wq_rot"], lw["wkv"])


LOG2E = 1.4426950408889634


def _attn_prompt_kernel(q_ref, k_ref, va_ref, vb_ref, o_ref, s_ref, m_ref, l_ref, acc_ref):
    tq = ATTN_TILE
    nh = MLA_HEADS
    qi = pl.program_id(1)
    rq = _div_pow2(lax.broadcasted_iota(jnp.int32, (tq, tq), 0), CHUNK)
    ck = _div_pow2(lax.broadcasted_iota(jnp.int32, (tq, tq), 1), CHUNK)
    diag_mask = ck <= rq
    c = ATTN_SCALE * LOG2E

    m_ref[...] = jnp.full(m_ref.shape, -jnp.inf, F32)
    l_ref[...] = jnp.zeros(l_ref.shape, F32)
    acc_ref[...] = jnp.zeros(acc_ref.shape, F32)

    def scores(kt, mask):
        ks = pl.multiple_of(kt * tq, tq)
        for h in range(nh):
            hs = slice(h * HEAD_PAD, (h + 1) * HEAD_PAD)
            s = _dot_nt(q_ref[:, hs], k_ref[pl.ds(ks, tq), hs]) * c
            if mask is not None:
                s = jnp.where(mask, s, -jnp.inf)
            s_ref[h, :, pl.ds(ks, tq)] = s
            m_ref[h] = jnp.maximum(m_ref[h], jnp.maximum(s[:, 0:LANES], s[:, LANES:2 * LANES]))

    def p1(kt, carry):
        scores(kt, None)
        return carry

    lax.fori_loop(0, qi, p1, 0)
    scores(qi, diag_mask)
    for h in range(nh):
        m_ref[h] = jnp.broadcast_to(jnp.max(m_ref[h], axis=-1, keepdims=True), (tq, LANES))

    def p2(kt, carry):
        ks = pl.multiple_of(kt * tq, tq)
        for h in range(nh):
            v_ref = va_ref if h % 2 == 0 else vb_ref
            pair = h // 2
            mb = m_ref[h]
            p_lo = jnp.exp2(s_ref[h, :, pl.ds(ks, LANES)] - mb)
            p_hi = jnp.exp2(s_ref[h, :, pl.ds(ks + LANES, LANES)] - mb)
            l_ref[h] += p_lo + p_hi
            p = jnp.concatenate([p_lo, p_hi], axis=1).astype(BF16)
            acc_ref[h] += _dot(p, v_ref[pl.ds(ks, tq), pair * LANES:(pair + 1) * LANES])
        return carry

    lax.fori_loop(0, qi + 1, p2, 0)
    for pair in range(nh // 2):
        he, ho = 2 * pair, 2 * pair + 1
        o = (acc_ref[he] / jnp.sum(l_ref[he], axis=-1, keepdims=True)
             + acc_ref[ho] / jnp.sum(l_ref[ho], axis=-1, keepdims=True))
        o_ref[:, pair * LANES:(pair + 1) * LANES] = o.astype(BF16)


def _attn_prompt(q, k, va, vb, batch, seq):
    tq = ATTN_TILE
    nq = seq // tq
    nh = MLA_HEADS
    return pl.pallas_call(
        _attn_prompt_kernel,
        grid=(batch, nq),
        in_specs=[
            pl.BlockSpec((tq, QK_PAD), lambda b, i: (b * nq + i, 0)),
            pl.BlockSpec((seq, QK_PAD), lambda b, i: (b, 0)),
            pl.BlockSpec((seq, 512), lambda b, i: (b, 0)),
            pl.BlockSpec((seq, 512), lambda b, i: (b, 0)),
        ],
        out_specs=pl.BlockSpec((tq, 512), lambda b, i: (b * nq + i, 0)),
        out_shape=jax.ShapeDtypeStruct((batch * seq, 512), BF16),
        scratch_shapes=[pltpu.VMEM((nh, tq, seq), F32), pltpu.VMEM((nh, tq, LANES), F32),
                        pltpu.VMEM((nh, tq, LANES), F32), pltpu.VMEM((nh, tq, LANES), F32)],
        compiler_params=_cparams(("parallel", "arbitrary")),
        name="attn_prompt",
    )(q, k, va, vb)


SAMPLE_KEY_TILE = 1024


def _attn_sample_kernel(q_ref, ckvn_ref, krn_ref, cc_ref, ck_ref, wukt_ref, psel_ref, wva_ref, wvb_ref,
                        o_ref, s_ref):
    ln = q_ref.shape[0]
    past = cc_ref.shape[0]
    nh = MLA_HEADS
    qlat, qrope = [], []
    for h in range(nh):
        qh = q_ref[:, h * HEAD_PAD:(h + 1) * HEAD_PAD]
        qlat.append(_dot(qh, wukt_ref[h]).astype(BF16))
        qrope.append(_dot(qh, psel_ref[...]).astype(BF16))
    qlat = jnp.concatenate(qlat, axis=0)
    qrope = jnp.concatenate(qrope, axis=0)

    ckv_new = ckvn_ref[...].astype(BF16)
    kr_new = krn_ref[...].astype(BF16)
    s_new = (_dot_nt(qlat, ckv_new) + _dot_nt(qrope, kr_new)) * ATTN_SCALE
    m = jnp.max(s_new, axis=-1, keepdims=True)
    tk = SAMPLE_KEY_TILE
    for kt in range(past // tk):
        rows = slice(kt * tk, (kt + 1) * tk)
        s = (_dot_nt(qlat, cc_ref[rows, :].astype(BF16)) + _dot_nt(qrope, ck_ref[rows, :].astype(BF16))) * ATTN_SCALE
        s_ref[:, rows] = s
        m = jnp.maximum(m, jnp.max(s, axis=-1, keepdims=True))
    p_new = jnp.exp(s_new - m)
    l = jnp.sum(p_new, axis=-1, keepdims=True)
    lat = _dot(p_new.astype(BF16), ckv_new)
    for kt in range(past // tk):
        rows = slice(kt * tk, (kt + 1) * tk)
        p = jnp.exp(s_ref[:, rows] - m)
        l = l + jnp.sum(p, axis=-1, keepdims=True)
        lat = lat + _dot(p.astype(BF16), cc_ref[rows, :].astype(BF16))
    lat = (lat / l).astype(BF16)
    for pair in range(nh // 2):
        he, ho = 2 * pair, 2 * pair + 1
        o = _dot(lat[he * ln:(he + 1) * ln], wva_ref[pair]) + _dot(lat[ho * ln:(ho + 1) * ln], wvb_ref[pair])
        o_ref[:, pair * LANES:(pair + 1) * LANES] = o.astype(BF16)


def _attn_sample(q, ckv_new, kr_new, cache_ckv, cache_kr, lw, batch, ln):
    past = cache_ckv.shape[1]
    nh = MLA_HEADS
    return pl.pallas_call(
        _attn_sample_kernel,
        grid=(batch,),
        in_specs=[
            pl.BlockSpec((ln, QK_PAD), lambda b: (b, 0)),
            pl.BlockSpec((ln, KV_LORA), lambda b: (b, 0)),
            pl.BlockSpec((ln, ROPE_DIM), lambda b: (b, 0)),
            pl.BlockSpec((None, past, KV_LORA), lambda b: (b, 0, 0)),
            pl.BlockSpec((None, past, ROPE_DIM), lambda b: (b, 0, 0)),
            _const_spec((nh, HEAD_PAD, KV_LORA)),
            _const_spec((HEAD_PAD, ROPE_DIM)),
            _const_spec((nh // 2, KV_LORA, LANES)),
            _const_spec((nh // 2, KV_LORA, LANES)),
        ],
        out_specs=pl.BlockSpec((ln, 512), lambda b: (b, 0)),
        out_shape=jax.ShapeDtypeStruct((batch * ln, 512), BF16),
        scratch_shapes=[pltpu.VMEM((nh * ln, past), F32)],
        compiler_params=_cparams(("parallel",)),
        name="attn_sample",
    )(q, ckv_new, kr_new, cache_ckv, cache_kr, lw["wuk_t"], lw["p_sel"], lw["wuv_a3"], lw["wuv_b3"])


def _hgrn_kernel(*refs, layer, chunk, has_init):
    if has_init:
        f_ref, q_ref, v_ref, g_ref, lb_ref, nrm_ref, s0_ref, o_ref, sout_ref, st_ref = refs
    else:
        f_ref, q_ref, v_ref, g_ref, lb_ref, nrm_ref, o_ref, sout_ref, st_ref = refs
        s0_ref = None
    nseq, tt = f_ref.shape[0], f_ref.shape[1]
    c = chunk
    ti = pl.program_id(1)

    @pl.when(ti == 0)
    def _():
        for sq in range(nseq):
            for h in range(HG_HEADS):
                if has_init:
                    st_ref[sq, h] = s0_ref[sq, h].T
                else:
                    st_ref[sq, h] = jnp.zeros((HG_DV, HG_DK), F32)

    lbp = lb_ref[...]
    e = jnp.exp(lbp - jnp.max(lbp, axis=0, keepdims=True))
    tot = jnp.sum(e, axis=0, keepdims=True)
    part = jnp.zeros_like(tot)
    for j in range(1, layer + 1):
        part = part + e[j:j + 1]
    lb = part / tot
    log_lb = jnp.log(lb)
    log_1m = jnp.log1p(-lb)
    one_m = 1.0 - lb

    rid = lax.broadcasted_iota(jnp.int32, (c, 1), 0)
    rr = lax.broadcasted_iota(jnp.int32, (c, c), 0)
    cc = lax.broadcasted_iota(jnp.int32, (c, c), 1)
    levels = [c >> k for k in range(c.bit_length() - 1)]
    sums = [cc <= rr, cc > rr]
    for g in levels:
        ref_row = jnp.bitwise_and(rr, -g) + (g // 2 - 1)
        is_right = jnp.bitwise_and(rr, g - 1) >= g // 2
        lo_row = jnp.where(is_right, ref_row, rr)
        hi_row = jnp.where(is_right, rr, ref_row)
        sums.append(jnp.logical_and(cc > lo_row, cc <= hi_row))
    sum_mat = jnp.concatenate([m.astype(BF16) for m in sums], axis=0)
    pair_masks = [None] + [_div_pow2(rr, g) == _div_pow2(cc, g) for g in levels[1:]] + [rr == cc]

    for ci, sq in [(ci, sq) for ci in range(tt // c) for sq in range(nseq)]:
        rows = slice(ci * c, (ci + 1) * c)
        z = f_ref[sq, rows, :]
        ez = jnp.exp(-jnp.abs(z))
        rz = 1.0 / (1.0 + ez)
        log_sig = jnp.minimum(z, 0.0) - jnp.log1p(ez)
        kk = one_m * jnp.where(z >= 0, ez * rz, rz)
        y = log_1m + log_sig
        logf = jnp.maximum(log_lb, y) + jnp.log1p(jnp.exp(-jnp.abs(log_lb - y)))
        qp = q_ref[sq, rows, :]
        qq = qp * _sigmoid(qp)

        nl = -logf
        hi = nl.astype(BF16)
        lo = (nl - hi.astype(F32)).astype(BF16)
        acc = _dot(sum_mat, hi) + _dot(sum_mat, lo)
        q_in = (qq * jnp.exp(-acc[0:c])).astype(BF16)
        dec = jnp.exp(-acc[c - 1:c])
        k_out = (kk * jnp.exp(-acc[c:2 * c])).astype(BF16)
        qs, ks = [], []
        for li, g in enumerate(levels):
            x = jnp.exp(-acc[(2 + li) * c:(3 + li) * c])
            right = jnp.bitwise_and(rid, g - 1) >= g // 2
            qs.append(jnp.where(right, qq * x, 0.0).astype(BF16))
            ks.append(jnp.where(right, 0.0, kk * x).astype(BF16))
        qs.append(qq.astype(BF16))
        ks.append(kk.astype(BF16))

        gp = g_ref[sq, rows, :]
        gate = gp * _sigmoid(gp) * nrm_ref[...]

        for h in range(HG_HEADS):
            hs = slice(h * HG_DK, (h + 1) * HG_DK)
            a = None
            for ql, kl, mk in zip(qs, ks, pair_masks):
                al = _dot_nt(ql[:, hs], kl[:, hs])
                if mk is not None:
                    al = jnp.where(mk, al, 0.0)
                a = al if a is None else a + al
            vh = v_ref[sq, rows, hs]
            st = st_ref[sq, h]
            o = _dot_nt(q_in[:, hs], st.astype(BF16)) + _dot(a.astype(BF16), vh)
            o = o * lax.rsqrt(jnp.mean(o * o, axis=-1, keepdims=True) + EPS) * gate[:, hs]
            o_ref[sq, rows, hs] = o.astype(BF16)
            st_ref[sq, h] = st * dec[:, hs] + _dot_tn(vh, k_out[:, hs])

    @pl.when(ti == pl.num_programs(1) - 1)
    def _():
        for sq in range(nseq):
            for h in range(HG_HEADS):
                sout_ref[sq, h] = st_ref[sq, h].T


def _hgrn(f_pre, q_pre, v, g_pre, hg_lb, hg_norm, s0, layer, batch, ln):
    tt = min(ln, 256)
    chunk = min(tt, HG_CHUNK)
    nt = ln // tt
    nseq = HG_SEQS_PER_STEP
    has_init = s0 is not None
    depth = hg_lb.shape[0]
    blk = lambda b, t: (b, t, 0)
    st_spec = pl.BlockSpec((nseq, HG_HEADS, HG_DK, HG_DV), lambda b, t: (b, 0, 0, 0))
    in_specs = [
        pl.BlockSpec((nseq, tt, HG_FDIM), blk), pl.BlockSpec((nseq, tt, HG_FDIM), blk),
        pl.BlockSpec((nseq, tt, HG_IDIM), blk), pl.BlockSpec((nseq, tt, HG_IDIM), blk),
        _const_spec((depth, HG_FDIM)), _const_spec((1, HG_IDIM)),
    ]
    args = [x.reshape(batch, ln, x.shape[-1]) for x in (f_pre, q_pre, v, g_pre)] + [hg_lb, hg_norm]
    if has_init:
        in_specs.append(st_spec)
        args.append(s0)
    o, st = pl.pallas_call(
        functools.partial(_hgrn_kernel, layer=layer, chunk=chunk, has_init=has_init),
        grid=(batch // nseq, nt),
        in_specs=in_specs,
        out_specs=[pl.BlockSpec((nseq, tt, HG_IDIM), blk), st_spec],
        out_shape=[jax.ShapeDtypeStruct((batch, ln, HG_IDIM), BF16),
                   jax.ShapeDtypeStruct((batch, HG_HEADS, HG_DK, HG_DV), F32)],
        scratch_shapes=[pltpu.VMEM((nseq, HG_HEADS, HG_DV, HG_DK), F32)],
        compiler_params=_cparams(("parallel", "arbitrary")),
        name="hgrn",
    )(*args)
    return o.reshape(batch * ln, HG_IDIM), st


ROUTE_GROUP_LANE = N_EXPERTS


def _route(logits, cnt_ref):
    lane_i = lax.broadcasted_iota(jnp.int32, logits.shape, 1)
    lane = lane_i.astype(F32)
    big = float(LANES)
    neg = -jnp.inf
    is_g = jnp.logical_and(lane_i >= ROUTE_GROUP_LANE, lane_i < ROUTE_GROUP_LANE + N_GROUPS)
    gl = jnp.where(is_g, logits, neg)
    gmax = jnp.max(gl, axis=-1, keepdims=True)
    gidx = jnp.min(jnp.where(gl == gmax, lane - ROUTE_GROUP_LANE, big), axis=-1, keepdims=True)
    g_top = 1.0 / jnp.sum(jnp.exp(gl - gmax), axis=-1, keepdims=True)
    lane_group = _div_pow2(lane_i, EXPERTS_PER_GROUP).astype(F32)
    in_group = jnp.logical_and(lane_i < N_EXPERTS, lane_group == gidx)
    el = jnp.where(in_group, logits, neg)
    m1 = jnp.max(el, axis=-1, keepdims=True)
    i1 = jnp.min(jnp.where(el == m1, lane, big), axis=-1, keepdims=True)
    el2 = jnp.where(lane == i1, neg, el)
    m2 = jnp.max(el2, axis=-1, keepdims=True)
    i2 = jnp.min(jnp.where(el2 == m2, lane, big), axis=-1, keepdims=True)
    r = jnp.exp(m2 - m1)
    w1 = g_top / (1.0 + r)
    w2 = g_top * r / (1.0 + r)
    tm = logits.shape[0]
    hit = jnp.logical_or(lane == i1, lane == i2)
    rr = lax.broadcasted_iota(jnp.int32, (tm, tm), 0)
    cc = lax.broadcasted_iota(jnp.int32, (tm, tm), 1)
    before = _dot((cc < rr).astype(BF16), jnp.where(hit, 1.0, 0.0).astype(BF16)) + cnt_ref[...]
    r1 = jnp.sum(jnp.where(lane == i1, before, 0.0), axis=-1, keepdims=True)
    r2 = jnp.sum(jnp.where(lane == i2, before, 0.0), axis=-1, keepdims=True)
    cnt_ref[...] += jnp.sum(jnp.where(hit, 1.0, 0.0), axis=0, keepdims=True)
    out = jnp.where(lane_i == 0, i1, 0.0)
    out = jnp.where(lane_i == 1, i2, out)
    out = jnp.where(lane_i == 2, w1, out)
    out = jnp.where(lane_i == 3, w2, out)
    out = jnp.where(lane_i == 4, r1, out)
    out = jnp.where(lane_i == 5, r2, out)
    return out


def _post_kernel(*refs, alpha, pre_ln):
    x, refs = _input_rows(refs, pre_ln)
    (a_ref, o_ref, ple_ref, wg_ref, wa_ref, wb_ref, wo_ref, g1_ref, b1_ref,
     wr_ref, wpg_ref, wp_ref, x1_ref, base_ref, route_ref, cnt_ref) = refs

    @pl.when(pl.program_id(0) == 0)
    def _():
        cnt_ref[...] = jnp.zeros_like(cnt_ref)

    xb = x.astype(BF16)
    ya = _dot(a_ref[...], wa_ref[...])
    merged = _sigmoid(_dot(xb, wg_ref[:, 0:D_MODEL])) * ya
    yb = _dot(o_ref[...], wb_ref[...])
    merged = merged + _sigmoid(_dot(xb, wg_ref[:, D_MODEL:2 * D_MODEL])) * yb
    mix = _dot(merged.astype(BF16), wo_ref[...])
    x1 = _layernorm(alpha * x + mix, g1_ref[...], b1_ref[...])
    _to_token_tiles(x1_ref, x1)
    x1b = x1.astype(BF16)
    route_ref[...] = _route(_dot(x1b, wr_ref[...]), cnt_ref)
    ple = _dot(ple_ref[...].astype(BF16), wp_ref[...])
    base_ref[...] = alpha * x1 + _sigmoid(_dot(x1b, wpg_ref[...])) * ple


def _post(xn, a, o, ple, lw, alpha, ln0):
    t = xn.shape[0]
    tm = TOKEN_TILE
    row = lambda i: (i, 0)
    d = D_MODEL
    pre_ln = ln0 is not None
    x_specs = [pl.BlockSpec((tm, d), row)]
    x_args = [xn]
    if pre_ln:
        x_specs += [_const_spec((1, d)), _const_spec((1, d))]
        x_args += [ln0[0].reshape(1, d), ln0[1].reshape(1, d)]
    return pl.pallas_call(
        functools.partial(_post_kernel, alpha=alpha, pre_ln=pre_ln),
        grid=(t // tm,),
        in_specs=x_specs + [
            pl.BlockSpec((tm, 512), row), pl.BlockSpec((tm, HG_IDIM), row),
            pl.BlockSpec((tm, PLE_DIM), row),
            _const_spec((d, 2 * d)), _const_spec((512, d)), _const_spec((HG_IDIM, d)), _const_spec((d, d)),
            _const_spec((1, d)), _const_spec((1, d)),
            _const_spec((d, LANES)), _const_spec((d, d)), _const_spec((PLE_DIM, d)),
        ],
        out_specs=[pl.BlockSpec((tm * TILE_ROWS, LANES), row), pl.BlockSpec((tm, d), row),
                   pl.BlockSpec((tm, LANES), row), pl.BlockSpec((1, LANES), lambda i: (0, 0))],
        out_shape=[jax.ShapeDtypeStruct((t * TILE_ROWS, LANES), F32), jax.ShapeDtypeStruct((t, d), F32),
                   jax.ShapeDtypeStruct((t, LANES), F32), jax.ShapeDtypeStruct((1, LANES), F32)],
        compiler_params=_cparams(("arbitrary",)),
        name="post",
    )(*x_args, a, o, ple, lw["w_gate"], lw["w_br_a"], lw["w_br_b"], lw["w_out"], lw["ln1_g"], lw["ln1_b"],
      lw["w_route"], lw["w_ple_gate"], lw["w_ple"])


TILE_ROWS = D_MODEL // LANES


def _to_token_tiles(ref, x):
    n = x.shape[0]
    for c in range(TILE_ROWS):
        ref[pl.ds(c, n, stride=TILE_ROWS), :] = x[:, c * LANES:(c + 1) * LANES]


def _from_token_tiles(ref, first_token, n):
    return jnp.concatenate(
        [ref[pl.ds(first_token * TILE_ROWS + c, n, stride=TILE_ROWS), :] for c in range(TILE_ROWS)], axis=1)


def _tile_copy(idx_ref, j, src_hbm, dst, dst_token, sem):
    r = pl.multiple_of(idx_ref[0, j], TILE_ROWS)
    d = dst_token * TILE_ROWS
    if not isinstance(d, int):
        d = pl.multiple_of(d, TILE_ROWS)
    return pltpu.make_async_copy(src_hbm.at[pl.ds(r, TILE_ROWS)], dst.at[pl.ds(d, TILE_ROWS)], sem)


def _issue_tiles_loop(idx_ref, src_hbm, dst, sem, n_tokens, dst_off=0):
    def body(j, carry):
        _tile_copy(idx_ref, j, src_hbm, dst, dst_off + j, sem).start()
        return carry

    lax.fori_loop(0, n_tokens, body, 0, unroll=8)


def _issue_tiles_static(idx_ref, src_hbm, dst, sem, lo, hi, dst_off=0):
    for j in range(lo, hi):
        _tile_copy(idx_ref, j, src_hbm, dst, dst_off + j, sem).start(priority=j % 2)


def _wait_tiles(src_hbm, dst, sem):
    pltpu.make_async_copy(src_hbm.at[pl.ds(0, dst.shape[0])], dst, sem).wait()


def _idx_spec(tile, n_blocks, ahead):
    if ahead is None:
        imap = lambda i, *_: (0, 0, 0)
    else:
        imap = lambda i, *_: (jnp.minimum(i + ahead, n_blocks - 1), 0, 0)
    return pl.BlockSpec((None, 1, tile), imap, memory_space=pltpu.SMEM)


MOE_SLOTS = 3


def _moe_kernel(te_ref, idx0_ref, idx1_ref, idx2_ref, x_hbm, w1_ref, w3_ref, w2_ref, y_ref, xbuf, sem):
    i = pl.program_id(0)
    n = pl.num_programs(0)
    tm = MOE_TILE
    slot = i % MOE_SLOTS

    @pl.when(i == 0)
    def _():
        _issue_tiles_loop(idx0_ref, x_hbm, xbuf.at[0], sem.at[0], tm)
        _issue_tiles_loop(idx1_ref, x_hbm, xbuf.at[1], sem.at[1], tm)

    _wait_tiles(x_hbm, xbuf.at[slot], sem.at[slot])
    xb = _from_token_tiles(xbuf.at[slot], 0, tm).astype(BF16)
    ahead = (i + 2) % MOE_SLOTS
    nxt = xbuf.at[ahead]
    nsem = sem.at[ahead]
    group = tm // 4
    hs = []
    half = D_EXPERT // 2
    for c in range(2):
        cs = slice(c * half, (c + 1) * half)
        h1 = _dot(xb, w1_ref[:, cs])
        _issue_tiles_static(idx2_ref, x_hbm, nxt, nsem, 2 * c * group, (2 * c + 1) * group)
        h3 = _dot(xb, w3_ref[:, cs])
        _issue_tiles_static(idx2_ref, x_hbm, nxt, nsem, (2 * c + 1) * group, (2 * c + 2) * group)
        hs.append((h1 * _sigmoid(h1) * h3).astype(BF16))
    _to_token_tiles(y_ref, _dot(jnp.concatenate(hs, axis=1), w2_ref[...]))

    @pl.when(i == n - 1)
    def _():
        _wait_tiles(x_hbm, xbuf.at[(i + 1) % MOE_SLOTS], sem.at[(i + 1) % MOE_SLOTS])
        _wait_tiles(x_hbm, nxt, nsem)


def _moe(x_tiles, src, tile_e, w1, w3, w2):
    tm = MOE_TILE
    n_tiles = src.shape[0]
    d = D_MODEL
    grid_spec = pltpu.PrefetchScalarGridSpec(
        num_scalar_prefetch=1,
        grid=(n_tiles,),
        in_specs=[
            _idx_spec(tm, n_tiles, None), _idx_spec(tm, n_tiles, 1), _idx_spec(tm, n_tiles, 2),
            pl.BlockSpec(memory_space=pl.ANY),
            pl.BlockSpec((None, d, D_EXPERT), lambda i, te: (te[i], 0, 0)),
            pl.BlockSpec((None, d, D_EXPERT), lambda i, te: (te[i], 0, 0)),
            pl.BlockSpec((None, D_EXPERT, d), lambda i, te: (te[i], 0, 0)),
        ],
        out_specs=pl.BlockSpec((tm * TILE_ROWS, LANES), lambda i, te: (i, 0)),
        scratch_shapes=[pltpu.VMEM((MOE_SLOTS, tm * TILE_ROWS, LANES), F32), pltpu.SemaphoreType.DMA((MOE_SLOTS,))],
    )
    src3 = src.reshape(n_tiles, 1, tm)
    return pl.pallas_call(
        _moe_kernel,
        grid_spec=grid_spec,
        out_shape=jax.ShapeDtypeStruct((n_tiles * tm * TILE_ROWS, LANES), F32),
        compiler_params=_cparams(("arbitrary",)),
        name="moe_experts",
    )(tile_e, src3, src3, src3, x_tiles, w1, w3, w2)


def _dispatch_plan(route, cnt, n_tiles):
    t = route.shape[0]
    a = t * TOP_K
    tm = MOE_TILE
    eid = jnp.arange(N_EXPERTS, dtype=jnp.int32)
    counts = cnt[0, 0:N_EXPERTS].astype(jnp.int32)
    tiles_e = (counts + tm - 1) // tm
    tile_end = jnp.cumsum(tiles_e)
    pstart = (tile_end - tiles_e) * tm
    e = route[:, 0:TOP_K].astype(jnp.int32)
    rank = route[:, 4:4 + TOP_K].astype(jnp.int32)
    pos = rank + jnp.sum(jnp.where(e[:, :, None] == eid[None, None, :], pstart[None, None, :], 0), axis=-1)
    pad_e = tiles_e * tm - counts
    cpad = jnp.cumsum(pad_e)
    k = jnp.arange(n_tiles * tm - a, dtype=jnp.int32)
    ke = jnp.sum((cpad[None, :] <= k[:, None]).astype(jnp.int32), axis=1)
    inside = k + jnp.sum(jnp.where(jnp.minimum(ke, N_EXPERTS - 1)[:, None] == eid[None, :],
                                   (pstart + counts - (cpad - pad_e))[None, :], 0), axis=1)
    padpos = jnp.where(ke < N_EXPERTS, inside, tile_end[-1] * tm + (k - cpad[-1]))
    keys = jnp.concatenate([pos.reshape(a), padpos])
    vals = jnp.concatenate([jnp.repeat(jnp.arange(t, dtype=jnp.int32), TOP_K), k % t])
    _, src = lax.sort_key_val(keys, vals)
    jt = jnp.arange(n_tiles, dtype=jnp.int32)
    tile_e = jnp.minimum(jnp.sum((tile_end[None, :] <= jt[:, None]).astype(jnp.int32), axis=1), N_EXPERTS - 1)
    return (src * TILE_ROWS).reshape(n_tiles, tm), pos.reshape(a) * TILE_ROWS, tile_e.astype(jnp.int32)


def _final_kernel(p0c_ref, p1c_ref, p0n_ref, p1n_ref, base_ref, route_ref, g_ref, b_ref, y_hbm, o_ref, ybuf, sem):
    i = pl.program_id(0)
    n = pl.num_programs(0)
    tm = FINAL_TILE
    slot = i % 2

    @pl.when(i == 0)
    def _():
        _issue_tiles_loop(p0c_ref, y_hbm, ybuf.at[0], sem.at[0], tm, 0)
        _issue_tiles_loop(p1c_ref, y_hbm, ybuf.at[0], sem.at[0], tm, tm)

    _wait_tiles(y_hbm, ybuf.at[slot], sem.at[slot])
    nxt = ybuf.at[1 - slot]
    nsem = sem.at[1 - slot]
    _issue_tiles_static(p0n_ref, y_hbm, nxt, nsem, 0, tm, 0)
    _issue_tiles_static(p1n_ref, y_hbm, nxt, nsem, 0, tm, tm)
    w = route_ref[...]
    cur = ybuf.at[slot]
    h = base_ref[...] + w[:, 2:3] * _from_token_tiles(cur, 0, tm) + w[:, 3:4] * _from_token_tiles(cur, tm, tm)
    o_ref[...] = _layernorm(h, g_ref[...], b_ref[...])

    @pl.when(i == n - 1)
    def _():
        _wait_tiles(y_hbm, nxt, nsem)


def _final(base, y_tiles, pos, route, g, b):
    t, d = base.shape
    tm = FINAL_TILE
    nb = t // tm
    row = lambda i: (i, 0)
    pos2 = pos.reshape(t, TOP_K)
    p0 = pos2[:, 0].reshape(nb, 1, tm)
    p1 = pos2[:, 1].reshape(nb, 1, tm)
    first, nxt = _idx_spec(tm, nb, None), _idx_spec(tm, nb, 1)
    return pl.pallas_call(
        _final_kernel,
        grid=(nb,),
        in_specs=[first, first, nxt, nxt,
                  pl.BlockSpec((tm, d), row), pl.BlockSpec((tm, LANES), row),
                  _const_spec((1, d)), _const_spec((1, d)),
                  pl.BlockSpec(memory_space=pl.ANY)],
        out_specs=pl.BlockSpec((tm, d), row),
        out_shape=jax.ShapeDtypeStruct((t, d), F32),
        scratch_shapes=[pltpu.VMEM((2, 2 * tm * TILE_ROWS, LANES), F32), pltpu.SemaphoreType.DMA((2,))],
        compiler_params=_cparams(("arbitrary",)),
        name="final_ln",
    )(p0, p1, p0, p1, base, route, g, b, y_tiles)


def _rot_cols(w):
    half = ROPE_DIM // 2
    return jnp.concatenate([-w[..., half:], w[..., :half]], axis=-1)


def _layer_weights(i, w_in, q_norm, w_uq, kv_norm, w_uk, w_uv, hg_norm, w_br_a, w_br_b, w_out, ln1_g, ln1_b,
                   w_rg, w_re, w_e_gate, w_e_up, w_e_down, w_ple, w_ple_gate, ln2_g, ln2_b):
    d = D_MODEL
    nh = MLA_HEADS
    win = w_in[i]
    c_kr = Q_LORA + KV_LORA
    c_f = c_kr + ROPE_DIM
    c_ga = c_f + N_HG
    w_kr = win[:, c_kr:c_f]
    zpad = jnp.zeros((d, LANES - ROPE_DIM), F32)
    wa = jnp.concatenate([win[:, 0:c_kr], win[:, c_f:c_ga], w_kr, zpad, _rot_cols(w_kr), zpad], axis=1)

    wq3 = w_uq[i].reshape(Q_LORA, nh, NOPE_DIM + ROPE_DIM)
    zq = jnp.zeros((Q_LORA, nh, HEAD_PAD - NOPE_DIM - ROPE_DIM), F32)
    wq = jnp.concatenate([wq3, zq], axis=-1).reshape(Q_LORA, QK_PAD)
    wq_rot = jnp.concatenate([jnp.zeros((Q_LORA, nh, NOPE_DIM), F32), _rot_cols(wq3[..., NOPE_DIM:]), zq],
                             axis=-1).reshape(Q_LORA, QK_PAD)

    wuk = w_uk[i]
    wuv = w_uv[i]
    k_nope = jnp.concatenate([wuk, jnp.zeros((KV_LORA, nh, HEAD_PAD - NOPE_DIM), F32)], axis=-1)
    eye = jnp.eye(ROPE_DIM, dtype=F32)
    k_rope = jnp.concatenate([jnp.zeros((ROPE_DIM, NOPE_DIM), F32), eye,
                              jnp.zeros((ROPE_DIM, HEAD_PAD - NOPE_DIM - ROPE_DIM), F32)], axis=-1)
    k_rope = jnp.concatenate([jnp.tile(k_rope[:, None, :], (1, nh, 1)).reshape(ROPE_DIM, QK_PAD),
                              jnp.zeros((LANES - ROPE_DIM, QK_PAD), F32)], axis=0)
    zv = jnp.zeros((KV_LORA, nh // 2, V_DIM), F32)
    wuv_a3 = jnp.concatenate([wuv[:, 0::2, :], zv], axis=-1)
    wuv_b3 = jnp.concatenate([zv, wuv[:, 1::2, :]], axis=-1)
    wkv_top = jnp.concatenate([k_nope.reshape(KV_LORA, QK_PAD), wuv_a3.reshape(KV_LORA, 512),
                               wuv_b3.reshape(KV_LORA, 512)], axis=1)
    wkv_bot = jnp.concatenate([k_rope, jnp.zeros((LANES, QK_PAD), F32)], axis=1)
    wkv = jnp.concatenate([wkv_top, wkv_bot], axis=0)

    wuk_t = jnp.concatenate([jnp.transpose(wuk, (1, 2, 0)),
                             jnp.zeros((nh, HEAD_PAD - NOPE_DIM, KV_LORA), F32)], axis=1)
    p_sel = jnp.concatenate([jnp.zeros((NOPE_DIM, ROPE_DIM), F32), eye,
                             jnp.zeros((HEAD_PAD - NOPE_DIM - ROPE_DIM, ROPE_DIM), F32)], axis=0)

    w_route = jnp.concatenate([w_re[i], w_rg[i], jnp.zeros((d, LANES - N_EXPERTS - N_GROUPS), F32)], axis=1)
    bf = lambda w: w.astype(BF16)
    return {
        "wa": bf(wa), "q_norm": q_norm[i].reshape(1, -1), "kv_norm": kv_norm[i].reshape(1, -1),
        "wq": bf(wq), "wq_rot": bf(wq_rot), "wkv": bf(wkv),
        "wuk_t": bf(wuk_t), "p_sel": bf(p_sel),
        "wuv_a3": bf(jnp.transpose(wuv_a3, (1, 0, 2))), "wuv_b3": bf(jnp.transpose(wuv_b3, (1, 0, 2))),
        "hg_norm": hg_norm[i].reshape(1, -1),
        "w_gate": bf(win[:, c_ga:c_ga + 2 * d]), "w_br_a": bf(w_br_a[i]), "w_br_b": bf(w_br_b[i]),
        "w_out": bf(w_out[i]), "ln1_g": ln1_g[i].reshape(1, -1), "ln1_b": ln1_b[i].reshape(1, -1),
        "w_route": bf(w_route), "w_ple_gate": bf(w_ple_gate[i]), "w_ple": bf(w_ple[i]),
        "w1": bf(w_e_gate[i]), "w3": bf(w_e_up[i]), "w2": bf(w_e_down[i]),
        "ln2_g": ln2_g[i].reshape(1, -1), "ln2_b": ln2_b[i].reshape(1, -1),
    }


def _rope_table(pos, rows):
    half = ROPE_DIM // 2
    inv = ROPE_THETA ** (-jnp.arange(half, dtype=F32) / half)
    ang = pos.astype(F32)[:, None] * inv[None, :]
    c2 = jnp.concatenate([jnp.cos(ang), jnp.cos(ang)], axis=1)
    s2 = jnp.concatenate([jnp.sin(ang), jnp.sin(ang)], axis=1)
    n = pos.shape[0]
    ones = jnp.ones((n, NOPE_DIM), F32)
    z = lambda w: jnp.zeros((n, w), F32)
    tab = jnp.concatenate([
        ones, c2, z(HEAD_PAD - NOPE_DIM - ROPE_DIM),
        z(NOPE_DIM), s2, z(HEAD_PAD - NOPE_DIM - ROPE_DIM),
        c2, z(LANES - ROPE_DIM),
        s2, z(LANES - ROPE_DIM)], axis=1)
    if rows > n:
        tab = jnp.tile(tab, (rows // n, 1))
    return tab


def _trunk_layer(i, xn, ple, tab, lw, hg_lb, batch, ln, cache, alpha, ln0):
    q, k, va, vb, ckv, kr, f_pre, q_pre, hv, g_pre = _proj(xn, tab, lw, ln0)
    if cache is None:
        a = _attn_prompt(q, k, va, vb, batch, ln)
        o, st = _hgrn(f_pre, q_pre, hv, g_pre, hg_lb, lw["hg_norm"], None, i, batch, ln)
    else:
        a = _attn_sample(q, ckv, kr, cache[0], cache[1], lw, batch, ln)
        o, st = _hgrn(f_pre, q_pre, hv, g_pre, hg_lb, lw["hg_norm"], cache[2], i, batch, ln)
    x1, base, route, cnt = _post(xn, a, o, ple, lw, alpha, ln0)

    t = xn.shape[0]
    n_tiles = (t * TOP_K) // MOE_TILE + N_EXPERTS
    src, pos, tile_e = _dispatch_plan(route, cnt, n_tiles)
    y = _moe(x1, src, tile_e, lw["w1"], lw["w3"], lw["w2"])
    x2 = _final(base, y, pos, route, lw["ln2_g"], lw["ln2_b"])
    return x2, ckv, kr, st


def kernel(x_prompt, x_sample, p_prompt, p_sample, cache_ckv, cache_krope, state_hgrn, ln0_g, ln0_b, w_in, q_norm, w_uq, kv_norm, w_uk, w_uv, hg_lb, hg_norm, w_br_a, w_br_b, w_out, ln1_g, ln1_b, w_router_group, w_router_expert, w_e_gate, w_e_up, w_e_down, w_ple, w_ple_gate, ln2_g, ln2_b):
    depth = w_in.shape[0]
    bp, sp, d = x_prompt.shape
    bs, ss, _ = x_sample.shape
    past = cache_ckv.shape[2]
    alpha = (2 * depth) ** 0.25

    xp = x_prompt.reshape(bp * sp, d)
    xs = x_sample.reshape(bs * ss, d)
    tab_p = _rope_table(jnp.arange(sp, dtype=jnp.int32), max(sp, TOKEN_TILE))
    tab_s = _rope_table(past + jnp.arange(ss, dtype=jnp.int32), max(ss, TOKEN_TILE))

    outs = [[] for _ in range(6)]
    for i in range(depth):
        lw = _layer_weights(i, w_in, q_norm, w_uq, kv_norm, w_uk, w_uv, hg_norm, w_br_a, w_br_b, w_out,
                            ln1_g, ln1_b, w_router_group, w_router_expert, w_e_gate, w_e_up, w_e_down,
                            w_ple, w_ple_gate, ln2_g, ln2_b)
        ln0 = (ln0_g, ln0_b) if i == 0 else None
        xp, c, r, s = _trunk_layer(i, xp, p_prompt[i].reshape(bp * sp, -1), tab_p, lw, hg_lb, bp, sp, None, alpha, ln0)
        outs[0].append(c.reshape(bp, sp, -1))
        outs[1].append(r.reshape(bp, sp, -1))
        outs[2].append(s)
        cache = (cache_ckv[i], cache_krope[i], state_hgrn[i])
        xs, c, r, s = _trunk_layer(i, xs, p_sample[i].reshape(bs * ss, -1), tab_s, lw, hg_lb, bs, ss, cache, alpha, ln0)
        outs[3].append(c.reshape(bs, ss, -1))
        outs[4].append(r.reshape(bs, ss, -1))
        outs[5].append(s)
    return (xp.reshape(bp, sp, d), xs.reshape(bs, ss, d)) + tuple(jnp.stack(o) for o in outs)
```

```python
import functools

import numpy as np
import jax
import jax.numpy as jnp
from jax import lax
from jax.experimental import pallas as pl
from jax.experimental.pallas import tpu as pltpu

F32 = jnp.float32
BF16 = jnp.bfloat16

D_MODEL = 1024
CHUNK = 64
PLE_DIM = 256
MLA_HEADS = 8
NOPE_DIM = 64
ROPE_DIM = 32
V_DIM = 64
Q_LORA = 384
KV_LORA = 256
ROPE_THETA = 10000.0
ATTN_SCALE = (NOPE_DIM + ROPE_DIM) ** -0.5
HG_HEADS = 4
HG_DK = 128
HG_DV = 128
HG_FDIM = HG_HEADS * HG_DK
HG_IDIM = HG_HEADS * HG_DV
N_GROUPS = 4
EXPERTS_PER_GROUP = 8
N_EXPERTS = N_GROUPS * EXPERTS_PER_GROUP
TOP_K = 2
D_EXPERT = 512
EPS = 1e-6

LANES = 128
HEAD_PAD = LANES
QK_PAD = MLA_HEADS * HEAD_PAD
VMEM_LIMIT = 56 * 1024 * 1024

TOKEN_TILE = 512
ATTN_TILE = 256
MOE_TILE = 256
HG_CHUNK = 128
HG_SEQS_PER_STEP = 4
FINAL_TILE = 256


def _cparams(sem):
    return pltpu.CompilerParams(dimension_semantics=sem, vmem_limit_bytes=VMEM_LIMIT)


def _const_spec(shape):
    nd = len(shape)
    return pl.BlockSpec(shape, lambda *_: (0,) * nd, pipeline_mode=pl.Buffered(1))


def _dot(a, b):
    return jnp.dot(a, b, preferred_element_type=F32)


def _dot_nt(a, b):
    return lax.dot_general(a, b, (((1,), (1,)), ((), ())), preferred_element_type=F32)


def _dot_tn(a, b):
    return lax.dot_general(a, b, (((0,), (0,)), ((), ())), preferred_element_type=F32)


def _div_pow2(x, d):
    return jnp.right_shift(x, int(d).bit_length() - 1)


def _sigmoid(x):
    return 1.0 / (1.0 + jnp.exp(-x))


def _layernorm(x, g, b):
    mu = jnp.mean(x, axis=-1, keepdims=True)
    xc = x - mu
    var = jnp.mean(xc * xc, axis=-1, keepdims=True)
    return xc * lax.rsqrt(var + EPS) * g + b


def _rmsnorm(x, g):
    return x * lax.rsqrt(jnp.mean(x * x, axis=-1, keepdims=True) + EPS) * g


N_LAT = Q_LORA + KV_LORA
N_HG = 2 * HG_FDIM + 2 * HG_IDIM
COL_HG = N_LAT
COL_KR = N_LAT + N_HG
N_PROJ = COL_KR + 2 * LANES


def _input_rows(refs, pre_ln):
    if pre_ln:
        x_ref, g_ref, b_ref = refs[:3]
        return _layernorm(x_ref[...], g_ref[...], b_ref[...]), refs[3:]
    return refs[0][...], refs[1:]


def _proj_kernel(*refs, pre_ln):
    x, refs = _input_rows(refs, pre_ln)
    (tab_ref, wa_ref, qn_ref, kvn_ref, wq_ref, wqr_ref, wkv_ref,
     q_ref, k_ref, va_ref, vb_ref, ckv_ref, kr_ref, f_ref, hq_ref, hv_ref, hg_ref) = refs
    xb = x.astype(BF16)
    cq_tab = tab_ref[:, 0:LANES]
    sq_tab = tab_ref[:, LANES:2 * LANES]
    ck_tab = tab_ref[:, 2 * LANES:3 * LANES]
    sk_tab = tab_ref[:, 3 * LANES:4 * LANES]

    lat = _dot(xb, wa_ref[:, 0:N_LAT])
    cqn = _rmsnorm(lat[:, 0:Q_LORA], qn_ref[...]).astype(BF16)
    ckvn = _rmsnorm(lat[:, Q_LORA:N_LAT], kvn_ref[...])
    ckv_ref[...] = ckvn

    qa = _dot(cqn, wq_ref[...])
    qb = _dot(cqn, wqr_ref[...])
    for h in range(MLA_HEADS):
        sl = slice(h * HEAD_PAD, (h + 1) * HEAD_PAD)
        q_ref[:, sl] = (qa[:, sl] * cq_tab + qb[:, sl] * sq_tab).astype(BF16)

    kr2 = _dot(xb, wa_ref[:, COL_KR:N_PROJ])
    kr = kr2[:, 0:LANES] * ck_tab + kr2[:, LANES:2 * LANES] * sk_tab
    kr_ref[...] = kr[:, 0:ROPE_DIM]

    kvin = jnp.concatenate([ckvn.astype(BF16), kr.astype(BF16)], axis=1)
    kv = _dot(kvin, wkv_ref[...])
    k_ref[...] = kv[:, 0:QK_PAD].astype(BF16)
    va_ref[...] = kv[:, QK_PAD:QK_PAD + 512].astype(BF16)
    vb_ref[...] = kv[:, QK_PAD + 512:QK_PAD + 1024].astype(BF16)

    f_ref[...] = _dot(xb, wa_ref[:, COL_HG:COL_HG + 512])
    hq_ref[...] = _dot(xb, wa_ref[:, COL_HG + 512:COL_HG + 1024])
    hv_ref[...] = _dot(xb, wa_ref[:, COL_HG + 1024:COL_HG + 1536]).astype(BF16)
    hg_ref[...] = _dot(xb, wa_ref[:, COL_HG + 1536:COL_HG + 2048])


def _proj(xn, tab, lw, ln0):
    t = xn.shape[0]
    pre_ln = ln0 is not None
    x_specs = [pl.BlockSpec((TOKEN_TILE, D_MODEL), lambda i: (i, 0))]
    x_args = [xn]
    if pre_ln:
        x_specs += [_const_spec((1, D_MODEL)), _const_spec((1, D_MODEL))]
        x_args += [ln0[0].reshape(1, D_MODEL), ln0[1].reshape(1, D_MODEL)]
    tm = TOKEN_TILE
    ntab = tab.shape[0] // tm
    row = lambda i: (i, 0)
    outs = [
        (QK_PAD, BF16), (QK_PAD, BF16), (512, BF16), (512, BF16), (KV_LORA, F32), (ROPE_DIM, F32),
        (HG_FDIM, F32), (HG_FDIM, F32), (HG_IDIM, BF16), (HG_IDIM, F32),
    ]
    return pl.pallas_call(
        functools.partial(_proj_kernel, pre_ln=pre_ln),
        grid=(t // tm,),
        in_specs=x_specs + [
            pl.BlockSpec((tm, 4 * LANES), lambda i: (i % ntab, 0)),
            _const_spec((D_MODEL, N_PROJ)),
            _const_spec((1, Q_LORA)), _const_spec((1, KV_LORA)),
            _const_spec((Q_LORA, QK_PAD)), _const_spec((Q_LORA, QK_PAD)),
            _const_spec((KV_LORA + LANES, 2 * QK_PAD)),
        ],
        out_specs=[pl.BlockSpec((tm, w), row) for w, _ in outs],
        out_shape=[jax.ShapeDtypeStruct((t, w), dt) for w, dt in outs],
        compiler_params=_cparams(("parallel",)),
        name="proj",
    )(*x_args, tab, lw["wa"], lw["q_norm"], lw["kv_norm"], lw["wq"], lw["wq_rot"], lw["wkv"])


LOG2E = 1.4426950408889634


def _attn_prompt_kernel(q_ref, k_ref, va_ref, vb_ref, o_ref, s_ref, m_ref, l_ref, acc_ref):
    tq = ATTN_TILE
    nh = MLA_HEADS
    qi = pl.program_id(1)
    rq = _div_pow2(lax.broadcasted_iota(jnp.int32, (tq, tq), 0), CHUNK)
    ck = _div_pow2(lax.broadcasted_iota(jnp.int32, (tq, tq), 1), CHUNK)
    diag_mask = ck <= rq
    c = ATTN_SCALE * LOG2E

    m_ref[...] = jnp.full(m_ref.shape, -jnp.inf, F32)
    l_ref[...] = jnp.zeros(l_ref.shape, F32)
    acc_ref[...] = jnp.zeros(acc_ref.shape, F32)

    def scores(kt, mask):
        ks = pl.multiple_of(kt * tq, tq)
        for h in range(nh):
            hs = slice(h * HEAD_PAD, (h + 1) * HEAD_PAD)
            s = _dot_nt(q_ref[:, hs], k_ref[pl.ds(ks, tq), hs]) * c
            if mask is not None:
                s = jnp.where(mask, s, -jnp.inf)
            s_ref[h, :, pl.ds(ks, tq)] = s
            m_ref[h] = jnp.maximum(m_ref[h], jnp.maximum(s[:, 0:LANES], s[:, LANES:2 * LANES]))

    def p1(kt, carry):
        scores(kt, None)
        return carry

    lax.fori_loop(0, qi, p1, 0)
    scores(qi, diag_mask)
    for h in range(nh):
        m_ref[h] = jnp.broadcast_to(jnp.max(m_ref[h], axis=-1, keepdims=True), (tq, LANES))

    def p2(kt, carry):
        ks = pl.multiple_of(kt * tq, tq)
        for h in range(nh):
            v_ref = va_ref if h % 2 == 0 else vb_ref
            pair = h // 2
            mb = m_ref[h]
            p_lo = jnp.exp2(s_ref[h, :, pl.ds(ks, LANES)] - mb)
            p_hi = jnp.exp2(s_ref[h, :, pl.ds(ks + LANES, LANES)] - mb)
            l_ref[h] += p_lo + p_hi
            p = jnp.concatenate([p_lo, p_hi], axis=1).astype(BF16)
            acc_ref[h] += _dot(p, v_ref[pl.ds(ks, tq), pair * LANES:(pair + 1) * LANES])
        return carry

    lax.fori_loop(0, qi + 1, p2, 0)
    for pair in range(nh // 2):
        he, ho = 2 * pair, 2 * pair + 1
        o = (acc_ref[he] / jnp.sum(l_ref[he], axis=-1, keepdims=True)
             + acc_ref[ho] / jnp.sum(l_ref[ho], axis=-1, keepdims=True))
        o_ref[:, pair * LANES:(pair + 1) * LANES] = o.astype(BF16)


def _attn_prompt(q, k, va, vb, batch, seq):
    tq = ATTN_TILE
    nq = seq // tq
    nh = MLA_HEADS
    return pl.pallas_call(
        _attn_prompt_kernel,
        grid=(batch, nq),
        in_specs=[
            pl.BlockSpec((tq, QK_PAD), lambda b, i: (b * nq + i, 0)),
            pl.BlockSpec((seq, QK_PAD), lambda b, i: (b, 0)),
            pl.BlockSpec((seq, 512), lambda b, i: (b, 0)),
            pl.BlockSpec((seq, 512), lambda b, i: (b, 0)),
        ],
        out_specs=pl.BlockSpec((tq, 512), lambda b, i: (b * nq + i, 0)),
        out_shape=jax.ShapeDtypeStruct((batch * seq, 512), BF16),
        scratch_shapes=[pltpu.VMEM((nh, tq, seq), F32), pltpu.VMEM((nh, tq, LANES), F32),
                        pltpu.VMEM((nh, tq, LANES), F32), pltpu.VMEM((nh, tq, LANES), F32)],
        compiler_params=_cparams(("parallel", "arbitrary")),
        name="attn_prompt",
    )(q, k, va, vb)


SAMPLE_KEY_TILE = 1024


def _attn_sample_kernel(q_ref, ckvn_ref, krn_ref, cc_ref, ck_ref, wukt_ref, psel_ref, wva_ref, wvb_ref,
                        o_ref, s_ref):
    ln = q_ref.shape[0]
    past = cc_ref.shape[0]
    nh = MLA_HEADS
    qlat, qrope = [], []
    for h in range(nh):
        qh = q_ref[:, h * HEAD_PAD:(h + 1) * HEAD_PAD]
        qlat.append(_dot(qh, wukt_ref[h]).astype(BF16))
        qrope.append(_dot(qh, psel_ref[...]).astype(BF16))
    qlat = jnp.concatenate(qlat, axis=0)
    qrope = jnp.concatenate(qrope, axis=0)

    ckv_new = ckvn_ref[...].astype(BF16)
    kr_new = krn_ref[...].astype(BF16)
    s_new = (_dot_nt(qlat, ckv_new) + _dot_nt(qrope, kr_new)) * ATTN_SCALE
    m = jnp.max(s_new, axis=-1, keepdims=True)
    tk = SAMPLE_KEY_TILE
    for kt in range(past // tk):
        rows = slice(kt * tk, (kt + 1) * tk)
        s = (_dot_nt(qlat, cc_ref[rows, :].astype(BF16)) + _dot_nt(qrope, ck_ref[rows, :].astype(BF16))) * ATTN_SCALE
        s_ref[:, rows] = s
        m = jnp.maximum(m, jnp.max(s, axis=-1, keepdims=True))
    p_new = jnp.exp(s_new - m)
    l = jnp.sum(p_new, axis=-1, keepdims=True)
    lat = _dot(p_new.astype(BF16), ckv_new)
    for kt in range(past // tk):
        rows = slice(kt * tk, (kt + 1) * tk)
        p = jnp.exp(s_ref[:, rows] - m)
        l = l + jnp.sum(p, axis=-1, keepdims=True)
        lat = lat + _dot(p.astype(BF16), cc_ref[rows, :].astype(BF16))
    lat = (lat / l).astype(BF16)
    for pair in range(nh // 2):
        he, ho = 2 * pair, 2 * pair + 1
        o = _dot(lat[he * ln:(he + 1) * ln], wva_ref[pair]) + _dot(lat[ho * ln:(ho + 1) * ln], wvb_ref[pair])
        o_ref[:, pair * LANES:(pair + 1) * LANES] = o.astype(BF16)


def _attn_sample(q, ckv_new, kr_new, cache_ckv, cache_kr, lw, batch, ln):
    past = cache_ckv.shape[1]
    nh = MLA_HEADS
    return pl.pallas_call(
        _attn_sample_kernel,
        grid=(batch,),
        in_specs=[
            pl.BlockSpec((ln, QK_PAD), lambda b: (b, 0)),
            pl.BlockSpec((ln, KV_LORA), lambda b: (b, 0)),
            pl.BlockSpec((ln, ROPE_DIM), lambda b: (b, 0)),
            pl.BlockSpec((None, past, KV_LORA), lambda b: (b, 0, 0)),
            pl.BlockSpec((None, past, ROPE_DIM), lambda b: (b, 0, 0)),
            _const_spec((nh, HEAD_PAD, KV_LORA)),
            _const_spec((HEAD_PAD, ROPE_DIM)),
            _const_spec((nh // 2, KV_LORA, LANES)),
            _const_spec((nh // 2, KV_LORA, LANES)),
        ],
        out_specs=pl.BlockSpec((ln, 512), lambda b: (b, 0)),
        out_shape=jax.ShapeDtypeStruct((batch * ln, 512), BF16),
        scratch_shapes=[pltpu.VMEM((nh * ln, past), F32)],
        compiler_params=_cparams(("parallel",)),
        name="attn_sample",
    )(q, ckv_new, kr_new, cache_ckv, cache_kr, lw["wuk_t"], lw["p_sel"], lw["wuv_a3"], lw["wuv_b3"])


def _hgrn_kernel(*refs, layer, chunk, has_init):
    if has_init:
        f_ref, q_ref, v_ref, g_ref, lb_ref, nrm_ref, s0_ref, o_ref, sout_ref, st_ref = refs
    else:
        f_ref, q_ref, v_ref, g_ref, lb_ref, nrm_ref, o_ref, sout_ref, st_ref = refs
        s0_ref = None
    nseq, tt = f_ref.shape[0], f_ref.shape[1]
    c = chunk
    ti = pl.program_id(1)

    @pl.when(ti == 0)
    def _():
        for sq in range(nseq):
            for h in range(HG_HEADS):
                if has_init:
                    st_ref[sq, h] = s0_ref[sq, h].T
                else:
                    st_ref[sq, h] = jnp.zeros((HG_DV, HG_DK), F32)

    lbp = lb_ref[...]
    e = jnp.exp(lbp - jnp.max(lbp, axis=0, keepdims=True))
    tot = jnp.sum(e, axis=0, keepdims=True)
    part = jnp.zeros_like(tot)
    for j in range(1, layer + 1):
        part = part + e[j:j + 1]
    lb = part / tot
    log_lb = jnp.log(lb)
    log_1m = jnp.log1p(-lb)
    one_m = 1.0 - lb

    rid = lax.broadcasted_iota(jnp.int32, (c, 1), 0)
    rr = lax.broadcasted_iota(jnp.int32, (c, c), 0)
    cc = lax.broadcasted_iota(jnp.int32, (c, c), 1)
    levels = [c >> k for k in range(c.bit_length() - 1)]
    sums = [cc <= rr, cc > rr]
    for g in levels:
        ref_row = jnp.bitwise_and(rr, -g) + (g // 2 - 1)
        is_right = jnp.bitwise_and(rr, g - 1) >= g // 2
        lo_row = jnp.where(is_right, ref_row, rr)
        hi_row = jnp.where(is_right, rr, ref_row)
        sums.append(jnp.logical_and(cc > lo_row, cc <= hi_row))
    sum_mat = jnp.concatenate([m.astype(BF16) for m in sums], axis=0)
    pair_masks = [None] + [_div_pow2(rr, g) == _div_pow2(cc, g) for g in levels[1:]] + [rr == cc]

    for ci, sq in [(ci, sq) for ci in range(tt // c) for sq in range(nseq)]:
        rows = slice(ci * c, (ci + 1) * c)
        z = f_ref[sq, rows, :]
        ez = jnp.exp(-jnp.abs(z))
        rz = 1.0 / (1.0 + ez)
        log_sig = jnp.minimum(z, 0.0) - jnp.log1p(ez)
        kk = one_m * jnp.where(z >= 0, ez * rz, rz)
        y = log_1m + log_sig
        logf = jnp.maximum(log_lb, y) + jnp.log1p(jnp.exp(-jnp.abs(log_lb - y)))
        qp = q_ref[sq, rows, :]
        qq = qp * _sigmoid(qp)

        nl = -logf
        hi = nl.astype(BF16)
        lo = (nl - hi.astype(F32)).astype(BF16)
        acc = _dot(sum_mat, hi) + _dot(sum_mat, lo)
        q_in = (qq * jnp.exp(-acc[0:c])).astype(BF16)
        dec = jnp.exp(-acc[c - 1:c])
        k_out = (kk * jnp.exp(-acc[c:2 * c])).astype(BF16)
        qs, ks = [], []
        for li, g in enumerate(levels):
            x = jnp.exp(-acc[(2 + li) * c:(3 + li) * c])
            right = jnp.bitwise_and(rid, g - 1) >= g // 2
            qs.append(jnp.where(right, qq * x, 0.0).astype(BF16))
            ks.append(jnp.where(right, 0.0, kk * x).astype(BF16))
        qs.append(qq.astype(BF16))
        ks.append(kk.astype(BF16))

        gp = g_ref[sq, rows, :]
        gate = gp * _sigmoid(gp) * nrm_ref[...]

        for h in range(HG_HEADS):
            hs = slice(h * HG_DK, (h + 1) * HG_DK)
            a = None
            for ql, kl, mk in zip(qs, ks, pair_masks):
                al = _dot_nt(ql[:, hs], kl[:, hs])
                if mk is not None:
                    al = jnp.where(mk, al, 0.0)
                a = al if a is None else a + al
            vh = v_ref[sq, rows, hs]
            st = st_ref[sq, h]
            o = _dot_nt(q_in[:, hs], st.astype(BF16)) + _dot(a.astype(BF16), vh)
            o = o * lax.rsqrt(jnp.mean(o * o, axis=-1, keepdims=True) + EPS) * gate[:, hs]
            o_ref[sq, rows, hs] = o.astype(BF16)
            st_ref[sq, h] = st * dec[:, hs] + _dot_tn(vh, k_out[:, hs])

    @pl.when(ti == pl.num_programs(1) - 1)
    def _():
        for sq in range(nseq):
            for h in range(HG_HEADS):
                sout_ref[sq, h] = st_ref[sq, h].T


def _hgrn(f_pre, q_pre, v, g_pre, hg_lb, hg_norm, s0, layer, batch, ln):
    tt = min(ln, 256)
    chunk = min(tt, HG_CHUNK)
    nt = ln // tt
    nseq = HG_SEQS_PER_STEP
    has_init = s0 is not None
    depth = hg_lb.shape[0]
    blk = lambda b, t: (b, t, 0)
    st_spec = pl.BlockSpec((nseq, HG_HEADS, HG_DK, HG_DV), lambda b, t: (b, 0, 0, 0))
    in_specs = [
        pl.BlockSpec((nseq, tt, HG_FDIM), blk), pl.BlockSpec((nseq, tt, HG_FDIM), blk),
        pl.BlockSpec((nseq, tt, HG_IDIM), blk), pl.BlockSpec((nseq, tt, HG_IDIM), blk),
        _const_spec((depth, HG_FDIM)), _const_spec((1, HG_IDIM)),
    ]
    args = [x.reshape(batch, ln, x.shape[-1]) for x in (f_pre, q_pre, v, g_pre)] + [hg_lb, hg_norm]
    if has_init:
        in_specs.append(st_spec)
        args.append(s0)
    o, st = pl.pallas_call(
        functools.partial(_hgrn_kernel, layer=layer, chunk=chunk, has_init=has_init),
        grid=(batch // nseq, nt),
        in_specs=in_specs,
        out_specs=[pl.BlockSpec((nseq, tt, HG_IDIM), blk), st_spec],
        out_shape=[jax.ShapeDtypeStruct((batch, ln, HG_IDIM), BF16),
                   jax.ShapeDtypeStruct((batch, HG_HEADS, HG_DK, HG_DV), F32)],
        scratch_shapes=[pltpu.VMEM((nseq, HG_HEADS, HG_DV, HG_DK), F32)],
        compiler_params=_cparams(("parallel", "arbitrary")),
        name="hgrn",
    )(*args)
    return o.reshape(batch * ln, HG_IDIM), st


ROUTE_GROUP_LANE = N_EXPERTS


def _route(logits, cnt_ref):
    lane_i = lax.broadcasted_iota(jnp.int32, logits.shape, 1)
    lane = lane_i.astype(F32)
    big = float(LANES)
    neg = -jnp.inf
    is_g = jnp.logical_and(lane_i >= ROUTE_GROUP_LANE, lane_i < ROUTE_GROUP_LANE + N_GROUPS)
    gl = jnp.where(is_g, logits, neg)
    gmax = jnp.max(gl, axis=-1, keepdims=True)
    gidx = jnp.min(jnp.where(gl == gmax, lane - ROUTE_GROUP_LANE, big), axis=-1, keepdims=True)
    g_top = 1.0 / jnp.sum(jnp.exp(gl - gmax), axis=-1, keepdims=True)
    lane_group = _div_pow2(lane_i, EXPERTS_PER_GROUP).astype(F32)
    in_group = jnp.logical_and(lane_i < N_EXPERTS, lane_group == gidx)
    el = jnp.where(in_group, logits, neg)
    m1 = jnp.max(el, axis=-1, keepdims=True)
    i1 = jnp.min(jnp.where(el == m1, lane, big), axis=-1, keepdims=True)
    el2 = jnp.where(lane == i1, neg, el)
    m2 = jnp.max(el2, axis=-1, keepdims=True)
    i2 = jnp.min(jnp.where(el2 == m2, lane, big), axis=-1, keepdims=True)
    r = jnp.exp(m2 - m1)
    w1 = g_top / (1.0 + r)
    w2 = g_top * r / (1.0 + r)
    tm = logits.shape[0]
    hit = jnp.logical_or(lane == i1, lane == i2)
    rr = lax.broadcasted_iota(jnp.int32, (tm, tm), 0)
    cc = lax.broadcasted_iota(jnp.int32, (tm, tm), 1)
    before = _dot((cc < rr).astype(BF16), jnp.where(hit, 1.0, 0.0).astype(BF16)) + cnt_ref[...]
    r1 = jnp.sum(jnp.where(lane == i1, before, 0.0), axis=-1, keepdims=True)
    r2 = jnp.sum(jnp.where(lane == i2, before, 0.0), axis=-1, keepdims=True)
    cnt_ref[...] += jnp.sum(jnp.where(hit, 1.0, 0.0), axis=0, keepdims=True)
    out = jnp.where(lane_i == 0, i1, 0.0)
    out = jnp.where(lane_i == 1, i2, out)
    out = jnp.where(lane_i == 2, w1, out)
    out = jnp.where(lane_i == 3, w2, out)
    out = jnp.where(lane_i == 4, r1, out)
    out = jnp.where(lane_i == 5, r2, out)
    return out


def _post_kernel(*refs, alpha, pre_ln):
    x, refs = _input_rows(refs, pre_ln)
    (a_ref, o_ref, ple_ref, wg_ref, wa_ref, wb_ref, wo_ref, g1_ref, b1_ref,
     wr_ref, wpg_ref, wp_ref, x1_ref, base_ref, route_ref, cnt_ref) = refs

    @pl.when(pl.program_id(0) == 0)
    def _():
        cnt_ref[...] = jnp.zeros_like(cnt_ref)

    xb = x.astype(BF16)
    ya = _dot(a_ref[...], wa_ref[...])
    merged = _sigmoid(_dot(xb, wg_ref[:, 0:D_MODEL])) * ya
    yb = _dot(o_ref[...], wb_ref[...])
    merged = merged + _sigmoid(_dot(xb, wg_ref[:, D_MODEL:2 * D_MODEL])) * yb
    mix = _dot(merged.astype(BF16), wo_ref[...])
    x1 = _layernorm(alpha * x + mix, g1_ref[...], b1_ref[...])
    _to_token_tiles(x1_ref, x1)
    x1b = x1.astype(BF16)
    route_ref[...] = _route(_dot(x1b, wr_ref[...]), cnt_ref)
    ple = _dot(ple_ref[...].astype(BF16), wp_ref[...])
    base_ref[...] = alpha * x1 + _sigmoid(_dot(x1b, wpg_ref[...])) * ple


def _post(xn, a, o, ple, lw, alpha, ln0):
    t = xn.shape[0]
    tm = TOKEN_TILE
    row = lambda i: (i, 0)
    d = D_MODEL
    pre_ln = ln0 is not None
    x_specs = [pl.BlockSpec((tm, d), row)]
    x_args = [xn]
    if pre_ln:
        x_specs += [_const_spec((1, d)), _const_spec((1, d))]
        x_args += [ln0[0].reshape(1, d), ln0[1].reshape(1, d)]
    return pl.pallas_call(
        functools.partial(_post_kernel, alpha=alpha, pre_ln=pre_ln),
        grid=(t // tm,),
        in_specs=x_specs + [
            pl.BlockSpec((tm, 512), row), pl.BlockSpec((tm, HG_IDIM), row),
            pl.BlockSpec((tm, PLE_DIM), row),
            _const_spec((d, 2 * d)), _const_spec((512, d)), _const_spec((HG_IDIM, d)), _const_spec((d, d)),
            _const_spec((1, d)), _const_spec((1, d)),
            _const_spec((d, LANES)), _const_spec((d, d)), _const_spec((PLE_DIM, d)),
        ],
        out_specs=[pl.BlockSpec((tm * TILE_ROWS, LANES), row), pl.BlockSpec((tm, d), row),
                   pl.BlockSpec((tm, LANES), row), pl.BlockSpec((1, LANES), lambda i: (0, 0))],
        out_shape=[jax.ShapeDtypeStruct((t * TILE_ROWS, LANES), F32), jax.ShapeDtypeStruct((t, d), F32),
                   jax.ShapeDtypeStruct((t, LANES), F32), jax.ShapeDtypeStruct((1, LANES), F32)],
        compiler_params=_cparams(("arbitrary",)),
        name="post",
    )(*x_args, a, o, ple, lw["w_gate"], lw["w_br_a"], lw["w_br_b"], lw["w_out"], lw["ln1_g"], lw["ln1_b"],
      lw["w_route"], lw["w_ple_gate"], lw["w_ple"])


TILE_ROWS = D_MODEL // LANES


def _to_token_tiles(ref, x):
    n = x.shape[0]
    for c in range(TILE_ROWS):
        ref[pl.ds(c, n, stride=TILE_ROWS), :] = x[:, c * LANES:(c + 1) * LANES]


def _from_token_tiles(ref, first_token, n):
    return jnp.concatenate(
        [ref[pl.ds(first_token * TILE_ROWS + c, n, stride=TILE_ROWS), :] for c in range(TILE_ROWS)], axis=1)


def _tile_copy(idx_ref, j, src_hbm, dst, dst_token, sem):
    r = pl.multiple_of(idx_ref[0, j], TILE_ROWS)
    d = dst_token * TILE_ROWS
    if not isinstance(d, int):
        d = pl.multiple_of(d, TILE_ROWS)
    return pltpu.make_async_copy(src_hbm.at[pl.ds(r, TILE_ROWS)], dst.at[pl.ds(d, TILE_ROWS)], sem)


def _issue_tiles_loop(idx_ref, src_hbm, dst, sem, n_tokens, dst_off=0):
    def body(j, carry):
        _tile_copy(idx_ref, j, src_hbm, dst, dst_off + j, sem).start()
        return carry

    lax.fori_loop(0, n_tokens, body, 0, unroll=8)


def _issue_tiles_static(idx_ref, src_hbm, dst, sem, lo, hi, dst_off=0):
    for j in range(lo, hi):
        _tile_copy(idx_ref, j, src_hbm, dst, dst_off + j, sem).start(priority=j % 2)


def _wait_tiles(src_hbm, dst, sem):
    pltpu.make_async_copy(src_hbm.at[pl.ds(0, dst.shape[0])], dst, sem).wait()


def _idx_spec(tile, n_blocks, ahead):
    if ahead is None:
        imap = lambda i, *_: (0, 0, 0)
    else:
        imap = lambda i, *_: (jnp.minimum(i + ahead, n_blocks - 1), 0, 0)
    return pl.BlockSpec((None, 1, tile), imap, memory_space=pltpu.SMEM)


MOE_SLOTS = 3


def _moe_kernel(te_ref, idx0_ref, idx1_ref, idx2_ref, x_hbm, w1_ref, w3_ref, w2_ref, y_ref, xbuf, sem):
    i = pl.program_id(0)
    n = pl.num_programs(0)
    tm = MOE_TILE
    slot = i % MOE_SLOTS

    @pl.when(i == 0)
    def _():
        _issue_tiles_loop(idx0_ref, x_hbm, xbuf.at[0], sem.at[0], tm)
        _issue_tiles_loop(idx1_ref, x_hbm, xbuf.at[1], sem.at[1], tm)

    _wait_tiles(x_hbm, xbuf.at[slot], sem.at[slot])
    xb = _from_token_tiles(xbuf.at[slot], 0, tm).astype(BF16)
    ahead = (i + 2) % MOE_SLOTS
    nxt = xbuf.at[ahead]
    nsem = sem.at[ahead]
    group = tm // 4
    hs = []
    half = D_EXPERT // 2
    for c in range(2):
        cs = slice(c * half, (c + 1) * half)
        h1 = _dot(xb, w1_ref[:, cs])
        _issue_tiles_static(idx2_ref, x_hbm, nxt, nsem, 2 * c * group, (2 * c + 1) * group)
        h3 = _dot(xb, w3_ref[:, cs])
        _issue_tiles_static(idx2_ref, x_hbm, nxt, nsem, (2 * c + 1) * group, (2 * c + 2) * group)
        hs.append((h1 * _sigmoid(h1) * h3).astype(BF16))
    _to_token_tiles(y_ref, _dot(jnp.concatenate(hs, axis=1), w2_ref[...]))

    @pl.when(i == n - 1)
    def _():
        _wait_tiles(x_hbm, xbuf.at[(i + 1) % MOE_SLOTS], sem.at[(i + 1) % MOE_SLOTS])
        _wait_tiles(x_hbm, nxt, nsem)


def _moe(x_tiles, src, tile_e, w1, w3, w2):
    tm = MOE_TILE
    n_tiles = src.shape[0]
    d = D_MODEL
    grid_spec = pltpu.PrefetchScalarGridSpec(
        num_scalar_prefetch=1,
        grid=(n_tiles,),
        in_specs=[
            _idx_spec(tm, n_tiles, None), _idx_spec(tm, n_tiles, 1), _idx_spec(tm, n_tiles, 2),
            pl.BlockSpec(memory_space=pl.ANY),
            pl.BlockSpec((None, d, D_EXPERT), lambda i, te: (te[i], 0, 0)),
            pl.BlockSpec((None, d, D_EXPERT), lambda i, te: (te[i], 0, 0)),
            pl.BlockSpec((None, D_EXPERT, d), lambda i, te: (te[i], 0, 0)),
        ],
        out_specs=pl.BlockSpec((tm * TILE_ROWS, LANES), lambda i, te: (i, 0)),
        scratch_shapes=[pltpu.VMEM((MOE_SLOTS, tm * TILE_ROWS, LANES), F32), pltpu.SemaphoreType.DMA((MOE_SLOTS,))],
    )
    src3 = src.reshape(n_tiles, 1, tm)
    return pl.pallas_call(
        _moe_kernel,
        grid_spec=grid_spec,
        out_shape=jax.ShapeDtypeStruct((n_tiles * tm * TILE_ROWS, LANES), F32),
        compiler_params=_cparams(("arbitrary",)),
        name="moe_experts",
    )(tile_e, src3, src3, src3, x_tiles, w1, w3, w2)


def _dispatch_plan(route, cnt, n_tiles):
    t = route.shape[0]
    a = t * TOP_K
    tm = MOE_TILE
    eid = jnp.arange(N_EXPERTS, dtype=jnp.int32)
    counts = cnt[0, 0:N_EXPERTS].astype(jnp.int32)
    tiles_e = (counts + tm - 1) // tm
    tile_end = jnp.cumsum(tiles_e)
    pstart = (tile_end - tiles_e) * tm
    e = route[:, 0:TOP_K].astype(jnp.int32)
    rank = route[:, 4:4 + TOP_K].astype(jnp.int32)
    pos = rank + jnp.sum(jnp.where(e[:, :, None] == eid[None, None, :], pstart[None, None, :], 0), axis=-1)
    pad_e = tiles_e * tm - counts
    cpad = jnp.cumsum(pad_e)
    k = jnp.arange(n_tiles * tm - a, dtype=jnp.int32)
    ke = jnp.sum((cpad[None, :] <= k[:, None]).astype(jnp.int32), axis=1)
    inside = k + jnp.sum(jnp.where(jnp.minimum(ke, N_EXPERTS - 1)[:, None] == eid[None, :],
                                   (pstart + counts - (cpad - pad_e))[None, :], 0), axis=1)
    padpos = jnp.where(ke < N_EXPERTS, inside, tile_end[-1] * tm + (k - cpad[-1]))
    keys = jnp.concatenate([pos.reshape(a), padpos])
    vals = jnp.concatenate([jnp.repeat(jnp.arange(t, dtype=jnp.int32), TOP_K), k % t])
    _, src = lax.sort_key_val(keys, vals)
    jt = jnp.arange(n_tiles, dtype=jnp.int32)
    tile_e = jnp.minimum(jnp.sum((tile_end[None, :] <= jt[:, None]).astype(jnp.int32), axis=1), N_EXPERTS - 1)
    return (src * TILE_ROWS).reshape(n_tiles, tm), pos.reshape(a) * TILE_ROWS, tile_e.astype(jnp.int32)


def _final_kernel(p0c_ref, p1c_ref, p0n_ref, p1n_ref, base_ref, route_ref, g_ref, b_ref, y_hbm, o_ref, ybuf, sem):
    i = pl.program_id(0)
    n = pl.num_programs(0)
    tm = FINAL_TILE
    slot = i % 2

    @pl.when(i == 0)
    def _():
        _issue_tiles_loop(p0c_ref, y_hbm, ybuf.at[0], sem.at[0], tm, 0)
        _issue_tiles_loop(p1c_ref, y_hbm, ybuf.at[0], sem.at[0], tm, tm)

    _wait_tiles(y_hbm, ybuf.at[slot], sem.at[slot])
    nxt = ybuf.at[1 - slot]
    nsem = sem.at[1 - slot]
    _issue_tiles_static(p0n_ref, y_hbm, nxt, nsem, 0, tm, 0)
    _issue_tiles_static(p1n_ref, y_hbm, nxt, nsem, 0, tm, tm)
    w = route_ref[...]
    cur = ybuf.at[slot]
    h = base_ref[...] + w[:, 2:3] * _from_token_tiles(cur, 0, tm) + w[:, 3:4] * _from_token_tiles(cur, tm, tm)
    o_ref[...] = _layernorm(h, g_ref[...], b_ref[...])

    @pl.when(i == n - 1)
    def _():
        _wait_tiles(y_hbm, nxt, nsem)


def _final(base, y_tiles, pos, route, g, b):
    t, d = base.shape
    tm = FINAL_TILE
    nb = t // tm
    row = lambda i: (i, 0)
    pos2 = pos.reshape(t, TOP_K)
    p0 = pos2[:, 0].reshape(nb, 1, tm)
    p1 = pos2[:, 1].reshape(nb, 1, tm)
    first, nxt = _idx_spec(tm, nb, None), _idx_spec(tm, nb, 1)
    return pl.pallas_call(
        _final_kernel,
        grid=(nb,),
        in_specs=[first, first, nxt, nxt,
                  pl.BlockSpec((tm, d), row), pl.BlockSpec((tm, LANES), row),
                  _const_spec((1, d)), _const_spec((1, d)),
                  pl.BlockSpec(memory_space=pl.ANY)],
        out_specs=pl.BlockSpec((tm, d), row),
        out_shape=jax.ShapeDtypeStruct((t, d), F32),
        scratch_shapes=[pltpu.VMEM((2, 2 * tm * TILE_ROWS, LANES), F32), pltpu.SemaphoreType.DMA((2,))],
        compiler_params=_cparams(("arbitrary",)),
        name="final_ln",
    )(p0, p1, p0, p1, base, route, g, b, y_tiles)


def _rot_cols(w):
    half = ROPE_DIM // 2
    return jnp.concatenate([-w[..., half:], w[..., :half]], axis=-1)


def _layer_weights(i, w_in, q_norm, w_uq, kv_norm, w_uk, w_uv, hg_norm, w_br_a, w_br_b, w_out, ln1_g, ln1_b,
                   w_rg, w_re, w_e_gate, w_e_up, w_e_down, w_ple, w_ple_gate, ln2_g, ln2_b):
    d = D_MODEL
    nh = MLA_HEADS
    win = w_in[i]
    c_kr = Q_LORA + KV_LORA
    c_f = c_kr + ROPE_DIM
    c_ga = c_f + N_HG
    w_kr = win[:, c_kr:c_f]
    zpad = jnp.zeros((d, LANES - ROPE_DIM), F32)
    wa = jnp.concatenate([win[:, 0:c_kr], win[:, c_f:c_ga], w_kr, zpad, _rot_cols(w_kr), zpad], axis=1)

    wq3 = w_uq[i].reshape(Q_LORA, nh, NOPE_DIM + ROPE_DIM)
    zq = jnp.zeros((Q_LORA, nh, HEAD_PAD - NOPE_DIM - ROPE_DIM), F32)
    wq = jnp.concatenate([wq3, zq], axis=-1).reshape(Q_LORA, QK_PAD)
    wq_rot = jnp.concatenate([jnp.zeros((Q_LORA, nh, NOPE_DIM), F32), _rot_cols(wq3[..., NOPE_DIM:]), zq],
                             axis=-1).reshape(Q_LORA, QK_PAD)

    wuk = w_uk[i]
    wuv = w_uv[i]
    k_nope = jnp.concatenate([wuk, jnp.zeros((KV_LORA, nh, HEAD_PAD - NOPE_DIM), F32)], axis=-1)
    eye = jnp.eye(ROPE_DIM, dtype=F32)
    k_rope = jnp.concatenate([jnp.zeros((ROPE_DIM, NOPE_DIM), F32), eye,
                              jnp.zeros((ROPE_DIM, HEAD_PAD - NOPE_DIM - ROPE_DIM), F32)], axis=-1)
    k_rope = jnp.concatenate([jnp.tile(k_rope[:, None, :], (1, nh, 1)).reshape(ROPE_DIM, QK_PAD),
                              jnp.zeros((LANES - ROPE_DIM, QK_PAD), F32)], axis=0)
    zv = jnp.zeros((KV_LORA, nh // 2, V_DIM), F32)
    wuv_a3 = jnp.concatenate([wuv[:, 0::2, :], zv], axis=-1)
    wuv_b3 = jnp.concatenate([zv, wuv[:, 1::2, :]], axis=-1)
    wkv_top = jnp.concatenate([k_nope.reshape(KV_LORA, QK_PAD), wuv_a3.reshape(KV_LORA, 512),
                               wuv_b3.reshape(KV_LORA, 512)], axis=1)
    wkv_bot = jnp.concatenate([k_rope, jnp.zeros((LANES, QK_PAD), F32)], axis=1)
    wkv = jnp.concatenate([wkv_top, wkv_bot], axis=0)

    wuk_t = jnp.concatenate([jnp.transpose(wuk, (1, 2, 0)),
                             jnp.zeros((nh, HEAD_PAD - NOPE_DIM, KV_LORA), F32)], axis=1)
    p_sel = jnp.concatenate([jnp.zeros((NOPE_DIM, ROPE_DIM), F32), eye,
                             jnp.zeros((HEAD_PAD - NOPE_DIM - ROPE_DIM, ROPE_DIM), F32)], axis=0)

    w_route = jnp.concatenate([w_re[i], w_rg[i], jnp.zeros((d, LANES - N_EXPERTS - N_GROUPS), F32)], axis=1)
    bf = lambda w: w.astype(BF16)
    return {
        "wa": bf(wa), "q_norm": q_norm[i].reshape(1, -1), "kv_norm": kv_norm[i].reshape(1, -1),
        "wq": bf(wq), "wq_rot": bf(wq_rot), "wkv": bf(wkv),
        "wuk_t": bf(wuk_t), "p_sel": bf(p_sel),
        "wuv_a3": bf(jnp.transpose(wuv_a3, (1, 0, 2))), "wuv_b3": bf(jnp.transpose(wuv_b3, (1, 0, 2))),
        "hg_norm": hg_norm[i].reshape(1, -1),
        "w_gate": bf(win[:, c_ga:c_ga + 2 * d]), "w_br_a": bf(w_br_a[i]), "w_br_b": bf(w_br_b[i]),
        "w_out": bf(w_out[i]), "ln1_g": ln1_g[i].reshape(1, -1), "ln1_b": ln1_b[i].reshape(1, -1),
        "w_route": bf(w_route), "w_ple_gate": bf(w_ple_gate[i]), "w_ple": bf(w_ple[i]),
        "w1": bf(w_e_gate[i]), "w3": bf(w_e_up[i]), "w2": bf(w_e_down[i]),
        "ln2_g": ln2_g[i].reshape(1, -1), "ln2_b": ln2_b[i].reshape(1, -1),
    }


def _rope_table(pos, rows):
    half = ROPE_DIM // 2
    inv = ROPE_THETA ** (-jnp.arange(half, dtype=F32) / half)
    ang = pos.astype(F32)[:, None] * inv[None, :]
    c2 = jnp.concatenate([jnp.cos(ang), jnp.cos(ang)], axis=1)
    s2 = jnp.concatenate([jnp.sin(ang), jnp.sin(ang)], axis=1)
    n = pos.shape[0]
    ones = jnp.ones((n, NOPE_DIM), F32)
    z = lambda w: jnp.zeros((n, w), F32)
    tab = jnp.concatenate([
        ones, c2, z(HEAD_PAD - NOPE_DIM - ROPE_DIM),
        z(NOPE_DIM), s2, z(HEAD_PAD - NOPE_DIM - ROPE_DIM),
        c2, z(LANES - ROPE_DIM),
        s2, z(LANES - ROPE_DIM)], axis=1)
    if rows > n:
        tab = jnp.tile(tab, (rows // n, 1))
    return tab


def _trunk_layer(i, xn, ple, tab, lw, hg_lb, batch, ln, cache, alpha, ln0):
    q, k, va, vb, ckv, kr, f_pre, q_pre, hv, g_pre = _proj(xn, tab, lw, ln0)
    if cache is None:
        a = _attn_prompt(q, k, va, vb, batch, ln)
        o, st = _hgrn(f_pre, q_pre, hv, g_pre, hg_lb, lw["hg_norm"], None, i, batch, ln)
    else:
        a = _attn_sample(q, ckv, kr, cache[0], cache[1], lw, batch, ln)
        o, st = _hgrn(f_pre, q_pre, hv, g_pre, hg_lb, lw["hg_norm"], cache[2], i, batch, ln)
    x1, base, route, cnt = _post(xn, a, o, ple, lw, alpha, ln0)

    t = xn.shape[0]
    n_tiles = (t * TOP_K) // MOE_TILE + N_EXPERTS
    src, pos, tile_e = _dispatch_plan(route, cnt, n_tiles)
    y = _moe(x1, src, tile_e, lw["w1"], lw["w3"], lw["w2"])
    x2 = _final(base, y, pos, route, lw["ln2_g"], lw["ln2_b"])
    return x2, ckv, kr, st


def kernel(x_prompt, x_sample, p_prompt, p_sample, cache_ckv, cache_krope, state_hgrn, ln0_g, ln0_b, w_in, q_norm, w_uq, kv_norm, w_uk, w_uv, hg_lb, hg_norm, w_br_a, w_br_b, w_out, ln1_g, ln1_b, w_router_group, w_router_expert, w_e_gate, w_e_up, w_e_down, w_ple, w_ple_gate, ln2_g, ln2_b):
    depth = w_in.shape[0]
    bp, sp, d = x_prompt.shape
    bs, ss, _ = x_sample.shape
    past = cache_ckv.shape[2]
    alpha = (2 * depth) ** 0.25

    xp = x_prompt.reshape(bp * sp, d)
    xs = x_sample.reshape(bs * ss, d)
    tab_p = _rope_table(jnp.arange(sp, dtype=jnp.int32), max(sp, TOKEN_TILE))
    tab_s = _rope_table(past + jnp.arange(ss, dtype=jnp.int32), max(ss, TOKEN_TILE))

    outs = [[] for _ in range(6)]
    for i in range(depth):
        lw = _layer_weights(i, w_in, q_norm, w_uq, kv_norm, w_uk, w_uv, hg_norm, w_br_a, w_br_b, w_out,
                            ln1_g, ln1_b, w_router_group, w_router_expert, w_e_gate, w_e_up, w_e_down,
                            w_ple, w_ple_gate, ln2_g, ln2_b)
        ln0 = (ln0_g, ln0_b) if i == 0 else None
        xp, c, r, s = _trunk_layer(i, xp, p_prompt[i].reshape(bp * sp, -1), tab_p, lw, hg_lb, bp, sp, None, alpha, ln0)
        outs[0].append(c.reshape(bp, sp, -1))
        outs[1].append(r.reshape(bp, sp, -1))
        outs[2].append(s)
        cache = (cache_ckv[i], cache_krope[i], state_hgrn[i])
        xs, c, r, s = _trunk_layer(i, xs, p_sample[i].reshape(bs * ss, -1), tab_s, lw, hg_lb, bs, ss, cache, alpha, ln0)
        outs[3].append(c.reshape(bs, ss, -1))
        outs[4].append(r.reshape(bs, ss, -1))
        outs[5].append(s)
    return (xp.reshape(bp, sp, d), xs.reshape(bs, ss, d)) + tuple(jnp.stack(o) for o in outs)
```

```python
import functools

import numpy as np
import jax
import jax.numpy as jnp
from jax import lax
from jax.experimental import pallas as pl
from jax.experimental.pallas import tpu as pltpu

F32 = jnp.float32
BF16 = jnp.bfloat16

D_MODEL = 1024
CHUNK = 64
PLE_DIM = 256
MLA_HEADS = 8
NOPE_DIM = 64
ROPE_DIM = 32
V_DIM = 64
Q_LORA = 384
KV_LORA = 256
ROPE_THETA = 10000.0
ATTN_SCALE = (NOPE_DIM + ROPE_DIM) ** -0.5
HG_HEADS = 4
HG_DK = 128
HG_DV = 128
HG_FDIM = HG_HEADS * HG_DK
HG_IDIM = HG_HEADS * HG_DV
N_GROUPS = 4
EXPERTS_PER_GROUP = 8
N_EXPERTS = N_GROUPS * EXPERTS_PER_GROUP
TOP_K = 2
D_EXPERT = 512
EPS = 1e-6

LANES = 128
HEAD_PAD = LANES
QK_PAD = MLA_HEADS * HEAD_PAD
VMEM_LIMIT = 56 * 1024 * 1024

TOKEN_TILE = 512
ATTN_TILE = 256
MOE_TILE = 512
HG_CHUNK = 128
HG_SEQS_PER_STEP = 4
FINAL_TILE = 256


def _cparams(sem):
    return pltpu.CompilerParams(dimension_semantics=sem, vmem_limit_bytes=VMEM_LIMIT)


def _const_spec(shape):
    nd = len(shape)
    return pl.BlockSpec(shape, lambda *_: (0,) * nd, pipeline_mode=pl.Buffered(1))


def _dot(a, b):
    return jnp.dot(a, b, preferred_element_type=F32)


def _dot_nt(a, b):
    return lax.dot_general(a, b, (((1,), (1,)), ((), ())), preferred_element_type=F32)


def _dot_tn(a, b):
    return lax.dot_general(a, b, (((0,), (0,)), ((), ())), preferred_element_type=F32)


def _div_pow2(x, d):
    return jnp.right_shift(x, int(d).bit_length() - 1)


def _sigmoid(x):
    return 1.0 / (1.0 + jnp.exp(-x))


def _layernorm(x, g, b):
    mu = jnp.mean(x, axis=-1, keepdims=True)
    xc = x - mu
    var = jnp.mean(xc * xc, axis=-1, keepdims=True)
    return xc * lax.rsqrt(var + EPS) * g + b


def _rmsnorm(x, g):
    return x * lax.rsqrt(jnp.mean(x * x, axis=-1, keepdims=True) + EPS) * g


N_LAT = Q_LORA + KV_LORA
N_HG = 2 * HG_FDIM + 2 * HG_IDIM
COL_HG = N_LAT
COL_KR = N_LAT + N_HG
N_PROJ = COL_KR + 2 * LANES


def _input_rows(refs, pre_ln):
    if pre_ln:
        x_ref, g_ref, b_ref = refs[:3]
        return _layernorm(x_ref[...], g_ref[...], b_ref[...]), refs[3:]
    return refs[0][...], refs[1:]


def _proj_kernel(*refs, pre_ln):
    x, refs = _input_rows(refs, pre_ln)
    (tab_ref, wa_ref, qn_ref, kvn_ref, wq_ref, wqr_ref, wkv_ref,
     q_ref, k_ref, va_ref, vb_ref, ckv_ref, kr_ref, f_ref, hq_ref, hv_ref, hg_ref) = refs
    xb = x.astype(BF16)
    cq_tab = tab_ref[:, 0:LANES]
    sq_tab = tab_ref[:, LANES:2 * LANES]
    ck_tab = tab_ref[:, 2 * LANES:3 * LANES]
    sk_tab = tab_ref[:, 3 * LANES:4 * LANES]

    lat = _dot(xb, wa_ref[:, 0:N_LAT])
    cqn = _rmsnorm(lat[:, 0:Q_LORA], qn_ref[...]).astype(BF16)
    ckvn = _rmsnorm(lat[:, Q_LORA:N_LAT], kvn_ref[...])
    ckv_ref[...] = ckvn

    qa = _dot(cqn, wq_ref[...])
    qb = _dot(cqn, wqr_ref[...])
    for h in range(MLA_HEADS):
        sl = slice(h * HEAD_PAD, (h + 1) * HEAD_PAD)
        q_ref[:, sl] = (qa[:, sl] * cq_tab + qb[:, sl] * sq_tab).astype(BF16)

    kr2 = _dot(xb, wa_ref[:, COL_KR:N_PROJ])
    kr = kr2[:, 0:LANES] * ck_tab + kr2[:, LANES:2 * LANES] * sk_tab
    kr_ref[...] = kr[:, 0:ROPE_DIM]

    kvin = jnp.concatenate([ckvn.astype(BF16), kr.astype(BF16)], axis=1)
    kv = _dot(kvin, wkv_ref[...])
    k_ref[...] = kv[:, 0:QK_PAD].astype(BF16)
    va_ref[...] = kv[:, QK_PAD:QK_PAD + 512].astype(BF16)
    vb_ref[...] = kv[:, QK_PAD + 512:QK_PAD + 1024].astype(BF16)

    f_ref[...] = _dot(xb, wa_ref[:, COL_HG:COL_HG + 512])
    hq_ref[...] = _dot(xb, wa_ref[:, COL_HG + 512:COL_HG + 1024])
    hv_ref[...] = _dot(xb, wa_ref[:, COL_HG + 1024:COL_HG + 1536]).astype(BF16)
    hg_ref[...] = _dot(xb, wa_ref[:, COL_HG + 1536:COL_HG + 2048])


def _proj(xn, tab, lw, ln0):
    t = xn.shape[0]
    pre_ln = ln0 is not None
    x_specs = [pl.BlockSpec((TOKEN_TILE, D_MODEL), lambda i: (i, 0))]
    x_args = [xn]
    if pre_ln:
        x_specs += [_const_spec((1, D_MODEL)), _const_spec((1, D_MODEL))]
        x_args += [ln0[0].reshape(1, D_MODEL), ln0[1].reshape(1, D_MODEL)]
    tm = TOKEN_TILE
    ntab = tab.shape[0] // tm
    row = lambda i: (i, 0)
    outs = [
        (QK_PAD, BF16), (QK_PAD, BF16), (512, BF16), (512, BF16), (KV_LORA, F32), (ROPE_DIM, F32),
        (HG_FDIM, F32), (HG_FDIM, F32), (HG_IDIM, BF16), (HG_IDIM, F32),
    ]
    return pl.pallas_call(
        functools.partial(_proj_kernel, pre_ln=pre_ln),
        grid=(t // tm,),
        in_specs=x_specs + [
            pl.BlockSpec((tm, 4 * LANES), lambda i: (i % ntab, 0)),
            _const_spec((D_MODEL, N_PROJ)),
            _const_spec((1, Q_LORA)), _const_spec((1, KV_LORA)),
            _const_spec((Q_LORA, QK_PAD)), _const_spec((Q_LORA, QK_PAD)),
            _const_spec((KV_LORA + LANES, 2 * QK_PAD)),
        ],
        out_specs=[pl.BlockSpec((tm, w), row) for w, _ in outs],
        out_shape=[jax.ShapeDtypeStruct((t, w), dt) for w, dt in outs],
        compiler_params=_cparams(("parallel",)),
        name="proj",
    )(*x_args, tab, lw["wa"], lw["q_norm"], lw["kv_norm"], lw["wq"], lw["wq_rot"], lw["wkv"])


LOG2E = 1.4426950408889634


def _attn_prompt_kernel(q_ref, k_ref, va_ref, vb_ref, o_ref, s_ref, m_ref, l_ref, acc_ref):
    tq = ATTN_TILE
    nh = MLA_HEADS
    qi = pl.program_id(1)
    rq = _div_pow2(lax.broadcasted_iota(jnp.int32, (tq, tq), 0), CHUNK)
    ck = _div_pow2(lax.broadcasted_iota(jnp.int32, (tq, tq), 1), CHUNK)
    diag_mask = ck <= rq
    c = ATTN_SCALE * LOG2E

    m_ref[...] = jnp.full(m_ref.shape, -jnp.inf, F32)
    l_ref[...] = jnp.zeros(l_ref.shape, F32)
    acc_ref[...] = jnp.zeros(acc_ref.shape, F32)

    def scores(kt, mask):
        ks = pl.multiple_of(kt * tq, tq)
        for h in range(nh):
            hs = slice(h * HEAD_PAD, (h + 1) * HEAD_PAD)
            s = _dot_nt(q_ref[:, hs], k_ref[pl.ds(ks, tq), hs]) * c
            if mask is not None:
                s = jnp.where(mask, s, -jnp.inf)
            s_ref[h, :, pl.ds(ks, tq)] = s
            m_ref[h] = jnp.maximum(m_ref[h], jnp.maximum(s[:, 0:LANES], s[:, LANES:2 * LANES]))

    def p1(kt, carry):
        scores(kt, None)
        return carry

    lax.fori_loop(0, qi, p1, 0)
    scores(qi, diag_mask)
    for h in range(nh):
        m_ref[h] = jnp.broadcast_to(jnp.max(m_ref[h], axis=-1, keepdims=True), (tq, LANES))

    def p2(kt, carry):
        ks = pl.multiple_of(kt * tq, tq)
        for h in range(nh):
            v_ref = va_ref if h % 2 == 0 else vb_ref
            pair = h // 2
            mb = m_ref[h]
            p_lo = jnp.exp2(s_ref[h, :, pl.ds(ks, LANES)] - mb)
            p_hi = jnp.exp2(s_ref[h, :, pl.ds(ks + LANES, LANES)] - mb)
            l_ref[h] += p_lo + p_hi
            p = jnp.concatenate([p_lo, p_hi], axis=1).astype(BF16)
            acc_ref[h] += _dot(p, v_ref[pl.ds(ks, tq), pair * LANES:(pair + 1) * LANES])
        return carry

    lax.fori_loop(0, qi + 1, p2, 0)
    for pair in range(nh // 2):
        he, ho = 2 * pair, 2 * pair + 1
        o = (acc_ref[he] / jnp.sum(l_ref[he], axis=-1, keepdims=True)
             + acc_ref[ho] / jnp.sum(l_ref[ho], axis=-1, keepdims=True))
        o_ref[:, pair * LANES:(pair + 1) * LANES] = o.astype(BF16)


def _attn_prompt(q, k, va, vb, batch, seq):
    tq = ATTN_TILE
    nq = seq // tq
    nh = MLA_HEADS
    return pl.pallas_call(
        _attn_prompt_kernel,
        grid=(batch, nq),
        in_specs=[
            pl.BlockSpec((tq, QK_PAD), lambda b, i: (b * nq + i, 0)),
            pl.BlockSpec((seq, QK_PAD), lambda b, i: (b, 0)),
            pl.BlockSpec((seq, 512), lambda b, i: (b, 0)),
            pl.BlockSpec((seq, 512), lambda b, i: (b, 0)),
        ],
        out_specs=pl.BlockSpec((tq, 512), lambda b, i: (b * nq + i, 0)),
        out_shape=jax.ShapeDtypeStruct((batch * seq, 512), BF16),
        scratch_shapes=[pltpu.VMEM((nh, tq, seq), F32), pltpu.VMEM((nh, tq, LANES), F32),
                        pltpu.VMEM((nh, tq, LANES), F32), pltpu.VMEM((nh, tq, LANES), F32)],
        compiler_params=_cparams(("parallel", "arbitrary")),
        name="attn_prompt",
    )(q, k, va, vb)


SAMPLE_KEY_TILE = 1024


def _attn_sample_kernel(q_ref, ckvn_ref, krn_ref, cc_ref, ck_ref, wukt_ref, psel_ref, wva_ref, wvb_ref,
                        o_ref, s_ref):
    ln = q_ref.shape[0]
    past = cc_ref.shape[0]
    nh = MLA_HEADS
    qlat, qrope = [], []
    for h in range(nh):
        qh = q_ref[:, h * HEAD_PAD:(h + 1) * HEAD_PAD]
        qlat.append(_dot(qh, wukt_ref[h]).astype(BF16))
        qrope.append(_dot(qh, psel_ref[...]).astype(BF16))
    qlat = jnp.concatenate(qlat, axis=0)
    qrope = jnp.concatenate(qrope, axis=0)

    ckv_new = ckvn_ref[...].astype(BF16)
    kr_new = krn_ref[...].astype(BF16)
    s_new = (_dot_nt(qlat, ckv_new) + _dot_nt(qrope, kr_new)) * ATTN_SCALE
    m = jnp.max(s_new, axis=-1, keepdims=True)
    tk = SAMPLE_KEY_TILE
    for kt in range(past // tk):
        rows = slice(kt * tk, (kt + 1) * tk)
        s = (_dot_nt(qlat, cc_ref[rows, :].astype(BF16)) + _dot_nt(qrope, ck_ref[rows, :].astype(BF16))) * ATTN_SCALE
        s_ref[:, rows] = s
        m = jnp.maximum(m, jnp.max(s, axis=-1, keepdims=True))
    p_new = jnp.exp(s_new - m)
    l = jnp.sum(p_new, axis=-1, keepdims=True)
    lat = _dot(p_new.astype(BF16), ckv_new)
    for kt in range(past // tk):
        rows = slice(kt * tk, (kt + 1) * tk)
        p = jnp.exp(s_ref[:, rows] - m)
        l = l + jnp.sum(p, axis=-1, keepdims=True)
        lat = lat + _dot(p.astype(BF16), cc_ref[rows, :].astype(BF16))
    lat = (lat / l).astype(BF16)
    for pair in range(nh // 2):
        he, ho = 2 * pair, 2 * pair + 1
        o = _dot(lat[he * ln:(he + 1) * ln], wva_ref[pair]) + _dot(lat[ho * ln:(ho + 1) * ln], wvb_ref[pair])
        o_ref[:, pair * LANES:(pair + 1) * LANES] = o.astype(BF16)


def _attn_sample(q, ckv_new, kr_new, cache_ckv, cache_kr, lw, batch, ln):
    past = cache_ckv.shape[1]
    nh = MLA_HEADS
    return pl.pallas_call(
        _attn_sample_kernel,
        grid=(batch,),
        in_specs=[
            pl.BlockSpec((ln, QK_PAD), lambda b: (b, 0)),
            pl.BlockSpec((ln, KV_LORA), lambda b: (b, 0)),
            pl.BlockSpec((ln, ROPE_DIM), lambda b: (b, 0)),
            pl.BlockSpec((None, past, KV_LORA), lambda b: (b, 0, 0)),
            pl.BlockSpec((None, past, ROPE_DIM), lambda b: (b, 0, 0)),
            _const_spec((nh, HEAD_PAD, KV_LORA)),
            _const_spec((HEAD_PAD, ROPE_DIM)),
            _const_spec((nh // 2, KV_LORA, LANES)),
            _const_spec((nh // 2, KV_LORA, LANES)),
        ],
        out_specs=pl.BlockSpec((ln, 512), lambda b: (b, 0)),
        out_shape=jax.ShapeDtypeStruct((batch * ln, 512), BF16),
        scratch_shapes=[pltpu.VMEM((nh * ln, past), F32)],
        compiler_params=_cparams(("parallel",)),
        name="attn_sample",
    )(q, ckv_new, kr_new, cache_ckv, cache_kr, lw["wuk_t"], lw["p_sel"], lw["wuv_a3"], lw["wuv_b3"])


def _hgrn_kernel(*refs, layer, chunk, has_init):
    if has_init:
        f_ref, q_ref, v_ref, g_ref, lb_ref, nrm_ref, s0_ref, o_ref, sout_ref, st_ref = refs
    else:
        f_ref, q_ref, v_ref, g_ref, lb_ref, nrm_ref, o_ref, sout_ref, st_ref = refs
        s0_ref = None
    nseq, tt = f_ref.shape[0], f_ref.shape[1]
    c = chunk
    ti = pl.program_id(1)

    @pl.when(ti == 0)
    def _():
        for sq in range(nseq):
            for h in range(HG_HEADS):
                if has_init:
                    st_ref[sq, h] = s0_ref[sq, h].T
                else:
                    st_ref[sq, h] = jnp.zeros((HG_DV, HG_DK), F32)

    lbp = lb_ref[...]
    e = jnp.exp(lbp - jnp.max(lbp, axis=0, keepdims=True))
    tot = jnp.sum(e, axis=0, keepdims=True)
    part = jnp.zeros_like(tot)
    for j in range(1, layer + 1):
        part = part + e[j:j + 1]
    lb = part / tot
    log_lb = jnp.log(lb)
    log_1m = jnp.log1p(-lb)
    one_m = 1.0 - lb

    rid = lax.broadcasted_iota(jnp.int32, (c, 1), 0)
    rr = lax.broadcasted_iota(jnp.int32, (c, c), 0)
    cc = lax.broadcasted_iota(jnp.int32, (c, c), 1)
    levels = [c >> k for k in range(c.bit_length() - 1)]
    sums = [cc <= rr, cc > rr]
    for g in levels:
        ref_row = jnp.bitwise_and(rr, -g) + (g // 2 - 1)
        is_right = jnp.bitwise_and(rr, g - 1) >= g // 2
        lo_row = jnp.where(is_right, ref_row, rr)
        hi_row = jnp.where(is_right, rr, ref_row)
        sums.append(jnp.logical_and(cc > lo_row, cc <= hi_row))
    sum_mat = jnp.concatenate([m.astype(BF16) for m in sums], axis=0)
    pair_masks = [None] + [_div_pow2(rr, g) == _div_pow2(cc, g) for g in levels[1:]] + [rr == cc]

    for ci, sq in [(ci, sq) for ci in range(tt // c) for sq in range(nseq)]:
        rows = slice(ci * c, (ci + 1) * c)
        z = f_ref[sq, rows, :]
        ez = jnp.exp(-jnp.abs(z))
        rz = 1.0 / (1.0 + ez)
        log_sig = jnp.minimum(z, 0.0) - jnp.log1p(ez)
        kk = one_m * jnp.where(z >= 0, ez * rz, rz)
        y = log_1m + log_sig
        logf = jnp.maximum(log_lb, y) + jnp.log1p(jnp.exp(-jnp.abs(log_lb - y)))
        qp = q_ref[sq, rows, :]
        qq = qp * _sigmoid(qp)

        nl = -logf
        hi = nl.astype(BF16)
        lo = (nl - hi.astype(F32)).astype(BF16)
        acc = _dot(sum_mat, hi) + _dot(sum_mat, lo)
        q_in = (qq * jnp.exp(-acc[0:c])).astype(BF16)
        dec = jnp.exp(-acc[c - 1:c])
        k_out = (kk * jnp.exp(-acc[c:2 * c])).astype(BF16)
        qs, ks = [], []
        for li, g in enumerate(levels):
            x = jnp.exp(-acc[(2 + li) * c:(3 + li) * c])
            right = jnp.bitwise_and(rid, g - 1) >= g // 2
            qs.append(jnp.where(right, qq * x, 0.0).astype(BF16))
            ks.append(jnp.where(right, 0.0, kk * x).astype(BF16))
        qs.append(qq.astype(BF16))
        ks.append(kk.astype(BF16))

        gp = g_ref[sq, rows, :]
        gate = gp * _sigmoid(gp) * nrm_ref[...]

        for h in range(HG_HEADS):
            hs = slice(h * HG_DK, (h + 1) * HG_DK)
            a = None
            for ql, kl, mk in zip(qs, ks, pair_masks):
                al = _dot_nt(ql[:, hs], kl[:, hs])
                if mk is not None:
                    al = jnp.where(mk, al, 0.0)
                a = al if a is None else a + al
            vh = v_ref[sq, rows, hs]
            st = st_ref[sq, h]
            o = _dot_nt(q_in[:, hs], st.astype(BF16)) + _dot(a.astype(BF16), vh)
            o = o * lax.rsqrt(jnp.mean(o * o, axis=-1, keepdims=True) + EPS) * gate[:, hs]
            o_ref[sq, rows, hs] = o.astype(BF16)
            st_ref[sq, h] = st * dec[:, hs] + _dot_tn(vh, k_out[:, hs])

    @pl.when(ti == pl.num_programs(1) - 1)
    def _():
        for sq in range(nseq):
            for h in range(HG_HEADS):
                sout_ref[sq, h] = st_ref[sq, h].T


def _hgrn(f_pre, q_pre, v, g_pre, hg_lb, hg_norm, s0, layer, batch, ln):
    tt = min(ln, 256)
    chunk = min(tt, HG_CHUNK)
    nt = ln // tt
    nseq = HG_SEQS_PER_STEP
    has_init = s0 is not None
    depth = hg_lb.shape[0]
    blk = lambda b, t: (b, t, 0)
    st_spec = pl.BlockSpec((nseq, HG_HEADS, HG_DK, HG_DV), lambda b, t: (b, 0, 0, 0))
    in_specs = [
        pl.BlockSpec((nseq, tt, HG_FDIM), blk), pl.BlockSpec((nseq, tt, HG_FDIM), blk),
        pl.BlockSpec((nseq, tt, HG_IDIM), blk), pl.BlockSpec((nseq, tt, HG_IDIM), blk),
        _const_spec((depth, HG_FDIM)), _const_spec((1, HG_IDIM)),
    ]
    args = [x.reshape(batch, ln, x.shape[-1]) for x in (f_pre, q_pre, v, g_pre)] + [hg_lb, hg_norm]
    if has_init:
        in_specs.append(st_spec)
        args.append(s0)
    o, st = pl.pallas_call(
        functools.partial(_hgrn_kernel, layer=layer, chunk=chunk, has_init=has_init),
        grid=(batch // nseq, nt),
        in_specs=in_specs,
        out_specs=[pl.BlockSpec((nseq, tt, HG_IDIM), blk), st_spec],
        out_shape=[jax.ShapeDtypeStruct((batch, ln, HG_IDIM), BF16),
                   jax.ShapeDtypeStruct((batch, HG_HEADS, HG_DK, HG_DV), F32)],
        scratch_shapes=[pltpu.VMEM((nseq, HG_HEADS, HG_DV, HG_DK), F32)],
        compiler_params=_cparams(("parallel", "arbitrary")),
        name="hgrn",
    )(*args)
    return o.reshape(batch * ln, HG_IDIM), st


ROUTE_GROUP_LANE = N_EXPERTS


def _route(logits, cnt_ref):
    lane_i = lax.broadcasted_iota(jnp.int32, logits.shape, 1)
    lane = lane_i.astype(F32)
    big = float(LANES)
    neg = -jnp.inf
    is_g = jnp.logical_and(lane_i >= ROUTE_GROUP_LANE, lane_i < ROUTE_GROUP_LANE + N_GROUPS)
    gl = jnp.where(is_g, logits, neg)
    gmax = jnp.max(gl, axis=-1, keepdims=True)
    gidx = jnp.min(jnp.where(gl == gmax, lane - ROUTE_GROUP_LANE, big), axis=-1, keepdims=True)
    g_top = 1.0 / jnp.sum(jnp.exp(gl - gmax), axis=-1, keepdims=True)
    lane_group = _div_pow2(lane_i, EXPERTS_PER_GROUP).astype(F32)
    in_group = jnp.logical_and(lane_i < N_EXPERTS, lane_group == gidx)
    el = jnp.where(in_group, logits, neg)
    m1 = jnp.max(el, axis=-1, keepdims=True)
    i1 = jnp.min(jnp.where(el == m1, lane, big), axis=-1, keepdims=True)
    el2 = jnp.where(lane == i1, neg, el)
    m2 = jnp.max(el2, axis=-1, keepdims=True)
    i2 = jnp.min(jnp.where(el2 == m2, lane, big), axis=-1, keepdims=True)
    r = jnp.exp(m2 - m1)
    w1 = g_top / (1.0 + r)
    w2 = g_top * r / (1.0 + r)
    tm = logits.shape[0]
    hit = jnp.logical_or(lane == i1, lane == i2)
    rr = lax.broadcasted_iota(jnp.int32, (tm, tm), 0)
    cc = lax.broadcasted_iota(jnp.int32, (tm, tm), 1)
    before = _dot((cc < rr).astype(BF16), jnp.where(hit, 1.0, 0.0).astype(BF16)) + cnt_ref[...]
    r1 = jnp.sum(jnp.where(lane == i1, before, 0.0), axis=-1, keepdims=True)
    r2 = jnp.sum(jnp.where(lane == i2, before, 0.0), axis=-1, keepdims=True)
    cnt_ref[...] += jnp.sum(jnp.where(hit, 1.0, 0.0), axis=0, keepdims=True)
    out = jnp.where(lane_i == 0, i1, 0.0)
    out = jnp.where(lane_i == 1, i2, out)
    out = jnp.where(lane_i == 2, w1, out)
    out = jnp.where(lane_i == 3, w2, out)
    out = jnp.where(lane_i == 4, r1, out)
    out = jnp.where(lane_i == 5, r2, out)
    return out


def _post_kernel(*refs, alpha, pre_ln):
    x, refs = _input_rows(refs, pre_ln)
    (a_ref, o_ref, ple_ref, wg_ref, wa_ref, wb_ref, wo_ref, g1_ref, b1_ref,
     wr_ref, wpg_ref, wp_ref, x1_ref, base_ref, route_ref, cnt_ref) = refs

    @pl.when(pl.program_id(0) == 0)
    def _():
        cnt_ref[...] = jnp.zeros_like(cnt_ref)

    xb = x.astype(BF16)
    ya = _dot(a_ref[...], wa_ref[...])
    merged = _sigmoid(_dot(xb, wg_ref[:, 0:D_MODEL])) * ya
    yb = _dot(o_ref[...], wb_ref[...])
    merged = merged + _sigmoid(_dot(xb, wg_ref[:, D_MODEL:2 * D_MODEL])) * yb
    mix = _dot(merged.astype(BF16), wo_ref[...])
    x1 = _layernorm(alpha * x + mix, g1_ref[...], b1_ref[...])
    _to_token_tiles(x1_ref, x1)
    x1b = x1.astype(BF16)
    route_ref[...] = _route(_dot(x1b, wr_ref[...]), cnt_ref)
    ple = _dot(ple_ref[...].astype(BF16), wp_ref[...])
    base_ref[...] = alpha * x1 + _sigmoid(_dot(x1b, wpg_ref[...])) * ple


def _post(xn, a, o, ple, lw, alpha, ln0):
    t = xn.shape[0]
    tm = TOKEN_TILE
    row = lambda i: (i, 0)
    d = D_MODEL
    pre_ln = ln0 is not None
    x_specs = [pl.BlockSpec((tm, d), row)]
    x_args = [xn]
    if pre_ln:
        x_specs += [_const_spec((1, d)), _const_spec((1, d))]
        x_args += [ln0[0].reshape(1, d), ln0[1].reshape(1, d)]
    return pl.pallas_call(
        functools.partial(_post_kernel, alpha=alpha, pre_ln=pre_ln),
        grid=(t // tm,),
        in_specs=x_specs + [
            pl.BlockSpec((tm, 512), row), pl.BlockSpec((tm, HG_IDIM), row),
            pl.BlockSpec((tm, PLE_DIM), row),
            _const_spec((d, 2 * d)), _const_spec((512, d)), _const_spec((HG_IDIM, d)), _const_spec((d, d)),
            _const_spec((1, d)), _const_spec((1, d)),
            _const_spec((d, LANES)), _const_spec((d, d)), _const_spec((PLE_DIM, d)),
        ],
        out_specs=[pl.BlockSpec((tm * TILE_ROWS, LANES), row), pl.BlockSpec((tm, d), row),
                   pl.BlockSpec((tm, LANES), row), pl.BlockSpec((1, LANES), lambda i: (0, 0))],
        out_shape=[jax.ShapeDtypeStruct((t * TILE_ROWS, LANES), F32), jax.ShapeDtypeStruct((t, d), F32),
                   jax.ShapeDtypeStruct((t, LANES), F32), jax.ShapeDtypeStruct((1, LANES), F32)],
        compiler_params=_cparams(("arbitrary",)),
        name="post",
    )(*x_args, a, o, ple, lw["w_gate"], lw["w_br_a"], lw["w_br_b"], lw["w_out"], lw["ln1_g"], lw["ln1_b"],
      lw["w_route"], lw["w_ple_gate"], lw["w_ple"])


TILE_ROWS = D_MODEL // LANES


def _to_token_tiles(ref, x):
    n = x.shape[0]
    for c in range(TILE_ROWS):
        ref[pl.ds(c, n, stride=TILE_ROWS), :] = x[:, c * LANES:(c + 1) * LANES]


def _from_token_tiles(ref, first_token, n):
    return jnp.concatenate(
        [ref[pl.ds(first_token * TILE_ROWS + c, n, stride=TILE_ROWS), :] for c in range(TILE_ROWS)], axis=1)


def _tile_copy(idx_ref, j, src_hbm, dst, dst_token, sem):
    r = pl.multiple_of(idx_ref[0, j], TILE_ROWS)
    d = dst_token * TILE_ROWS
    if not isinstance(d, int):
        d = pl.multiple_of(d, TILE_ROWS)
    return pltpu.make_async_copy(src_hbm.at[pl.ds(r, TILE_ROWS)], dst.at[pl.ds(d, TILE_ROWS)], sem)


def _issue_tiles_loop(idx_ref, src_hbm, dst, sem, n_tokens, dst_off=0):
    def body(j, carry):
        _tile_copy(idx_ref, j, src_hbm, dst, dst_off + j, sem).start()
        return carry

    lax.fori_loop(0, n_tokens, body, 0, unroll=8)


def _issue_tiles_static(idx_ref, src_hbm, dst, sem, lo, hi, dst_off=0):
    for j in range(lo, hi):
        _tile_copy(idx_ref, j, src_hbm, dst, dst_off + j, sem).start(priority=j % 2)


def _wait_tiles(src_hbm, dst, sem):
    pltpu.make_async_copy(src_hbm.at[pl.ds(0, dst.shape[0])], dst, sem).wait()


def _idx_spec(tile, n_blocks, ahead):
    if ahead is None:
        imap = lambda i, *_: (0, 0, 0)
    else:
        imap = lambda i, *_: (jnp.minimum(i + ahead, n_blocks - 1), 0, 0)
    return pl.BlockSpec((None, 1, tile), imap, memory_space=pltpu.SMEM)


MOE_SLOTS = 3


def _moe_kernel(te_ref, idx0_ref, idx1_ref, idx2_ref, x_hbm, w1_ref, w3_ref, w2_ref, y_ref, xbuf, sem):
    i = pl.program_id(0)
    n = pl.num_programs(0)
    tm = MOE_TILE
    slot = i % MOE_SLOTS

    @pl.when(i == 0)
    def _():
        _issue_tiles_loop(idx0_ref, x_hbm, xbuf.at[0], sem.at[0], tm)
        _issue_tiles_loop(idx1_ref, x_hbm, xbuf.at[1], sem.at[1], tm)

    _wait_tiles(x_hbm, xbuf.at[slot], sem.at[slot])
    xb = _from_token_tiles(xbuf.at[slot], 0, tm).astype(BF16)
    ahead = (i + 2) % MOE_SLOTS
    nxt = xbuf.at[ahead]
    nsem = sem.at[ahead]
    group = tm // 4
    hs = []
    half = D_EXPERT // 2
    for c in range(2):
        cs = slice(c * half, (c + 1) * half)
        h1 = _dot(xb, w1_ref[:, cs])
        _issue_tiles_static(idx2_ref, x_hbm, nxt, nsem, 2 * c * group, (2 * c + 1) * group)
        h3 = _dot(xb, w3_ref[:, cs])
        _issue_tiles_static(idx2_ref, x_hbm, nxt, nsem, (2 * c + 1) * group, (2 * c + 2) * group)
        hs.append((h1 * _sigmoid(h1) * h3).astype(BF16))
    _to_token_tiles(y_ref, _dot(jnp.concatenate(hs, axis=1), w2_ref[...]))

    @pl.when(i == n - 1)
    def _():
        _wait_tiles(x_hbm, xbuf.at[(i + 1) % MOE_SLOTS], sem.at[(i + 1) % MOE_SLOTS])
        _wait_tiles(x_hbm, nxt, nsem)


def _moe(x_tiles, src, tile_e, w1, w3, w2):
    tm = MOE_TILE
    n_tiles = src.shape[0]
    d = D_MODEL
    grid_spec = pltpu.PrefetchScalarGridSpec(
        num_scalar_prefetch=1,
        grid=(n_tiles,),
        in_specs=[
            _idx_spec(tm, n_tiles, None), _idx_spec(tm, n_tiles, 1), _idx_spec(tm, n_tiles, 2),
            pl.BlockSpec(memory_space=pl.ANY),
            pl.BlockSpec((None, d, D_EXPERT), lambda i, te: (te[i], 0, 0)),
            pl.BlockSpec((None, d, D_EXPERT), lambda i, te: (te[i], 0, 0)),
            pl.BlockSpec((None, D_EXPERT, d), lambda i, te: (te[i], 0, 0)),
        ],
        out_specs=pl.BlockSpec((tm * TILE_ROWS, LANES), lambda i, te: (i, 0)),
        scratch_shapes=[pltpu.VMEM((MOE_SLOTS, tm * TILE_ROWS, LANES), F32), pltpu.SemaphoreType.DMA((MOE_SLOTS,))],
    )
    src3 = src.reshape(n_tiles, 1, tm)
    return pl.pallas_call(
        _moe_kernel,
        grid_spec=grid_spec,
        out_shape=jax.ShapeDtypeStruct((n_tiles * tm * TILE_ROWS, LANES), F32),
        compiler_params=_cparams(("arbitrary",)),
        name="moe_experts",
    )(tile_e, src3, src3, src3, x_tiles, w1, w3, w2)


def _dispatch_plan(route, cnt, n_tiles):
    t = route.shape[0]
    a = t * TOP_K
    tm = MOE_TILE
    eid = jnp.arange(N_EXPERTS, dtype=jnp.int32)
    counts = cnt[0, 0:N_EXPERTS].astype(jnp.int32)
    tiles_e = (counts + tm - 1) // tm
    tile_end = jnp.cumsum(tiles_e)
    pstart = (tile_end - tiles_e) * tm
    e = route[:, 0:TOP_K].astype(jnp.int32)
    rank = route[:, 4:4 + TOP_K].astype(jnp.int32)
    pos = rank + jnp.sum(jnp.where(e[:, :, None] == eid[None, None, :], pstart[None, None, :], 0), axis=-1)
    pad_e = tiles_e * tm - counts
    cpad = jnp.cumsum(pad_e)
    k = jnp.arange(n_tiles * tm - a, dtype=jnp.int32)
    ke = jnp.sum((cpad[None, :] <= k[:, None]).astype(jnp.int32), axis=1)
    inside = k + jnp.sum(jnp.where(jnp.minimum(ke, N_EXPERTS - 1)[:, None] == eid[None, :],
                                   (pstart + counts - (cpad - pad_e))[None, :], 0), axis=1)
    padpos = jnp.where(ke < N_EXPERTS, inside, tile_end[-1] * tm + (k - cpad[-1]))
    keys = jnp.concatenate([pos.reshape(a), padpos])
    vals = jnp.concatenate([jnp.repeat(jnp.arange(t, dtype=jnp.int32), TOP_K), k % t])
    _, src = lax.sort_key_val(keys, vals)
    jt = jnp.arange(n_tiles, dtype=jnp.int32)
    tile_e = jnp.minimum(jnp.sum((tile_end[None, :] <= jt[:, None]).astype(jnp.int32), axis=1), N_EXPERTS - 1)
    return (src * TILE_ROWS).reshape(n_tiles, tm), pos.reshape(a) * TILE_ROWS, tile_e.astype(jnp.int32)


def _final_kernel(p0c_ref, p1c_ref, p0n_ref, p1n_ref, base_ref, route_ref, g_ref, b_ref, y_hbm, o_ref, ybuf, sem):
    i = pl.program_id(0)
    n = pl.num_programs(0)
    tm = FINAL_TILE
    slot = i % 2

    @pl.when(i == 0)
    def _():
        _issue_tiles_loop(p0c_ref, y_hbm, ybuf.at[0], sem.at[0], tm, 0)
        _issue_tiles_loop(p1c_ref, y_hbm, ybuf.at[0], sem.at[0], tm, tm)

    _wait_tiles(y_hbm, ybuf.at[slot], sem.at[slot])
    nxt = ybuf.at[1 - slot]
    nsem = sem.at[1 - slot]
    _issue_tiles_static(p0n_ref, y_hbm, nxt, nsem, 0, tm, 0)
    _issue_tiles_static(p1n_ref, y_hbm, nxt, nsem, 0, tm, tm)
    w = route_ref[...]
    cur = ybuf.at[slot]
    h = base_ref[...] + w[:, 2:3] * _from_token_tiles(cur, 0, tm) + w[:, 3:4] * _from_token_tiles(cur, tm, tm)
    o_ref[...] = _layernorm(h, g_ref[...], b_ref[...])

    @pl.when(i == n - 1)
    def _():
        _wait_tiles(y_hbm, nxt, nsem)


def _final(base, y_tiles, pos, route, g, b):
    t, d = base.shape
    tm = FINAL_TILE
    nb = t // tm
    row = lambda i: (i, 0)
    pos2 = pos.reshape(t, TOP_K)
    p0 = pos2[:, 0].reshape(nb, 1, tm)
    p1 = pos2[:, 1].reshape(nb, 1, tm)
    first, nxt = _idx_spec(tm, nb, None), _idx_spec(tm, nb, 1)
    return pl.pallas_call(
        _final_kernel,
        grid=(nb,),
        in_specs=[first, first, nxt, nxt,
                  pl.BlockSpec((tm, d), row), pl.BlockSpec((tm, LANES), row),
                  _const_spec((1, d)), _const_spec((1, d)),
                  pl.BlockSpec(memory_space=pl.ANY)],
        out_specs=pl.BlockSpec((tm, d), row),
        out_shape=jax.ShapeDtypeStruct((t, d), F32),
        scratch_shapes=[pltpu.VMEM((2, 2 * tm * TILE_ROWS, LANES), F32), pltpu.SemaphoreType.DMA((2,))],
        compiler_params=_cparams(("arbitrary",)),
        name="final_ln",
    )(p0, p1, p0, p1, base, route, g, b, y_tiles)


def _rot_cols(w):
    half = ROPE_DIM // 2
    return jnp.concatenate([-w[..., half:], w[..., :half]], axis=-1)


def _layer_weights(i, w_in, q_norm, w_uq, kv_norm, w_uk, w_uv, hg_norm, w_br_a, w_br_b, w_out, ln1_g, ln1_b,
                   w_rg, w_re, w_e_gate, w_e_up, w_e_down, w_ple, w_ple_gate, ln2_g, ln2_b):
    d = D_MODEL
    nh = MLA_HEADS
    win = w_in[i]
    c_kr = Q_LORA + KV_LORA
    c_f = c_kr + ROPE_DIM
    c_ga = c_f + N_HG
    w_kr = win[:, c_kr:c_f]
    zpad = jnp.zeros((d, LANES - ROPE_DIM), F32)
    wa = jnp.concatenate([win[:, 0:c_kr], win[:, c_f:c_ga], w_kr, zpad, _rot_cols(w_kr), zpad], axis=1)

    wq3 = w_uq[i].reshape(Q_LORA, nh, NOPE_DIM + ROPE_DIM)
    zq = jnp.zeros((Q_LORA, nh, HEAD_PAD - NOPE_DIM - ROPE_DIM), F32)
    wq = jnp.concatenate([wq3, zq], axis=-1).reshape(Q_LORA, QK_PAD)
    wq_rot = jnp.concatenate([jnp.zeros((Q_LORA, nh, NOPE_DIM), F32), _rot_cols(wq3[..., NOPE_DIM:]), zq],
                             axis=-1).reshape(Q_LORA, QK_PAD)

    wuk = w_uk[i]
    wuv = w_uv[i]
    k_nope = jnp.concatenate([wuk, jnp.zeros((KV_LORA, nh, HEAD_PAD - NOPE_DIM), F32)], axis=-1)
    eye = jnp.eye(ROPE_DIM, dtype=F32)
    k_rope = jnp.concatenate([jnp.zeros((ROPE_DIM, NOPE_DIM), F32), eye,
                              jnp.zeros((ROPE_DIM, HEAD_PAD - NOPE_DIM - ROPE_DIM), F32)], axis=-1)
    k_rope = jnp.concatenate([jnp.tile(k_rope[:, None, :], (1, nh, 1)).reshape(ROPE_DIM, QK_PAD),
                              jnp.zeros((LANES - ROPE_DIM, QK_PAD), F32)], axis=0)
    zv = jnp.zeros((KV_LORA, nh // 2, V_DIM), F32)
    wuv_a3 = jnp.concatenate([wuv[:, 0::2, :], zv], axis=-1)
    wuv_b3 = jnp.concatenate([zv, wuv[:, 1::2, :]], axis=-1)
    wkv_top = jnp.concatenate([k_nope.reshape(KV_LORA, QK_PAD), wuv_a3.reshape(KV_LORA, 512),
                               wuv_b3.reshape(KV_LORA, 512)], axis=1)
    wkv_bot = jnp.concatenate([k_rope, jnp.zeros((LANES, QK_PAD), F32)], axis=1)
    wkv = jnp.concatenate([wkv_top, wkv_bot], axis=0)

    wuk_t = jnp.concatenate([jnp.transpose(wuk, (1, 2, 0)),
                             jnp.zeros((nh, HEAD_PAD - NOPE_DIM, KV_LORA), F32)], axis=1)
    p_sel = jnp.concatenate([jnp.zeros((NOPE_DIM, ROPE_DIM), F32), eye,
                             jnp.zeros((HEAD_PAD - NOPE_DIM - ROPE_DIM, ROPE_DIM), F32)], axis=0)

    w_route = jnp.concatenate([w_re[i], w_rg[i], jnp.zeros((d, LANES - N_EXPERTS - N_GROUPS), F32)], axis=1)
    bf = lambda w: w.astype(BF16)
    return {
        "wa": bf(wa), "q_norm": q_norm[i].reshape(1, -1), "kv_norm": kv_norm[i].reshape(1, -1),
        "wq": bf(wq), "wq_rot": bf(wq_rot), "wkv": bf(wkv),
        "wuk_t": bf(wuk_t), "p_sel": bf(p_sel),
        "wuv_a3": bf(jnp.transpose(wuv_a3, (1, 0, 2))), "wuv_b3": bf(jnp.transpose(wuv_b3, (1, 0, 2))),
        "hg_norm": hg_norm[i].reshape(1, -1),
        "w_gate": bf(win[:, c_ga:c_ga + 2 * d]), "w_br_a": bf(w_br_a[i]), "w_br_b": bf(w_br_b[i]),
        "w_out": bf(w_out[i]), "ln1_g": ln1_g[i].reshape(1, -1), "ln1_b": ln1_b[i].reshape(1, -1),
        "w_route": bf(w_route), "w_ple_gate": bf(w_ple_gate[i]), "w_ple": bf(w_ple[i]),
        "w1": bf(w_e_gate[i]), "w3": bf(w_e_up[i]), "w2": bf(w_e_down[i]),
        "ln2_g": ln2_g[i].reshape(1, -1), "ln2_b": ln2_b[i].reshape(1, -1),
    }


def _rope_table(pos, rows):
    half = ROPE_DIM // 2
    inv = ROPE_THETA ** (-jnp.arange(half, dtype=F32) / half)
    ang = pos.astype(F32)[:, None] * inv[None, :]
    c2 = jnp.concatenate([jnp.cos(ang), jnp.cos(ang)], axis=1)
    s2 = jnp.concatenate([jnp.sin(ang), jnp.sin(ang)], axis=1)
    n = pos.shape[0]
    ones = jnp.ones((n, NOPE_DIM), F32)
    z = lambda w: jnp.zeros((n, w), F32)
    tab = jnp.concatenate([
        ones, c2, z(HEAD_PAD - NOPE_DIM - ROPE_DIM),
        z(NOPE_DIM), s2, z(HEAD_PAD - NOPE_DIM - ROPE_DIM),
        c2, z(LANES - ROPE_DIM),
        s2, z(LANES - ROPE_DIM)], axis=1)
    if rows > n:
        tab = jnp.tile(tab, (rows // n, 1))
    return tab


def _trunk_layer(i, xn, ple, tab, lw, hg_lb, batch, ln, cache, alpha, ln0):
    q, k, va, vb, ckv, kr, f_pre, q_pre, hv, g_pre = _proj(xn, tab, lw, ln0)
    if cache is None:
        a = _attn_prompt(q, k, va, vb, batch, ln)
        o, st = _hgrn(f_pre, q_pre, hv, g_pre, hg_lb, lw["hg_norm"], None, i, batch, ln)
    else:
        a = _attn_sample(q, ckv, kr, cache[0], cache[1], lw, batch, ln)
        o, st = _hgrn(f_pre, q_pre, hv, g_pre, hg_lb, lw["hg_norm"], cache[2], i, batch, ln)
    x1, base, route, cnt = _post(xn, a, o, ple, lw, alpha, ln0)

    t = xn.shape[0]
    n_tiles = (t * TOP_K) // MOE_TILE + N_EXPERTS
    src, pos, tile_e = _dispatch_plan(route, cnt, n_tiles)
    y = _moe(x1, src, tile_e, lw["w1"], lw["w3"], lw["w2"])
    x2 = _final(base, y, pos, route, lw["ln2_g"], lw["ln2_b"])
    return x2, ckv, kr, st


def kernel(x_prompt, x_sample, p_prompt, p_sample, cache_ckv, cache_krope, state_hgrn, ln0_g, ln0_b, w_in, q_norm, w_uq, kv_norm, w_uk, w_uv, hg_lb, hg_norm, w_br_a, w_br_b, w_out, ln1_g, ln1_b, w_router_group, w_router_expert, w_e_gate, w_e_up, w_e_down, w_ple, w_ple_gate, ln2_g, ln2_b):
    depth = w_in.shape[0]
    bp, sp, d = x_prompt.shape
    bs, ss, _ = x_sample.shape
    past = cache_ckv.shape[2]
    alpha = (2 * depth) ** 0.25

    xp = x_prompt.reshape(bp * sp, d)
    xs = x_sample.reshape(bs * ss, d)
    tab_p = _rope_table(jnp.arange(sp, dtype=jnp.int32), max(sp, TOKEN_TILE))
    tab_s = _rope_table(past + jnp.arange(ss, dtype=jnp.int32), max(ss, TOKEN_TILE))

    outs = [[] for _ in range(6)]
    for i in range(depth):
        lw = _layer_weights(i, w_in, q_norm, w_uq, kv_norm, w_uk, w_uv, hg_norm, w_br_a, w_br_b, w_out,
                            ln1_g, ln1_b, w_router_group, w_router_expert, w_e_gate, w_e_up, w_e_down,
                            w_ple, w_ple_gate, ln2_g, ln2_b)
        ln0 = (ln0_g, ln0_b) if i == 0 else None
        xp, c, r, s = _trunk_layer(i, xp, p_prompt[i].reshape(bp * sp, -1), tab_p, lw, hg_lb, bp, sp, None, alpha, ln0)
        outs[0].append(c.reshape(bp, sp, -1))
        outs[1].append(r.reshape(bp, sp, -1))
        outs[2].append(s)
        cache = (cache_ckv[i], cache_krope[i], state_hgrn[i])
        xs, c, r, s = _trunk_layer(i, xs, p_sample[i].reshape(bs * ss, -1), tab_s, lw, hg_lb, bs, ss, cache, alpha, ln0)
        outs[3].append(c.reshape(bs, ss, -1))
        outs[4].append(r.reshape(bs, ss, -1))
        outs[5].append(s)
    return (xp.reshape(bp, sp, d), xs.reshape(bs, ss, d)) + tuple(jnp.stack(o) for o in outs)
```

```python
import functools

import numpy as np
import jax
import jax.numpy as jnp
from jax import lax
from jax.experimental import pallas as pl
from jax.experimental.pallas import tpu as pltpu

F32 = jnp.float32
BF16 = jnp.bfloat16

D_MODEL = 1024
CHUNK = 64
PLE_DIM = 256
MLA_HEADS = 8
NOPE_DIM = 64
ROPE_DIM = 32
V_DIM = 64
Q_LORA = 384
KV_LORA = 256
ROPE_THETA = 10000.0
ATTN_SCALE = (NOPE_DIM + ROPE_DIM) ** -0.5
HG_HEADS = 4
HG_DK = 128
HG_DV = 128
HG_FDIM = HG_HEADS * HG_DK
HG_IDIM = HG_HEADS * HG_DV
N_GROUPS = 4
EXPERTS_PER_GROUP = 8
N_EXPERTS = N_GROUPS * EXPERTS_PER_GROUP
TOP_K = 2
D_EXPERT = 512
EPS = 1e-6

LANES = 128
HEAD_PAD = LANES
QK_PAD = MLA_HEADS * HEAD_PAD
VMEM_LIMIT = 56 * 1024 * 1024

TOKEN_TILE = 512
ATTN_TILE = 256
MOE_TILE = 512
HG_CHUNK = 128
HG_SEQS_PER_STEP = 4
FINAL_TILE = 256


def _cparams(sem):
    return pltpu.CompilerParams(dimension_semantics=sem, vmem_limit_bytes=VMEM_LIMIT)


def _const_spec(shape):
    nd = len(shape)
    return pl.BlockSpec(shape, lambda *_: (0,) * nd, pipeline_mode=pl.Buffered(1))


def _dot(a, b):
    return jnp.dot(a, b, preferred_element_type=F32)


def _dot_nt(a, b):
    return lax.dot_general(a, b, (((1,), (1,)), ((), ())), preferred_element_type=F32)


def _dot_tn(a, b):
    return lax.dot_general(a, b, (((0,), (0,)), ((), ())), preferred_element_type=F32)


def _div_pow2(x, d):
    return jnp.right_shift(x, int(d).bit_length() - 1)


def _sigmoid(x):
    return 1.0 / (1.0 + jnp.exp(-x))


def _layernorm(x, g, b):
    mu = jnp.mean(x, axis=-1, keepdims=True)
    xc = x - mu
    var = jnp.mean(xc * xc, axis=-1, keepdims=True)
    return xc * lax.rsqrt(var + EPS) * g + b


def _rmsnorm(x, g):
    return x * lax.rsqrt(jnp.mean(x * x, axis=-1, keepdims=True) + EPS) * g


N_LAT = Q_LORA + KV_LORA
N_HG = 2 * HG_FDIM + 2 * HG_IDIM
COL_HG = N_LAT
COL_KR = N_LAT + N_HG
N_PROJ = COL_KR + 2 * LANES


def _input_rows(refs, pre_ln):
    if pre_ln:
        x_ref, g_ref, b_ref = refs[:3]
        return _layernorm(x_ref[...], g_ref[...], b_ref[...]), refs[3:]
    return refs[0][...], refs[1:]


def _proj_kernel(*refs, pre_ln):
    x, refs = _input_rows(refs, pre_ln)
    (tab_ref, wa_ref, qn_ref, kvn_ref, wq_ref, wqr_ref, wkv_ref,
     q_ref, k_ref, va_ref, vb_ref, ckv_ref, kr_ref, f_ref, hq_ref, hv_ref, hg_ref) = refs
    xb = x.astype(BF16)
    cq_tab = tab_ref[:, 0:LANES]
    sq_tab = tab_ref[:, LANES:2 * LANES]
    ck_tab = tab_ref[:, 2 * LANES:3 * LANES]
    sk_tab = tab_ref[:, 3 * LANES:4 * LANES]

    lat = _dot(xb, wa_ref[:, 0:N_LAT])
    cqn = _rmsnorm(lat[:, 0:Q_LORA], qn_ref[...]).astype(BF16)
    ckvn = _rmsnorm(lat[:, Q_LORA:N_LAT], kvn_ref[...])
    ckv_ref[...] = ckvn

    qa = _dot(cqn, wq_ref[...])
    qb = _dot(cqn, wqr_ref[...])
    for h in range(MLA_HEADS):
        sl = slice(h * HEAD_PAD, (h + 1) * HEAD_PAD)
        q_ref[:, sl] = (qa[:, sl] * cq_tab + qb[:, sl] * sq_tab).astype(BF16)

    kr2 = _dot(xb, wa_ref[:, COL_KR:N_PROJ])
    kr = kr2[:, 0:LANES] * ck_tab + kr2[:, LANES:2 * LANES] * sk_tab
    kr_ref[...] = kr[:, 0:ROPE_DIM]

    kvin = jnp.concatenate([ckvn.astype(BF16), kr.astype(BF16)], axis=1)
    kv = _dot(kvin, wkv_ref[...])
    k_ref[...] = kv[:, 0:QK_PAD].astype(BF16)
    va_ref[...] = kv[:, QK_PAD:QK_PAD + 512].astype(BF16)
    vb_ref[...] = kv[:, QK_PAD + 512:QK_PAD + 1024].astype(BF16)

    f_ref[...] = _dot(xb, wa_ref[:, COL_HG:COL_HG + 512])
    hq_ref[...] = _dot(xb, wa_ref[:, COL_HG + 512:COL_HG + 1024])
    hv_ref[...] = _dot(xb, wa_ref[:, COL_HG + 1024:COL_HG + 1536]).astype(BF16)
    hg_ref[...] = _dot(xb, wa_ref[:, COL_HG + 1536:COL_HG + 2048])


def _proj(xn, tab, lw, ln0):
    t = xn.shape[0]
    pre_ln = ln0 is not None
    x_specs = [pl.BlockSpec((TOKEN_TILE, D_MODEL), lambda i: (i, 0))]
    x_args = [xn]
    if pre_ln:
        x_specs += [_const_spec((1, D_MODEL)), _const_spec((1, D_MODEL))]
        x_args += [ln0[0].reshape(1, D_MODEL), ln0[1].reshape(1, D_MODEL)]
    tm = TOKEN_TILE
    ntab = tab.shape[0] // tm
    row = lambda i: (i, 0)
    outs = [
        (QK_PAD, BF16), (QK_PAD, BF16), (512, BF16), (512, BF16), (KV_LORA, F32), (ROPE_DIM, F32),
        (HG_FDIM, F32), (HG_FDIM, F32), (HG_IDIM, BF16), (HG_IDIM, F32),
    ]
    return pl.pallas_call(
        functools.partial(_proj_kernel, pre_ln=pre_ln),
        grid=(t // tm,),
        in_specs=x_specs + [
            pl.BlockSpec((tm, 4 * LANES), lambda i: (i % ntab, 0)),
            _const_spec((D_MODEL, N_PROJ)),
            _const_spec((1, Q_LORA)), _const_spec((1, KV_LORA)),
            _const_spec((Q_LORA, QK_PAD)), _const_spec((Q_LORA, QK_PAD)),
            _const_spec((KV_LORA + LANES, 2 * QK_PAD)),
        ],
        out_specs=[pl.BlockSpec((tm, w), row) for w, _ in outs],
        out_shape=[jax.ShapeDtypeStruct((t, w), dt) for w, dt in outs],
        compiler_params=_cparams(("parallel",)),
        name="proj",
    )(*x_args, tab, lw["wa"], lw["q_norm"], lw["kv_norm"], lw["wq"], lw["wq_rot"], lw["wkv"])


LOG2E = 1.4426950408889634


def _attn_prompt_kernel(q_ref, k_ref, va_ref, vb_ref, o_ref, s_ref, m_ref, l_ref, acc_ref):
    tq = ATTN_TILE
    nh = MLA_HEADS
    qi = pl.program_id(1)
    rq = _div_pow2(lax.broadcasted_iota(jnp.int32, (tq, tq), 0), CHUNK)
    ck = _div_pow2(lax.broadcasted_iota(jnp.int32, (tq, tq), 1), CHUNK)
    diag_mask = ck <= rq
    c = ATTN_SCALE * LOG2E

    m_ref[...] = jnp.full(m_ref.shape, -jnp.inf, F32)
    l_ref[...] = jnp.zeros(l_ref.shape, F32)
    acc_ref[...] = jnp.zeros(acc_ref.shape, F32)

    def scores(kt, mask):
        ks = pl.multiple_of(kt * tq, tq)
        for h in range(nh):
            hs = slice(h * HEAD_PAD, (h + 1) * HEAD_PAD)
            s = _dot_nt(q_ref[:, hs], k_ref[pl.ds(ks, tq), hs]) * c
            if mask is not None:
                s = jnp.where(mask, s, -jnp.inf)
            s_ref[h, :, pl.ds(ks, tq)] = s
            m_ref[h] = jnp.maximum(m_ref[h], jnp.maximum(s[:, 0:LANES], s[:, LANES:2 * LANES]))

    def p1(kt, carry):
        scores(kt, None)
        return carry

    lax.fori_loop(0, qi, p1, 0)
    scores(qi, diag_mask)
    for h in range(nh):
        m_ref[h] = jnp.broadcast_to(jnp.max(m_ref[h], axis=-1, keepdims=True), (tq, LANES))

    def p2(kt, carry):
        ks = pl.multiple_of(kt * tq, tq)
        for h in range(nh):
            v_ref = va_ref if h % 2 == 0 else vb_ref
            pair = h // 2
            mb = m_ref[h]
            p_lo = jnp.exp2(s_ref[h, :, pl.ds(ks, LANES)] - mb)
            p_hi = jnp.exp2(s_ref[h, :, pl.ds(ks + LANES, LANES)] - mb)
            l_ref[h] += p_lo + p_hi
            p = jnp.concatenate([p_lo, p_hi], axis=1).astype(BF16)
            acc_ref[h] += _dot(p, v_ref[pl.ds(ks, tq), pair * LANES:(pair + 1) * LANES])
        return carry

    lax.fori_loop(0, qi + 1, p2, 0)
    for pair in range(nh // 2):
        he, ho = 2 * pair, 2 * pair + 1
        o = (acc_ref[he] / jnp.sum(l_ref[he], axis=-1, keepdims=True)
             + acc_ref[ho] / jnp.sum(l_ref[ho], axis=-1, keepdims=True))
        o_ref[:, pair * LANES:(pair + 1) * LANES] = o.astype(BF16)


def _attn_prompt(q, k, va, vb, batch, seq):
    tq = ATTN_TILE
    assert seq % tq == 0 and tq % CHUNK == 0, (seq, tq)
    nq = seq // tq
    nh = MLA_HEADS
    return pl.pallas_call(
        _attn_prompt_kernel,
        grid=(batch, nq),
        in_specs=[
            pl.BlockSpec((tq, QK_PAD), lambda b, i: (b * nq + i, 0)),
            pl.BlockSpec((seq, QK_PAD), lambda b, i: (b, 0)),
            pl.BlockSpec((seq, 512), lambda b, i: (b, 0)),
            pl.BlockSpec((seq, 512), lambda b, i: (b, 0)),
        ],
        out_specs=pl.BlockSpec((tq, 512), lambda b, i: (b * nq + i, 0)),
        out_shape=jax.ShapeDtypeStruct((batch * seq, 512), BF16),
        scratch_shapes=[pltpu.VMEM((nh, tq, seq), F32), pltpu.VMEM((nh, tq, LANES), F32),
                        pltpu.VMEM((nh, tq, LANES), F32), pltpu.VMEM((nh, tq, LANES), F32)],
        compiler_params=_cparams(("parallel", "arbitrary")),
        name="attn_prompt",
    )(q, k, va, vb)


SAMPLE_KEY_TILE = 1024


def _attn_sample_kernel(q_ref, ckvn_ref, krn_ref, cc_ref, ck_ref, wukt_ref, psel_ref, wva_ref, wvb_ref,
                        o_ref, s_ref):
    ln = q_ref.shape[0]
    past = cc_ref.shape[0]
    nh = MLA_HEADS
    qlat, qrope = [], []
    for h in range(nh):
        qh = q_ref[:, h * HEAD_PAD:(h + 1) * HEAD_PAD]
        qlat.append(_dot(qh, wukt_ref[h]).astype(BF16))
        qrope.append(_dot(qh, psel_ref[...]).astype(BF16))
    qlat = jnp.concatenate(qlat, axis=0)
    qrope = jnp.concatenate(qrope, axis=0)

    ckv_new = ckvn_ref[...].astype(BF16)
    kr_new = krn_ref[...].astype(BF16)
    s_new = (_dot_nt(qlat, ckv_new) + _dot_nt(qrope, kr_new)) * ATTN_SCALE
    m = jnp.max(s_new, axis=-1, keepdims=True)
    tk = SAMPLE_KEY_TILE
    for kt in range(past // tk):
        rows = slice(kt * tk, (kt + 1) * tk)
        s = (_dot_nt(qlat, cc_ref[rows, :].astype(BF16)) + _dot_nt(qrope, ck_ref[rows, :].astype(BF16))) * ATTN_SCALE
        s_ref[:, rows] = s
        m = jnp.maximum(m, jnp.max(s, axis=-1, keepdims=True))
    p_new = jnp.exp(s_new - m)
    l = jnp.sum(p_new, axis=-1, keepdims=True)
    lat = _dot(p_new.astype(BF16), ckv_new)
    for kt in range(past // tk):
        rows = slice(kt * tk, (kt + 1) * tk)
        p = jnp.exp(s_ref[:, rows] - m)
        l = l + jnp.sum(p, axis=-1, keepdims=True)
        lat = lat + _dot(p.astype(BF16), cc_ref[rows, :].astype(BF16))
    lat = (lat / l).astype(BF16)
    for pair in range(nh // 2):
        he, ho = 2 * pair, 2 * pair + 1
        o = _dot(lat[he * ln:(he + 1) * ln], wva_ref[pair]) + _dot(lat[ho * ln:(ho + 1) * ln], wvb_ref[pair])
        o_ref[:, pair * LANES:(pair + 1) * LANES] = o.astype(BF16)


def _attn_sample(q, ckv_new, kr_new, cache_ckv, cache_kr, lw, batch, ln):
    past = cache_ckv.shape[1]
    assert past % SAMPLE_KEY_TILE == 0, past
    nh = MLA_HEADS
    return pl.pallas_call(
        _attn_sample_kernel,
        grid=(batch,),
        in_specs=[
            pl.BlockSpec((ln, QK_PAD), lambda b: (b, 0)),
            pl.BlockSpec((ln, KV_LORA), lambda b: (b, 0)),
            pl.BlockSpec((ln, ROPE_DIM), lambda b: (b, 0)),
            pl.BlockSpec((None, past, KV_LORA), lambda b: (b, 0, 0)),
            pl.BlockSpec((None, past, ROPE_DIM), lambda b: (b, 0, 0)),
            _const_spec((nh, HEAD_PAD, KV_LORA)),
            _const_spec((HEAD_PAD, ROPE_DIM)),
            _const_spec((nh // 2, KV_LORA, LANES)),
            _const_spec((nh // 2, KV_LORA, LANES)),
        ],
        out_specs=pl.BlockSpec((ln, 512), lambda b: (b, 0)),
        out_shape=jax.ShapeDtypeStruct((batch * ln, 512), BF16),
        scratch_shapes=[pltpu.VMEM((nh * ln, past), F32)],
        compiler_params=_cparams(("parallel",)),
        name="attn_sample",
    )(q, ckv_new, kr_new, cache_ckv, cache_kr, lw["wuk_t"], lw["p_sel"], lw["wuv_a3"], lw["wuv_b3"])


def _hgrn_kernel(*refs, layer, chunk, has_init):
    if has_init:
        f_ref, q_ref, v_ref, g_ref, lb_ref, nrm_ref, s0_ref, o_ref, sout_ref, st_ref = refs
    else:
        f_ref, q_ref, v_ref, g_ref, lb_ref, nrm_ref, o_ref, sout_ref, st_ref = refs
        s0_ref = None
    nseq, tt = f_ref.shape[0], f_ref.shape[1]
    c = chunk
    ti = pl.program_id(1)

    @pl.when(ti == 0)
    def _():
        for sq in range(nseq):
            for h in range(HG_HEADS):
                if has_init:
                    st_ref[sq, h] = s0_ref[sq, h].T
                else:
                    st_ref[sq, h] = jnp.zeros((HG_DV, HG_DK), F32)

    lbp = lb_ref[...]
    e = jnp.exp(lbp - jnp.max(lbp, axis=0, keepdims=True))
    tot = jnp.sum(e, axis=0, keepdims=True)
    part = jnp.zeros_like(tot)
    for j in range(1, layer + 1):
        part = part + e[j:j + 1]
    lb = part / tot
    log_lb = jnp.log(lb)
    log_1m = jnp.log1p(-lb)
    one_m = 1.0 - lb

    rid = lax.broadcasted_iota(jnp.int32, (c, 1), 0)
    rr = lax.broadcasted_iota(jnp.int32, (c, c), 0)
    cc = lax.broadcasted_iota(jnp.int32, (c, c), 1)
    levels = [c >> k for k in range(c.bit_length() - 1)]
    sums = [cc <= rr, cc > rr]
    for g in levels:
        ref_row = jnp.bitwise_and(rr, -g) + (g // 2 - 1)
        is_right = jnp.bitwise_and(rr, g - 1) >= g // 2
        lo_row = jnp.where(is_right, ref_row, rr)
        hi_row = jnp.where(is_right, rr, ref_row)
        sums.append(jnp.logical_and(cc > lo_row, cc <= hi_row))
    sum_mat = jnp.concatenate([m.astype(BF16) for m in sums], axis=0)
    pair_masks = [None] + [_div_pow2(rr, g) == _div_pow2(cc, g) for g in levels[1:]] + [rr == cc]

    for ci, sq in [(ci, sq) for ci in range(tt // c) for sq in range(nseq)]:
        rows = slice(ci * c, (ci + 1) * c)
        z = f_ref[sq, rows, :]
        ez = jnp.exp(-jnp.abs(z))
        rz = 1.0 / (1.0 + ez)
        log_sig = jnp.minimum(z, 0.0) - jnp.log1p(ez)
        kk = one_m * jnp.where(z >= 0, ez * rz, rz)
        y = log_1m + log_sig
        logf = jnp.maximum(log_lb, y) + jnp.log1p(jnp.exp(-jnp.abs(log_lb - y)))
        qp = q_ref[sq, rows, :]
        qq = qp * _sigmoid(qp)

        nl = -logf
        hi = nl.astype(BF16)
        lo = (nl - hi.astype(F32)).astype(BF16)
        acc = _dot(sum_mat, hi) + _dot(sum_mat, lo)
        q_in = (qq * jnp.exp(-acc[0:c])).astype(BF16)
        dec = jnp.exp(-acc[c - 1:c])
        k_out = (kk * jnp.exp(-acc[c:2 * c])).astype(BF16)
        qs, ks = [], []
        for li, g in enumerate(levels):
            x = jnp.exp(-acc[(2 + li) * c:(3 + li) * c])
            right = jnp.bitwise_and(rid, g - 1) >= g // 2
            qs.append(jnp.where(right, qq * x, 0.0).astype(BF16))
            ks.append(jnp.where(right, 0.0, kk * x).astype(BF16))
        qs.append(qq.astype(BF16))
        ks.append(kk.astype(BF16))

        gp = g_ref[sq, rows, :]
        gate = gp * _sigmoid(gp) * nrm_ref[...]

        for h in range(HG_HEADS):
            hs = slice(h * HG_DK, (h + 1) * HG_DK)
            a = None
            for ql, kl, mk in zip(qs, ks, pair_masks):
                al = _dot_nt(ql[:, hs], kl[:, hs])
                if mk is not None:
                    al = jnp.where(mk, al, 0.0)
                a = al if a is None else a + al
            vh = v_ref[sq, rows, hs]
            st = st_ref[sq, h]
            o = _dot_nt(q_in[:, hs], st.astype(BF16)) + _dot(a.astype(BF16), vh)
            o = o * lax.rsqrt(jnp.mean(o * o, axis=-1, keepdims=True) + EPS) * gate[:, hs]
            o_ref[sq, rows, hs] = o.astype(BF16)
            st_ref[sq, h] = st * dec[:, hs] + _dot_tn(vh, k_out[:, hs])

    @pl.when(ti == pl.num_programs(1) - 1)
    def _():
        for sq in range(nseq):
            for h in range(HG_HEADS):
                sout_ref[sq, h] = st_ref[sq, h].T


def _hgrn(f_pre, q_pre, v, g_pre, hg_lb, hg_norm, s0, layer, batch, ln):
    tt = min(ln, 256)
    chunk = min(tt, HG_CHUNK)
    nt = ln // tt
    nseq = HG_SEQS_PER_STEP
    assert batch % nseq == 0 and ln % tt == 0 and tt % chunk == 0, (batch, ln, tt, chunk)
    has_init = s0 is not None
    depth = hg_lb.shape[0]
    blk = lambda b, t: (b, t, 0)
    st_spec = pl.BlockSpec((nseq, HG_HEADS, HG_DK, HG_DV), lambda b, t: (b, 0, 0, 0))
    in_specs = [
        pl.BlockSpec((nseq, tt, HG_FDIM), blk), pl.BlockSpec((nseq, tt, HG_FDIM), blk),
        pl.BlockSpec((nseq, tt, HG_IDIM), blk), pl.BlockSpec((nseq, tt, HG_IDIM), blk),
        _const_spec((depth, HG_FDIM)), _const_spec((1, HG_IDIM)),
    ]
    args = [x.reshape(batch, ln, x.shape[-1]) for x in (f_pre, q_pre, v, g_pre)] + [hg_lb, hg_norm]
    if has_init:
        in_specs.append(st_spec)
        args.append(s0)
    o, st = pl.pallas_call(
        functools.partial(_hgrn_kernel, layer=layer, chunk=chunk, has_init=has_init),
        grid=(batch // nseq, nt),
        in_specs=in_specs,
        out_specs=[pl.BlockSpec((nseq, tt, HG_IDIM), blk), st_spec],
        out_shape=[jax.ShapeDtypeStruct((batch, ln, HG_IDIM), BF16),
                   jax.ShapeDtypeStruct((batch, HG_HEADS, HG_DK, HG_DV), F32)],
        scratch_shapes=[pltpu.VMEM((nseq, HG_HEADS, HG_DV, HG_DK), F32)],
        compiler_params=_cparams(("parallel", "arbitrary")),
        name="hgrn",
    )(*args)
    return o.reshape(batch * ln, HG_IDIM), st


ROUTE_GROUP_LANE = N_EXPERTS


def _route(logits, cnt_ref):
    lane_i = lax.broadcasted_iota(jnp.int32, logits.shape, 1)
    lane = lane_i.astype(F32)
    big = float(LANES)
    neg = -jnp.inf
    is_g = jnp.logical_and(lane_i >= ROUTE_GROUP_LANE, lane_i < ROUTE_GROUP_LANE + N_GROUPS)
    gl = jnp.where(is_g, logits, neg)
    gmax = jnp.max(gl, axis=-1, keepdims=True)
    gidx = jnp.min(jnp.where(gl == gmax, lane - ROUTE_GROUP_LANE, big), axis=-1, keepdims=True)
    g_top = 1.0 / jnp.sum(jnp.exp(gl - gmax), axis=-1, keepdims=True)
    lane_group = _div_pow2(lane_i, EXPERTS_PER_GROUP).astype(F32)
    in_group = jnp.logical_and(lane_i < N_EXPERTS, lane_group == gidx)
    el = jnp.where(in_group, logits, neg)
    m1 = jnp.max(el, axis=-1, keepdims=True)
    i1 = jnp.min(jnp.where(el == m1, lane, big), axis=-1, keepdims=True)
    el2 = jnp.where(lane == i1, neg, el)
    m2 = jnp.max(el2, axis=-1, keepdims=True)
    i2 = jnp.min(jnp.where(el2 == m2, lane, big), axis=-1, keepdims=True)
    r = jnp.exp(m2 - m1)
    w1 = g_top / (1.0 + r)
    w2 = g_top * r / (1.0 + r)
    tm = logits.shape[0]
    hit = jnp.logical_or(lane == i1, lane == i2)
    rr = lax.broadcasted_iota(jnp.int32, (tm, tm), 0)
    cc = lax.broadcasted_iota(jnp.int32, (tm, tm), 1)
    before = _dot((cc < rr).astype(BF16), jnp.where(hit, 1.0, 0.0).astype(BF16)) + cnt_ref[...]
    r1 = jnp.sum(jnp.where(lane == i1, before, 0.0), axis=-1, keepdims=True)
    r2 = jnp.sum(jnp.where(lane == i2, before, 0.0), axis=-1, keepdims=True)
    cnt_ref[...] += jnp.sum(jnp.where(hit, 1.0, 0.0), axis=0, keepdims=True)
    out = jnp.where(lane_i == 0, i1, 0.0)
    out = jnp.where(lane_i == 1, i2, out)
    out = jnp.where(lane_i == 2, w1, out)
    out = jnp.where(lane_i == 3, w2, out)
    out = jnp.where(lane_i == 4, r1, out)
    out = jnp.where(lane_i == 5, r2, out)
    return out


def _post_kernel(*refs, alpha, pre_ln):
    x, refs = _input_rows(refs, pre_ln)
    (a_ref, o_ref, ple_ref, wg_ref, wa_ref, wb_ref, wo_ref, g1_ref, b1_ref,
     wr_ref, wpg_ref, wp_ref, x1_ref, base_ref, route_ref, cnt_ref) = refs

    @pl.when(pl.program_id(0) == 0)
    def _():
        cnt_ref[...] = jnp.zeros_like(cnt_ref)

    xb = x.astype(BF16)
    ya = _dot(a_ref[...], wa_ref[...])
    merged = _sigmoid(_dot(xb, wg_ref[:, 0:D_MODEL])) * ya
    yb = _dot(o_ref[...], wb_ref[...])
    merged = merged + _sigmoid(_dot(xb, wg_ref[:, D_MODEL:2 * D_MODEL])) * yb
    mix = _dot(merged.astype(BF16), wo_ref[...])
    x1 = _layernorm(alpha * x + mix, g1_ref[...], b1_ref[...])
    _to_token_tiles(x1_ref, x1)
    x1b = x1.astype(BF16)
    route_ref[...] = _route(_dot(x1b, wr_ref[...]), cnt_ref)
    ple = _dot(ple_ref[...].astype(BF16), wp_ref[...])
    base_ref[...] = alpha * x1 + _sigmoid(_dot(x1b, wpg_ref[...])) * ple


def _post(xn, a, o, ple, lw, alpha, ln0):
    t = xn.shape[0]
    tm = TOKEN_TILE
    row = lambda i: (i, 0)
    d = D_MODEL
    pre_ln = ln0 is not None
    x_specs = [pl.BlockSpec((tm, d), row)]
    x_args = [xn]
    if pre_ln:
        x_specs += [_const_spec((1, d)), _const_spec((1, d))]
        x_args += [ln0[0].reshape(1, d), ln0[1].reshape(1, d)]
    return pl.pallas_call(
        functools.partial(_post_kernel, alpha=alpha, pre_ln=pre_ln),
        grid=(t // tm,),
        in_specs=x_specs + [
            pl.BlockSpec((tm, 512), row), pl.BlockSpec((tm, HG_IDIM), row),
            pl.BlockSpec((tm, PLE_DIM), row),
            _const_spec((d, 2 * d)), _const_spec((512, d)), _const_spec((HG_IDIM, d)), _const_spec((d, d)),
            _const_spec((1, d)), _const_spec((1, d)),
            _const_spec((d, LANES)), _const_spec((d, d)), _const_spec((PLE_DIM, d)),
        ],
        out_specs=[pl.BlockSpec((tm * TILE_ROWS, LANES), row), pl.BlockSpec((tm, d), row),
                   pl.BlockSpec((tm, LANES), row), pl.BlockSpec((1, LANES), lambda i: (0, 0))],
        out_shape=[jax.ShapeDtypeStruct((t * TILE_ROWS, LANES), F32), jax.ShapeDtypeStruct((t, d), F32),
                   jax.ShapeDtypeStruct((t, LANES), F32), jax.ShapeDtypeStruct((1, LANES), F32)],
        compiler_params=_cparams(("arbitrary",)),
        name="post",
    )(*x_args, a, o, ple, lw["w_gate"], lw["w_br_a"], lw["w_br_b"], lw["w_out"], lw["ln1_g"], lw["ln1_b"],
      lw["w_route"], lw["w_ple_gate"], lw["w_ple"])


TILE_ROWS = D_MODEL // LANES


def _to_token_tiles(ref, x):
    n = x.shape[0]
    for c in range(TILE_ROWS):
        ref[pl.ds(c, n, stride=TILE_ROWS), :] = x[:, c * LANES:(c + 1) * LANES]


def _from_token_tiles(ref, first_token, n):
    return jnp.concatenate(
        [ref[pl.ds(first_token * TILE_ROWS + c, n, stride=TILE_ROWS), :] for c in range(TILE_ROWS)], axis=1)


def _tile_copy(idx_ref, j, src_hbm, dst, dst_token, sem):
    r = pl.multiple_of(idx_ref[0, j], TILE_ROWS)
    d = dst_token * TILE_ROWS
    if not isinstance(d, int):
        d = pl.multiple_of(d, TILE_ROWS)
    return pltpu.make_async_copy(src_hbm.at[pl.ds(r, TILE_ROWS)], dst.at[pl.ds(d, TILE_ROWS)], sem)


def _issue_tiles_loop(idx_ref, src_hbm, dst, sem, n_tokens, dst_off=0):
    def body(j, carry):
        _tile_copy(idx_ref, j, src_hbm, dst, dst_off + j, sem).start()
        return carry

    lax.fori_loop(0, n_tokens, body, 0, unroll=8)


def _issue_tiles_static(idx_ref, src_hbm, dst, sem, lo, hi, dst_off=0):
    for j in range(lo, hi):
        _tile_copy(idx_ref, j, src_hbm, dst, dst_off + j, sem).start(priority=j % 2)


def _wait_tiles(src_hbm, dst, sem):
    pltpu.make_async_copy(src_hbm.at[pl.ds(0, dst.shape[0])], dst, sem).wait()


def _idx_spec(tile, n_blocks, ahead):
    if ahead is None:
        imap = lambda i, *_: (0, 0, 0)
    else:
        imap = lambda i, *_: (jnp.minimum(i + ahead, n_blocks - 1), 0, 0)
    return pl.BlockSpec((None, 1, tile), imap, memory_space=pltpu.SMEM)


MOE_SLOTS = 3


def _moe_kernel(te_ref, idx0_ref, idx1_ref, idx2_ref, x_hbm, w1_ref, w3_ref, w2_ref, y_ref, xbuf, sem):
    i = pl.program_id(0)
    n = pl.num_programs(0)
    tm = MOE_TILE
    slot = i % MOE_SLOTS

    @pl.when(i == 0)
    def _():
        _issue_tiles_loop(idx0_ref, x_hbm, xbuf.at[0], sem.at[0], tm)
        _issue_tiles_loop(idx1_ref, x_hbm, xbuf.at[1], sem.at[1], tm)

    _wait_tiles(x_hbm, xbuf.at[slot], sem.at[slot])
    xb = _from_token_tiles(xbuf.at[slot], 0, tm).astype(BF16)
    ahead = (i + 2) % MOE_SLOTS
    nxt = xbuf.at[ahead]
    nsem = sem.at[ahead]
    group = tm // 4
    hs = []
    half = D_EXPERT // 2
    for c in range(2):
        cs = slice(c * half, (c + 1) * half)
        h1 = _dot(xb, w1_ref[:, cs])
        _issue_tiles_static(idx2_ref, x_hbm, nxt, nsem, 2 * c * group, (2 * c + 1) * group)
        h3 = _dot(xb, w3_ref[:, cs])
        _issue_tiles_static(idx2_ref, x_hbm, nxt, nsem, (2 * c + 1) * group, (2 * c + 2) * group)
        hs.append((h1 * _sigmoid(h1) * h3).astype(BF16))
    _to_token_tiles(y_ref, _dot(jnp.concatenate(hs, axis=1), w2_ref[...]))

    @pl.when(i == n - 1)
    def _():
        _wait_tiles(x_hbm, xbuf.at[(i + 1) % MOE_SLOTS], sem.at[(i + 1) % MOE_SLOTS])
        _wait_tiles(x_hbm, nxt, nsem)


def _moe(x_tiles, src, tile_e, w1, w3, w2):
    tm = MOE_TILE
    n_tiles = src.shape[0]
    d = D_MODEL
    grid_spec = pltpu.PrefetchScalarGridSpec(
        num_scalar_prefetch=1,
        grid=(n_tiles,),
        in_specs=[
            _idx_spec(tm, n_tiles, None), _idx_spec(tm, n_tiles, 1), _idx_spec(tm, n_tiles, 2),
            pl.BlockSpec(memory_space=pl.ANY),
            pl.BlockSpec((None, d, D_EXPERT), lambda i, te: (te[i], 0, 0)),
            pl.BlockSpec((None, d, D_EXPERT), lambda i, te: (te[i], 0, 0)),
            pl.BlockSpec((None, D_EXPERT, d), lambda i, te: (te[i], 0, 0)),
        ],
        out_specs=pl.BlockSpec((tm * TILE_ROWS, LANES), lambda i, te: (i, 0)),
        scratch_shapes=[pltpu.VMEM((MOE_SLOTS, tm * TILE_ROWS, LANES), F32), pltpu.SemaphoreType.DMA((MOE_SLOTS,))],
    )
    src3 = src.reshape(n_tiles, 1, tm)
    return pl.pallas_call(
        _moe_kernel,
        grid_spec=grid_spec,
        out_shape=jax.ShapeDtypeStruct((n_tiles * tm * TILE_ROWS, LANES), F32),
        compiler_params=_cparams(("arbitrary",)),
        name="moe_experts",
    )(tile_e, src3, src3, src3, x_tiles, w1, w3, w2)


def _dispatch_plan(route, cnt, n_tiles):
    t = route.shape[0]
    a = t * TOP_K
    tm = MOE_TILE
    eid = jnp.arange(N_EXPERTS, dtype=jnp.int32)
    counts = cnt[0, 0:N_EXPERTS].astype(jnp.int32)
    tiles_e = (counts + tm - 1) // tm
    tile_end = jnp.cumsum(tiles_e)
    pstart = (tile_end - tiles_e) * tm
    e = route[:, 0:TOP_K].astype(jnp.int32)
    rank = route[:, 4:4 + TOP_K].astype(jnp.int32)
    pos = rank + jnp.sum(jnp.where(e[:, :, None] == eid[None, None, :], pstart[None, None, :], 0), axis=-1)
    pad_e = tiles_e * tm - counts
    cpad = jnp.cumsum(pad_e)
    k = jnp.arange(n_tiles * tm - a, dtype=jnp.int32)
    ke = jnp.sum((cpad[None, :] <= k[:, None]).astype(jnp.int32), axis=1)
    inside = k + jnp.sum(jnp.where(jnp.minimum(ke, N_EXPERTS - 1)[:, None] == eid[None, :],
                                   (pstart + counts - (cpad - pad_e))[None, :], 0), axis=1)
    padpos = jnp.where(ke < N_EXPERTS, inside, tile_end[-1] * tm + (k - cpad[-1]))
    keys = jnp.concatenate([pos.reshape(a), padpos])
    vals = jnp.concatenate([jnp.repeat(jnp.arange(t, dtype=jnp.int32), TOP_K), k % t])
    _, src = lax.sort_key_val(keys, vals)
    jt = jnp.arange(n_tiles, dtype=jnp.int32)
    tile_e = jnp.minimum(jnp.sum((tile_end[None, :] <= jt[:, None]).astype(jnp.int32), axis=1), N_EXPERTS - 1)
    return (src * TILE_ROWS).reshape(n_tiles, tm), pos.reshape(a) * TILE_ROWS, tile_e.astype(jnp.int32)


def _final_kernel(p0c_ref, p1c_ref, p0n_ref, p1n_ref, base_ref, route_ref, g_ref, b_ref, y_hbm, o_ref, ybuf, sem):
    i = pl.program_id(0)
    n = pl.num_programs(0)
    tm = FINAL_TILE
    slot = i % 2

    @pl.when(i == 0)
    def _():
        _issue_tiles_loop(p0c_ref, y_hbm, ybuf.at[0], sem.at[0], tm, 0)
        _issue_tiles_loop(p1c_ref, y_hbm, ybuf.at[0], sem.at[0], tm, tm)

    _wait_tiles(y_hbm, ybuf.at[slot], sem.at[slot])
    nxt = ybuf.at[1 - slot]
    nsem = sem.at[1 - slot]
    _issue_tiles_static(p0n_ref, y_hbm, nxt, nsem, 0, tm, 0)
    _issue_tiles_static(p1n_ref, y_hbm, nxt, nsem, 0, tm, tm)
    w = route_ref[...]
    cur = ybuf.at[slot]
    h = base_ref[...] + w[:, 2:3] * _from_token_tiles(cur, 0, tm) + w[:, 3:4] * _from_token_tiles(cur, tm, tm)
    o_ref[...] = _layernorm(h, g_ref[...], b_ref[...])

    @pl.when(i == n - 1)
    def _():
        _wait_tiles(y_hbm, nxt, nsem)


def _final(base, y_tiles, pos, route, g, b):
    t, d = base.shape
    tm = FINAL_TILE
    nb = t // tm
    row = lambda i: (i, 0)
    pos2 = pos.reshape(t, TOP_K)
    p0 = pos2[:, 0].reshape(nb, 1, tm)
    p1 = pos2[:, 1].reshape(nb, 1, tm)
    first, nxt = _idx_spec(tm, nb, None), _idx_spec(tm, nb, 1)
    return pl.pallas_call(
        _final_kernel,
        grid=(nb,),
        in_specs=[first, first, nxt, nxt,
                  pl.BlockSpec((tm, d), row), pl.BlockSpec((tm, LANES), row),
                  _const_spec((1, d)), _const_spec((1, d)),
                  pl.BlockSpec(memory_space=pl.ANY)],
        out_specs=pl.BlockSpec((tm, d), row),
        out_shape=jax.ShapeDtypeStruct((t, d), F32),
        scratch_shapes=[pltpu.VMEM((2, 2 * tm * TILE_ROWS, LANES), F32), pltpu.SemaphoreType.DMA((2,))],
        compiler_params=_cparams(("arbitrary",)),
        name="final_ln",
    )(p0, p1, p0, p1, base, route, g, b, y_tiles)


def _rot_cols(w):
    half = ROPE_DIM // 2
    return jnp.concatenate([-w[..., half:], w[..., :half]], axis=-1)


def _layer_weights(i, w_in, q_norm, w_uq, kv_norm, w_uk, w_uv, hg_norm, w_br_a, w_br_b, w_out, ln1_g, ln1_b,
                   w_rg, w_re, w_e_gate, w_e_up, w_e_down, w_ple, w_ple_gate, ln2_g, ln2_b):
    d = D_MODEL
    nh = MLA_HEADS
    win = w_in[i]
    c_kr = Q_LORA + KV_LORA
    c_f = c_kr + ROPE_DIM
    c_ga = c_f + N_HG
    w_kr = win[:, c_kr:c_f]
    zpad = jnp.zeros((d, LANES - ROPE_DIM), F32)
    wa = jnp.concatenate([win[:, 0:c_kr], win[:, c_f:c_ga], w_kr, zpad, _rot_cols(w_kr), zpad], axis=1)

    wq3 = w_uq[i].reshape(Q_LORA, nh, NOPE_DIM + ROPE_DIM)
    zq = jnp.zeros((Q_LORA, nh, HEAD_PAD - NOPE_DIM - ROPE_DIM), F32)
    wq = jnp.concatenate([wq3, zq], axis=-1).reshape(Q_LORA, QK_PAD)
    wq_rot = jnp.concatenate([jnp.zeros((Q_LORA, nh, NOPE_DIM), F32), _rot_cols(wq3[..., NOPE_DIM:]), zq],
                             axis=-1).reshape(Q_LORA, QK_PAD)

    wuk = w_uk[i]
    wuv = w_uv[i]
    k_nope = jnp.concatenate([wuk, jnp.zeros((KV_LORA, nh, HEAD_PAD - NOPE_DIM), F32)], axis=-1)
    eye = jnp.eye(ROPE_DIM, dtype=F32)
    k_rope = jnp.concatenate([jnp.zeros((ROPE_DIM, NOPE_DIM), F32), eye,
                              jnp.zeros((ROPE_DIM, HEAD_PAD - NOPE_DIM - ROPE_DIM), F32)], axis=-1)
    k_rope = jnp.concatenate([jnp.tile(k_rope[:, None, :], (1, nh, 1)).reshape(ROPE_DIM, QK_PAD),
                              jnp.zeros((LANES - ROPE_DIM, QK_PAD), F32)], axis=0)
    zv = jnp.zeros((KV_LORA, nh // 2, V_DIM), F32)
    wuv_a3 = jnp.concatenate([wuv[:, 0::2, :], zv], axis=-1)
    wuv_b3 = jnp.concatenate([zv, wuv[:, 1::2, :]], axis=-1)
    wkv_top = jnp.concatenate([k_nope.reshape(KV_LORA, QK_PAD), wuv_a3.reshape(KV_LORA, 512),
                               wuv_b3.reshape(KV_LORA, 512)], axis=1)
    wkv_bot = jnp.concatenate([k_rope, jnp.zeros((LANES, QK_PAD), F32)], axis=1)
    wkv = jnp.concatenate([wkv_top, wkv_bot], axis=0)

    wuk_t = jnp.concatenate([jnp.transpose(wuk, (1, 2, 0)),
                             jnp.zeros((nh, HEAD_PAD - NOPE_DIM, KV_LORA), F32)], axis=1)
    p_sel = jnp.concatenate([jnp.zeros((NOPE_DIM, ROPE_DIM), F32), eye,
                             jnp.zeros((HEAD_PAD - NOPE_DIM - ROPE_DIM, ROPE_DIM), F32)], axis=0)

    w_route = jnp.concatenate([w_re[i], w_rg[i], jnp.zeros((d, LANES - N_EXPERTS - N_GROUPS), F32)], axis=1)
    bf = lambda w: w.astype(BF16)
    return {
        "wa": bf(wa), "q_norm": q_norm[i].reshape(1, -1), "kv_norm": kv_norm[i].reshape(1, -1),
        "wq": bf(wq), "wq_rot": bf(wq_rot), "wkv": bf(wkv),
        "wuk_t": bf(wuk_t), "p_sel": bf(p_sel),
        "wuv_a3": bf(jnp.transpose(wuv_a3, (1, 0, 2))), "wuv_b3": bf(jnp.transpose(wuv_b3, (1, 0, 2))),
        "hg_norm": hg_norm[i].reshape(1, -1),
        "w_gate": bf(win[:, c_ga:c_ga + 2 * d]), "w_br_a": bf(w_br_a[i]), "w_br_b": bf(w_br_b[i]),
        "w_out": bf(w_out[i]), "ln1_g": ln1_g[i].reshape(1, -1), "ln1_b": ln1_b[i].reshape(1, -1),
        "w_route": bf(w_route), "w_ple_gate": bf(w_ple_gate[i]), "w_ple": bf(w_ple[i]),
        "w1": bf(w_e_gate[i]), "w3": bf(w_e_up[i]), "w2": bf(w_e_down[i]),
        "ln2_g": ln2_g[i].reshape(1, -1), "ln2_b": ln2_b[i].reshape(1, -1),
    }


def _rope_table(pos, rows):
    half = ROPE_DIM // 2
    inv = ROPE_THETA ** (-jnp.arange(half, dtype=F32) / half)
    ang = pos.astype(F32)[:, None] * inv[None, :]
    c2 = jnp.concatenate([jnp.cos(ang), jnp.cos(ang)], axis=1)
    s2 = jnp.concatenate([jnp.sin(ang), jnp.sin(ang)], axis=1)
    n = pos.shape[0]
    ones = jnp.ones((n, NOPE_DIM), F32)
    z = lambda w: jnp.zeros((n, w), F32)
    tab = jnp.concatenate([
        ones, c2, z(HEAD_PAD - NOPE_DIM - ROPE_DIM),
        z(NOPE_DIM), s2, z(HEAD_PAD - NOPE_DIM - ROPE_DIM),
        c2, z(LANES - ROPE_DIM),
        s2, z(LANES - ROPE_DIM)], axis=1)
    if rows > n:
        tab = jnp.tile(tab, (rows // n, 1))
    return tab


def _trunk_layer(i, xn, ple, tab, lw, hg_lb, batch, ln, cache, alpha, ln0):
    q, k, va, vb, ckv, kr, f_pre, q_pre, hv, g_pre = _proj(xn, tab, lw, ln0)
    if cache is None:
        a = _attn_prompt(q, k, va, vb, batch, ln)
        o, st = _hgrn(f_pre, q_pre, hv, g_pre, hg_lb, lw["hg_norm"], None, i, batch, ln)
    else:
        a = _attn_sample(q, ckv, kr, cache[0], cache[1], lw, batch, ln)
        o, st = _hgrn(f_pre, q_pre, hv, g_pre, hg_lb, lw["hg_norm"], cache[2], i, batch, ln)
    x1, base, route, cnt = _post(xn, a, o, ple, lw, alpha, ln0)

    t = xn.shape[0]
    assert t % TOKEN_TILE == 0 and t % FINAL_TILE == 0 and (t * TOP_K) % MOE_TILE == 0, t
    n_tiles = (t * TOP_K) // MOE_TILE + N_EXPERTS
    src, pos, tile_e = _dispatch_plan(route, cnt, n_tiles)
    y = _moe(x1, src, tile_e, lw["w1"], lw["w3"], lw["w2"])
    x2 = _final(base, y, pos, route, lw["ln2_g"], lw["ln2_b"])
    return x2, ckv, kr, st


def kernel(x_prompt, x_sample, p_prompt, p_sample, cache_ckv, cache_krope, state_hgrn, ln0_g, ln0_b, w_in, q_norm, w_uq, kv_norm, w_uk, w_uv, hg_lb, hg_norm, w_br_a, w_br_b, w_out, ln1_g, ln1_b, w_router_group, w_router_expert, w_e_gate, w_e_up, w_e_down, w_ple, w_ple_gate, ln2_g, ln2_b):
    depth = w_in.shape[0]
    bp, sp, d = x_prompt.shape
    bs, ss, _ = x_sample.shape
    past = cache_ckv.shape[2]
    alpha = (2 * depth) ** 0.25

    xp = x_prompt.reshape(bp * sp, d)
    xs = x_sample.reshape(bs * ss, d)
    tab_p = _rope_table(jnp.arange(sp, dtype=jnp.int32), max(sp, TOKEN_TILE))
    tab_s = _rope_table(past + jnp.arange(ss, dtype=jnp.int32), max(ss, TOKEN_TILE))

    outs = [[] for _ in range(6)]
    for i in range(depth):
        lw = _layer_weights(i, w_in, q_norm, w_uq, kv_norm, w_uk, w_uv, hg_norm, w_br_a, w_br_b, w_out,
                            ln1_g, ln1_b, w_router_group, w_router_expert, w_e_gate, w_e_up, w_e_down,
                            w_ple, w_ple_gate, ln2_g, ln2_b)
        ln0 = (ln0_g, ln0_b) if i == 0 else None
        xp, c, r, s = _trunk_layer(i, xp, p_prompt[i].reshape(bp * sp, -1), tab_p, lw, hg_lb, bp, sp, None, alpha, ln0)
        outs[0].append(c.reshape(bp, sp, -1))
        outs[1].append(r.reshape(bp, sp, -1))
        outs[2].append(s)
        cache = (cache_ckv[i], cache_krope[i], state_hgrn[i])
        xs, c, r, s = _trunk_layer(i, xs, p_sample[i].reshape(bs * ss, -1), tab_s, lw, hg_lb, bs, ss, cache, alpha, ln0)
        outs[3].append(c.reshape(bs, ss, -1))
        outs[4].append(r.reshape(bs, ss, -1))
        outs[5].append(s)
    return (xp.reshape(bp, sp, d), xs.reshape(bs, ss, d)) + tuple(jnp.stack(o) for o in outs)
```
